```python
import math
import jax
import jax.numpy as jnp
from jax import lax
import numpy as np

D_MODEL = 1024
BATCH = 8
SEQ = 2048
DEPTH = 1
DEC_BATCH = 128
DEC_SEQ = 4
PAST_LEN = 2048
PAGE_SIZE = 128

ATTN_W = D_MODEL // 2
POOL_W = D_MODEL - ATTN_W
HEAD_DIM = 64
QK_HEAD = 2 * HEAD_DIM
VAL_DIM = 2 * HEAD_DIM
N_HEADS = ATTN_W // VAL_DIM
QK_W = N_HEADS * QK_HEAD
POOL_WINDOWS = (2, 4, 8, 16)
N_POOL_GROUPS = len(POOL_WINDOWS)
POOL_GROUP_W = POOL_W // N_POOL_GROUPS
POOL_BUF = max(POOL_WINDOWS) - 1
IN_W = POOL_W + 2 * QK_W + N_HEADS * VAL_DIM
MIX_W = POOL_W + N_HEADS * VAL_DIM
N_EXPERTS = 32
TOP_K = 4
D_FF = D_MODEL
SWIGLU_LIMIT = 7.0
SWIGLU_ALPHA = 1.702
EXPERT_BLOCK = 128
Q_BLOCK = 128
RMS_EPS = 1e-6
NEG_INF = -1e30

kernel_name = 'hymba_pool_diffattn_moe_step'


def _rmsnorm(x, g):
    xf = x.astype(jnp.float32)
    y = xf * lax.rsqrt(jnp.mean(xf * xf, axis=-1, keepdims=True) + RMS_EPS)
    return (y * g.astype(jnp.float32)).astype(x.dtype)


def _pool_mix(u_ext, pos0, n_out, pool_w, pool_scale):
    B, L, _ = u_ext.shape
    ug = u_ext.astype(jnp.float32).reshape(B, L, N_POOL_GROUPS, POOL_GROUP_W)
    cs = jnp.cumsum(ug, axis=1)
    pos = (pos0 + jnp.arange(L)).astype(jnp.float32)
    diffs = []
    for g, w in enumerate(POOL_WINDOWS):
        c = cs[:, :, g]
        c_prev = jnp.pad(c, ((0, 0), (w, 0), (0, 0)))[:, :L]
        cnt = jnp.minimum(float(w), pos + 1.0)
        diffs.append((c - c_prev) / cnt[None, :, None] - ug[:, :, g])
    d = jnp.stack(diffs, axis=2)[:, L - n_out:]
    out = jnp.einsum('btgc,gce->btge', d.astype(pool_w.dtype), pool_w)
    return (out.reshape(B, n_out, POOL_W) * pool_scale).astype(u_ext.dtype)


def _diff_attn_block(q, qpos, k, v, kpos, slopes, lam):
    s = jnp.einsum('bqhcd,bkhcd->bhcqk', q, k, preferred_element_type=jnp.float32) * (HEAD_DIM ** -0.5)
    dist = (qpos[:, None] - kpos[None, :]).astype(jnp.float32)
    s = s - slopes[None, :, None, None, None] * dist
    s = jnp.where(dist >= 0.0, s, NEG_INF)
    p = jax.nn.softmax(s, axis=-1)
    a = p[:, :, 0] - lam * p[:, :, 1]
    return jnp.einsum('bhqk,bkhe->bqhe', a.astype(v.dtype), v, preferred_element_type=jnp.float32)


def _diff_attention(q, k, v, qpos, kpos, slopes, lam):
    B, T = q.shape[0], q.shape[1]
    qb = Q_BLOCK if T % Q_BLOCK == 0 else T
    nb = T // qb
    qs = q.reshape(B, nb, qb, N_HEADS, 2, HEAD_DIM).swapaxes(0, 1)
    ps = qpos.reshape(nb, qb)
    out = lax.map(lambda a: _diff_attn_block(a[0], a[1], k, v, kpos, slopes, lam), (qs, ps))
    return out.swapaxes(0, 1).reshape(B, T, N_HEADS, VAL_DIM)


def _moe(h, router_w, router_b, w1, b1, w2, b2):
    N, D = h.shape
    logits = jnp.matmul(h, router_w, preferred_element_type=jnp.float32) + router_b.astype(jnp.float32)
    top_v, top_e = lax.top_k(logits, TOP_K)
    gates = jax.nn.softmax(top_v, axis=-1)
    A = N * TOP_K
    e_flat = top_e.reshape(A)
    g_flat = gates.reshape(A)
    tok_flat = jnp.arange(A, dtype=jnp.int32) // TOP_K
    order = jnp.argsort(e_flat)
    e_sorted = e_flat[order]
    counts = jnp.bincount(e_flat, length=N_EXPERTS)
    padded = (counts + EXPERT_BLOCK - 1) // EXPERT_BLOCK * EXPERT_BLOCK
    starts = jnp.cumsum(counts) - counts
    pad_ends = jnp.cumsum(padded)
    pad_starts = pad_ends - padded
    dest = pad_starts[e_sorted] + jnp.arange(A, dtype=jnp.int32) - starts[e_sorted]
    n_blocks = -(-A // EXPERT_BLOCK) + N_EXPERTS
    P = n_blocks * EXPERT_BLOCK
    row_tok = jnp.full((P,), N, jnp.int32).at[dest].set(tok_flat[order])
    row_gate = jnp.zeros((P,), jnp.float32).at[dest].set(g_flat[order])
    block_e = jnp.minimum(jnp.searchsorted(pad_ends, jnp.arange(n_blocks, dtype=jnp.int32) * EXPERT_BLOCK, side='right'), N_EXPERTS - 1)
    h_pad = jnp.concatenate([h, jnp.zeros((1, D), h.dtype)], axis=0)
    xb = h_pad[row_tok].reshape(n_blocks, EXPERT_BLOCK, D)

    def expert_block(args):
        xe, e = args
        a = xe @ w1[e] + b1[e]
        glu = jnp.minimum(a[:, 0::2], SWIGLU_LIMIT)
        lin = jnp.clip(a[:, 1::2], -SWIGLU_LIMIT, SWIGLU_LIMIT)
        act = glu * jax.nn.sigmoid(SWIGLU_ALPHA * glu) * (lin + 1.0)
        return act @ w2[e] + b2[e]

    yb = lax.map(expert_block, (xb, block_e)).reshape(P, D)
    y = jnp.zeros((N + 1, D), jnp.float32).at[row_tok].add(yb.astype(jnp.float32) * row_gate[:, None])
    return y[:N].astype(h.dtype)


def _layer(x, pos0, pool_prefix, k_prefix, v_prefix, lambda_init,
           norm1_g, w_in, pool_w, pool_scale, lq1, lk1, lq2, lk2, subln_g,
           w_out, norm2_g, router_w, router_b, w1, b1, w2, b2):
    B, T, D = x.shape
    xn = _rmsnorm(x, norm1_g)
    z = xn @ w_in
    u = z[..., :POOL_W]
    q = z[..., POOL_W:POOL_W + QK_W].reshape(B, T, N_HEADS, 2, HEAD_DIM)
    k_new = z[..., POOL_W + QK_W:POOL_W + 2 * QK_W].reshape(B, T, N_HEADS, QK_HEAD)
    v_new = z[..., POOL_W + 2 * QK_W:].reshape(B, T, N_HEADS, VAL_DIM)
    u_ext = u if pool_prefix is None else jnp.concatenate([pool_prefix.astype(u.dtype), u], axis=1)
    pool_pos0 = pos0 - (u_ext.shape[1] - T)
    pool_out = _pool_mix(u_ext, pool_pos0, T, pool_w, pool_scale)
    new_pool = u_ext[:, u_ext.shape[1] - POOL_BUF:]
    k_all = k_new if k_prefix is None else jnp.concatenate([k_prefix.astype(k_new.dtype), k_new], axis=1)
    v_all = v_new if v_prefix is None else jnp.concatenate([v_prefix.astype(v_new.dtype), v_new], axis=1)
    Lk = k_all.shape[1]
    qpos = pos0 + jnp.arange(T, dtype=jnp.int32)
    kpos = jnp.arange(Lk, dtype=jnp.int32)
    slopes = jnp.exp2(-8.0 * (jnp.arange(N_HEADS, dtype=jnp.float32) + 1.0) / N_HEADS)
    f32 = jnp.float32
    lam = (jnp.exp(jnp.sum(lq1.astype(f32) * lk1.astype(f32)))
           - jnp.exp(jnp.sum(lq2.astype(f32) * lk2.astype(f32))) + lambda_init)
    o = _diff_attention(q, k_all.reshape(B, Lk, N_HEADS, 2, HEAD_DIM), v_all, qpos, kpos, slopes, lam)
    o = (_rmsnorm(o, subln_g) * (1.0 - lambda_init)).astype(x.dtype).reshape(B, T, N_HEADS * VAL_DIM)
    mix = jnp.concatenate([pool_out, o], axis=-1) @ w_out
    h = x + mix
    ff = _moe(_rmsnorm(h, norm2_g).reshape(B * T, D), router_w, router_b, w1, b1, w2, b2)
    h = h + ff.reshape(B, T, D)
    return h, new_pool, k_new, v_new


def setup_inputs(seed: int = 0) -> dict:
    key = jax.random.key(seed)
    ks = jax.random.split(key, 26)
    f32 = jnp.float32
    n_pages = PAST_LEN // PAGE_SIZE
    n_used = DEC_BATCH * n_pages
    n_phys = n_used + max(1, n_used // 4)

    def nrm(k, shape, s):
        return jax.random.normal(k, shape, f32) * s

    page_table = jax.random.permutation(ks[5], n_phys)[:n_used].reshape(DEC_BATCH, n_pages).astype(jnp.int32)
    return {
        'x_prompt': nrm(ks[0], (BATCH, SEQ, D_MODEL), 1.0),
        'x_sample': nrm(ks[1], (DEC_BATCH, DEC_SEQ, D_MODEL), 1.0),
        'state_pool': nrm(ks[2], (DEPTH, DEC_BATCH, POOL_BUF, POOL_W), 1.0),
        'cache_k': nrm(ks[3], (DEPTH, n_phys, PAGE_SIZE, N_HEADS, QK_HEAD), 1.0),
        'cache_v': nrm(ks[4], (DEPTH, n_phys, PAGE_SIZE, N_HEADS, VAL_DIM), 1.0),
        'page_table': page_table,
        'norm1_g': 1.0 + nrm(ks[6], (DEPTH, D_MODEL), 0.02),
        'w_in': nrm(ks[7], (DEPTH, D_MODEL, IN_W), D_MODEL ** -0.5),
        'pool_w': nrm(ks[8], (DEPTH, N_POOL_GROUPS, POOL_GROUP_W, POOL_GROUP_W), POOL_GROUP_W ** -0.5),
        'pool_scale': 1.0 + nrm(ks[9], (DEPTH, POOL_W), 0.1),
        'lambda_q1': nrm(ks[10], (DEPTH, HEAD_DIM), 0.1),
        'lambda_k1': nrm(ks[11], (DEPTH, HEAD_DIM), 0.1),
        'lambda_q2': nrm(ks[12], (DEPTH, HEAD_DIM), 0.1),
        'lambda_k2': nrm(ks[13], (DEPTH, HEAD_DIM), 0.1),
        'subln_g': 1.0 + nrm(ks[14], (DEPTH, VAL_DIM), 0.02),
        'w_out': nrm(ks[15], (DEPTH, MIX_W, D_MODEL), MIX_W ** -0.5),
        'norm2_g': 1.0 + nrm(ks[16], (DEPTH, D_MODEL), 0.02),
        'router_w': nrm(ks[17], (DEPTH, D_MODEL, N_EXPERTS), D_MODEL ** -0.5),
        'router_b': nrm(ks[18], (DEPTH, N_EXPERTS), 0.01),
        'w1': nrm(ks[19], (DEPTH, N_EXPERTS, D_MODEL, 2 * D_FF), D_MODEL ** -0.5),
        'b1': nrm(ks[20], (DEPTH, N_EXPERTS, 2 * D_FF), 0.01),
        'w2': nrm(ks[21], (DEPTH, N_EXPERTS, D_FF, D_MODEL), D_FF ** -0.5),
        'b2': nrm(ks[22], (DEPTH, N_EXPERTS, D_MODEL), 0.01),
        'normf_g': 1.0 + nrm(ks[23], (D_MODEL,), 0.02),
    }


def reference(x_prompt, x_sample, state_pool, cache_k, cache_v, page_table,
              norm1_g, w_in, pool_w, pool_scale, lambda_q1, lambda_k1, lambda_q2, lambda_k2,
              subln_g, w_out, norm2_g, router_w, router_b, w1, b1, w2, b2, normf_g):
    dec_batch, n_pages = page_table.shape
    past_len = n_pages * cache_k.shape[2]
    hp = x_prompt
    hs = x_sample
    pool_p, pool_s, kp_l, vp_l, ks_l, vs_l = [], [], [], [], [], []
    for l in range(DEPTH):
        lambda_init = 0.8 - 0.6 * math.exp(-0.3 * l)
        params = (norm1_g[l], w_in[l], pool_w[l], pool_scale[l], lambda_q1[l], lambda_k1[l],
                  lambda_q2[l], lambda_k2[l], subln_g[l], w_out[l], norm2_g[l],
                  router_w[l], router_b[l], w1[l], b1[l], w2[l], b2[l])
        hp, pp, kp, vp = _layer(hp, 0, None, None, None, lambda_init, *params)
        k_past = cache_k[l][page_table].reshape(dec_batch, past_len, N_HEADS, QK_HEAD)
        v_past = cache_v[l][page_table].reshape(dec_batch, past_len, N_HEADS, VAL_DIM)
        hs, ps, kss, vss = _layer(hs, past_len, state_pool[l], k_past, v_past, lambda_init, *params)
        pool_p.append(pp)
        pool_s.append(ps)
        kp_l.append(kp)
        vp_l.append(vp)
        ks_l.append(kss)
        vs_l.append(vss)
    y_prompt = _rmsnorm(hp, normf_g)
    y_sample = _rmsnorm(hs, normf_g)
    return (y_prompt, y_sample, jnp.stack(pool_p), jnp.stack(pool_s), jnp.stack(kp_l), jnp.stack(vp_l), jnp.stack(ks_l), jnp.stack(vs_l))
```

```python
import functools
import math

import jax
import jax.numpy as jnp
from jax import lax
from jax.experimental import pallas as pl
from jax.experimental.pallas import tpu as pltpu

F32 = jnp.float32
BF16 = jnp.bfloat16
I32 = jnp.int32

RMS_EPS = 1e-6
NEG_INF = -1e30
HEAD_DIM = 64
HEAD_W = 2 * HEAD_DIM
POOL_WINDOWS = (2, 4, 8, 16)
POOL_HALO = 16
TOP_K = 4
SWIGLU_LIMIT = 7.0
SWIGLU_ALPHA = 1.702
LANES = 128
SUBLANES = 8
ROW_TILE = 512
ATTN_TILE = 256
EXPERT_ROWS = 512
COMBINE_TILE = 256
VMEM_LIMIT = 48 * 1024 * 1024


def _cparams(*sem):
    return pltpu.CompilerParams(dimension_semantics=sem, vmem_limit_bytes=VMEM_LIMIT)


def _dot(a, b):
    return jnp.dot(a, b, preferred_element_type=F32)


def _dot_nt(a, b):
    return lax.dot_general(a, b, (((1,), (1,)), ((), ())), preferred_element_type=F32)


def _rms(x, g):
    return x * lax.rsqrt(jnp.mean(x * x, axis=-1, keepdims=True) + RMS_EPS) * g


def _inproj_kernel(x_ref, g_ref, w_ref, u_ref, q_ref, k_ref, v_ref, kb_ref, vb_ref, *, pool_w, qk_w):
    xb = _rms(x_ref[...], g_ref[...]).astype(BF16)
    c0, c1, c2 = pool_w, pool_w + qk_w, pool_w + 2 * qk_w
    u_ref[...] = _dot(xb, w_ref[:, 0:c0])
    q_ref[...] = _dot(xb, w_ref[:, c0:c1]) * (HEAD_DIM ** -0.5)
    k = _dot(xb, w_ref[:, c1:c2])
    k_ref[...] = k
    kb_ref[...] = k.astype(BF16)
    v = _dot(xb, w_ref[:, c2:])
    v_ref[...] = v
    vb_ref[...] = v.astype(BF16)


def _inproj(x, g1, w_in_b, pool_w, qk_w):
    n, d = x.shape
    in_w = w_in_b.shape[1]
    val_w = in_w - pool_w - 2 * qk_w
    tm = min(ROW_TILE, n)
    row = lambda w: pl.BlockSpec((tm, w), lambda i: (i, 0))
    full = lambda a: pl.BlockSpec(a.shape, lambda i: (0,) * a.ndim)
    return pl.pallas_call(
        functools.partial(_inproj_kernel, pool_w=pool_w, qk_w=qk_w),
        grid=(n // tm,),
        in_specs=[row(d), full(g1), full(w_in_b)],
        out_specs=[row(pool_w), row(qk_w), row(qk_w), row(val_w), row(qk_w), row(val_w)],
        out_shape=[jax.ShapeDtypeStruct((n, pool_w), F32), jax.ShapeDtypeStruct((n, qk_w), F32),
                   jax.ShapeDtypeStruct((n, qk_w), F32), jax.ShapeDtypeStruct((n, val_w), F32),
                   jax.ShapeDtypeStruct((n, qk_w), BF16), jax.ShapeDtypeStruct((n, val_w), BF16)],
        compiler_params=_cparams("parallel"),
        name="inproj",
    )(x, g1, w_in_b)


def _pool_group(s_win, tok, cnt, pw, ps):
    d = s_win / cnt - tok
    return _dot(d.astype(BF16), pw) * ps


def _pool_prompt_kernel(u_ref, pw_ref, ps_ref, o_ref, carry_ref, *, tt):
    j = pl.program_id(1)

    @pl.when(j == 0)
    def _():
        carry_ref[...] = jnp.zeros_like(carry_ref)

    cur = u_ref[...]
    ext = jnp.concatenate([carry_ref[...], cur], axis=0)
    carry_ref[...] = cur[tt - POOL_HALO:, :]
    pos = (j * tt + lax.broadcasted_iota(I32, (tt, 1), 0)).astype(F32)
    gw = cur.shape[1] // len(POOL_WINDOWS)
    for g, w in enumerate(POOL_WINDOWS):
        cols = slice(g * gw, (g + 1) * gw)
        e = ext[:, cols]
        s, span = e, 1
        while span < w:
            s = s + pltpu.roll(s, span, axis=0)
            span *= 2
        cnt = jnp.minimum(float(w), pos + 1.0)
        out = _pool_group(s[POOL_HALO:, :], e[POOL_HALO:, :], cnt, pw_ref[g], ps_ref[:, cols])
        o_ref[:, cols] = out.astype(o_ref.dtype)


def _pool_prompt(u, pool_w_b, pool_scale, batch, seq):
    pw = u.shape[1]
    tt = min(ROW_TILE, seq)
    u3 = u.reshape(batch, seq, pw)
    out = pl.pallas_call(
        functools.partial(_pool_prompt_kernel, tt=tt),
        grid=(batch, seq // tt),
        in_specs=[pl.BlockSpec((None, tt, pw), lambda b, j: (b, j, 0)),
                  pl.BlockSpec(pool_w_b.shape, lambda b, j: (0, 0, 0)),
                  pl.BlockSpec(pool_scale.shape, lambda b, j: (0, 0))],
        out_specs=pl.BlockSpec((None, tt, pw), lambda b, j: (b, j, 0)),
        out_shape=jax.ShapeDtypeStruct((batch, seq, pw), BF16),
        scratch_shapes=[pltpu.VMEM((POOL_HALO, pw), F32)],
        compiler_params=_cparams("arbitrary", "arbitrary"),
        name="pool_prompt",
    )(u3, pool_w_b, pool_scale)
    return out.reshape(batch * seq, pw)


def _pool_sample_kernel(sp_ref, u_ref, pw_ref, ps_ref, o_ref, np_ref, *, n_new, n_buf):
    rows = [sp_ref[:, r, :] for r in range(n_buf)] + [u_ref[:, r, :] for r in range(n_new)]
    for r in range(n_buf):
        np_ref[:, r, :] = rows[n_new + r]
    gw = rows[0].shape[1] // len(POOL_WINDOWS)
    for t in range(n_new):
        i = n_buf + t
        for g, w in enumerate(POOL_WINDOWS):
            cols = slice(g * gw, (g + 1) * gw)
            s = rows[i][:, cols]
            for back in range(1, w):
                s = s + rows[i - back][:, cols]
            out = _pool_group(s, rows[i][:, cols], float(w), pw_ref[g], ps_ref[:, cols])
            o_ref[:, t, cols] = out


def _pool_sample(state_pool, u3, pool_w_b, pool_scale):
    bs, n_buf, pw = state_pool.shape
    n_new = u3.shape[1]
    full = lambda a: pl.BlockSpec(a.shape, lambda i: (0,) * a.ndim)
    return pl.pallas_call(
        functools.partial(_pool_sample_kernel, n_new=n_new, n_buf=n_buf),
        grid=(1,),
        in_specs=[full(state_pool), full(u3), full(pool_w_b), full(pool_scale)],
        out_specs=[pl.BlockSpec((bs, n_new, pw), lambda i: (0, 0, 0)),
                   pl.BlockSpec((bs, n_buf, pw), lambda i: (0, 0, 0))],
        out_shape=[jax.ShapeDtypeStruct((bs, n_new, pw), F32),
                   jax.ShapeDtypeStruct((bs, n_buf, pw), F32)],
        compiler_params=_cparams("arbitrary"),
        name="pool_sample",
    )(state_pool, u3, pool_w_b, pool_scale)


def _lambda_full(lq1, lk1, lq2, lk2, lam_init):
    e1 = jnp.exp(jnp.sum(lq1 * lk1, axis=-1, keepdims=True))
    e2 = jnp.exp(jnp.sum(lq2 * lk2, axis=-1, keepdims=True))
    return e1 - e2 + lam_init


def _split_halves(q):
    lane = lax.broadcasted_iota(I32, q.shape, 1)
    zero = jnp.zeros_like(q)
    return jnp.concatenate([jnp.where(lane < HEAD_DIM, q, zero), jnp.where(lane >= HEAD_DIM, q, zero)], axis=0)


def _subln(o, g, lam_init):
    return _rms(o, g) * (1.0 - lam_init)


def _attn_prompt_kernel(slopes_ref, q_ref, k_ref, v_ref, lq1_ref, lk1_ref, lq2_ref, lk2_ref, g_ref, o_ref,
                        *, tq, lam_init):
    h = pl.program_id(1)
    i = pl.program_id(2)
    slope = slopes_ref[h]
    lam = _lambda_full(lq1_ref[...], lk1_ref[...], lq2_ref[...], lk2_ref[...], lam_init)
    qq = _split_halves(q_ref[...].astype(BF16))
    r = lax.broadcasted_iota(I32, (2 * tq, tq), 0)
    r = jnp.where(r >= tq, r - tq, r)
    c = lax.broadcasted_iota(I32, (2 * tq, tq), 1)
    rel = r - c
    bias = slope * rel.astype(F32)

    def step(j, carry, diag):
        m, l, acc = carry
        start = pl.multiple_of(j * tq, tq)
        kj = k_ref[pl.ds(start, tq), :]
        vj = v_ref[pl.ds(start, tq), :]
        s = _dot_nt(qq, kj) - (bias + slope * ((i - j) * tq).astype(F32))
        if diag:
            s = jnp.where(rel >= 0, s, NEG_INF)
        m_new = jnp.maximum(m, jnp.max(s, axis=-1, keepdims=True))
        alpha = jnp.exp(m - m_new)
        p = jnp.exp(s - m_new)
        l = alpha * l + jnp.sum(p, axis=-1, keepdims=True)
        acc = alpha * acc + _dot(p.astype(BF16), vj)
        return m_new, l, acc

    init = (jnp.full((2 * tq, 1), NEG_INF, F32), jnp.zeros((2 * tq, 1), F32), jnp.zeros((2 * tq, HEAD_W), F32))
    carry = lax.fori_loop(0, i, lambda j, cr: step(j, cr, False), init)
    _, l, acc = step(i, carry, True)
    o = acc / l
    o = o[:tq] - lam * o[tq:]
    o_ref[...] = _subln(o, g_ref[...], lam_init).astype(o_ref.dtype)


def _attn_prompt(q, kb, vb, slopes, lams, subln_g, batch, seq, n_heads, lam_init):
    tq = min(ATTN_TILE, seq)
    q3, k3, v3 = (a.reshape(batch, seq, n_heads * HEAD_W) for a in (q, kb, vb))
    small = lambda a: pl.BlockSpec(a.shape, lambda b, h, i, s: (0,) * a.ndim)
    grid_spec = pltpu.PrefetchScalarGridSpec(
        num_scalar_prefetch=1,
        grid=(batch, n_heads, seq // tq),
        in_specs=[pl.BlockSpec((None, tq, HEAD_W), lambda b, h, i, s: (b, i, h)),
                  pl.BlockSpec((None, seq, HEAD_W), lambda b, h, i, s: (b, 0, h)),
                  pl.BlockSpec((None, seq, HEAD_W), lambda b, h, i, s: (b, 0, h)),
                  *[small(a) for a in lams], small(subln_g)],
        out_specs=pl.BlockSpec((None, tq, HEAD_W), lambda b, h, i, s: (b, i, h)),
    )
    out = pl.pallas_call(
        functools.partial(_attn_prompt_kernel, tq=tq, lam_init=lam_init),
        grid_spec=grid_spec,
        out_shape=jax.ShapeDtypeStruct((batch, seq, n_heads * HEAD_W), BF16),
        compiler_params=_cparams("parallel", "parallel", "arbitrary"),
        name="attn_prompt",
    )(slopes, q3, k3, v3, *lams, subln_g)
    return out.reshape(batch * seq, n_heads * HEAD_W)


def _attn_sample_kernel(pt_ref, slopes_ref, q_ref, kn_ref, vn_ref, kc_ref, vc_ref,
                        lq1_ref, lk1_ref, lq2_ref, lk2_ref, g_ref, o_ref, kbuf, vbuf, sem,
                        *, n_pages, page_rows, n_heads, n_new, lam_init):
    b = pl.program_id(0)
    nb = pl.num_programs(0)
    slot = b % 2
    past = n_pages * page_rows // n_heads

    def page_copies(seq, sl, pg):
        p = pt_ref[seq * n_pages + pg]
        dst = pl.ds(pg * page_rows, page_rows)
        return (pltpu.make_async_copy(kc_ref.at[p], kbuf.at[sl, dst, :], sem.at[sl, 0]),
                pltpu.make_async_copy(vc_ref.at[p], vbuf.at[sl, dst, :], sem.at[sl, 1]))

    def start_seq(seq, sl):
        for pg in range(n_pages):
            for cp in page_copies(seq, sl, pg):
                cp.start()

    @pl.when(b == 0)
    def _():
        start_seq(0, 0)

    @pl.when(b + 1 < nb)
    def _():
        start_seq(b + 1, 1 - slot)

    for pg in range(n_pages):
        for cp in page_copies(b, slot, pg):
            cp.wait()

    lam = _lambda_full(lq1_ref[...], lk1_ref[...], lq2_ref[...], lk2_ref[...], lam_init)
    rows = 2 * SUBLANES
    r = lax.broadcasted_iota(I32, (rows, past), 0)
    t_past = jnp.where(r >= n_new, r - n_new, r)
    dist_past = (past + t_past - lax.broadcasted_iota(I32, (rows, past), 1)).astype(F32)
    r1 = lax.broadcasted_iota(I32, (rows, 1), 0)
    t_new = jnp.where(r1 >= n_new, r1 - n_new, r1)
    for h in range(n_heads):
        slope = slopes_ref[h]
        cols = slice(h * HEAD_W, (h + 1) * HEAD_W)
        q = q_ref[:, cols]
        qq = jnp.concatenate([_split_halves(q), jnp.zeros((rows - 2 * n_new, HEAD_W), F32)], axis=0)
        kh = kbuf[slot, pl.ds(h, past, stride=n_heads), :].astype(BF16)
        vh = vbuf[slot, pl.ds(h, past, stride=n_heads), :].astype(BF16)
        s = _dot_nt(qq.astype(BF16), kh) - slope * dist_past
        kn = kn_ref[:, cols]
        vn = vn_ref[:, cols]
        s_new = []
        for c in range(n_new):
            sc = jnp.sum(qq * kn[c:c + 1, :], axis=-1, keepdims=True) - slope * (t_new - c).astype(F32)
            s_new.append(jnp.where(t_new >= c, sc, NEG_INF))
        m = jnp.max(s, axis=-1, keepdims=True)
        for sc in s_new:
            m = jnp.maximum(m, sc)
        p = jnp.exp(s - m)
        l = jnp.sum(p, axis=-1, keepdims=True)
        acc = _dot(p.astype(BF16), vh)
        for c, sc in enumerate(s_new):
            pc = jnp.exp(sc - m)
            l = l + pc
            acc = acc + pc * vn[c:c + 1, :]
        o = acc / l
        o = o[0:n_new] - lam * o[n_new:2 * n_new]
        o_ref[:, cols] = _subln(o, g_ref[...], lam_init)


def _attn_sample(q3, kn3, vn3, kc, vc, page_table, slopes, lams, subln_g, n_heads, lam_init):
    bs, n_new, qk_w = q3.shape
    n_pages = page_table.shape[1]
    page_rows = kc.shape[1]
    small = lambda a: pl.BlockSpec(a.shape, lambda b, pt, s: (0,) * a.ndim)
    tok = pl.BlockSpec((None, n_new, qk_w), lambda b, pt, s: (b, 0, 0))
    grid_spec = pltpu.PrefetchScalarGridSpec(
        num_scalar_prefetch=2,
        grid=(bs,),
        in_specs=[tok, tok, tok, pl.BlockSpec(memory_space=pl.ANY), pl.BlockSpec(memory_space=pl.ANY),
                  *[small(a) for a in lams], small(subln_g)],
        out_specs=tok,
        scratch_shapes=[pltpu.VMEM((2, n_pages * page_rows, LANES), F32),
                        pltpu.VMEM((2, n_pages * page_rows, LANES), F32),
                        pltpu.SemaphoreType.DMA((2, 2))],
    )
    return pl.pallas_call(
        functools.partial(_attn_sample_kernel, n_pages=n_pages, page_rows=page_rows, n_heads=n_heads,
                          n_new=n_new, lam_init=lam_init),
        grid_spec=grid_spec,
        out_shape=jax.ShapeDtypeStruct((bs, n_new, qk_w), F32),
        compiler_params=_cparams("arbitrary"),
        name="attn_sample",
    )(page_table.reshape(-1), slopes, q3, kn3, vn3, kc, vc, *lams, subln_g)


def _outproj_kernel(pool_ref, o_ref, x_ref, wo_ref, g2_ref, rw_ref, rb_ref, cnt_ref, hn_in_ref,
                    h_ref, hn_ref, te_ref, gate_ref, rank_ref, cnt_out_ref, run_ref, *, tm, n_experts):
    del hn_in_ref
    i = pl.program_id(0)

    @pl.when(i == 0)
    def _():
        run_ref[...] = cnt_ref[...]

    pw = pool_ref.shape[1]
    mix = _dot(pool_ref[...].astype(BF16), wo_ref[0:pw, :]) + _dot(o_ref[...].astype(BF16), wo_ref[pw:, :])
    h = x_ref[...] + mix
    h_ref[...] = h
    hn = _rms(h, g2_ref[...])
    for s in range(hn.shape[1] // LANES):
        hn_ref[pl.ds(s, tm, stride=SUBLANES), :] = hn[:, s * LANES:(s + 1) * LANES]
    logits = _dot(hn.astype(BF16), rw_ref[...]) + rb_ref[...]
    lane = lax.broadcasted_iota(I32, logits.shape, 1)
    vals, idxs = [], []
    for _ in range(TOP_K):
        m = jnp.max(logits, axis=-1, keepdims=True)
        idx = jnp.min(jnp.where(logits == m, lane, n_experts), axis=-1, keepdims=True)
        vals.append(m)
        idxs.append(idx)
        logits = jnp.where(lane == idx, -jnp.inf, logits)
    ex = [jnp.exp(v - vals[0]) for v in vals]
    den = ex[0]
    for e in ex[1:]:
        den = den + e
    chosen = jnp.zeros(logits.shape, F32)
    for idx in idxs:
        chosen = chosen + (lane == idx).astype(F32)
    row = lax.broadcasted_iota(I32, (tm, tm), 0)
    col = lax.broadcasted_iota(I32, (tm, tm), 1)
    before = _dot((col < row).astype(BF16), chosen.astype(BF16)) + run_ref[...]
    for k in range(TOP_K):
        te_ref[:, k:k + 1] = idxs[k]
        gate_ref[:, k:k + 1] = ex[k] / den
        rank_ref[:, k:k + 1] = jnp.sum(jnp.where(lane == idxs[k], before, 0.0), axis=-1, keepdims=True).astype(I32)
    run_ref[...] = run_ref[...] + jnp.sum(chosen, axis=0, keepdims=True)
    cnt_out_ref[...] = run_ref[...]


def _outproj(pool_out, o, x, w_out_b, g2, rw_b, rb, counts, hn_buf, tok_offset, n_total):
    n, d = x.shape
    n_experts = rw_b.shape[1]
    tm = min(ROW_TILE, n)
    assert tok_offset % tm == 0
    off = tok_offset // tm
    sub = d // LANES
    assert sub == SUBLANES
    row = lambda w: pl.BlockSpec((tm, w), lambda i: (i, 0))
    full = lambda a: pl.BlockSpec(a.shape, lambda i: (0,) * a.ndim)
    assert hn_buf.shape == (n_total * sub, LANES)
    in_specs = [row(pool_out.shape[1]), row(o.shape[1]), row(d), full(w_out_b), full(g2), full(rw_b), full(rb),
                full(counts), pl.BlockSpec(memory_space=pl.ANY)]
    args = [pool_out, o, x, w_out_b, g2, rw_b, rb, counts, hn_buf]
    aliases = {len(args) - 1: 1}
    return pl.pallas_call(
        functools.partial(_outproj_kernel, tm=tm, n_experts=n_experts),
        grid=(n // tm,),
        in_specs=in_specs,
        out_specs=[row(d), pl.BlockSpec((tm * sub, LANES), lambda i: (i + off, 0)),
                   row(TOP_K), row(TOP_K), row(TOP_K), full(counts)],
        out_shape=[jax.ShapeDtypeStruct((n, d), F32), jax.ShapeDtypeStruct((n_total * sub, LANES), F32),
                   jax.ShapeDtypeStruct((n, TOP_K), I32), jax.ShapeDtypeStruct((n, TOP_K), F32),
                   jax.ShapeDtypeStruct((n, TOP_K), I32), jax.ShapeDtypeStruct(counts.shape, F32)],
        scratch_shapes=[pltpu.VMEM(counts.shape, F32)],
        input_output_aliases=aliases,
        compiler_params=_cparams("arbitrary"),
        name="outproj_router",
    )(*args)


def _expert_kernel(be_ref, tok_ref, nu_ref, hn_ref, w1g_ref, w1l_ref, b1g_ref, b1l_ref, w2_ref, b2_ref,
                   y_ref, xbuf, sem, *, rows):
    j = pl.program_id(0)
    n_used = nu_ref[0]
    slot = j % 2
    tile = rows * SUBLANES

    def gather(blk, sl):
        def body(r, carry):
            src = pl.multiple_of(tok_ref[blk * rows + r] * SUBLANES, SUBLANES)
            dst = pl.multiple_of(r * SUBLANES, SUBLANES)
            pltpu.make_async_copy(hn_ref.at[pl.ds(src, SUBLANES), :], xbuf.at[sl, pl.ds(dst, SUBLANES), :],
                                  sem.at[sl]).start()
            return carry
        lax.fori_loop(0, rows, body, 0)

    @pl.when(j == 0)
    def _():
        gather(0, 0)

    @pl.when(j + 1 < n_used)
    def _():
        gather(j + 1, 1 - slot)

    @pl.when(j < n_used)
    def _():
        pltpu.make_async_copy(hn_ref.at[pl.ds(0, tile), :], xbuf.at[slot], sem.at[slot]).wait()
        x = jnp.concatenate([xbuf[slot, pl.ds(s, rows, stride=SUBLANES), :] for s in range(SUBLANES)],
                            axis=-1).astype(BF16)
        glu = jnp.minimum(_dot(x, w1g_ref[...]) + b1g_ref[...], SWIGLU_LIMIT)
        lin = jnp.clip(_dot(x, w1l_ref[...]) + b1l_ref[...], -SWIGLU_LIMIT, SWIGLU_LIMIT)
        act = glu * jax.nn.sigmoid(SWIGLU_ALPHA * glu) * (lin + 1.0)
        y = _dot(act.astype(BF16), w2_ref[...]) + b2_ref[...]
        for s in range(SUBLANES):
            y_ref[pl.ds(s, rows, stride=SUBLANES), :] = y[:, s * LANES:(s + 1) * LANES]

    @pl.when(j >= n_used)
    def _():
        y_ref[...] = jnp.zeros_like(y_ref)


def _experts(block_e, row_tok, n_used, hn_buf, w1g, w1l, b1g, b1l, w2b, b2, n_blocks):
    rows = EXPERT_ROWS
    n_exp, d, ff = w1g.shape
    tile = rows * SUBLANES
    wspec = lambda a: pl.BlockSpec((None,) + a.shape[1:], lambda j, be, tk, nu: (be[j], 0, 0))
    grid_spec = pltpu.PrefetchScalarGridSpec(
        num_scalar_prefetch=3,
        grid=(n_blocks,),
        in_specs=[pl.BlockSpec(memory_space=pl.ANY), wspec(w1g), wspec(w1l), wspec(b1g), wspec(b1l),
                  wspec(w2b), wspec(b2)],
        out_specs=pl.BlockSpec((tile, LANES), lambda j, be, tk, nu: (j, 0)),
        scratch_shapes=[pltpu.VMEM((2, tile, LANES), F32), pltpu.SemaphoreType.DMA((2,))],
    )
    return pl.pallas_call(
        functools.partial(_expert_kernel, rows=rows),
        grid_spec=grid_spec,
        out_shape=jax.ShapeDtypeStruct((n_blocks * tile, LANES), F32),
        compiler_params=_cparams("arbitrary"),
        name="experts",
    )(block_e, row_tok, n_used, hn_buf, w1g, w1l, b1g, b1l, w2b, b2)


def _combine_kernel(dest_ref, yb_ref, gate_ref, h_ref, g_ref, y_ref, cbuf, sem, *, tm):
    i = pl.program_id(0)
    n = pl.num_programs(0)
    slot = i % 2
    tile = tm * SUBLANES

    def gather(blk, sl):
        def body(t, carry):
            for k in range(TOP_K):
                src = pl.multiple_of(dest_ref[(blk * tm + t) * TOP_K + k] * SUBLANES, SUBLANES)
                dst = pl.multiple_of((k * tm + t) * SUBLANES, SUBLANES)
                pltpu.make_async_copy(yb_ref.at[pl.ds(src, SUBLANES), :], cbuf.at[sl, pl.ds(dst, SUBLANES), :],
                                      sem.at[sl]).start()
            return carry
        lax.fori_loop(0, tm, body, 0)

    @pl.when(i == 0)
    def _():
        gather(0, 0)

    @pl.when(i + 1 < n)
    def _():
        gather(i + 1, 1 - slot)

    pltpu.make_async_copy(yb_ref.at[pl.ds(0, TOP_K * tile), :], cbuf.at[slot], sem.at[slot]).wait()
    gates = gate_ref[...]
    slabs = []
    for s in range(SUBLANES):
        acc = None
        for k in range(TOP_K):
            part = gates[:, k:k + 1] * cbuf[slot, pl.ds(k * tile + s, tm, stride=SUBLANES), :]
            acc = part if acc is None else acc + part
        slabs.append(acc)
    hf = h_ref[...] + jnp.concatenate(slabs, axis=-1)
    y_ref[...] = _rms(hf, g_ref[...])


def _combine(dest, yb, gates, h, gf):
    n, d = h.shape
    tm = min(COMBINE_TILE, n)
    row = lambda w: pl.BlockSpec((tm, w), lambda i, ds: (i, 0))
    grid_spec = pltpu.PrefetchScalarGridSpec(
        num_scalar_prefetch=1,
        grid=(n // tm,),
        in_specs=[pl.BlockSpec(memory_space=pl.ANY), row(TOP_K), row(d), pl.BlockSpec(gf.shape, lambda i, ds: (0, 0))],
        out_specs=row(d),
        scratch_shapes=[pltpu.VMEM((2, TOP_K * tm * SUBLANES, LANES), F32), pltpu.SemaphoreType.DMA((2,))],
    )
    return pl.pallas_call(
        functools.partial(_combine_kernel, tm=tm),
        grid_spec=grid_spec,
        out_shape=jax.ShapeDtypeStruct((n, d), F32),
        compiler_params=_cparams("arbitrary"),
        name="combine_norm",
    )(dest.reshape(-1), yb, gates, h, gf)


def kernel(x_prompt, x_sample, state_pool, cache_k, cache_v, page_table, norm1_g, w_in, pool_w, pool_scale,
           lambda_q1, lambda_k1, lambda_q2, lambda_k2, subln_g, w_out, norm2_g, router_w, router_b,
           w1, b1, w2, b2, normf_g):
    batch, seq, d = x_prompt.shape
    bs, n_new, _ = x_sample.shape
    depth = w_in.shape[0]
    assert depth == 1, "single-layer trunk"
    n_heads = cache_k.shape[3]
    page_size = cache_k.shape[2]
    pw = pool_w.shape[1] * pool_w.shape[2]
    qk_w = n_heads * HEAD_W
    n_experts = router_w.shape[2]
    np_, ns_ = batch * seq, bs * n_new
    n_total = np_ + ns_
    lam_init = 0.8 - 0.6 * math.exp(-0.3 * 0)

    g1 = norm1_g[0][None]
    w_in_b = w_in[0].astype(BF16)
    pool_w_b = pool_w[0].astype(BF16)
    ps = pool_scale[0][None]
    lams = [a[0][None] for a in (lambda_q1, lambda_k1, lambda_q2, lambda_k2)]
    sg = subln_g[0][None]
    slopes = jnp.exp2(-8.0 * (jnp.arange(n_heads, dtype=F32) + 1.0) / n_heads)
    w_out_b = w_out[0].astype(BF16)
    g2 = norm2_g[0][None]
    rw_b = router_w[0].astype(BF16)
    rb = router_b[0][None]
    w1g = w1[0][:, :, 0::2].astype(BF16)
    w1l = w1[0][:, :, 1::2].astype(BF16)
    b1g = b1[0][:, None, 0::2]
    b1l = b1[0][:, None, 1::2]
    w2b = w2[0].astype(BF16)
    b2e = b2[0][:, None, :]
    gf = normf_g[None]

    xp = x_prompt.reshape(np_, d)
    u_p, q_p, k_p, v_p, kb_p, vb_p = _inproj(xp, g1, w_in_b, pw, qk_w)
    pool_p = _pool_prompt(u_p, pool_w_b, ps, batch, seq)
    o_p = _attn_prompt(q_p, kb_p, vb_p, slopes, lams, sg, batch, seq, n_heads, lam_init)

    xs = x_sample.reshape(ns_, d)
    u_s, q_s, k_s, v_s, _, _ = _inproj(xs, g1, w_in_b, pw, qk_w)
    pool_s, new_pool_s = _pool_sample(state_pool[0], u_s.reshape(bs, n_new, pw), pool_w_b, ps)
    kc = cache_k[0].reshape(cache_k.shape[1], page_size * n_heads, HEAD_W)
    vc = cache_v[0].reshape(cache_v.shape[1], page_size * n_heads, HEAD_W)
    o_s = _attn_sample(q_s.reshape(bs, n_new, qk_w), k_s.reshape(bs, n_new, qk_w), v_s.reshape(bs, n_new, qk_w),
                       kc, vc, page_table, slopes, lams, sg, n_heads, lam_init)

    zero_counts = jnp.zeros((1, n_experts), F32)
    hn_buf = jnp.zeros((n_total * (d // LANES), LANES), F32)
    h_p, hn_buf, te_p, gate_p, rank_p, cnt_p = _outproj(pool_p, o_p, xp, w_out_b, g2, rw_b, rb, zero_counts,
                                                        hn_buf, 0, n_total)
    h_s, hn_buf, te_s, gate_s, rank_s, cnt = _outproj(pool_s.reshape(ns_, pw), o_s.reshape(ns_, qk_w), xs, w_out_b,
                                                      g2, rw_b, rb, cnt_p, hn_buf, np_, n_total)

    counts = cnt[0].astype(I32)
    n_assign = n_total * TOP_K
    n_blocks = -(-n_assign // EXPERT_ROWS) + n_experts
    padded = (counts + EXPERT_ROWS - 1) // EXPERT_ROWS * EXPERT_ROWS
    pad_ends = jnp.cumsum(padded)
    pad_starts = pad_ends - padded
    dest_p = pad_starts[te_p] + rank_p
    dest_s = pad_starts[te_s] + rank_s
    dest_all = jnp.concatenate([dest_p.reshape(-1), dest_s.reshape(-1)])
    tok_all = jnp.arange(n_assign, dtype=I32) // TOP_K
    row_tok = jnp.zeros((n_blocks * EXPERT_ROWS,), I32).at[dest_all].set(tok_all)
    n_used = (pad_ends[-1:] // EXPERT_ROWS).astype(I32)
    block_e = jnp.minimum(jnp.searchsorted(pad_ends, jnp.arange(n_blocks, dtype=I32) * EXPERT_ROWS, side='right'),
                          n_experts - 1).astype(I32)

    yb = _experts(block_e, row_tok, n_used, hn_buf, w1g, w1l, b1g, b1l, w2b, b2e, n_blocks)
    y_p = _combine(dest_p, yb, gate_p, h_p, gf)
    y_s = _combine(dest_s, yb, gate_s, h_s, gf)

    return (y_p.reshape(batch, seq, d), y_s.reshape(bs, n_new, d),
            u_p.reshape(batch, seq, pw)[None, :, seq - state_pool.shape[2]:],
            new_pool_s[None],
            k_p.reshape(1, batch, seq, n_heads, HEAD_W), v_p.reshape(1, batch, seq, n_heads, HEAD_W),
            k_s.reshape(1, bs, n_new, n_heads, HEAD_W), v_s.reshape(1, bs, n_new, n_heads, HEAD_W))
```

```python
import functools
import math

import jax
import jax.numpy as jnp
from jax import lax
from jax.experimental import pallas as pl
from jax.experimental.pallas import tpu as pltpu

F32 = jnp.float32
BF16 = jnp.bfloat16
I32 = jnp.int32

RMS_EPS = 1e-6
NEG_INF = -1e30
HEAD_DIM = 64
HEAD_W = 2 * HEAD_DIM
POOL_WINDOWS = (2, 4, 8, 16)
POOL_HALO = 16
TOP_K = 4
SWIGLU_LIMIT = 7.0
SWIGLU_ALPHA = 1.702
LANES = 128
SUBLANES = 8
ROW_TILE = 512
ATTN_TILE = 256
EXPERT_ROWS = 512
COMBINE_TILE = 256
VMEM_LIMIT = 48 * 1024 * 1024


def _cparams(*sem):
    return pltpu.CompilerParams(dimension_semantics=sem, vmem_limit_bytes=VMEM_LIMIT)


def _dot(a, b):
    return jnp.dot(a, b, preferred_element_type=F32)


def _dot_nt(a, b):
    return lax.dot_general(a, b, (((1,), (1,)), ((), ())), preferred_element_type=F32)


def _rms(x, g):
    return x * lax.rsqrt(jnp.mean(x * x, axis=-1, keepdims=True) + RMS_EPS) * g


def _inproj_kernel(x_ref, g_ref, w_ref, u_ref, q_ref, k_ref, v_ref, kb_ref, vb_ref, *, pool_w, qk_w):
    xb = _rms(x_ref[...], g_ref[...]).astype(BF16)
    c0, c1, c2 = pool_w, pool_w + qk_w, pool_w + 2 * qk_w
    u_ref[...] = _dot(xb, w_ref[:, 0:c0])
    q_ref[...] = _dot(xb, w_ref[:, c0:c1]) * (HEAD_DIM ** -0.5)
    k = _dot(xb, w_ref[:, c1:c2])
    k_ref[...] = k
    kb_ref[...] = k.astype(BF16)
    v = _dot(xb, w_ref[:, c2:])
    v_ref[...] = v
    vb_ref[...] = v.astype(BF16)


def _inproj(x, g1, w_in_b, pool_w, qk_w):
    n, d = x.shape
    in_w = w_in_b.shape[1]
    val_w = in_w - pool_w - 2 * qk_w
    tm = min(ROW_TILE, n)
    row = lambda w: pl.BlockSpec((tm, w), lambda i: (i, 0))
    full = lambda a: pl.BlockSpec(a.shape, lambda i: (0,) * a.ndim)
    return pl.pallas_call(
        functools.partial(_inproj_kernel, pool_w=pool_w, qk_w=qk_w),
        grid=(n // tm,),
        in_specs=[row(d), full(g1), full(w_in_b)],
        out_specs=[row(pool_w), row(qk_w), row(qk_w), row(val_w), row(qk_w), row(val_w)],
        out_shape=[jax.ShapeDtypeStruct((n, pool_w), F32), jax.ShapeDtypeStruct((n, qk_w), F32),
                   jax.ShapeDtypeStruct((n, qk_w), F32), jax.ShapeDtypeStruct((n, val_w), F32),
                   jax.ShapeDtypeStruct((n, qk_w), BF16), jax.ShapeDtypeStruct((n, val_w), BF16)],
        compiler_params=_cparams("parallel"),
        name="inproj",
    )(x, g1, w_in_b)


def _pool_group(s_win, tok, cnt, pw, ps):
    d = s_win / cnt - tok
    return _dot(d.astype(BF16), pw) * ps


def _pool_prompt_kernel(u_ref, pw_ref, ps_ref, o_ref, carry_ref, *, tt):
    j = pl.program_id(1)

    @pl.when(j == 0)
    def _():
        carry_ref[...] = jnp.zeros_like(carry_ref)

    cur = u_ref[...]
    ext = jnp.concatenate([carry_ref[...], cur], axis=0)
    carry_ref[...] = cur[tt - POOL_HALO:, :]
    pos = (j * tt + lax.broadcasted_iota(I32, (tt, 1), 0)).astype(F32)
    gw = cur.shape[1] // len(POOL_WINDOWS)
    for g, w in enumerate(POOL_WINDOWS):
        cols = slice(g * gw, (g + 1) * gw)
        e = ext[:, cols]
        s, span = e, 1
        while span < w:
            s = s + pltpu.roll(s, span, axis=0)
            span *= 2
        cnt = jnp.minimum(float(w), pos + 1.0)
        out = _pool_group(s[POOL_HALO:, :], e[POOL_HALO:, :], cnt, pw_ref[g], ps_ref[:, cols])
        o_ref[:, cols] = out.astype(o_ref.dtype)


def _pool_prompt(u, pool_w_b, pool_scale, batch, seq):
    pw = u.shape[1]
    tt = min(ROW_TILE, seq)
    u3 = u.reshape(batch, seq, pw)
    out = pl.pallas_call(
        functools.partial(_pool_prompt_kernel, tt=tt),
        grid=(batch, seq // tt),
        in_specs=[pl.BlockSpec((None, tt, pw), lambda b, j: (b, j, 0)),
                  pl.BlockSpec(pool_w_b.shape, lambda b, j: (0, 0, 0)),
                  pl.BlockSpec(pool_scale.shape, lambda b, j: (0, 0))],
        out_specs=pl.BlockSpec((None, tt, pw), lambda b, j: (b, j, 0)),
        out_shape=jax.ShapeDtypeStruct((batch, seq, pw), BF16),
        scratch_shapes=[pltpu.VMEM((POOL_HALO, pw), F32)],
        compiler_params=_cparams("arbitrary", "arbitrary"),
        name="pool_prompt",
    )(u3, pool_w_b, pool_scale)
    return out.reshape(batch * seq, pw)


def _pool_sample_kernel(sp_ref, u_ref, pw_ref, ps_ref, o_ref, np_ref, *, n_new, n_buf):
    rows = [sp_ref[:, r, :] for r in range(n_buf)] + [u_ref[:, r, :] for r in range(n_new)]
    for r in range(n_buf):
        np_ref[:, r, :] = rows[n_new + r]
    gw = rows[0].shape[1] // len(POOL_WINDOWS)
    for t in range(n_new):
        i = n_buf + t
        for g, w in enumerate(POOL_WINDOWS):
            cols = slice(g * gw, (g + 1) * gw)
            s = rows[i][:, cols]
            for back in range(1, w):
                s = s + rows[i - back][:, cols]
            out = _pool_group(s, rows[i][:, cols], float(w), pw_ref[g], ps_ref[:, cols])
            o_ref[:, t, cols] = out


def _pool_sample(state_pool, u3, pool_w_b, pool_scale):
    bs, n_buf, pw = state_pool.shape
    n_new = u3.shape[1]
    full = lambda a: pl.BlockSpec(a.shape, lambda i: (0,) * a.ndim)
    return pl.pallas_call(
        functools.partial(_pool_sample_kernel, n_new=n_new, n_buf=n_buf),
        grid=(1,),
        in_specs=[full(state_pool), full(u3), full(pool_w_b), full(pool_scale)],
        out_specs=[pl.BlockSpec((bs, n_new, pw), lambda i: (0, 0, 0)),
                   pl.BlockSpec((bs, n_buf, pw), lambda i: (0, 0, 0))],
        out_shape=[jax.ShapeDtypeStruct((bs, n_new, pw), F32),
                   jax.ShapeDtypeStruct((bs, n_buf, pw), F32)],
        compiler_params=_cparams("arbitrary"),
        name="pool_sample",
    )(state_pool, u3, pool_w_b, pool_scale)


def _lambda_full(lq1, lk1, lq2, lk2, lam_init):
    e1 = jnp.exp(jnp.sum(lq1 * lk1, axis=-1, keepdims=True))
    e2 = jnp.exp(jnp.sum(lq2 * lk2, axis=-1, keepdims=True))
    return e1 - e2 + lam_init


def _split_halves(q):
    lane = lax.broadcasted_iota(I32, q.shape, 1)
    zero = jnp.zeros_like(q)
    return jnp.concatenate([jnp.where(lane < HEAD_DIM, q, zero), jnp.where(lane >= HEAD_DIM, q, zero)], axis=0)


def _subln(o, g, lam_init):
    return _rms(o, g) * (1.0 - lam_init)


def _attn_prompt_kernel(slopes_ref, q_ref, k_ref, v_ref, lq1_ref, lk1_ref, lq2_ref, lk2_ref, g_ref, o_ref,
                        *, tq, lam_init):
    h = pl.program_id(1)
    i = pl.program_id(2)
    slope = slopes_ref[h]
    lam = _lambda_full(lq1_ref[...], lk1_ref[...], lq2_ref[...], lk2_ref[...], lam_init)
    qq = _split_halves(q_ref[...].astype(BF16))
    r = lax.broadcasted_iota(I32, (2 * tq, tq), 0)
    r = jnp.where(r >= tq, r - tq, r)
    c = lax.broadcasted_iota(I32, (2 * tq, tq), 1)
    rel = r - c
    bias = slope * rel.astype(F32)

    def step(j, carry, diag):
        m, l, acc = carry
        start = pl.multiple_of(j * tq, tq)
        kj = k_ref[pl.ds(start, tq), :]
        vj = v_ref[pl.ds(start, tq), :]
        s = _dot_nt(qq, kj) - (bias + slope * ((i - j) * tq).astype(F32))
        if diag:
            s = jnp.where(rel >= 0, s, NEG_INF)
        m_new = jnp.maximum(m, jnp.max(s, axis=-1, keepdims=True))
        alpha = jnp.exp(m - m_new)
        p = jnp.exp(s - m_new)
        l = alpha * l + jnp.sum(p, axis=-1, keepdims=True)
        acc = alpha * acc + _dot(p.astype(BF16), vj)
        return m_new, l, acc

    init = (jnp.full((2 * tq, 1), NEG_INF, F32), jnp.zeros((2 * tq, 1), F32), jnp.zeros((2 * tq, HEAD_W), F32))
    carry = lax.fori_loop(0, i, lambda j, cr: step(j, cr, False), init)
    _, l, acc = step(i, carry, True)
    o = acc / l
    o = o[:tq] - lam * o[tq:]
    o_ref[...] = _subln(o, g_ref[...], lam_init).astype(o_ref.dtype)


def _attn_prompt(q, kb, vb, slopes, lams, subln_g, batch, seq, n_heads, lam_init):
    tq = min(ATTN_TILE, seq)
    q3, k3, v3 = (a.reshape(batch, seq, n_heads * HEAD_W) for a in (q, kb, vb))
    small = lambda a: pl.BlockSpec(a.shape, lambda b, h, i, s: (0,) * a.ndim)
    grid_spec = pltpu.PrefetchScalarGridSpec(
        num_scalar_prefetch=1,
        grid=(batch, n_heads, seq // tq),
        in_specs=[pl.BlockSpec((None, tq, HEAD_W), lambda b, h, i, s: (b, i, h)),
                  pl.BlockSpec((None, seq, HEAD_W), lambda b, h, i, s: (b, 0, h)),
                  pl.BlockSpec((None, seq, HEAD_W), lambda b, h, i, s: (b, 0, h)),
                  *[small(a) for a in lams], small(subln_g)],
        out_specs=pl.BlockSpec((None, tq, HEAD_W), lambda b, h, i, s: (b, i, h)),
    )
    out = pl.pallas_call(
        functools.partial(_attn_prompt_kernel, tq=tq, lam_init=lam_init),
        grid_spec=grid_spec,
        out_shape=jax.ShapeDtypeStruct((batch, seq, n_heads * HEAD_W), BF16),
        compiler_params=_cparams("parallel", "parallel", "arbitrary"),
        name="attn_prompt",
    )(slopes, q3, k3, v3, *lams, subln_g)
    return out.reshape(batch * seq, n_heads * HEAD_W)


def _attn_sample_kernel(pt_ref, slopes_ref, q_ref, kn_ref, vn_ref, kc_ref, vc_ref,
                        lq1_ref, lk1_ref, lq2_ref, lk2_ref, g_ref, o_ref, kbuf, vbuf, sem,
                        *, n_pages, page_rows, n_heads, n_new, lam_init):
    b = pl.program_id(0)
    nb = pl.num_programs(0)
    slot = b % 2
    past = n_pages * page_rows // n_heads

    def page_copies(seq, sl, pg):
        p = pt_ref[seq * n_pages + pg]
        dst = pl.ds(pg * page_rows, page_rows)
        return (pltpu.make_async_copy(kc_ref.at[p], kbuf.at[sl, dst, :], sem.at[sl, 0]),
                pltpu.make_async_copy(vc_ref.at[p], vbuf.at[sl, dst, :], sem.at[sl, 1]))

    def start_seq(seq, sl):
        for pg in range(n_pages):
            for cp in page_copies(seq, sl, pg):
                cp.start()

    @pl.when(b == 0)
    def _():
        start_seq(0, 0)

    @pl.when(b + 1 < nb)
    def _():
        start_seq(b + 1, 1 - slot)

    for pg in range(n_pages):
        for cp in page_copies(b, slot, pg):
            cp.wait()

    lam = _lambda_full(lq1_ref[...], lk1_ref[...], lq2_ref[...], lk2_ref[...], lam_init)
    rows = 2 * SUBLANES
    r = lax.broadcasted_iota(I32, (rows, past), 0)
    t_past = jnp.where(r >= n_new, r - n_new, r)
    dist_past = (past + t_past - lax.broadcasted_iota(I32, (rows, past), 1)).astype(F32)
    r1 = lax.broadcasted_iota(I32, (rows, 1), 0)
    t_new = jnp.where(r1 >= n_new, r1 - n_new, r1)
    for h in range(n_heads):
        slope = slopes_ref[h]
        cols = slice(h * HEAD_W, (h + 1) * HEAD_W)
        q = q_ref[:, cols]
        qq = jnp.concatenate([_split_halves(q), jnp.zeros((rows - 2 * n_new, HEAD_W), F32)], axis=0)
        kh = kbuf[slot, pl.ds(h, past, stride=n_heads), :].astype(BF16)
        vh = vbuf[slot, pl.ds(h, past, stride=n_heads), :].astype(BF16)
        s = _dot_nt(qq.astype(BF16), kh) - slope * dist_past
        kn = kn_ref[:, cols]
        vn = vn_ref[:, cols]
        s_new = []
        for c in range(n_new):
            sc = jnp.sum(qq * kn[c:c + 1, :], axis=-1, keepdims=True) - slope * (t_new - c).astype(F32)
            s_new.append(jnp.where(t_new >= c, sc, NEG_INF))
        m = jnp.max(s, axis=-1, keepdims=True)
        for sc in s_new:
            m = jnp.maximum(m, sc)
        p = jnp.exp(s - m)
        l = jnp.sum(p, axis=-1, keepdims=True)
        acc = _dot(p.astype(BF16), vh)
        for c, sc in enumerate(s_new):
            pc = jnp.exp(sc - m)
            l = l + pc
            acc = acc + pc * vn[c:c + 1, :]
        o = acc / l
        o = o[0:n_new] - lam * o[n_new:2 * n_new]
        o_ref[:, cols] = _subln(o, g_ref[...], lam_init)


def _attn_sample(q3, kn3, vn3, kc, vc, page_table, slopes, lams, subln_g, n_heads, lam_init):
    bs, n_new, qk_w = q3.shape
    n_pages = page_table.shape[1]
    page_rows = kc.shape[1]
    small = lambda a: pl.BlockSpec(a.shape, lambda b, pt, s: (0,) * a.ndim)
    tok = pl.BlockSpec((None, n_new, qk_w), lambda b, pt, s: (b, 0, 0))
    grid_spec = pltpu.PrefetchScalarGridSpec(
        num_scalar_prefetch=2,
        grid=(bs,),
        in_specs=[tok, tok, tok, pl.BlockSpec(memory_space=pl.ANY), pl.BlockSpec(memory_space=pl.ANY),
                  *[small(a) for a in lams], small(subln_g)],
        out_specs=tok,
        scratch_shapes=[pltpu.VMEM((2, n_pages * page_rows, LANES), F32),
                        pltpu.VMEM((2, n_pages * page_rows, LANES), F32),
                        pltpu.SemaphoreType.DMA((2, 2))],
    )
    return pl.pallas_call(
        functools.partial(_attn_sample_kernel, n_pages=n_pages, page_rows=page_rows, n_heads=n_heads,
                          n_new=n_new, lam_init=lam_init),
        grid_spec=grid_spec,
        out_shape=jax.ShapeDtypeStruct((bs, n_new, qk_w), F32),
        compiler_params=_cparams("arbitrary"),
        name="attn_sample",
    )(page_table.reshape(-1), slopes, q3, kn3, vn3, kc, vc, *lams, subln_g)


def _outproj_kernel(pool_ref, o_ref, x_ref, wo_ref, g2_ref, rw_ref, rb_ref, cnt_ref, hn_in_ref,
                    h_ref, hn_ref, te_ref, gate_ref, rank_ref, cnt_out_ref, run_ref, *, tm, n_experts):
    del hn_in_ref
    i = pl.program_id(0)

    @pl.when(i == 0)
    def _():
        run_ref[...] = cnt_ref[...]

    pw = pool_ref.shape[1]
    mix = _dot(pool_ref[...].astype(BF16), wo_ref[0:pw, :]) + _dot(o_ref[...].astype(BF16), wo_ref[pw:, :])
    h = x_ref[...] + mix
    h_ref[...] = h
    hn = _rms(h, g2_ref[...])
    for s in range(hn.shape[1] // LANES):
        hn_ref[pl.ds(s, tm, stride=SUBLANES), :] = hn[:, s * LANES:(s + 1) * LANES]
    logits = _dot(hn.astype(BF16), rw_ref[...]) + rb_ref[...]
    lane = lax.broadcasted_iota(I32, logits.shape, 1)
    vals, idxs = [], []
    for _ in range(TOP_K):
        m = jnp.max(logits, axis=-1, keepdims=True)
        idx = jnp.min(jnp.where(logits == m, lane, n_experts), axis=-1, keepdims=True)
        vals.append(m)
        idxs.append(idx)
        logits = jnp.where(lane == idx, -jnp.inf, logits)
    ex = [jnp.exp(v - vals[0]) for v in vals]
    den = ex[0]
    for e in ex[1:]:
        den = den + e
    chosen = jnp.zeros(logits.shape, F32)
    for idx in idxs:
        chosen = chosen + (lane == idx).astype(F32)
    row = lax.broadcasted_iota(I32, (tm, tm), 0)
    col = lax.broadcasted_iota(I32, (tm, tm), 1)
    before = _dot((col < row).astype(BF16), chosen.astype(BF16)) + run_ref[...]
    for k in range(TOP_K):
        te_ref[:, k:k + 1] = idxs[k]
        gate_ref[:, k:k + 1] = ex[k] / den
        rank_ref[:, k:k + 1] = jnp.sum(jnp.where(lane == idxs[k], before, 0.0), axis=-1, keepdims=True).astype(I32)
    run_ref[...] = run_ref[...] + jnp.sum(chosen, axis=0, keepdims=True)
    cnt_out_ref[...] = run_ref[...]


def _outproj(pool_out, o, x, w_out_b, g2, rw_b, rb, counts, hn_buf, tok_offset, n_total):
    n, d = x.shape
    n_experts = rw_b.shape[1]
    tm = min(ROW_TILE, n)
    assert tok_offset % tm == 0
    off = tok_offset // tm
    sub = d // LANES
    assert sub == SUBLANES
    row = lambda w: pl.BlockSpec((tm, w), lambda i: (i, 0))
    full = lambda a: pl.BlockSpec(a.shape, lambda i: (0,) * a.ndim)
    assert hn_buf.shape == (n_total * sub, LANES)
    in_specs = [row(pool_out.shape[1]), row(o.shape[1]), row(d), full(w_out_b), full(g2), full(rw_b), full(rb),
                full(counts), pl.BlockSpec(memory_space=pl.ANY)]
    args = [pool_out, o, x, w_out_b, g2, rw_b, rb, counts, hn_buf]
    aliases = {len(args) - 1: 1}
    return pl.pallas_call(
        functools.partial(_outproj_kernel, tm=tm, n_experts=n_experts),
        grid=(n // tm,),
        in_specs=in_specs,
        out_specs=[row(d), pl.BlockSpec((tm * sub, LANES), lambda i: (i + off, 0)),
                   row(TOP_K), row(TOP_K), row(TOP_K), full(counts)],
        out_shape=[jax.ShapeDtypeStruct((n, d), F32), jax.ShapeDtypeStruct((n_total * sub, LANES), F32),
                   jax.ShapeDtypeStruct((n, TOP_K), I32), jax.ShapeDtypeStruct((n, TOP_K), F32),
                   jax.ShapeDtypeStruct((n, TOP_K), I32), jax.ShapeDtypeStruct(counts.shape, F32)],
        scratch_shapes=[pltpu.VMEM(counts.shape, F32)],
        input_output_aliases=aliases,
        compiler_params=_cparams("arbitrary"),
        name="outproj_router",
    )(*args)


def _expert_kernel(be_ref, tok_ref, nu_ref, hn_ref, w1g_ref, w1l_ref, b1g_ref, b1l_ref, w2_ref, b2_ref,
                   y_ref, xbuf, sem, *, rows):
    j = pl.program_id(0)
    n_used = nu_ref[0]
    slot = j % 2
    tile = rows * SUBLANES

    def gather(blk, sl):
        def body(r, carry):
            src = pl.multiple_of(tok_ref[blk * rows + r] * SUBLANES, SUBLANES)
            dst = pl.multiple_of(r * SUBLANES, SUBLANES)
            pltpu.make_async_copy(hn_ref.at[pl.ds(src, SUBLANES), :], xbuf.at[sl, pl.ds(dst, SUBLANES), :],
                                  sem.at[sl]).start()
            return carry
        lax.fori_loop(0, rows, body, 0)

    @pl.when(j == 0)
    def _():
        gather(0, 0)

    @pl.when(j + 1 < n_used)
    def _():
        gather(j + 1, 1 - slot)

    @pl.when(j < n_used)
    def _():
        pltpu.make_async_copy(hn_ref.at[pl.ds(0, tile), :], xbuf.at[slot], sem.at[slot]).wait()
        x = jnp.concatenate([xbuf[slot, pl.ds(s, rows, stride=SUBLANES), :] for s in range(SUBLANES)],
                            axis=-1).astype(BF16)
        glu = jnp.minimum(_dot(x, w1g_ref[...]) + b1g_ref[...], SWIGLU_LIMIT)
        lin = jnp.clip(_dot(x, w1l_ref[...]) + b1l_ref[...], -SWIGLU_LIMIT, SWIGLU_LIMIT)
        act = glu * jax.nn.sigmoid(SWIGLU_ALPHA * glu) * (lin + 1.0)
        y = _dot(act.astype(BF16), w2_ref[...]) + b2_ref[...]
        for s in range(SUBLANES):
            y_ref[pl.ds(s, rows, stride=SUBLANES), :] = y[:, s * LANES:(s + 1) * LANES]

    @pl.when(j >= n_used)
    def _():
        y_ref[...] = jnp.zeros_like(y_ref)


def _experts(block_e, row_tok, n_used, hn_buf, w1g, w1l, b1g, b1l, w2b, b2, n_blocks):
    rows = EXPERT_ROWS
    n_exp, d, ff = w1g.shape
    tile = rows * SUBLANES
    wspec = lambda a: pl.BlockSpec((None,) + a.shape[1:], lambda j, be, tk, nu: (be[j], 0, 0))
    grid_spec = pltpu.PrefetchScalarGridSpec(
        num_scalar_prefetch=3,
        grid=(n_blocks,),
        in_specs=[pl.BlockSpec(memory_space=pl.ANY), wspec(w1g), wspec(w1l), wspec(b1g), wspec(b1l),
                  wspec(w2b), wspec(b2)],
        out_specs=pl.BlockSpec((tile, LANES), lambda j, be, tk, nu: (j, 0)),
        scratch_shapes=[pltpu.VMEM((2, tile, LANES), F32), pltpu.SemaphoreType.DMA((2,))],
    )
    return pl.pallas_call(
        functools.partial(_expert_kernel, rows=rows),
        grid_spec=grid_spec,
        out_shape=jax.ShapeDtypeStruct((n_blocks * tile, LANES), F32),
        compiler_params=_cparams("arbitrary"),
        name="experts",
    )(block_e, row_tok, n_used, hn_buf, w1g, w1l, b1g, b1l, w2b, b2)


def _combine_kernel(dest_ref, yb_ref, gate_ref, h_ref, g_ref, y_ref, cbuf, sem, *, tm):
    i = pl.program_id(0)
    n = pl.num_programs(0)
    slot = i % 2
    tile = tm * SUBLANES

    def gather(blk, sl):
        def body(t, carry):
            for k in range(TOP_K):
                src = pl.multiple_of(dest_ref[(blk * tm + t) * TOP_K + k] * SUBLANES, SUBLANES)
                dst = pl.multiple_of((k * tm + t) * SUBLANES, SUBLANES)
                pltpu.make_async_copy(yb_ref.at[pl.ds(src, SUBLANES), :], cbuf.at[sl, pl.ds(dst, SUBLANES), :],
                                      sem.at[sl]).start()
            return carry
        lax.fori_loop(0, tm, body, 0)

    @pl.when(i == 0)
    def _():
        gather(0, 0)

    @pl.when(i + 1 < n)
    def _():
        gather(i + 1, 1 - slot)

    pltpu.make_async_copy(yb_ref.at[pl.ds(0, TOP_K * tile), :], cbuf.at[slot], sem.at[slot]).wait()
    gates = gate_ref[...]
    slabs = []
    for s in range(SUBLANES):
        acc = None
        for k in range(TOP_K):
            part = gates[:, k:k + 1] * cbuf[slot, pl.ds(k * tile + s, tm, stride=SUBLANES), :]
            acc = part if acc is None else acc + part
        slabs.append(acc)
    hf = h_ref[...] + jnp.concatenate(slabs, axis=-1)
    y_ref[...] = _rms(hf, g_ref[...])


def _combine(dest, yb, gates, h, gf):
    n, d = h.shape
    tm = min(COMBINE_TILE, n)
    row = lambda w: pl.BlockSpec((tm, w), lambda i, ds: (i, 0))
    grid_spec = pltpu.PrefetchScalarGridSpec(
        num_scalar_prefetch=1,
        grid=(n // tm,),
        in_specs=[pl.BlockSpec(memory_space=pl.ANY), row(TOP_K), row(d), pl.BlockSpec(gf.shape, lambda i, ds: (0, 0))],
        out_specs=row(d),
        scratch_shapes=[pltpu.VMEM((2, TOP_K * tm * SUBLANES, LANES), F32), pltpu.SemaphoreType.DMA((2,))],
    )
    return pl.pallas_call(
        functools.partial(_combine_kernel, tm=tm),
        grid_spec=grid_spec,
        out_shape=jax.ShapeDtypeStruct((n, d), F32),
        compiler_params=_cparams("arbitrary"),
        name="combine_norm",
    )(dest.reshape(-1), yb, gates, h, gf)


def kernel(x_prompt, x_sample, state_pool, cache_k, cache_v, page_table, norm1_g, w_in, pool_w, pool_scale,
           lambda_q1, lambda_k1, lambda_q2, lambda_k2, subln_g, w_out, norm2_g, router_w, router_b,
           w1, b1, w2, b2, normf_g):
    batch, seq, d = x_prompt.shape
    bs, n_new, _ = x_sample.shape
    depth = w_in.shape[0]
    assert depth == 1, "single-layer trunk"
    n_heads = cache_k.shape[3]
    page_size = cache_k.shape[2]
    pw = pool_w.shape[1] * pool_w.shape[2]
    qk_w = n_heads * HEAD_W
    n_experts = router_w.shape[2]
    np_, ns_ = batch * seq, bs * n_new
    n_total = np_ + ns_
    lam_init = 0.8 - 0.6 * math.exp(-0.3 * 0)

    g1 = norm1_g[0][None]
    w_in_b = w_in[0].astype(BF16)
    pool_w_b = pool_w[0].astype(BF16)
    ps = pool_scale[0][None]
    lams = [a[0][None] for a in (lambda_q1, lambda_k1, lambda_q2, lambda_k2)]
    sg = subln_g[0][None]
    slopes = jnp.exp2(-8.0 * (jnp.arange(n_heads, dtype=F32) + 1.0) / n_heads)
    w_out_b = w_out[0].astype(BF16)
    g2 = norm2_g[0][None]
    rw_b = router_w[0].astype(BF16)
    rb = router_b[0][None]
    w1g = w1[0][:, :, 0::2].astype(BF16)
    w1l = w1[0][:, :, 1::2].astype(BF16)
    b1g = b1[0][:, None, 0::2]
    b1l = b1[0][:, None, 1::2]
    w2b = w2[0].astype(BF16)
    b2e = b2[0][:, None, :]
    gf = normf_g[None]

    xp = x_prompt.reshape(np_, d)
    u_p, q_p, k_p, v_p, kb_p, vb_p = _inproj(xp, g1, w_in_b, pw, qk_w)
    pool_p = _pool_prompt(u_p, pool_w_b, ps, batch, seq)
    o_p = _attn_prompt(q_p, kb_p, vb_p, slopes, lams, sg, batch, seq, n_heads, lam_init)

    xs = x_sample.reshape(ns_, d)
    u_s, q_s, k_s, v_s, _, _ = _inproj(xs, g1, w_in_b, pw, qk_w)
    pool_s, new_pool_s = _pool_sample(state_pool[0], u_s.reshape(bs, n_new, pw), pool_w_b, ps)
    kc = cache_k.reshape(cache_k.shape[1], page_size * n_heads, HEAD_W)
    vc = cache_v.reshape(cache_v.shape[1], page_size * n_heads, HEAD_W)
    o_s = _attn_sample(q_s.reshape(bs, n_new, qk_w), k_s.reshape(bs, n_new, qk_w), v_s.reshape(bs, n_new, qk_w),
                       kc, vc, page_table, slopes, lams, sg, n_heads, lam_init)

    zero_counts = jnp.zeros((1, n_experts), F32)
    hn_buf = jnp.zeros((n_total * (d // LANES), LANES), F32)
    h_p, hn_buf, te_p, gate_p, rank_p, cnt_p = _outproj(pool_p, o_p, xp, w_out_b, g2, rw_b, rb, zero_counts,
                                                        hn_buf, 0, n_total)
    h_s, hn_buf, te_s, gate_s, rank_s, cnt = _outproj(pool_s.reshape(ns_, pw), o_s.reshape(ns_, qk_w), xs, w_out_b,
                                                      g2, rw_b, rb, cnt_p, hn_buf, np_, n_total)

    counts = cnt[0].astype(I32)
    n_assign = n_total * TOP_K
    n_blocks = -(-n_assign // EXPERT_ROWS) + n_experts
    padded = (counts + EXPERT_ROWS - 1) // EXPERT_ROWS * EXPERT_ROWS
    pad_ends = jnp.cumsum(padded)
    pad_starts = pad_ends - padded
    dest_p = pad_starts[te_p] + rank_p
    dest_s = pad_starts[te_s] + rank_s
    dest_all = jnp.concatenate([dest_p.reshape(-1), dest_s.reshape(-1)])
    tok_all = jnp.arange(n_assign, dtype=I32) // TOP_K
    row_tok = jnp.zeros((n_blocks * EXPERT_ROWS,), I32).at[dest_all].set(tok_all)
    n_used = (pad_ends[-1:] // EXPERT_ROWS).astype(I32)
    block_start = jnp.arange(n_blocks, dtype=I32) * EXPERT_ROWS
    block_e = jnp.minimum(jnp.sum((pad_ends[None, :] <= block_start[:, None]).astype(I32), axis=1), n_experts - 1)

    yb = _experts(block_e, row_tok, n_used, hn_buf, w1g, w1l, b1g, b1l, w2b, b2e, n_blocks)
    y_p = _combine(dest_p, yb, gate_p, h_p, gf)
    y_s = _combine(dest_s, yb, gate_s, h_s, gf)

    return (y_p.reshape(batch, seq, d), y_s.reshape(bs, n_new, d),
            u_p.reshape(batch, seq, pw)[None, :, seq - state_pool.shape[2]:],
            new_pool_s[None],
            k_p.reshape(1, batch, seq, n_heads, HEAD_W), v_p.reshape(1, batch, seq, n_heads, HEAD_W),
            k_s.reshape(1, bs, n_new, n_heads, HEAD_W), v_s.reshape(1, bs, n_new, n_heads, HEAD_W))
```

```python
import functools
import math

import jax
import jax.numpy as jnp
from jax import lax
from jax.experimental import pallas as pl
from jax.experimental.pallas import tpu as pltpu

F32 = jnp.float32
BF16 = jnp.bfloat16
I32 = jnp.int32

RMS_EPS = 1e-6
NEG_INF = -1e30
HEAD_DIM = 64
HEAD_W = 2 * HEAD_DIM
POOL_WINDOWS = (2, 4, 8, 16)
POOL_HALO = 16
TOP_K = 4
SWIGLU_LIMIT = 7.0
SWIGLU_ALPHA = 1.702
LANES = 128
SUBLANES = 8
ROW_TILE = 512
ATTN_TILE = 512
EXPERT_ROWS = 512
COMBINE_TILE = 256
VMEM_LIMIT = 48 * 1024 * 1024


def _cparams(*sem):
    return pltpu.CompilerParams(dimension_semantics=sem, vmem_limit_bytes=VMEM_LIMIT)


def _dot(a, b):
    return jnp.dot(a, b, preferred_element_type=F32)


def _dot_nt(a, b):
    return lax.dot_general(a, b, (((1,), (1,)), ((), ())), preferred_element_type=F32)


def _rms(x, g):
    return x * lax.rsqrt(jnp.mean(x * x, axis=-1, keepdims=True) + RMS_EPS) * g


def _inproj_kernel(x_ref, g_ref, w_ref, wvt_ref, u_ref, q_ref, k_ref, v_ref, *attn_refs, pool_w, qk_w, key_tile):
    xb = _rms(x_ref[...], g_ref[...]).astype(BF16)
    c0, c1, c2 = pool_w, pool_w + qk_w, pool_w + 2 * qk_w
    u_ref[...] = _dot(xb, w_ref[:, 0:c0])
    q_ref[...] = _dot(xb, w_ref[:, c0:c1]) * (HEAD_DIM ** -0.5)
    k = _dot(xb, w_ref[:, c1:c2])
    k_ref[...] = k
    v_ref[...] = _dot(xb, w_ref[:, c2:])
    if attn_refs:
        kb_ref, vt_ref = attn_refs
        kb_ref[...] = k.astype(BF16)
        vt = _dot_nt(wvt_ref[...], xb)
        for c in range(vt_ref.shape[0]):
            vt_ref[c] = vt[:, c * key_tile:(c + 1) * key_tile].astype(BF16)


def _inproj(x, g1, w_in_b, wvt_b, pool_w, qk_w, key_tile=None):
    n, d = x.shape
    val_w = wvt_b.shape[0]
    tm = min(ROW_TILE, n)
    row = lambda w: pl.BlockSpec((tm, w), lambda i: (i, 0))
    full = lambda a: pl.BlockSpec(a.shape, lambda i: (0,) * a.ndim)
    out_specs = [row(pool_w), row(qk_w), row(qk_w), row(val_w)]
    out_shape = [jax.ShapeDtypeStruct((n, pool_w), F32), jax.ShapeDtypeStruct((n, qk_w), F32),
                 jax.ShapeDtypeStruct((n, qk_w), F32), jax.ShapeDtypeStruct((n, val_w), F32)]
    if key_tile is not None:
        assert tm % key_tile == 0
        out_specs += [row(qk_w), pl.BlockSpec((tm // key_tile, val_w, key_tile), lambda i: (i, 0, 0))]
        out_shape += [jax.ShapeDtypeStruct((n, qk_w), BF16), jax.ShapeDtypeStruct((n // key_tile, val_w, key_tile), BF16)]
    return pl.pallas_call(
        functools.partial(_inproj_kernel, pool_w=pool_w, qk_w=qk_w, key_tile=key_tile),
        grid=(n // tm,),
        in_specs=[row(d), full(g1), full(w_in_b), full(wvt_b)],
        out_specs=out_specs,
        out_shape=out_shape,
        compiler_params=_cparams("parallel"),
        name="inproj",
    )(x, g1, w_in_b, wvt_b)


def _pool_group(s_win, tok, cnt, pw, ps):
    d = s_win / cnt - tok
    return _dot(d.astype(BF16), pw) * ps


def _pool_prompt_kernel(u_ref, pw_ref, ps_ref, o_ref, carry_ref, *, tt):
    j = pl.program_id(1)

    @pl.when(j == 0)
    def _():
        carry_ref[...] = jnp.zeros_like(carry_ref)

    cur = u_ref[...]
    ext = jnp.concatenate([carry_ref[...], cur], axis=0)
    carry_ref[...] = cur[tt - POOL_HALO:, :]
    pos = (j * tt + lax.broadcasted_iota(I32, (tt, 1), 0)).astype(F32)
    gw = cur.shape[1] // len(POOL_WINDOWS)
    for g, w in enumerate(POOL_WINDOWS):
        cols = slice(g * gw, (g + 1) * gw)
        e = ext[:, cols]
        s, span = e, 1
        while span < w:
            s = s + pltpu.roll(s, span, axis=0)
            span *= 2
        cnt = jnp.minimum(float(w), pos + 1.0)
        out = _pool_group(s[POOL_HALO:, :], e[POOL_HALO:, :], cnt, pw_ref[g], ps_ref[:, cols])
        o_ref[:, cols] = out.astype(o_ref.dtype)


def _pool_prompt(u, pool_w_b, pool_scale, batch, seq):
    pw = u.shape[1]
    tt = min(ROW_TILE, seq)
    u3 = u.reshape(batch, seq, pw)
    out = pl.pallas_call(
        functools.partial(_pool_prompt_kernel, tt=tt),
        grid=(batch, seq // tt),
        in_specs=[pl.BlockSpec((None, tt, pw), lambda b, j: (b, j, 0)),
                  pl.BlockSpec(pool_w_b.shape, lambda b, j: (0, 0, 0)),
                  pl.BlockSpec(pool_scale.shape, lambda b, j: (0, 0))],
        out_specs=pl.BlockSpec((None, tt, pw), lambda b, j: (b, j, 0)),
        out_shape=jax.ShapeDtypeStruct((batch, seq, pw), BF16),
        scratch_shapes=[pltpu.VMEM((POOL_HALO, pw), F32)],
        compiler_params=_cparams("arbitrary", "arbitrary"),
        name="pool_prompt",
    )(u3, pool_w_b, pool_scale)
    return out.reshape(batch * seq, pw)


def _pool_sample_kernel(sp_ref, u_ref, pw_ref, ps_ref, o_ref, np_ref, *, n_new, n_buf):
    rows = [sp_ref[:, r, :] for r in range(n_buf)] + [u_ref[:, r, :] for r in range(n_new)]
    for r in range(n_buf):
        np_ref[:, r, :] = rows[n_new + r]
    gw = rows[0].shape[1] // len(POOL_WINDOWS)
    for t in range(n_new):
        i = n_buf + t
        for g, w in enumerate(POOL_WINDOWS):
            cols = slice(g * gw, (g + 1) * gw)
            s = rows[i][:, cols]
            for back in range(1, w):
                s = s + rows[i - back][:, cols]
            out = _pool_group(s, rows[i][:, cols], float(w), pw_ref[g], ps_ref[:, cols])
            o_ref[:, t, cols] = out


def _pool_sample(state_pool, u3, pool_w_b, pool_scale):
    bs, n_buf, pw = state_pool.shape
    n_new = u3.shape[1]
    full = lambda a: pl.BlockSpec(a.shape, lambda i: (0,) * a.ndim)
    return pl.pallas_call(
        functools.partial(_pool_sample_kernel, n_new=n_new, n_buf=n_buf),
        grid=(1,),
        in_specs=[full(state_pool), full(u3), full(pool_w_b), full(pool_scale)],
        out_specs=[pl.BlockSpec((bs, n_new, pw), lambda i: (0, 0, 0)),
                   pl.BlockSpec((bs, n_buf, pw), lambda i: (0, 0, 0))],
        out_shape=[jax.ShapeDtypeStruct((bs, n_new, pw), F32),
                   jax.ShapeDtypeStruct((bs, n_buf, pw), F32)],
        compiler_params=_cparams("arbitrary"),
        name="pool_sample",
    )(state_pool, u3, pool_w_b, pool_scale)


def _lambda_full(lq1, lk1, lq2, lk2, lam_init):
    e1 = jnp.exp(jnp.sum(lq1 * lk1, axis=-1, keepdims=True))
    e2 = jnp.exp(jnp.sum(lq2 * lk2, axis=-1, keepdims=True))
    return e1 - e2 + lam_init


def _split_halves(q):
    lane = lax.broadcasted_iota(I32, q.shape, 1)
    zero = jnp.zeros_like(q)
    return jnp.concatenate([jnp.where(lane < HEAD_DIM, q, zero), jnp.where(lane >= HEAD_DIM, q, zero)], axis=0)


def _subln(o, g, lam_init):
    return _rms(o, g) * (1.0 - lam_init)


def _attn_prompt_kernel(slopes_ref, q_ref, k_ref, vt_ref, lq1_ref, lk1_ref, lq2_ref, lk2_ref, g_ref, o_ref,
                        *, tq, lam_init):
    h = pl.program_id(1)
    i = pl.program_id(2)
    slope = slopes_ref[h]
    lam = _lambda_full(lq1_ref[...], lk1_ref[...], lq2_ref[...], lk2_ref[...], lam_init)
    qq = _split_halves(q_ref[...].astype(BF16))
    n_rep = 2 * tq // LANES
    key_row = lax.broadcasted_iota(I32, (tq, LANES), 0)

    def step(j, carry, diag):
        m, l, acc = carry
        start = pl.multiple_of(j * tq, tq)
        st = _dot_nt(k_ref[pl.ds(start, tq), :], qq)
        bias = slope * (key_row + (j - i) * tq).astype(F32)
        st = st + jnp.concatenate([bias] * n_rep, axis=1)
        if diag:
            kr = lax.broadcasted_iota(I32, st.shape, 0)
            qc = lax.broadcasted_iota(I32, st.shape, 1)
            qc = jnp.where(qc >= tq, qc - tq, qc)
            st = jnp.where(qc >= kr, st, NEG_INF)
        m_new = jnp.maximum(m, jnp.max(st, axis=0, keepdims=True))
        alpha = jnp.exp(m - m_new)
        p = jnp.exp(st - m_new)
        l = alpha * l + jnp.sum(p, axis=0, keepdims=True)
        acc = alpha * acc + _dot(vt_ref[j], p.astype(BF16))
        return m_new, l, acc

    init = (jnp.full((1, 2 * tq), NEG_INF, F32), jnp.zeros((1, 2 * tq), F32), jnp.zeros((HEAD_W, 2 * tq), F32))
    carry = lax.fori_loop(0, i, lambda j, cr: step(j, cr, False), init)
    _, l, acc = step(i, carry, True)
    o = acc / l
    o = (o[:, :tq] - lam * o[:, tq:]).T
    o_ref[...] = _subln(o, g_ref[...], lam_init).astype(o_ref.dtype)


def _attn_prompt(q, kb, vt, slopes, lams, subln_g, batch, seq, n_heads, lam_init, tq):
    q3, k3 = (a.reshape(batch, seq, n_heads * HEAD_W) for a in (q, kb))
    small = lambda a: pl.BlockSpec(a.shape, lambda b, h, i, s: (0,) * a.ndim)
    grid_spec = pltpu.PrefetchScalarGridSpec(
        num_scalar_prefetch=1,
        grid=(batch, n_heads, seq // tq),
        in_specs=[pl.BlockSpec((None, tq, HEAD_W), lambda b, h, i, s: (b, i, h)),
                  pl.BlockSpec((None, seq, HEAD_W), lambda b, h, i, s: (b, 0, h)),
                  pl.BlockSpec((seq // tq, HEAD_W, tq), lambda b, h, i, s: (b, h, 0)),
                  *[small(a) for a in lams], small(subln_g)],
        out_specs=pl.BlockSpec((None, tq, HEAD_W), lambda b, h, i, s: (b, i, h)),
    )
    out = pl.pallas_call(
        functools.partial(_attn_prompt_kernel, tq=tq, lam_init=lam_init),
        grid_spec=grid_spec,
        out_shape=jax.ShapeDtypeStruct((batch, seq, n_heads * HEAD_W), BF16),
        compiler_params=_cparams("parallel", "parallel", "arbitrary"),
        name="attn_prompt",
    )(slopes, q3, k3, vt, *lams, subln_g)
    return out.reshape(batch * seq, n_heads * HEAD_W)


def _attn_sample_kernel(pt_ref, slopes_ref, q_ref, kn_ref, vn_ref, kc_ref, vc_ref,
                        lq1_ref, lk1_ref, lq2_ref, lk2_ref, g_ref, o_ref, kbuf, vbuf, sem,
                        *, n_pages, page_rows, n_heads, n_new, lam_init):
    b = pl.program_id(0)
    nb = pl.num_programs(0)
    slot = b % 2
    past = n_pages * page_rows // n_heads

    def page_copies(seq, sl, pg):
        p = pt_ref[seq * n_pages + pg]
        dst = pl.ds(pg * page_rows, page_rows)
        return (pltpu.make_async_copy(kc_ref.at[p], kbuf.at[sl, dst, :], sem.at[sl, 0]),
                pltpu.make_async_copy(vc_ref.at[p], vbuf.at[sl, dst, :], sem.at[sl, 1]))

    def start_seq(seq, sl):
        for pg in range(n_pages):
            for cp in page_copies(seq, sl, pg):
                cp.start()

    @pl.when(b == 0)
    def _():
        start_seq(0, 0)

    @pl.when(b + 1 < nb)
    def _():
        start_seq(b + 1, 1 - slot)

    for pg in range(n_pages):
        for cp in page_copies(b, slot, pg):
            cp.wait()

    lam = _lambda_full(lq1_ref[...], lk1_ref[...], lq2_ref[...], lk2_ref[...], lam_init)
    rows = 2 * SUBLANES
    r = lax.broadcasted_iota(I32, (rows, past), 0)
    t_past = jnp.where(r >= n_new, r - n_new, r)
    dist_past = (past + t_past - lax.broadcasted_iota(I32, (rows, past), 1)).astype(F32)
    r1 = lax.broadcasted_iota(I32, (rows, 1), 0)
    t_new = jnp.where(r1 >= n_new, r1 - n_new, r1)
    for h in range(n_heads):
        slope = slopes_ref[h]
        cols = slice(h * HEAD_W, (h + 1) * HEAD_W)
        q = q_ref[:, cols]
        qq = jnp.concatenate([_split_halves(q), jnp.zeros((rows - 2 * n_new, HEAD_W), F32)], axis=0)
        kh = kbuf[slot, pl.ds(h, past, stride=n_heads), :].astype(BF16)
        vh = vbuf[slot, pl.ds(h, past, stride=n_heads), :].astype(BF16)
        s = _dot_nt(qq.astype(BF16), kh) - slope * dist_past
        kn = kn_ref[:, cols]
        vn = vn_ref[:, cols]
        s_new = []
        for c in range(n_new):
            sc = jnp.sum(qq * kn[c:c + 1, :], axis=-1, keepdims=True) - slope * (t_new - c).astype(F32)
            s_new.append(jnp.where(t_new >= c, sc, NEG_INF))
        m = jnp.max(s, axis=-1, keepdims=True)
        for sc in s_new:
            m = jnp.maximum(m, sc)
        p = jnp.exp(s - m)
        l = jnp.sum(p, axis=-1, keepdims=True)
        acc = _dot(p.astype(BF16), vh)
        for c, sc in enumerate(s_new):
            pc = jnp.exp(sc - m)
            l = l + pc
            acc = acc + pc * vn[c:c + 1, :]
        o = acc / l
        o = o[0:n_new] - lam * o[n_new:2 * n_new]
        o_ref[:, cols] = _subln(o, g_ref[...], lam_init)


def _attn_sample(q3, kn3, vn3, kc, vc, page_table, slopes, lams, subln_g, n_heads, lam_init):
    bs, n_new, qk_w = q3.shape
    n_pages = page_table.shape[1]
    page_rows = kc.shape[1]
    small = lambda a: pl.BlockSpec(a.shape, lambda b, pt, s: (0,) * a.ndim)
    tok = pl.BlockSpec((None, n_new, qk_w), lambda b, pt, s: (b, 0, 0))
    grid_spec = pltpu.PrefetchScalarGridSpec(
        num_scalar_prefetch=2,
        grid=(bs,),
        in_specs=[tok, tok, tok, pl.BlockSpec(memory_space=pl.ANY), pl.BlockSpec(memory_space=pl.ANY),
                  *[small(a) for a in lams], small(subln_g)],
        out_specs=tok,
        scratch_shapes=[pltpu.VMEM((2, n_pages * page_rows, LANES), F32),
                        pltpu.VMEM((2, n_pages * page_rows, LANES), F32),
                        pltpu.SemaphoreType.DMA((2, 2))],
    )
    return pl.pallas_call(
        functools.partial(_attn_sample_kernel, n_pages=n_pages, page_rows=page_rows, n_heads=n_heads,
                          n_new=n_new, lam_init=lam_init),
        grid_spec=grid_spec,
        out_shape=jax.ShapeDtypeStruct((bs, n_new, qk_w), F32),
        compiler_params=_cparams("arbitrary"),
        name="attn_sample",
    )(page_table.reshape(-1), slopes, q3, kn3, vn3, kc, vc, *lams, subln_g)


def _outproj_kernel(pool_p_ref, o_p_ref, x_p_ref, pool_s_ref, o_s_ref, x_s_ref, wo_ref, g2_ref, rw_ref, rb_ref,
                    h_ref, hn_ref, te_ref, gate_ref, rank_ref, cnt_out_ref, run_ref, *, tm, n_experts, prompt_tiles):
    i = pl.program_id(0)

    @pl.when(i == 0)
    def _():
        run_ref[...] = jnp.zeros_like(run_ref)

    tile = functools.partial(_outproj_tile, wo_ref=wo_ref, g2_ref=g2_ref, rw_ref=rw_ref, rb_ref=rb_ref, h_ref=h_ref,
                             hn_ref=hn_ref, te_ref=te_ref, gate_ref=gate_ref, rank_ref=rank_ref,
                             cnt_out_ref=cnt_out_ref, run_ref=run_ref, tm=tm, n_experts=n_experts)
    pl.when(i < prompt_tiles)(functools.partial(tile, pool_p_ref, o_p_ref, x_p_ref))
    pl.when(i >= prompt_tiles)(functools.partial(tile, pool_s_ref, o_s_ref, x_s_ref))


def _outproj_tile(pool_ref, o_ref, x_ref, *, wo_ref, g2_ref, rw_ref, rb_ref, h_ref, hn_ref, te_ref, gate_ref, rank_ref,
                  cnt_out_ref, run_ref, tm, n_experts):
    pw = pool_ref.shape[1]
    mix = _dot(pool_ref[...].astype(BF16), wo_ref[0:pw, :]) + _dot(o_ref[...].astype(BF16), wo_ref[pw:, :])
    h = x_ref[...] + mix
    h_ref[...] = h
    hn = _rms(h, g2_ref[...])
    for s in range(hn.shape[1] // LANES):
        hn_ref[pl.ds(s, tm, stride=SUBLANES), :] = hn[:, s * LANES:(s + 1) * LANES]
    logits = _dot(hn.astype(BF16), rw_ref[...]) + rb_ref[...]
    lane = lax.broadcasted_iota(I32, logits.shape, 1)
    vals, idxs = [], []
    for _ in range(TOP_K):
        m = jnp.max(logits, axis=-1, keepdims=True)
        idx = jnp.min(jnp.where(logits == m, lane, n_experts), axis=-1, keepdims=True)
        vals.append(m)
        idxs.append(idx)
        logits = jnp.where(lane == idx, -jnp.inf, logits)
    ex = [jnp.exp(v - vals[0]) for v in vals]
    den = ex[0]
    for e in ex[1:]:
        den = den + e
    chosen = jnp.zeros(logits.shape, F32)
    for idx in idxs:
        chosen = chosen + (lane == idx).astype(F32)
    row = lax.broadcasted_iota(I32, (tm, tm), 0)
    col = lax.broadcasted_iota(I32, (tm, tm), 1)
    before = _dot((col < row).astype(BF16), chosen.astype(BF16)) + run_ref[...]
    for k in range(TOP_K):
        te_ref[:, k:k + 1] = idxs[k]
        gate_ref[:, k:k + 1] = ex[k] / den
        rank_ref[:, k:k + 1] = jnp.sum(jnp.where(lane == idxs[k], before, 0.0), axis=-1, keepdims=True).astype(I32)
    run_ref[...] = run_ref[...] + jnp.sum(chosen, axis=0, keepdims=True)
    cnt_out_ref[...] = run_ref[...]


def _outproj(prompt, sample, w_out_b, g2, rw_b, rb):
    np_, d = prompt[2].shape
    ns_ = sample[2].shape[0]
    n_total = np_ + ns_
    n_experts = rw_b.shape[1]
    tm = min(ROW_TILE, math.gcd(np_, ns_))
    prompt_tiles = np_ // tm
    assert d == SUBLANES * LANES
    first = lambda a: pl.BlockSpec((tm, a.shape[1]), lambda i: (jnp.minimum(i, prompt_tiles - 1), 0))
    second = lambda a: pl.BlockSpec((tm, a.shape[1]), lambda i: (jnp.maximum(i - prompt_tiles, 0), 0))
    row = lambda w: pl.BlockSpec((tm, w), lambda i: (i, 0))
    full = lambda a: pl.BlockSpec(a.shape, lambda i: (0,) * a.ndim)
    counts = jax.ShapeDtypeStruct((1, n_experts), F32)
    return pl.pallas_call(
        functools.partial(_outproj_kernel, tm=tm, n_experts=n_experts, prompt_tiles=prompt_tiles),
        grid=(n_total // tm,),
        in_specs=[*[first(a) for a in prompt], *[second(a) for a in sample],
                  full(w_out_b), full(g2), full(rw_b), full(rb)],
        out_specs=[row(d), pl.BlockSpec((tm * SUBLANES, LANES), lambda i: (i, 0)),
                   row(TOP_K), row(TOP_K), row(TOP_K), pl.BlockSpec(counts.shape, lambda i: (0, 0))],
        out_shape=[jax.ShapeDtypeStruct((n_total, d), F32), jax.ShapeDtypeStruct((n_total * SUBLANES, LANES), F32),
                   jax.ShapeDtypeStruct((n_total, TOP_K), I32), jax.ShapeDtypeStruct((n_total, TOP_K), F32),
                   jax.ShapeDtypeStruct((n_total, TOP_K), I32), counts],
        scratch_shapes=[pltpu.VMEM(counts.shape, F32)],
        compiler_params=_cparams("arbitrary"),
        name="outproj_router",
    )(*prompt, *sample, w_out_b, g2, rw_b, rb)


def _dispatch_kernel(dest_ref, zlo_ref, zhi_ref, nu_ref, hn_ref, xs_ref, zero_ref, sem, *, tm, rows, n_experts,
                     n_blocks):
    i = pl.program_id(0)
    tile = tm * SUBLANES
    block = rows * SUBLANES

    def pad_pieces(e):
        lo, n = zlo_ref[e], zhi_ref[e] - zlo_ref[e]
        for bit in range(rows.bit_length() - 1):
            size = 1 << bit
            below = n & (size - 1)
            start = pl.multiple_of((lo + below) * SUBLANES, SUBLANES)
            cp = pltpu.make_async_copy(zero_ref.at[pl.ds(0, size * SUBLANES), :],
                                       xs_ref.at[pl.ds(start, size * SUBLANES), :], sem.at[1])
            yield (n & size) != 0, cp

    def tail_copy(blk):
        start = pl.multiple_of(blk * block, block)
        return pltpu.make_async_copy(zero_ref, xs_ref.at[pl.ds(start, block), :], sem.at[1])

    def for_zero_copies(fn):
        def per_expert(e, carry):
            for used, cp in pad_pieces(e):
                pl.when(used)(functools.partial(fn, cp))
            return carry
        lax.fori_loop(0, n_experts, per_expert, 0)

        def per_block(blk, carry):
            fn(tail_copy(blk))
            return carry
        lax.fori_loop(nu_ref[0], n_blocks, per_block, 0)

    @pl.when(i == 0)
    def _():
        zero_ref[...] = jnp.zeros_like(zero_ref)
        for_zero_copies(lambda cp: cp.start())

    def body(t, carry):
        src = pl.multiple_of(t * SUBLANES, SUBLANES)
        for k in range(TOP_K):
            dst = pl.multiple_of(dest_ref[(i * tm + t) * TOP_K + k] * SUBLANES, SUBLANES)
            pltpu.make_async_copy(hn_ref.at[pl.ds(src, SUBLANES), :], xs_ref.at[pl.ds(dst, SUBLANES), :],
                                  sem.at[0]).start()
        return carry
    lax.fori_loop(0, tm, body, 0)

    for _ in range(TOP_K):
        pltpu.make_async_copy(hn_ref, xs_ref.at[pl.ds(0, tile), :], sem.at[0]).wait()

    @pl.when(i == pl.num_programs(0) - 1)
    def _():
        for_zero_copies(lambda cp: cp.wait())


def _dispatch(dest, zlo, zhi, n_used, hn, n_blocks):
    n_total = hn.shape[0] // SUBLANES
    tm = math.gcd(n_total, ROW_TILE)
    rows = EXPERT_ROWS
    assert rows & (rows - 1) == 0
    grid_spec = pltpu.PrefetchScalarGridSpec(
        num_scalar_prefetch=4,
        grid=(n_total // tm,),
        in_specs=[pl.BlockSpec((tm * SUBLANES, LANES), lambda i, d, lo, hi, nu: (i, 0))],
        out_specs=pl.BlockSpec(memory_space=pl.ANY),
        scratch_shapes=[pltpu.VMEM((rows * SUBLANES, LANES), F32), pltpu.SemaphoreType.DMA((2,))],
    )
    return pl.pallas_call(
        functools.partial(_dispatch_kernel, tm=tm, rows=rows, n_experts=zlo.shape[0], n_blocks=n_blocks),
        grid_spec=grid_spec,
        out_shape=jax.ShapeDtypeStruct((n_blocks * rows * SUBLANES, LANES), F32),
        compiler_params=_cparams("arbitrary"),
        name="dispatch",
    )(dest, zlo, zhi, n_used, hn)


MXU_DIM = 256


def _split_w1_kernel(w_ref, g_ref, l_ref):
    r = lax.broadcasted_iota(I32, (MXU_DIM, MXU_DIM), 0)
    c = lax.broadcasted_iota(I32, (MXU_DIM, MXU_DIM), 1)
    half = MXU_DIM // 2
    source = jnp.where(c < half, 2 * c, 2 * (c - half) + 1)
    perm = jnp.where(r == source, 1.0, 0.0).astype(BF16)
    for blk in range(w_ref.shape[1] // MXU_DIM):
        res = _dot(w_ref[:, blk * MXU_DIM:(blk + 1) * MXU_DIM].astype(BF16), perm)
        g_ref[:, blk * half:(blk + 1) * half] = res[:, :half].astype(BF16)
        l_ref[:, blk * half:(blk + 1) * half] = res[:, half:].astype(BF16)


def _split_w1(w1):
    n_exp, d, ff2 = w1.shape
    tr = min(ROW_TILE, d)
    half = pl.BlockSpec((None, tr, ff2 // 2), lambda e, i: (e, i, 0))
    return pl.pallas_call(
        _split_w1_kernel,
        grid=(n_exp, d // tr),
        in_specs=[pl.BlockSpec((None, tr, ff2), lambda e, i: (e, i, 0))],
        out_specs=[half, half],
        out_shape=[jax.ShapeDtypeStruct((n_exp, d, ff2 // 2), BF16)] * 2,
        compiler_params=_cparams("parallel", "parallel"),
        name="split_w1",
    )(w1)


def _expert_kernel(be_ref, nu_ref, xs_ref, w1g_ref, w1l_ref, b1g_ref, b1l_ref, w2_ref, b2_ref, y_ref, *, rows):
    j = pl.program_id(0)
    n_used = nu_ref[0]

    @pl.when(j < n_used)
    def _():
        x = jnp.concatenate([xs_ref[pl.ds(s, rows, stride=SUBLANES), :] for s in range(SUBLANES)],
                            axis=-1).astype(BF16)
        glu = jnp.minimum(_dot(x, w1g_ref[...]) + b1g_ref[...], SWIGLU_LIMIT)
        lin = jnp.clip(_dot(x, w1l_ref[...]) + b1l_ref[...], -SWIGLU_LIMIT, SWIGLU_LIMIT)
        act = glu * jax.nn.sigmoid(SWIGLU_ALPHA * glu) * (lin + 1.0)
        y = _dot(act.astype(BF16), w2_ref[...].astype(BF16)) + b2_ref[...]
        for s in range(SUBLANES):
            y_ref[pl.ds(s, rows, stride=SUBLANES), :] = y[:, s * LANES:(s + 1) * LANES]

    @pl.when(j >= n_used)
    def _():
        y_ref[...] = jnp.zeros_like(y_ref)


def _experts(block_e, n_used, xs, w1g, w1l, b1g, b1l, w2, b2, n_blocks):
    rows = EXPERT_ROWS
    tile = rows * SUBLANES
    wspec = lambda a: pl.BlockSpec((None,) + a.shape[1:], lambda j, be, nu: (be[j], 0, 0))
    grid_spec = pltpu.PrefetchScalarGridSpec(
        num_scalar_prefetch=2,
        grid=(n_blocks,),
        in_specs=[pl.BlockSpec((tile, LANES), lambda j, be, nu: (jnp.minimum(j, nu[0] - 1), 0)),
                  wspec(w1g), wspec(w1l), wspec(b1g), wspec(b1l), wspec(w2), wspec(b2)],
        out_specs=pl.BlockSpec((tile, LANES), lambda j, be, nu: (j, 0)),
    )
    return pl.pallas_call(
        functools.partial(_expert_kernel, rows=rows),
        grid_spec=grid_spec,
        out_shape=jax.ShapeDtypeStruct((n_blocks * tile, LANES), F32),
        compiler_params=_cparams("arbitrary"),
        name="experts",
    )(block_e, n_used, xs, w1g, w1l, b1g, b1l, w2, b2)


def _combine_kernel(dest_ref, yb_ref, gate_ref, h_ref, g_ref, yp_ref, ys_ref, cbuf, sem, *, tm, prompt_tiles):
    i = pl.program_id(0)
    n = pl.num_programs(0)
    slot = i % 2
    tile = tm * SUBLANES

    def gather(blk, sl):
        def body(t, carry):
            for k in range(TOP_K):
                src = pl.multiple_of(dest_ref[(blk * tm + t) * TOP_K + k] * SUBLANES, SUBLANES)
                dst = pl.multiple_of((k * tm + t) * SUBLANES, SUBLANES)
                pltpu.make_async_copy(yb_ref.at[pl.ds(src, SUBLANES), :], cbuf.at[sl, pl.ds(dst, SUBLANES), :],
                                      sem.at[sl]).start()
            return carry
        lax.fori_loop(0, tm, body, 0)

    @pl.when(i == 0)
    def _():
        gather(0, 0)

    @pl.when(i + 1 < n)
    def _():
        gather(i + 1, 1 - slot)

    pltpu.make_async_copy(yb_ref.at[pl.ds(0, TOP_K * tile), :], cbuf.at[slot], sem.at[slot]).wait()
    gates = gate_ref[...]
    slabs = []
    for s in range(SUBLANES):
        acc = None
        for k in range(TOP_K):
            part = gates[:, k:k + 1] * cbuf[slot, pl.ds(k * tile + s, tm, stride=SUBLANES), :]
            acc = part if acc is None else acc + part
        slabs.append(acc)
    hf = h_ref[...] + jnp.concatenate(slabs, axis=-1)
    y = _rms(hf, g_ref[...])

    @pl.when(i < prompt_tiles)
    def _():
        yp_ref[...] = y

    @pl.when(i >= prompt_tiles)
    def _():
        ys_ref[...] = y


def _combine(dest, yb, gates, h, gf, np_):
    n, d = h.shape
    ns_ = n - np_
    tm = min(COMBINE_TILE, math.gcd(np_, ns_))
    prompt_tiles = np_ // tm
    row = lambda w: pl.BlockSpec((tm, w), lambda i, ds: (i, 0))
    grid_spec = pltpu.PrefetchScalarGridSpec(
        num_scalar_prefetch=1,
        grid=(n // tm,),
        in_specs=[pl.BlockSpec(memory_space=pl.ANY), row(TOP_K), row(d), pl.BlockSpec(gf.shape, lambda i, ds: (0, 0))],
        out_specs=[pl.BlockSpec((tm, d), lambda i, ds: (jnp.minimum(i, prompt_tiles - 1), 0)),
                   pl.BlockSpec((tm, d), lambda i, ds: (jnp.maximum(i - prompt_tiles, 0), 0))],
        scratch_shapes=[pltpu.VMEM((2, TOP_K * tm * SUBLANES, LANES), F32), pltpu.SemaphoreType.DMA((2,))],
    )
    return pl.pallas_call(
        functools.partial(_combine_kernel, tm=tm, prompt_tiles=prompt_tiles),
        grid_spec=grid_spec,
        out_shape=[jax.ShapeDtypeStruct((np_, d), F32), jax.ShapeDtypeStruct((ns_, d), F32)],
        compiler_params=_cparams("arbitrary"),
        name="combine_norm",
    )(dest, yb, gates, h, gf)


def kernel(x_prompt, x_sample, state_pool, cache_k, cache_v, page_table, norm1_g, w_in, pool_w, pool_scale,
           lambda_q1, lambda_k1, lambda_q2, lambda_k2, subln_g, w_out, norm2_g, router_w, router_b,
           w1, b1, w2, b2, normf_g):
    batch, seq, d = x_prompt.shape
    bs, n_new, _ = x_sample.shape
    depth = w_in.shape[0]
    assert depth == 1, "single-layer trunk"
    n_heads = cache_k.shape[3]
    page_size = cache_k.shape[2]
    pw = pool_w.shape[1] * pool_w.shape[2]
    qk_w = n_heads * HEAD_W
    n_experts = router_w.shape[2]
    np_, ns_ = batch * seq, bs * n_new
    n_total = np_ + ns_
    lam_init = 0.8 - 0.6 * math.exp(-0.3 * 0)

    g1 = norm1_g[0][None]
    w_in_b = w_in[0].astype(BF16)
    wvt_b = w_in_b[:, pw + 2 * qk_w:].T
    pool_w_b = pool_w[0].astype(BF16)
    ps = pool_scale[0][None]
    lams = [a[0][None] for a in (lambda_q1, lambda_k1, lambda_q2, lambda_k2)]
    sg = subln_g[0][None]
    slopes = jnp.exp2(-8.0 * (jnp.arange(n_heads, dtype=F32) + 1.0) / n_heads)
    w_out_b = w_out[0].astype(BF16)
    g2 = norm2_g[0][None]
    rw_b = router_w[0].astype(BF16)
    rb = router_b[0][None]
    w1g, w1l = _split_w1(w1.reshape(w1.shape[1:]))
    b1g = b1[0][:, None, 0::2]
    b1l = b1[0][:, None, 1::2]
    w2b = w2.reshape(w2.shape[1:])
    b2e = b2[0][:, None, :]
    gf = normf_g[None]

    xp = x_prompt.reshape(np_, d)
    tq = min(ATTN_TILE, seq)
    u_p, q_p, k_p, v_p, kb_p, vt_p = _inproj(xp, g1, w_in_b, wvt_b, pw, qk_w, key_tile=tq)
    pool_p = _pool_prompt(u_p, pool_w_b, ps, batch, seq)
    o_p = _attn_prompt(q_p, kb_p, vt_p, slopes, lams, sg, batch, seq, n_heads, lam_init, tq)

    xs = x_sample.reshape(ns_, d)
    u_s, q_s, k_s, v_s = _inproj(xs, g1, w_in_b, wvt_b, pw, qk_w)
    pool_s, new_pool_s = _pool_sample(state_pool[0], u_s.reshape(bs, n_new, pw), pool_w_b, ps)
    kc = cache_k.reshape(cache_k.shape[1], page_size * n_heads, HEAD_W)
    vc = cache_v.reshape(cache_v.shape[1], page_size * n_heads, HEAD_W)
    o_s = _attn_sample(q_s.reshape(bs, n_new, qk_w), k_s.reshape(bs, n_new, qk_w), v_s.reshape(bs, n_new, qk_w),
                       kc, vc, page_table, slopes, lams, sg, n_heads, lam_init)

    h, hn, te, gates, rank, cnt = _outproj((pool_p, o_p, xp), (pool_s.reshape(ns_, pw), o_s.reshape(ns_, qk_w), xs),
                                           w_out_b, g2, rw_b, rb)

    counts = cnt[0].astype(I32)
    n_blocks = -(-n_total * TOP_K // EXPERT_ROWS) + n_experts
    padded = (counts + EXPERT_ROWS - 1) // EXPERT_ROWS * EXPERT_ROWS
    pad_ends = jnp.cumsum(padded)
    pad_starts = pad_ends - padded
    te_flat = te.reshape(-1)
    dest = rank.reshape(-1)
    for e in range(n_experts):
        dest = dest + jnp.where(te_flat == e, pad_starts[e], 0)
    n_used = (pad_ends[-1:] // EXPERT_ROWS).astype(I32)
    block_start = jnp.arange(n_blocks, dtype=I32) * EXPERT_ROWS
    block_e = jnp.minimum(jnp.sum((pad_ends[None, :] <= block_start[:, None]).astype(I32), axis=1), n_experts - 1)

    xs_rows = _dispatch(dest, pad_starts + counts, pad_ends, n_used, hn, n_blocks)
    yb = _experts(block_e, n_used, xs_rows, w1g, w1l, b1g, b1l, w2b, b2e, n_blocks)
    y_p, y_s = _combine(dest, yb, gates, h, gf, np_)

    return (y_p.reshape(batch, seq, d), y_s.reshape(bs, n_new, d),
            u_p.reshape(batch, seq, pw)[None, :, seq - state_pool.shape[2]:],
            new_pool_s[None],
            k_p.reshape(1, batch, seq, n_heads, HEAD_W), v_p.reshape(1, batch, seq, n_heads, HEAD_W),
            k_s.reshape(1, bs, n_new, n_heads, HEAD_W), v_s.reshape(1, bs, n_new, n_heads, HEAD_W))
```

```python
import functools
import math

import jax
import jax.numpy as jnp
from jax import lax
from jax.experimental import pallas as pl
from jax.experimental.pallas import tpu as pltpu

F32 = jnp.float32
BF16 = jnp.bfloat16
I32 = jnp.int32

RMS_EPS = 1e-6
NEG_INF = -1e30
HEAD_DIM = 64
HEAD_W = 2 * HEAD_DIM
POOL_WINDOWS = (2, 4, 8, 16)
POOL_HALO = 16
TOP_K = 4
RANK_BITS = 20
SWIGLU_LIMIT = 7.0
SWIGLU_ALPHA = 1.702
LANES = 128
SUBLANES = 8
ROW_TILE = 512
ATTN_TILE = 512
EXPERT_ROWS = 512
COMBINE_TILE = 256
VMEM_LIMIT = 48 * 1024 * 1024


def _cparams(*sem):
    return pltpu.CompilerParams(dimension_semantics=sem, vmem_limit_bytes=VMEM_LIMIT)


def _dot(a, b):
    return jnp.dot(a, b, preferred_element_type=F32)


def _dot_nt(a, b):
    return lax.dot_general(a, b, (((1,), (1,)), ((), ())), preferred_element_type=F32)


def _rms(x, g):
    return x * lax.rsqrt(jnp.mean(x * x, axis=-1, keepdims=True) + RMS_EPS) * g


def _inproj_kernel(x_ref, g_ref, w_ref, wvt_ref, u_ref, q_ref, k_ref, v_ref, *attn_refs, pool_w, qk_w, key_tile):
    xb = _rms(x_ref[...], g_ref[...]).astype(BF16)
    c0, c1, c2 = pool_w, pool_w + qk_w, pool_w + 2 * qk_w
    u_ref[...] = _dot(xb, w_ref[:, 0:c0])
    q_ref[...] = _dot(xb, w_ref[:, c0:c1]) * (HEAD_DIM ** -0.5)
    k = _dot(xb, w_ref[:, c1:c2])
    v = _dot(xb, w_ref[:, c2:])
    for h in range(k_ref.shape[1]):
        k_ref[:, h, :] = k[:, h * HEAD_W:(h + 1) * HEAD_W]
        v_ref[:, h, :] = v[:, h * HEAD_W:(h + 1) * HEAD_W]
    if attn_refs:
        kb_ref, vt_ref = attn_refs
        kb_ref[...] = k.astype(BF16)
        vt = _dot_nt(wvt_ref[...], xb)
        for c in range(vt_ref.shape[0]):
            vt_ref[c] = vt[:, c * key_tile:(c + 1) * key_tile].astype(BF16)


def _inproj(x, g1, w_in_b, wvt_b, pool_w, qk_w, key_tile=None):
    n, d = x.shape
    val_w = wvt_b.shape[0]
    tm = min(ROW_TILE, n)
    row = lambda w: pl.BlockSpec((tm, w), lambda i: (i, 0))
    full = lambda a: pl.BlockSpec(a.shape, lambda i: (0,) * a.ndim)
    n_heads = qk_w // HEAD_W
    assert val_w == qk_w
    heads = pl.BlockSpec((tm, n_heads, HEAD_W), lambda i: (i, 0, 0))
    out_specs = [row(pool_w), row(qk_w), heads, heads]
    out_shape = [jax.ShapeDtypeStruct((n, pool_w), F32), jax.ShapeDtypeStruct((n, qk_w), F32),
                 jax.ShapeDtypeStruct((n, n_heads, HEAD_W), F32), jax.ShapeDtypeStruct((n, n_heads, HEAD_W), F32)]
    if key_tile is not None:
        assert tm % key_tile == 0
        out_specs += [row(qk_w), pl.BlockSpec((tm // key_tile, val_w, key_tile), lambda i: (i, 0, 0))]
        out_shape += [jax.ShapeDtypeStruct((n, qk_w), BF16), jax.ShapeDtypeStruct((n // key_tile, val_w, key_tile), BF16)]
    return pl.pallas_call(
        functools.partial(_inproj_kernel, pool_w=pool_w, qk_w=qk_w, key_tile=key_tile),
        grid=(n // tm,),
        in_specs=[row(d), full(g1), full(w_in_b), full(wvt_b)],
        out_specs=out_specs,
        out_shape=out_shape,
        compiler_params=_cparams("parallel"),
        name="inproj",
    )(x, g1, w_in_b, wvt_b)


def _pool_group(s_win, tok, cnt, pw, ps):
    d = s_win / cnt - tok
    return _dot(d.astype(BF16), pw) * ps


def _pool_prompt_kernel(u_ref, pw_ref, ps_ref, o_ref, carry_ref, *, tt):
    j = pl.program_id(1)

    @pl.when(j == 0)
    def _():
        carry_ref[...] = jnp.zeros_like(carry_ref)

    cur = u_ref[...]
    ext = jnp.concatenate([carry_ref[...], cur], axis=0)
    carry_ref[...] = cur[tt - POOL_HALO:, :]
    pos = (j * tt + lax.broadcasted_iota(I32, (tt, 1), 0)).astype(F32)
    gw = cur.shape[1] // len(POOL_WINDOWS)
    for g, w in enumerate(POOL_WINDOWS):
        cols = slice(g * gw, (g + 1) * gw)
        e = ext[:, cols]
        s, span = e, 1
        while span < w:
            s = s + pltpu.roll(s, span, axis=0)
            span *= 2
        cnt = jnp.minimum(float(w), pos + 1.0)
        out = _pool_group(s[POOL_HALO:, :], e[POOL_HALO:, :], cnt, pw_ref[g], ps_ref[:, cols])
        o_ref[:, cols] = out.astype(o_ref.dtype)


def _pool_prompt(u, pool_w_b, pool_scale, batch, seq):
    pw = u.shape[1]
    tt = min(ROW_TILE, seq)
    u3 = u.reshape(batch, seq, pw)
    out = pl.pallas_call(
        functools.partial(_pool_prompt_kernel, tt=tt),
        grid=(batch, seq // tt),
        in_specs=[pl.BlockSpec((None, tt, pw), lambda b, j: (b, j, 0)),
                  pl.BlockSpec(pool_w_b.shape, lambda b, j: (0, 0, 0)),
                  pl.BlockSpec(pool_scale.shape, lambda b, j: (0, 0))],
        out_specs=pl.BlockSpec((None, tt, pw), lambda b, j: (b, j, 0)),
        out_shape=jax.ShapeDtypeStruct((batch, seq, pw), BF16),
        scratch_shapes=[pltpu.VMEM((POOL_HALO, pw), F32)],
        compiler_params=_cparams("arbitrary", "arbitrary"),
        name="pool_prompt",
    )(u3, pool_w_b, pool_scale)
    return out.reshape(batch * seq, pw)


def _pool_sample_kernel(sp_ref, u_ref, pw_ref, ps_ref, o_ref, np_ref, *, n_new, n_buf):
    rows = [sp_ref[:, r, :] for r in range(n_buf)] + [u_ref[:, r, :] for r in range(n_new)]
    for r in range(n_buf):
        np_ref[:, r, :] = rows[n_new + r]
    gw = rows[0].shape[1] // len(POOL_WINDOWS)
    for t in range(n_new):
        i = n_buf + t
        for g, w in enumerate(POOL_WINDOWS):
            cols = slice(g * gw, (g + 1) * gw)
            s = rows[i][:, cols]
            for back in range(1, w):
                s = s + rows[i - back][:, cols]
            out = _pool_group(s, rows[i][:, cols], float(w), pw_ref[g], ps_ref[:, cols])
            o_ref[:, t, cols] = out


def _pool_sample(state_pool, u3, pool_w_b, pool_scale):
    bs, n_buf, pw = state_pool.shape
    n_new = u3.shape[1]
    full = lambda a: pl.BlockSpec(a.shape, lambda i: (0,) * a.ndim)
    return pl.pallas_call(
        functools.partial(_pool_sample_kernel, n_new=n_new, n_buf=n_buf),
        grid=(1,),
        in_specs=[full(state_pool), full(u3), full(pool_w_b), full(pool_scale)],
        out_specs=[pl.BlockSpec((bs, n_new, pw), lambda i: (0, 0, 0)),
                   pl.BlockSpec((bs, n_buf, pw), lambda i: (0, 0, 0))],
        out_shape=[jax.ShapeDtypeStruct((bs, n_new, pw), F32),
                   jax.ShapeDtypeStruct((bs, n_buf, pw), F32)],
        compiler_params=_cparams("arbitrary"),
        name="pool_sample",
    )(state_pool, u3, pool_w_b, pool_scale)


def _lambda_full(lq1, lk1, lq2, lk2, lam_init):
    e1 = jnp.exp(jnp.sum(lq1 * lk1, axis=-1, keepdims=True))
    e2 = jnp.exp(jnp.sum(lq2 * lk2, axis=-1, keepdims=True))
    return e1 - e2 + lam_init


def _split_halves(q):
    lane = lax.broadcasted_iota(I32, q.shape, 1)
    zero = jnp.zeros_like(q)
    return jnp.concatenate([jnp.where(lane < HEAD_DIM, q, zero), jnp.where(lane >= HEAD_DIM, q, zero)], axis=0)


def _subln(o, g, lam_init):
    return _rms(o, g) * (1.0 - lam_init)


def _attn_prompt_kernel(slopes_ref, q_ref, k_ref, vt_ref, lq1_ref, lk1_ref, lq2_ref, lk2_ref, g_ref, o_ref,
                        *, tq, lam_init):
    h = pl.program_id(1)
    i = pl.program_id(2)
    slope = slopes_ref[h]
    lam = _lambda_full(lq1_ref[...], lk1_ref[...], lq2_ref[...], lk2_ref[...], lam_init)
    qq = _split_halves(q_ref[...].astype(BF16))
    n_rep = 2 * tq // LANES
    key_row = lax.broadcasted_iota(I32, (tq, LANES), 0)

    def step(j, carry, diag):
        m, l, acc = carry
        start = pl.multiple_of(j * tq, tq)
        st = _dot_nt(k_ref[pl.ds(start, tq), :], qq)
        bias = slope * (key_row + (j - i) * tq).astype(F32)
        st = st + jnp.concatenate([bias] * n_rep, axis=1)
        if diag:
            kr = lax.broadcasted_iota(I32, st.shape, 0)
            qc = lax.broadcasted_iota(I32, st.shape, 1)
            qc = jnp.where(qc >= tq, qc - tq, qc)
            st = jnp.where(qc >= kr, st, NEG_INF)
        m_new = jnp.maximum(m, jnp.max(st, axis=0, keepdims=True))
        alpha = jnp.exp(m - m_new)
        p = jnp.exp(st - m_new)
        l = alpha * l + jnp.sum(p, axis=0, keepdims=True)
        acc = alpha * acc + _dot(vt_ref[j], p.astype(BF16))
        return m_new, l, acc

    init = (jnp.full((1, 2 * tq), NEG_INF, F32), jnp.zeros((1, 2 * tq), F32), jnp.zeros((HEAD_W, 2 * tq), F32))
    carry = lax.fori_loop(0, i, lambda j, cr: step(j, cr, False), init)
    _, l, acc = step(i, carry, True)
    o = acc / l
    o = (o[:, :tq] - lam * o[:, tq:]).T
    o_ref[...] = _subln(o, g_ref[...], lam_init).astype(o_ref.dtype)


def _attn_prompt(q, kb, vt, slopes, lams, subln_g, batch, seq, n_heads, lam_init, tq):
    q3, k3 = (a.reshape(batch, seq, n_heads * HEAD_W) for a in (q, kb))
    small = lambda a: pl.BlockSpec(a.shape, lambda b, h, i, s: (0,) * a.ndim)
    grid_spec = pltpu.PrefetchScalarGridSpec(
        num_scalar_prefetch=1,
        grid=(batch, n_heads, seq // tq),
        in_specs=[pl.BlockSpec((None, tq, HEAD_W), lambda b, h, i, s: (b, i, h)),
                  pl.BlockSpec((None, seq, HEAD_W), lambda b, h, i, s: (b, 0, h)),
                  pl.BlockSpec((seq // tq, HEAD_W, tq), lambda b, h, i, s: (b, h, 0)),
                  *[small(a) for a in lams], small(subln_g)],
        out_specs=pl.BlockSpec((None, tq, HEAD_W), lambda b, h, i, s: (b, i, h)),
    )
    out = pl.pallas_call(
        functools.partial(_attn_prompt_kernel, tq=tq, lam_init=lam_init),
        grid_spec=grid_spec,
        out_shape=jax.ShapeDtypeStruct((batch, seq, n_heads * HEAD_W), BF16),
        compiler_params=_cparams("parallel", "parallel", "arbitrary"),
        name="attn_prompt",
    )(slopes, q3, k3, vt, *lams, subln_g)
    return out.reshape(batch * seq, n_heads * HEAD_W)


def _attn_sample_kernel(pt_ref, slopes_ref, q_ref, kn_ref, vn_ref, kc_ref, vc_ref,
                        lq1_ref, lk1_ref, lq2_ref, lk2_ref, g_ref, o_ref, kbuf, vbuf, sem,
                        *, n_pages, page_rows, n_heads, n_new, lam_init):
    b = pl.program_id(0)
    nb = pl.num_programs(0)
    slot = b % 2
    past = n_pages * page_rows // n_heads

    def page_copies(seq, sl, pg):
        p = pt_ref[seq * n_pages + pg]
        dst = pl.ds(pg * page_rows, page_rows)
        return (pltpu.make_async_copy(kc_ref.at[p], kbuf.at[sl, dst, :], sem.at[sl, 0]),
                pltpu.make_async_copy(vc_ref.at[p], vbuf.at[sl, dst, :], sem.at[sl, 1]))

    def start_seq(seq, sl):
        for pg in range(n_pages):
            for cp in page_copies(seq, sl, pg):
                cp.start()

    @pl.when(b == 0)
    def _():
        start_seq(0, 0)

    @pl.when(b + 1 < nb)
    def _():
        start_seq(b + 1, 1 - slot)

    for pg in range(n_pages):
        for cp in page_copies(b, slot, pg):
            cp.wait()

    lam = _lambda_full(lq1_ref[...], lk1_ref[...], lq2_ref[...], lk2_ref[...], lam_init)
    rows = 2 * SUBLANES
    r = lax.broadcasted_iota(I32, (rows, past), 0)
    t_past = jnp.where(r >= n_new, r - n_new, r)
    dist_past = (past + t_past - lax.broadcasted_iota(I32, (rows, past), 1)).astype(F32)
    r1 = lax.broadcasted_iota(I32, (rows, 1), 0)
    t_new = jnp.where(r1 >= n_new, r1 - n_new, r1)
    for h in range(n_heads):
        slope = slopes_ref[h]
        cols = slice(h * HEAD_W, (h + 1) * HEAD_W)
        q = q_ref[:, cols]
        qq = jnp.concatenate([_split_halves(q), jnp.zeros((rows - 2 * n_new, HEAD_W), F32)], axis=0)
        kh = kbuf[slot, pl.ds(h, past, stride=n_heads), :].astype(BF16)
        vh = vbuf[slot, pl.ds(h, past, stride=n_heads), :].astype(BF16)
        s = _dot_nt(qq.astype(BF16), kh) - slope * dist_past
        kn = kn_ref[:, cols]
        vn = vn_ref[:, cols]
        s_new = []
        for c in range(n_new):
            sc = jnp.sum(qq * kn[c:c + 1, :], axis=-1, keepdims=True) - slope * (t_new - c).astype(F32)
            s_new.append(jnp.where(t_new >= c, sc, NEG_INF))
        m = jnp.max(s, axis=-1, keepdims=True)
        for sc in s_new:
            m = jnp.maximum(m, sc)
        p = jnp.exp(s - m)
        l = jnp.sum(p, axis=-1, keepdims=True)
        acc = _dot(p.astype(BF16), vh)
        for c, sc in enumerate(s_new):
            pc = jnp.exp(sc - m)
            l = l + pc
            acc = acc + pc * vn[c:c + 1, :]
        o = acc / l
        o = o[0:n_new] - lam * o[n_new:2 * n_new]
        o_ref[:, cols] = _subln(o, g_ref[...], lam_init)


def _attn_sample(q3, kn3, vn3, kc, vc, page_table, slopes, lams, subln_g, n_heads, lam_init):
    bs, n_new, qk_w = q3.shape
    n_pages = page_table.shape[1]
    page_rows = kc.shape[1]
    small = lambda a: pl.BlockSpec(a.shape, lambda b, pt, s: (0,) * a.ndim)
    tok = pl.BlockSpec((None, n_new, qk_w), lambda b, pt, s: (b, 0, 0))
    grid_spec = pltpu.PrefetchScalarGridSpec(
        num_scalar_prefetch=2,
        grid=(bs,),
        in_specs=[tok, tok, tok, pl.BlockSpec(memory_space=pl.ANY), pl.BlockSpec(memory_space=pl.ANY),
                  *[small(a) for a in lams], small(subln_g)],
        out_specs=tok,
        scratch_shapes=[pltpu.VMEM((2, n_pages * page_rows, LANES), F32),
                        pltpu.VMEM((2, n_pages * page_rows, LANES), F32),
                        pltpu.SemaphoreType.DMA((2, 2))],
    )
    return pl.pallas_call(
        functools.partial(_attn_sample_kernel, n_pages=n_pages, page_rows=page_rows, n_heads=n_heads,
                          n_new=n_new, lam_init=lam_init),
        grid_spec=grid_spec,
        out_shape=jax.ShapeDtypeStruct((bs, n_new, qk_w), F32),
        compiler_params=_cparams("arbitrary"),
        name="attn_sample",
    )(page_table.reshape(-1), slopes, q3, kn3, vn3, kc, vc, *lams, subln_g)


def _outproj_kernel(pool_p_ref, o_p_ref, x_p_ref, pool_s_ref, o_s_ref, x_s_ref, wo_ref, g2_ref, rw_ref, rb_ref,
                    h_ref, hn_ref, code_ref, gate_ref, cnt_out_ref, run_ref, *, tm, n_experts, prompt_tiles):
    i = pl.program_id(0)

    @pl.when(i == 0)
    def _():
        run_ref[...] = jnp.zeros_like(run_ref)

    tile = functools.partial(_outproj_tile, wo_ref=wo_ref, g2_ref=g2_ref, rw_ref=rw_ref, rb_ref=rb_ref, h_ref=h_ref,
                             hn_ref=hn_ref, code_ref=code_ref, gate_ref=gate_ref,
                             cnt_out_ref=cnt_out_ref, run_ref=run_ref, tm=tm, n_experts=n_experts)
    pl.when(i < prompt_tiles)(functools.partial(tile, pool_p_ref, o_p_ref, x_p_ref))
    pl.when(i >= prompt_tiles)(functools.partial(tile, pool_s_ref, o_s_ref, x_s_ref))


def _outproj_tile(pool_ref, o_ref, x_ref, *, wo_ref, g2_ref, rw_ref, rb_ref, h_ref, hn_ref, code_ref, gate_ref,
                  cnt_out_ref, run_ref, tm, n_experts):
    pw = pool_ref.shape[1]
    mix = _dot(pool_ref[...].astype(BF16), wo_ref[0:pw, :]) + _dot(o_ref[...].astype(BF16), wo_ref[pw:, :])
    h = x_ref[...] + mix
    h_ref[...] = h
    hn = _rms(h, g2_ref[...])
    for s in range(hn.shape[1] // LANES):
        hn_ref[pl.ds(s, tm, stride=SUBLANES), :] = hn[:, s * LANES:(s + 1) * LANES]
    logits = _dot(hn.astype(BF16), rw_ref[...]) + rb_ref[...]
    lane = lax.broadcasted_iota(I32, logits.shape, 1)
    vals, idxs = [], []
    for _ in range(TOP_K):
        m = jnp.max(logits, axis=-1, keepdims=True)
        idx = jnp.min(jnp.where(logits == m, lane, n_experts), axis=-1, keepdims=True)
        vals.append(m)
        idxs.append(idx)
        logits = jnp.where(lane == idx, -jnp.inf, logits)
    ex = [jnp.exp(v - vals[0]) for v in vals]
    den = ex[0]
    for e in ex[1:]:
        den = den + e
    chosen = jnp.zeros(logits.shape, F32)
    for idx in idxs:
        chosen = chosen + (lane == idx).astype(F32)
    row = lax.broadcasted_iota(I32, (tm, tm), 0)
    col = lax.broadcasted_iota(I32, (tm, tm), 1)
    before = _dot((col < row).astype(BF16), chosen.astype(BF16)) + run_ref[...]
    for k in range(TOP_K):
        gate_ref[:, k:k + 1] = ex[k] / den
        rank = jnp.sum(jnp.where(lane == idxs[k], before, 0.0), axis=-1, keepdims=True).astype(I32)
        code_ref[:, k:k + 1] = idxs[k] * (1 << RANK_BITS) + rank
    run_ref[...] = run_ref[...] + jnp.sum(chosen, axis=0, keepdims=True)
    cnt_out_ref[...] = run_ref[...]


def _outproj(prompt, sample, w_out_b, g2, rw_b, rb):
    np_, d = prompt[2].shape
    ns_ = sample[2].shape[0]
    n_total = np_ + ns_
    n_experts = rw_b.shape[1]
    tm = min(ROW_TILE, math.gcd(np_, ns_))
    prompt_tiles = np_ // tm
    assert d == SUBLANES * LANES and n_total < (1 << RANK_BITS)
    first = lambda a: pl.BlockSpec((tm, a.shape[1]), lambda i: (jnp.minimum(i, prompt_tiles - 1), 0))
    second = lambda a: pl.BlockSpec((tm, a.shape[1]), lambda i: (jnp.maximum(i - prompt_tiles, 0), 0))
    row = lambda w: pl.BlockSpec((tm, w), lambda i: (i, 0))
    full = lambda a: pl.BlockSpec(a.shape, lambda i: (0,) * a.ndim)
    counts = jax.ShapeDtypeStruct((1, n_experts), F32)
    return pl.pallas_call(
        functools.partial(_outproj_kernel, tm=tm, n_experts=n_experts, prompt_tiles=prompt_tiles),
        grid=(n_total // tm,),
        in_specs=[*[first(a) for a in prompt], *[second(a) for a in sample],
                  full(w_out_b), full(g2), full(rw_b), full(rb)],
        out_specs=[row(d), pl.BlockSpec((tm * SUBLANES, LANES), lambda i: (i, 0)),
                   row(TOP_K), row(TOP_K), pl.BlockSpec(counts.shape, lambda i: (0, 0))],
        out_shape=[jax.ShapeDtypeStruct((n_total, d), F32), jax.ShapeDtypeStruct((n_total * SUBLANES, LANES), F32),
                   jax.ShapeDtypeStruct((n_total, TOP_K), I32), jax.ShapeDtypeStruct((n_total, TOP_K), F32), counts],
        scratch_shapes=[pltpu.VMEM(counts.shape, F32)],
        compiler_params=_cparams("arbitrary"),
        name="outproj_router",
    )(*prompt, *sample, w_out_b, g2, rw_b, rb)


def _sorted_row(code_ref, start_ref, a):
    code = code_ref[a]
    return start_ref[code >> RANK_BITS] + (code & ((1 << RANK_BITS) - 1))


def _dispatch_kernel(code_ref, start_ref, zlo_ref, zhi_ref, nu_ref, hn_ref, xs_ref, zero_ref, sem, *, tm, rows,
                     n_experts, n_blocks):
    i = pl.program_id(0)
    tile = tm * SUBLANES
    block = rows * SUBLANES

    def pad_pieces(e):
        lo, n = zlo_ref[e], zhi_ref[e] - zlo_ref[e]
        for bit in range(rows.bit_length() - 1):
            size = 1 << bit
            below = n & (size - 1)
            start = pl.multiple_of((lo + below) * SUBLANES, SUBLANES)
            cp = pltpu.make_async_copy(zero_ref.at[pl.ds(0, size * SUBLANES), :],
                                       xs_ref.at[pl.ds(start, size * SUBLANES), :], sem.at[1])
            yield (n & size) != 0, cp

    def tail_copy(blk):
        start = pl.multiple_of(blk * block, block)
        return pltpu.make_async_copy(zero_ref, xs_ref.at[pl.ds(start, block), :], sem.at[1])

    def for_zero_copies(fn):
        def per_expert(e, carry):
            for used, cp in pad_pieces(e):
                pl.when(used)(functools.partial(fn, cp))
            return carry
        lax.fori_loop(0, n_experts, per_expert, 0)

        def per_block(blk, carry):
            fn(tail_copy(blk))
            return carry
        lax.fori_loop(nu_ref[0], n_blocks, per_block, 0)

    @pl.when(i == 0)
    def _():
        zero_ref[...] = jnp.zeros_like(zero_ref)
        for_zero_copies(lambda cp: cp.start())

    def body(t, carry):
        src = pl.multiple_of(t * SUBLANES, SUBLANES)
        for k in range(TOP_K):
            dst = pl.multiple_of(_sorted_row(code_ref, start_ref, (i * tm + t) * TOP_K + k) * SUBLANES, SUBLANES)
            pltpu.make_async_copy(hn_ref.at[pl.ds(src, SUBLANES), :], xs_ref.at[pl.ds(dst, SUBLANES), :],
                                  sem.at[0]).start()
        return carry
    lax.fori_loop(0, tm, body, 0)

    for _ in range(TOP_K):
        pltpu.make_async_copy(hn_ref, xs_ref.at[pl.ds(0, tile), :], sem.at[0]).wait()

    @pl.when(i == pl.num_programs(0) - 1)
    def _():
        for_zero_copies(lambda cp: cp.wait())


def _dispatch(code, starts, zlo, zhi, n_used, hn, n_blocks):
    n_total = hn.shape[0] // SUBLANES
    tm = math.gcd(n_total, ROW_TILE)
    rows = EXPERT_ROWS
    assert rows & (rows - 1) == 0
    grid_spec = pltpu.PrefetchScalarGridSpec(
        num_scalar_prefetch=5,
        grid=(n_total // tm,),
        in_specs=[pl.BlockSpec((tm * SUBLANES, LANES), lambda i, *_: (i, 0))],
        out_specs=pl.BlockSpec(memory_space=pl.ANY),
        scratch_shapes=[pltpu.VMEM((rows * SUBLANES, LANES), F32), pltpu.SemaphoreType.DMA((2,))],
    )
    return pl.pallas_call(
        functools.partial(_dispatch_kernel, tm=tm, rows=rows, n_experts=zlo.shape[0], n_blocks=n_blocks),
        grid_spec=grid_spec,
        out_shape=jax.ShapeDtypeStruct((n_blocks * rows * SUBLANES, LANES), F32),
        compiler_params=_cparams("arbitrary"),
        name="dispatch",
    )(code, starts, zlo, zhi, n_used, hn)


MXU_DIM = 256


def _split_w1_kernel(w_ref, g_ref, l_ref):
    r = lax.broadcasted_iota(I32, (MXU_DIM, MXU_DIM), 0)
    c = lax.broadcasted_iota(I32, (MXU_DIM, MXU_DIM), 1)
    half = MXU_DIM // 2
    source = jnp.where(c < half, 2 * c, 2 * (c - half) + 1)
    perm = jnp.where(r == source, 1.0, 0.0).astype(BF16)
    for blk in range(w_ref.shape[1] // MXU_DIM):
        res = _dot(w_ref[:, blk * MXU_DIM:(blk + 1) * MXU_DIM].astype(BF16), perm)
        g_ref[:, blk * half:(blk + 1) * half] = res[:, :half].astype(BF16)
        l_ref[:, blk * half:(blk + 1) * half] = res[:, half:].astype(BF16)


def _split_w1(w1):
    n_exp, d, ff2 = w1.shape
    tr = min(ROW_TILE, d)
    half = pl.BlockSpec((None, tr, ff2 // 2), lambda e, i: (e, i, 0))
    return pl.pallas_call(
        _split_w1_kernel,
        grid=(n_exp, d // tr),
        in_specs=[pl.BlockSpec((None, tr, ff2), lambda e, i: (e, i, 0))],
        out_specs=[half, half],
        out_shape=[jax.ShapeDtypeStruct((n_exp, d, ff2 // 2), BF16)] * 2,
        compiler_params=_cparams("parallel", "parallel"),
        name="split_w1",
    )(w1)


def _expert_kernel(be_ref, nu_ref, xs_ref, w1g_ref, w1l_ref, b1g_ref, b1l_ref, w2_ref, b2_ref, y_ref, *, rows):
    j = pl.program_id(0)
    n_used = nu_ref[0]

    @pl.when(j < n_used)
    def _():
        x = jnp.concatenate([xs_ref[pl.ds(s, rows, stride=SUBLANES), :] for s in range(SUBLANES)],
                            axis=-1).astype(BF16)
        glu = jnp.minimum(_dot(x, w1g_ref[...]) + b1g_ref[...], SWIGLU_LIMIT)
        lin = jnp.clip(_dot(x, w1l_ref[...]) + b1l_ref[...], -SWIGLU_LIMIT, SWIGLU_LIMIT)
        act = glu * jax.nn.sigmoid(SWIGLU_ALPHA * glu) * (lin + 1.0)
        y = _dot(act.astype(BF16), w2_ref[...].astype(BF16)) + b2_ref[...]
        for s in range(SUBLANES):
            y_ref[pl.ds(s, rows, stride=SUBLANES), :] = y[:, s * LANES:(s + 1) * LANES]

    @pl.when(j >= n_used)
    def _():
        y_ref[...] = jnp.zeros_like(y_ref)


def _experts(block_e, n_used, xs, w1g, w1l, b1g, b1l, w2, b2, n_blocks):
    rows = EXPERT_ROWS
    tile = rows * SUBLANES
    wspec = lambda a: pl.BlockSpec((None,) + a.shape[1:], lambda j, be, nu: (be[j], 0, 0))
    grid_spec = pltpu.PrefetchScalarGridSpec(
        num_scalar_prefetch=2,
        grid=(n_blocks,),
        in_specs=[pl.BlockSpec((tile, LANES), lambda j, be, nu: (jnp.minimum(j, nu[0] - 1), 0)),
                  wspec(w1g), wspec(w1l), wspec(b1g), wspec(b1l), wspec(w2), wspec(b2)],
        out_specs=pl.BlockSpec((tile, LANES), lambda j, be, nu: (j, 0)),
    )
    return pl.pallas_call(
        functools.partial(_expert_kernel, rows=rows),
        grid_spec=grid_spec,
        out_shape=jax.ShapeDtypeStruct((n_blocks * tile, LANES), F32),
        compiler_params=_cparams("arbitrary"),
        name="experts",
    )(block_e, n_used, xs, w1g, w1l, b1g, b1l, w2, b2)


def _combine_kernel(code_ref, start_ref, yb_ref, gate_ref, h_ref, g_ref, yp_ref, ys_ref, cbuf, sem, *, tm,
                    prompt_tiles):
    i = pl.program_id(0)
    n = pl.num_programs(0)
    slot = i % 2
    tile = tm * SUBLANES

    def gather(blk, sl):
        def body(t, carry):
            for k in range(TOP_K):
                src = pl.multiple_of(_sorted_row(code_ref, start_ref, (blk * tm + t) * TOP_K + k) * SUBLANES,
                                     SUBLANES)
                dst = pl.multiple_of((k * tm + t) * SUBLANES, SUBLANES)
                pltpu.make_async_copy(yb_ref.at[pl.ds(src, SUBLANES), :], cbuf.at[sl, pl.ds(dst, SUBLANES), :],
                                      sem.at[sl]).start()
            return carry
        lax.fori_loop(0, tm, body, 0)

    @pl.when(i == 0)
    def _():
        gather(0, 0)

    @pl.when(i + 1 < n)
    def _():
        gather(i + 1, 1 - slot)

    pltpu.make_async_copy(yb_ref.at[pl.ds(0, TOP_K * tile), :], cbuf.at[slot], sem.at[slot]).wait()
    gates = gate_ref[...]
    slabs = []
    for s in range(SUBLANES):
        acc = None
        for k in range(TOP_K):
            part = gates[:, k:k + 1] * cbuf[slot, pl.ds(k * tile + s, tm, stride=SUBLANES), :]
            acc = part if acc is None else acc + part
        slabs.append(acc)
    hf = h_ref[...] + jnp.concatenate(slabs, axis=-1)
    y = _rms(hf, g_ref[...])

    @pl.when(i < prompt_tiles)
    def _():
        yp_ref[...] = y

    @pl.when(i >= prompt_tiles)
    def _():
        ys_ref[...] = y


def _combine(code, starts, yb, gates, h, gf, np_):
    n, d = h.shape
    ns_ = n - np_
    tm = min(COMBINE_TILE, math.gcd(np_, ns_))
    prompt_tiles = np_ // tm
    row = lambda w: pl.BlockSpec((tm, w), lambda i, *_: (i, 0))
    grid_spec = pltpu.PrefetchScalarGridSpec(
        num_scalar_prefetch=2,
        grid=(n // tm,),
        in_specs=[pl.BlockSpec(memory_space=pl.ANY), row(TOP_K), row(d), pl.BlockSpec(gf.shape, lambda i, *_: (0, 0))],
        out_specs=[pl.BlockSpec((tm, d), lambda i, *_: (jnp.minimum(i, prompt_tiles - 1), 0)),
                   pl.BlockSpec((tm, d), lambda i, *_: (jnp.maximum(i - prompt_tiles, 0), 0))],
        scratch_shapes=[pltpu.VMEM((2, TOP_K * tm * SUBLANES, LANES), F32), pltpu.SemaphoreType.DMA((2,))],
    )
    return pl.pallas_call(
        functools.partial(_combine_kernel, tm=tm, prompt_tiles=prompt_tiles),
        grid_spec=grid_spec,
        out_shape=[jax.ShapeDtypeStruct((np_, d), F32), jax.ShapeDtypeStruct((ns_, d), F32)],
        compiler_params=_cparams("arbitrary"),
        name="combine_norm",
    )(code, starts, yb, gates, h, gf)


def kernel(x_prompt, x_sample, state_pool, cache_k, cache_v, page_table, norm1_g, w_in, pool_w, pool_scale,
           lambda_q1, lambda_k1, lambda_q2, lambda_k2, subln_g, w_out, norm2_g, router_w, router_b,
           w1, b1, w2, b2, normf_g):
    batch, seq, d = x_prompt.shape
    bs, n_new, _ = x_sample.shape
    depth = w_in.shape[0]
    assert depth == 1, "single-layer trunk"
    n_heads = cache_k.shape[3]
    page_size = cache_k.shape[2]
    pw = pool_w.shape[1] * pool_w.shape[2]
    qk_w = n_heads * HEAD_W
    n_experts = router_w.shape[2]
    np_, ns_ = batch * seq, bs * n_new
    n_total = np_ + ns_
    lam_init = 0.8 - 0.6 * math.exp(-0.3 * 0)

    g1 = norm1_g[0][None]
    w_in_b = w_in[0].astype(BF16)
    wvt_b = w_in_b[:, pw + 2 * qk_w:].T
    pool_w_b = pool_w[0].astype(BF16)
    ps = pool_scale[0][None]
    lams = [a[0][None] for a in (lambda_q1, lambda_k1, lambda_q2, lambda_k2)]
    sg = subln_g[0][None]
    slopes = jnp.exp2(-8.0 * (jnp.arange(n_heads, dtype=F32) + 1.0) / n_heads)
    w_out_b = w_out[0].astype(BF16)
    g2 = norm2_g[0][None]
    rw_b = router_w[0].astype(BF16)
    rb = router_b[0][None]
    w1g, w1l = _split_w1(w1.reshape(w1.shape[1:]))
    b1g = b1[0][:, None, 0::2]
    b1l = b1[0][:, None, 1::2]
    w2b = w2.reshape(w2.shape[1:])
    b2e = b2[0][:, None, :]
    gf = normf_g[None]

    xp = x_prompt.reshape(np_, d)
    tq = min(ATTN_TILE, seq)
    u_p, q_p, k_p, v_p, kb_p, vt_p = _inproj(xp, g1, w_in_b, wvt_b, pw, qk_w, key_tile=tq)
    pool_p = _pool_prompt(u_p, pool_w_b, ps, batch, seq)
    o_p = _attn_prompt(q_p, kb_p, vt_p, slopes, lams, sg, batch, seq, n_heads, lam_init, tq)

    xs = x_sample.reshape(ns_, d)
    u_s, q_s, k_s, v_s = _inproj(xs, g1, w_in_b, wvt_b, pw, qk_w)
    pool_s, new_pool_s = _pool_sample(state_pool[0], u_s.reshape(bs, n_new, pw), pool_w_b, ps)
    kc = cache_k.reshape(cache_k.shape[1], page_size * n_heads, HEAD_W)
    vc = cache_v.reshape(cache_v.shape[1], page_size * n_heads, HEAD_W)
    o_s = _attn_sample(q_s.reshape(bs, n_new, qk_w), k_s.reshape(bs, n_new, qk_w), v_s.reshape(bs, n_new, qk_w),
                       kc, vc, page_table, slopes, lams, sg, n_heads, lam_init)

    h, hn, code, gates, cnt = _outproj((pool_p, o_p, xp), (pool_s.reshape(ns_, pw), o_s.reshape(ns_, qk_w), xs),
                                       w_out_b, g2, rw_b, rb)
    code = code.reshape(-1)

    counts = cnt[0].astype(I32)
    n_blocks = -(-n_total * TOP_K // EXPERT_ROWS) + n_experts
    padded = (counts + EXPERT_ROWS - 1) // EXPERT_ROWS * EXPERT_ROWS
    pad_ends = jnp.cumsum(padded)
    pad_starts = pad_ends - padded
    n_used = (pad_ends[-1:] // EXPERT_ROWS).astype(I32)
    block_start = jnp.arange(n_blocks, dtype=I32) * EXPERT_ROWS
    block_e = jnp.minimum(jnp.sum((pad_ends[None, :] <= block_start[:, None]).astype(I32), axis=1), n_experts - 1)

    xs_rows = _dispatch(code, pad_starts, pad_starts + counts, pad_ends, n_used, hn, n_blocks)
    yb = _experts(block_e, n_used, xs_rows, w1g, w1l, b1g, b1l, w2b, b2e, n_blocks)
    y_p, y_s = _combine(code, pad_starts, yb, gates, h, gf, np_)

    return (y_p.reshape(batch, seq, d), y_s.reshape(bs, n_new, d),
            u_p.reshape(batch, seq, pw)[None, :, seq - state_pool.shape[2]:],
            new_pool_s[None],
            k_p.reshape(1, batch, seq, n_heads, HEAD_W), v_p.reshape(1, batch, seq, n_heads, HEAD_W),
            k_s.reshape(1, bs, n_new, n_heads, HEAD_W), v_s.reshape(1, bs, n_new, n_heads, HEAD_W))
```

```python
import functools
import math

import jax
import jax.numpy as jnp
from jax import lax
from jax.experimental import pallas as pl
from jax.experimental.pallas import tpu as pltpu

F32 = jnp.float32
BF16 = jnp.bfloat16
I32 = jnp.int32

RMS_EPS = 1e-6
NEG_INF = -1e30
HEAD_DIM = 64
HEAD_W = 2 * HEAD_DIM
POOL_WINDOWS = (2, 4, 8, 16)
POOL_HALO = 16
TOP_K = 4
RANK_BITS = 20
SWIGLU_LIMIT = 7.0
SWIGLU_ALPHA = 1.702
LANES = 128
SUBLANES = 8
ROW_TILE = 512
ATTN_TILE = 512
ATTN_HEADS = 2
LOG2E = 1.4426950408889634
N_SPLIT = 3
POS_LOW = 16
BF16_ROWS = 16
EXPERT_ROWS = 512
COMBINE_TILE = 256
VMEM_LIMIT = 48 * 1024 * 1024


def _cparams(*sem):
    return pltpu.CompilerParams(dimension_semantics=sem, vmem_limit_bytes=VMEM_LIMIT)


def _dot(a, b):
    return jnp.dot(a, b, preferred_element_type=F32)


def _dot_nt(a, b):
    return lax.dot_general(a, b, (((1,), (1,)), ((), ())), preferred_element_type=F32)


def _rms(x, g):
    return x * lax.rsqrt(jnp.mean(x * x, axis=-1, keepdims=True) + RMS_EPS) * g


def _inproj_kernel(x_ref, g_ref, w_ref, wvt_ref, u_ref, q_ref, k_ref, v_ref, *attn_refs, pool_w, qk_w, key_tile):
    xb = _rms(x_ref[...], g_ref[...]).astype(BF16)
    c0, c1, c2 = pool_w, pool_w + qk_w, pool_w + 2 * qk_w
    u_ref[...] = _dot(xb, w_ref[:, 0:c0])
    q_ref[...] = _dot(xb, w_ref[:, c0:c1]) * (HEAD_DIM ** -0.5)
    k = _dot(xb, w_ref[:, c1:c2])
    v = _dot(xb, w_ref[:, c2:])
    for h in range(k_ref.shape[1]):
        k_ref[:, h, :] = k[:, h * HEAD_W:(h + 1) * HEAD_W]
        v_ref[:, h, :] = v[:, h * HEAD_W:(h + 1) * HEAD_W]
    if attn_refs:
        kb_ref, vt_ref = attn_refs
        kb_ref[...] = k.astype(BF16)
        vt = _dot_nt(wvt_ref[...], xb)
        for c in range(vt_ref.shape[0]):
            vt_ref[c] = vt[:, c * key_tile:(c + 1) * key_tile].astype(BF16)


def _inproj(x, g1, w_in_b, wvt_b, pool_w, qk_w, key_tile=None):
    n, d = x.shape
    val_w = wvt_b.shape[0]
    tm = min(ROW_TILE, n)
    row = lambda w: pl.BlockSpec((tm, w), lambda i: (i, 0))
    full = lambda a: pl.BlockSpec(a.shape, lambda i: (0,) * a.ndim)
    n_heads = qk_w // HEAD_W
    assert val_w == qk_w
    heads = pl.BlockSpec((tm, n_heads, HEAD_W), lambda i: (i, 0, 0))
    out_specs = [row(pool_w), row(qk_w), heads, heads]
    out_shape = [jax.ShapeDtypeStruct((n, pool_w), F32), jax.ShapeDtypeStruct((n, qk_w), F32),
                 jax.ShapeDtypeStruct((n, n_heads, HEAD_W), F32), jax.ShapeDtypeStruct((n, n_heads, HEAD_W), F32)]
    if key_tile is not None:
        assert tm % key_tile == 0
        out_specs += [row(qk_w), pl.BlockSpec((tm // key_tile, val_w, key_tile), lambda i: (i, 0, 0))]
        out_shape += [jax.ShapeDtypeStruct((n, qk_w), BF16), jax.ShapeDtypeStruct((n // key_tile, val_w, key_tile), BF16)]
    return pl.pallas_call(
        functools.partial(_inproj_kernel, pool_w=pool_w, qk_w=qk_w, key_tile=key_tile),
        grid=(n // tm,),
        in_specs=[row(d), full(g1), full(w_in_b), full(wvt_b)],
        out_specs=out_specs,
        out_shape=out_shape,
        compiler_params=_cparams("parallel"),
        name="inproj",
    )(x, g1, w_in_b, wvt_b)


def _pool_group(s_win, tok, cnt, pw, ps):
    d = s_win / cnt - tok
    return _dot(d.astype(BF16), pw) * ps


def _pool_prompt_kernel(u_ref, pw_ref, ps_ref, o_ref, carry_ref, *, tt):
    j = pl.program_id(1)

    @pl.when(j == 0)
    def _():
        carry_ref[...] = jnp.zeros_like(carry_ref)

    cur = u_ref[...]
    ext = jnp.concatenate([carry_ref[...], cur], axis=0)
    carry_ref[...] = cur[tt - POOL_HALO:, :]
    pos = (j * tt + lax.broadcasted_iota(I32, (tt, 1), 0)).astype(F32)
    gw = cur.shape[1] // len(POOL_WINDOWS)
    for g, w in enumerate(POOL_WINDOWS):
        cols = slice(g * gw, (g + 1) * gw)
        e = ext[:, cols]
        s, span = e, 1
        while span < w:
            s = s + pltpu.roll(s, span, axis=0)
            span *= 2
        cnt = jnp.minimum(float(w), pos + 1.0)
        out = _pool_group(s[POOL_HALO:, :], e[POOL_HALO:, :], cnt, pw_ref[g], ps_ref[:, cols])
        o_ref[:, cols] = out.astype(o_ref.dtype)


def _pool_prompt(u, pool_w_b, pool_scale, batch, seq):
    pw = u.shape[1]
    tt = min(ROW_TILE, seq)
    u3 = u.reshape(batch, seq, pw)
    out = pl.pallas_call(
        functools.partial(_pool_prompt_kernel, tt=tt),
        grid=(batch, seq // tt),
        in_specs=[pl.BlockSpec((None, tt, pw), lambda b, j: (b, j, 0)),
                  pl.BlockSpec(pool_w_b.shape, lambda b, j: (0, 0, 0)),
                  pl.BlockSpec(pool_scale.shape, lambda b, j: (0, 0))],
        out_specs=pl.BlockSpec((None, tt, pw), lambda b, j: (b, j, 0)),
        out_shape=jax.ShapeDtypeStruct((batch, seq, pw), BF16),
        scratch_shapes=[pltpu.VMEM((POOL_HALO, pw), F32)],
        compiler_params=_cparams("arbitrary", "arbitrary"),
        name="pool_prompt",
    )(u3, pool_w_b, pool_scale)
    return out.reshape(batch * seq, pw)


def _pool_sample_kernel(sp_ref, u_ref, pw_ref, ps_ref, o_ref, np_ref, *, n_new, n_buf):
    rows = [sp_ref[:, r, :] for r in range(n_buf)] + [u_ref[:, r, :] for r in range(n_new)]
    for r in range(n_buf):
        np_ref[:, r, :] = rows[n_new + r]
    gw = rows[0].shape[1] // len(POOL_WINDOWS)
    for t in range(n_new):
        i = n_buf + t
        for g, w in enumerate(POOL_WINDOWS):
            cols = slice(g * gw, (g + 1) * gw)
            s = rows[i][:, cols]
            for back in range(1, w):
                s = s + rows[i - back][:, cols]
            out = _pool_group(s, rows[i][:, cols], float(w), pw_ref[g], ps_ref[:, cols])
            o_ref[:, t, cols] = out


def _pool_sample(state_pool, u3, pool_w_b, pool_scale):
    bs, n_buf, pw = state_pool.shape
    n_new = u3.shape[1]
    full = lambda a: pl.BlockSpec(a.shape, lambda i: (0,) * a.ndim)
    return pl.pallas_call(
        functools.partial(_pool_sample_kernel, n_new=n_new, n_buf=n_buf),
        grid=(1,),
        in_specs=[full(state_pool), full(u3), full(pool_w_b), full(pool_scale)],
        out_specs=[pl.BlockSpec((bs, n_new, pw), lambda i: (0, 0, 0)),
                   pl.BlockSpec((bs, n_buf, pw), lambda i: (0, 0, 0))],
        out_shape=[jax.ShapeDtypeStruct((bs, n_new, pw), F32),
                   jax.ShapeDtypeStruct((bs, n_buf, pw), F32)],
        compiler_params=_cparams("arbitrary"),
        name="pool_sample",
    )(state_pool, u3, pool_w_b, pool_scale)


def _lambda_full(lq1, lk1, lq2, lk2, lam_init):
    e1 = jnp.exp(jnp.sum(lq1 * lk1, axis=-1, keepdims=True))
    e2 = jnp.exp(jnp.sum(lq2 * lk2, axis=-1, keepdims=True))
    return e1 - e2 + lam_init


def _split_halves(q):
    lane = lax.broadcasted_iota(I32, q.shape, 1)
    zero = jnp.zeros_like(q)
    return jnp.concatenate([jnp.where(lane < HEAD_DIM, q, zero), jnp.where(lane >= HEAD_DIM, q, zero)], axis=0)


def _subln(o, g, lam_init):
    return _rms(o, g) * (1.0 - lam_init)


def _attn_prompt_kernel(slopes_ref, q_ref, k_ref, vt_ref, lq1_ref, lk1_ref, lq2_ref, lk2_ref, g_ref, o_ref,
                        *, tq, lam_init, heads):
    hg = pl.program_id(1)
    i = pl.program_id(2)
    lam = _lambda_full(lq1_ref[...], lk1_ref[...], lq2_ref[...], lk2_ref[...], lam_init)
    head_cols = [slice(hh * HEAD_W, (hh + 1) * HEAD_W) for hh in range(heads)]
    row = lax.broadcasted_iota(I32, (tq, LANES), 0)
    lane = lax.broadcasted_iota(I32, (tq, LANES), 1)
    pos_feat = jnp.where(lane < 2 * N_SPLIT, jnp.where(lane % 2 == 0, row & ~(POS_LOW - 1), row & (POS_LOW - 1)), 0)
    pos_feat = pos_feat.astype(F32).astype(BF16)
    qlane = lax.broadcasted_iota(I32, (2 * tq, LANES), 1)
    slopes2, qqs = [], []
    for hh in range(heads):
        s2 = slopes_ref[hg * heads + hh] * LOG2E
        slopes2.append(s2)
        rest = jnp.full((2 * tq, LANES), s2, F32)
        feat = jnp.zeros((2 * tq, LANES), F32)
        for part in range(N_SPLIT):
            piece = rest.astype(BF16).astype(F32)
            rest = rest - piece
            feat = jnp.where(qlane // 2 == part, piece, feat)
        qq = _split_halves((q_ref[:, head_cols[hh]] * LOG2E).astype(BF16))
        qqs.append(jnp.concatenate([qq, feat.astype(BF16)], axis=1))
    ones = jnp.ones((BF16_ROWS, tq), BF16)

    def head_step(hh, j, carry, diag):
        m, acc = carry
        start = pl.multiple_of(j * tq, tq)
        keys = jnp.concatenate([k_ref[pl.ds(start, tq), head_cols[hh]], pos_feat], axis=1)
        st = _dot_nt(keys, qqs[hh])
        if diag:
            kr = lax.broadcasted_iota(I32, st.shape, 0)
            qc = lax.broadcasted_iota(I32, st.shape, 1)
            qc = jnp.where(qc >= tq, qc - tq, qc)
            st = jnp.where(qc >= kr, st, NEG_INF)
        offset = slopes2[hh] * ((j - i) * tq).astype(F32)
        m_new = jnp.maximum(m, jnp.max(st, axis=0, keepdims=True) + offset)
        alpha = jnp.exp2(m - m_new)
        p = jnp.exp2(st - (m_new - offset)).astype(BF16)
        values = jnp.concatenate([vt_ref[j, head_cols[hh], :], ones], axis=0)
        acc = alpha * acc + _dot(values, p)
        return m_new, acc

    def step(j, carries, diag):
        return tuple(head_step(hh, j, carries[hh], diag) for hh in range(heads))

    init = (jnp.full((1, 2 * tq), NEG_INF, F32), jnp.zeros((HEAD_W + BF16_ROWS, 2 * tq), F32))
    carries = lax.fori_loop(0, i, lambda j, cr: step(j, cr, False), (init,) * heads)
    carries = step(i, carries, True)
    for hh, (_, acc) in enumerate(carries):
        o = acc[:HEAD_W] / acc[HEAD_W:HEAD_W + 1]
        o = (o[:, :tq] - lam * o[:, tq:]).T
        o_ref[:, head_cols[hh]] = _subln(o, g_ref[...], lam_init).astype(o_ref.dtype)


def _attn_prompt(q, kb, vt, slopes, lams, subln_g, batch, seq, n_heads, lam_init, tq):
    heads = math.gcd(n_heads, ATTN_HEADS)
    width = heads * HEAD_W
    q3, k3 = (a.reshape(batch, seq, n_heads * HEAD_W) for a in (q, kb))
    small = lambda a: pl.BlockSpec(a.shape, lambda b, h, i, s: (0,) * a.ndim)
    grid_spec = pltpu.PrefetchScalarGridSpec(
        num_scalar_prefetch=1,
        grid=(batch, n_heads // heads, seq // tq),
        in_specs=[pl.BlockSpec((None, tq, width), lambda b, h, i, s: (b, i, h)),
                  pl.BlockSpec((None, seq, width), lambda b, h, i, s: (b, 0, h)),
                  pl.BlockSpec((seq // tq, width, tq), lambda b, h, i, s: (b, h, 0)),
                  *[small(a) for a in lams], small(subln_g)],
        out_specs=pl.BlockSpec((None, tq, width), lambda b, h, i, s: (b, i, h)),
    )
    out = pl.pallas_call(
        functools.partial(_attn_prompt_kernel, tq=tq, lam_init=lam_init, heads=heads),
        grid_spec=grid_spec,
        out_shape=jax.ShapeDtypeStruct((batch, seq, n_heads * HEAD_W), BF16),
        compiler_params=_cparams("parallel", "parallel", "arbitrary"),
        name="attn_prompt",
    )(slopes, q3, k3, vt, *lams, subln_g)
    return out.reshape(batch * seq, n_heads * HEAD_W)


def _attn_sample_kernel(pt_ref, slopes_ref, q_ref, kn_ref, vn_ref, kc_ref, vc_ref,
                        lq1_ref, lk1_ref, lq2_ref, lk2_ref, g_ref, o_ref, kbuf, vbuf, sem,
                        *, n_pages, page_rows, n_heads, n_new, lam_init):
    b = pl.program_id(0)
    nb = pl.num_programs(0)
    slot = b % 2
    past = n_pages * page_rows // n_heads

    def page_copies(seq, sl, pg):
        p = pt_ref[seq * n_pages + pg]
        dst = pl.ds(pg * page_rows, page_rows)
        return (pltpu.make_async_copy(kc_ref.at[p], kbuf.at[sl, dst, :], sem.at[sl, 0]),
                pltpu.make_async_copy(vc_ref.at[p], vbuf.at[sl, dst, :], sem.at[sl, 1]))

    def start_seq(seq, sl):
        for pg in range(n_pages):
            for cp in page_copies(seq, sl, pg):
                cp.start()

    @pl.when(b == 0)
    def _():
        start_seq(0, 0)

    @pl.when(b + 1 < nb)
    def _():
        start_seq(b + 1, 1 - slot)

    for pg in range(n_pages):
        for cp in page_copies(b, slot, pg):
            cp.wait()

    lam = _lambda_full(lq1_ref[...], lk1_ref[...], lq2_ref[...], lk2_ref[...], lam_init)
    rows = 2 * SUBLANES
    r = lax.broadcasted_iota(I32, (rows, past), 0)
    t_past = jnp.where(r >= n_new, r - n_new, r)
    dist_past = (past + t_past - lax.broadcasted_iota(I32, (rows, past), 1)).astype(F32)
    r1 = lax.broadcasted_iota(I32, (rows, 1), 0)
    t_new = jnp.where(r1 >= n_new, r1 - n_new, r1)
    for h in range(n_heads):
        slope = slopes_ref[h]
        cols = slice(h * HEAD_W, (h + 1) * HEAD_W)
        q = q_ref[:, cols]
        qq = jnp.concatenate([_split_halves(q), jnp.zeros((rows - 2 * n_new, HEAD_W), F32)], axis=0)
        kh = kbuf[slot, pl.ds(h, past, stride=n_heads), :].astype(BF16)
        vh = vbuf[slot, pl.ds(h, past, stride=n_heads), :].astype(BF16)
        s = _dot_nt(qq.astype(BF16), kh) - slope * dist_past
        kn = kn_ref[:, cols]
        vn = vn_ref[:, cols]
        s_new = []
        for c in range(n_new):
            sc = jnp.sum(qq * kn[c:c + 1, :], axis=-1, keepdims=True) - slope * (t_new - c).astype(F32)
            s_new.append(jnp.where(t_new >= c, sc, NEG_INF))
        m = jnp.max(s, axis=-1, keepdims=True)
        for sc in s_new:
            m = jnp.maximum(m, sc)
        p = jnp.exp(s - m)
        l = jnp.sum(p, axis=-1, keepdims=True)
        acc = _dot(p.astype(BF16), vh)
        for c, sc in enumerate(s_new):
            pc = jnp.exp(sc - m)
            l = l + pc
            acc = acc + pc * vn[c:c + 1, :]
        o = acc / l
        o = o[0:n_new] - lam * o[n_new:2 * n_new]
        o_ref[:, cols] = _subln(o, g_ref[...], lam_init)


def _attn_sample(q3, kn3, vn3, kc, vc, page_table, slopes, lams, subln_g, n_heads, lam_init):
    bs, n_new, qk_w = q3.shape
    n_pages = page_table.shape[1]
    page_rows = kc.shape[1]
    small = lambda a: pl.BlockSpec(a.shape, lambda b, pt, s: (0,) * a.ndim)
    tok = pl.BlockSpec((None, n_new, qk_w), lambda b, pt, s: (b, 0, 0))
    grid_spec = pltpu.PrefetchScalarGridSpec(
        num_scalar_prefetch=2,
        grid=(bs,),
        in_specs=[tok, tok, tok, pl.BlockSpec(memory_space=pl.ANY), pl.BlockSpec(memory_space=pl.ANY),
                  *[small(a) for a in lams], small(subln_g)],
        out_specs=tok,
        scratch_shapes=[pltpu.VMEM((2, n_pages * page_rows, LANES), F32),
                        pltpu.VMEM((2, n_pages * page_rows, LANES), F32),
                        pltpu.SemaphoreType.DMA((2, 2))],
    )
    return pl.pallas_call(
        functools.partial(_attn_sample_kernel, n_pages=n_pages, page_rows=page_rows, n_heads=n_heads,
                          n_new=n_new, lam_init=lam_init),
        grid_spec=grid_spec,
        out_shape=jax.ShapeDtypeStruct((bs, n_new, qk_w), F32),
        compiler_params=_cparams("arbitrary"),
        name="attn_sample",
    )(page_table.reshape(-1), slopes, q3, kn3, vn3, kc, vc, *lams, subln_g)


def _outproj_kernel(pool_p_ref, o_p_ref, x_p_ref, pool_s_ref, o_s_ref, x_s_ref, wo_ref, g2_ref, rw_ref, rb_ref,
                    h_ref, hn_ref, code_ref, gate_ref, cnt_out_ref, run_ref, *, tm, n_experts, prompt_tiles):
    i = pl.program_id(0)

    @pl.when(i == 0)
    def _():
        run_ref[...] = jnp.zeros_like(run_ref)

    tile = functools.partial(_outproj_tile, wo_ref=wo_ref, g2_ref=g2_ref, rw_ref=rw_ref, rb_ref=rb_ref, h_ref=h_ref,
                             hn_ref=hn_ref, code_ref=code_ref, gate_ref=gate_ref,
                             cnt_out_ref=cnt_out_ref, run_ref=run_ref, tm=tm, n_experts=n_experts)
    pl.when(i < prompt_tiles)(functools.partial(tile, pool_p_ref, o_p_ref, x_p_ref))
    pl.when(i >= prompt_tiles)(functools.partial(tile, pool_s_ref, o_s_ref, x_s_ref))


def _outproj_tile(pool_ref, o_ref, x_ref, *, wo_ref, g2_ref, rw_ref, rb_ref, h_ref, hn_ref, code_ref, gate_ref,
                  cnt_out_ref, run_ref, tm, n_experts):
    pw = pool_ref.shape[1]
    mix = _dot(pool_ref[...].astype(BF16), wo_ref[0:pw, :]) + _dot(o_ref[...].astype(BF16), wo_ref[pw:, :])
    h = x_ref[...] + mix
    h_ref[...] = h
    hn = _rms(h, g2_ref[...])
    for s in range(hn.shape[1] // LANES):
        hn_ref[pl.ds(s, tm, stride=SUBLANES), :] = hn[:, s * LANES:(s + 1) * LANES]
    logits = _dot(hn.astype(BF16), rw_ref[...]) + rb_ref[...]
    lane = lax.broadcasted_iota(I32, logits.shape, 1)
    vals, idxs = [], []
    for _ in range(TOP_K):
        m = jnp.max(logits, axis=-1, keepdims=True)
        idx = jnp.min(jnp.where(logits == m, lane, n_experts), axis=-1, keepdims=True)
        vals.append(m)
        idxs.append(idx)
        logits = jnp.where(lane == idx, -jnp.inf, logits)
    ex = [jnp.exp(v - vals[0]) for v in vals]
    den = ex[0]
    for e in ex[1:]:
        den = den + e
    chosen = jnp.zeros(logits.shape, F32)
    for idx in idxs:
        chosen = chosen + (lane == idx).astype(F32)
    row = lax.broadcasted_iota(I32, (tm, tm), 0)
    col = lax.broadcasted_iota(I32, (tm, tm), 1)
    before = _dot((col < row).astype(BF16), chosen.astype(BF16)) + run_ref[...]
    for k in range(TOP_K):
        gate_ref[:, k:k + 1] = ex[k] / den
        rank = jnp.sum(jnp.where(lane == idxs[k], before, 0.0), axis=-1, keepdims=True).astype(I32)
        code_ref[:, k:k + 1] = idxs[k] * (1 << RANK_BITS) + rank
    run_ref[...] = run_ref[...] + jnp.sum(chosen, axis=0, keepdims=True)
    cnt_out_ref[...] = run_ref[...]


def _outproj(prompt, sample, w_out_b, g2, rw_b, rb):
    np_, d = prompt[2].shape
    ns_ = sample[2].shape[0]
    n_total = np_ + ns_
    n_experts = rw_b.shape[1]
    tm = min(ROW_TILE, math.gcd(np_, ns_))
    prompt_tiles = np_ // tm
    assert d == SUBLANES * LANES and n_total < (1 << RANK_BITS)
    first = lambda a: pl.BlockSpec((tm, a.shape[1]), lambda i: (jnp.minimum(i, prompt_tiles - 1), 0))
    second = lambda a: pl.BlockSpec((tm, a.shape[1]), lambda i: (jnp.maximum(i - prompt_tiles, 0), 0))
    row = lambda w: pl.BlockSpec((tm, w), lambda i: (i, 0))
    full = lambda a: pl.BlockSpec(a.shape, lambda i: (0,) * a.ndim)
    counts = jax.ShapeDtypeStruct((1, n_experts), F32)
    return pl.pallas_call(
        functools.partial(_outproj_kernel, tm=tm, n_experts=n_experts, prompt_tiles=prompt_tiles),
        grid=(n_total // tm,),
        in_specs=[*[first(a) for a in prompt], *[second(a) for a in sample],
                  full(w_out_b), full(g2), full(rw_b), full(rb)],
        out_specs=[row(d), pl.BlockSpec((tm * SUBLANES, LANES), lambda i: (i, 0)),
                   row(TOP_K), row(TOP_K), pl.BlockSpec(counts.shape, lambda i: (0, 0))],
        out_shape=[jax.ShapeDtypeStruct((n_total, d), F32), jax.ShapeDtypeStruct((n_total * SUBLANES, LANES), F32),
                   jax.ShapeDtypeStruct((n_total, TOP_K), I32), jax.ShapeDtypeStruct((n_total, TOP_K), F32), counts],
        scratch_shapes=[pltpu.VMEM(counts.shape, F32)],
        compiler_params=_cparams("arbitrary"),
        name="outproj_router",
    )(*prompt, *sample, w_out_b, g2, rw_b, rb)


def _sorted_row(code_ref, start_ref, a):
    code = code_ref[a]
    return start_ref[code >> RANK_BITS] + (code & ((1 << RANK_BITS) - 1))


def _dispatch_kernel(code_ref, start_ref, zlo_ref, zhi_ref, nu_ref, hn_ref, xs_ref, row_ref, zero_ref, sem, *, tm,
                     rows, n_experts, n_blocks):
    i = pl.program_id(0)
    tile = tm * SUBLANES
    block = rows * SUBLANES

    def pad_pieces(e):
        lo, n = zlo_ref[e], zhi_ref[e] - zlo_ref[e]
        for bit in range(rows.bit_length() - 1):
            size = 1 << bit
            below = n & (size - 1)
            start = pl.multiple_of((lo + below) * SUBLANES, SUBLANES)
            cp = pltpu.make_async_copy(zero_ref.at[pl.ds(0, size * SUBLANES), :],
                                       xs_ref.at[pl.ds(start, size * SUBLANES), :], sem.at[1])
            yield (n & size) != 0, cp

    def tail_copy(blk):
        start = pl.multiple_of(blk * block, block)
        return pltpu.make_async_copy(zero_ref, xs_ref.at[pl.ds(start, block), :], sem.at[1])

    def for_zero_copies(fn):
        def per_expert(e, carry):
            for used, cp in pad_pieces(e):
                pl.when(used)(functools.partial(fn, cp))
            return carry
        lax.fori_loop(0, n_experts, per_expert, 0)

        def per_block(blk, carry):
            fn(tail_copy(blk))
            return carry
        lax.fori_loop(nu_ref[0], n_blocks, per_block, 0)

    @pl.when(i == 0)
    def _():
        zero_ref[...] = jnp.zeros_like(zero_ref)
        for_zero_copies(lambda cp: cp.start())

    def body(t, carry):
        src = pl.multiple_of(t * SUBLANES, SUBLANES)
        for k in range(TOP_K):
            a = (i * tm + t) * TOP_K + k
            row = _sorted_row(code_ref, start_ref, a)
            row_ref[a] = row
            dst = pl.multiple_of(row * SUBLANES, SUBLANES)
            pltpu.make_async_copy(hn_ref.at[pl.ds(src, SUBLANES), :], xs_ref.at[pl.ds(dst, SUBLANES), :],
                                  sem.at[0]).start()
        return carry
    lax.fori_loop(0, tm, body, 0)

    for _ in range(TOP_K):
        pltpu.make_async_copy(hn_ref, xs_ref.at[pl.ds(0, tile), :], sem.at[0]).wait()

    @pl.when(i == pl.num_programs(0) - 1)
    def _():
        for_zero_copies(lambda cp: cp.wait())


def _dispatch(code, starts, zlo, zhi, n_used, hn, n_blocks):
    n_total = hn.shape[0] // SUBLANES
    tm = math.gcd(n_total, ROW_TILE)
    rows = EXPERT_ROWS
    assert rows & (rows - 1) == 0
    grid_spec = pltpu.PrefetchScalarGridSpec(
        num_scalar_prefetch=5,
        grid=(n_total // tm,),
        in_specs=[pl.BlockSpec((tm * SUBLANES, LANES), lambda i, *_: (i, 0))],
        out_specs=[pl.BlockSpec(memory_space=pl.ANY), pl.BlockSpec(memory_space=pltpu.SMEM)],
        scratch_shapes=[pltpu.VMEM((rows * SUBLANES, LANES), F32), pltpu.SemaphoreType.DMA((2,))],
    )
    return pl.pallas_call(
        functools.partial(_dispatch_kernel, tm=tm, rows=rows, n_experts=zlo.shape[0], n_blocks=n_blocks),
        grid_spec=grid_spec,
        out_shape=[jax.ShapeDtypeStruct((n_blocks * rows * SUBLANES, LANES), F32),
                   jax.ShapeDtypeStruct(code.shape, I32)],
        compiler_params=_cparams("arbitrary"),
        name="dispatch",
    )(code, starts, zlo, zhi, n_used, hn)


MXU_DIM = 256


def _split_w1_kernel(w_ref, g_ref, l_ref):
    r = lax.broadcasted_iota(I32, (MXU_DIM, MXU_DIM), 0)
    c = lax.broadcasted_iota(I32, (MXU_DIM, MXU_DIM), 1)
    half = MXU_DIM // 2
    source = jnp.where(c < half, 2 * c, 2 * (c - half) + 1)
    perm = jnp.where(r == source, 1.0, 0.0).astype(BF16)
    for blk in range(w_ref.shape[1] // MXU_DIM):
        res = _dot(w_ref[:, blk * MXU_DIM:(blk + 1) * MXU_DIM].astype(BF16), perm)
        g_ref[:, blk * half:(blk + 1) * half] = res[:, :half].astype(BF16)
        l_ref[:, blk * half:(blk + 1) * half] = res[:, half:].astype(BF16)


def _split_w1(w1):
    n_exp, d, ff2 = w1.shape
    tr = min(ROW_TILE, d)
    half = pl.BlockSpec((None, tr, ff2 // 2), lambda e, i: (e, i, 0))
    return pl.pallas_call(
        _split_w1_kernel,
        grid=(n_exp, d // tr),
        in_specs=[pl.BlockSpec((None, tr, ff2), lambda e, i: (e, i, 0))],
        out_specs=[half, half],
        out_shape=[jax.ShapeDtypeStruct((n_exp, d, ff2 // 2), BF16)] * 2,
        compiler_params=_cparams("parallel", "parallel"),
        name="split_w1",
    )(w1)


def _expert_kernel(be_ref, nu_ref, xs_ref, w1g_ref, w1l_ref, b1g_ref, b1l_ref, w2_ref, b2_ref, y_ref, *, rows):
    j = pl.program_id(0)
    n_used = nu_ref[0]

    @pl.when(j < n_used)
    def _():
        x = jnp.concatenate([xs_ref[pl.ds(s, rows, stride=SUBLANES), :] for s in range(SUBLANES)],
                            axis=-1).astype(BF16)
        glu = jnp.minimum(_dot(x, w1g_ref[...]) + b1g_ref[...], SWIGLU_LIMIT)
        lin = jnp.clip(_dot(x, w1l_ref[...]) + b1l_ref[...], -SWIGLU_LIMIT, SWIGLU_LIMIT)
        act = glu * jax.nn.sigmoid(SWIGLU_ALPHA * glu) * (lin + 1.0)
        y = _dot(act.astype(BF16), w2_ref[...].astype(BF16)) + b2_ref[...]
        for s in range(SUBLANES):
            y_ref[pl.ds(s, rows, stride=SUBLANES), :] = y[:, s * LANES:(s + 1) * LANES]

    @pl.when(j >= n_used)
    def _():
        y_ref[...] = jnp.zeros_like(y_ref)


def _experts(block_e, n_used, xs, w1g, w1l, b1g, b1l, w2, b2, n_blocks):
    rows = EXPERT_ROWS
    tile = rows * SUBLANES
    wspec = lambda a: pl.BlockSpec((None,) + a.shape[1:], lambda j, be, nu: (be[j], 0, 0))
    grid_spec = pltpu.PrefetchScalarGridSpec(
        num_scalar_prefetch=2,
        grid=(n_blocks,),
        in_specs=[pl.BlockSpec((tile, LANES), lambda j, be, nu: (jnp.minimum(j, nu[0] - 1), 0)),
                  wspec(w1g), wspec(w1l), wspec(b1g), wspec(b1l), wspec(w2), wspec(b2)],
        out_specs=pl.BlockSpec((tile, LANES), lambda j, be, nu: (j, 0)),
    )
    return pl.pallas_call(
        functools.partial(_expert_kernel, rows=rows),
        grid_spec=grid_spec,
        out_shape=jax.ShapeDtypeStruct((n_blocks * tile, LANES), F32),
        compiler_params=_cparams("arbitrary"),
        name="experts",
    )(block_e, n_used, xs, w1g, w1l, b1g, b1l, w2, b2)


def _combine_kernel(row_ref, yb_ref, gate_ref, h_ref, g_ref, yp_ref, ys_ref, cbuf, sem, *, tm, prompt_tiles):
    i = pl.program_id(0)
    n = pl.num_programs(0)
    slot = i % 2
    tile = tm * SUBLANES

    def gather(blk, sl):
        def body(t, carry):
            for k in range(TOP_K):
                src = pl.multiple_of(row_ref[(blk * tm + t) * TOP_K + k] * SUBLANES, SUBLANES)
                dst = pl.multiple_of((k * tm + t) * SUBLANES, SUBLANES)
                pltpu.make_async_copy(yb_ref.at[pl.ds(src, SUBLANES), :], cbuf.at[sl, pl.ds(dst, SUBLANES), :],
                                      sem.at[sl]).start()
            return carry
        lax.fori_loop(0, tm, body, 0)

    @pl.when(i == 0)
    def _():
        gather(0, 0)

    @pl.when(i + 1 < n)
    def _():
        gather(i + 1, 1 - slot)

    pltpu.make_async_copy(yb_ref.at[pl.ds(0, TOP_K * tile), :], cbuf.at[slot], sem.at[slot]).wait()
    gates = gate_ref[...]
    slabs = []
    for s in range(SUBLANES):
        acc = None
        for k in range(TOP_K):
            part = gates[:, k:k + 1] * cbuf[slot, pl.ds(k * tile + s, tm, stride=SUBLANES), :]
            acc = part if acc is None else acc + part
        slabs.append(acc)
    hf = h_ref[...] + jnp.concatenate(slabs, axis=-1)
    y = _rms(hf, g_ref[...])

    @pl.when(i < prompt_tiles)
    def _():
        yp_ref[...] = y

    @pl.when(i >= prompt_tiles)
    def _():
        ys_ref[...] = y


def _combine(rows, yb, gates, h, gf, np_):
    n, d = h.shape
    ns_ = n - np_
    tm = min(COMBINE_TILE, math.gcd(np_, ns_))
    prompt_tiles = np_ // tm
    row = lambda w: pl.BlockSpec((tm, w), lambda i, *_: (i, 0))
    grid_spec = pltpu.PrefetchScalarGridSpec(
        num_scalar_prefetch=1,
        grid=(n // tm,),
        in_specs=[pl.BlockSpec(memory_space=pl.ANY), row(TOP_K), row(d), pl.BlockSpec(gf.shape, lambda i, *_: (0, 0))],
        out_specs=[pl.BlockSpec((tm, d), lambda i, *_: (jnp.minimum(i, prompt_tiles - 1), 0)),
                   pl.BlockSpec((tm, d), lambda i, *_: (jnp.maximum(i - prompt_tiles, 0), 0))],
        scratch_shapes=[pltpu.VMEM((2, TOP_K * tm * SUBLANES, LANES), F32), pltpu.SemaphoreType.DMA((2,))],
    )
    return pl.pallas_call(
        functools.partial(_combine_kernel, tm=tm, prompt_tiles=prompt_tiles),
        grid_spec=grid_spec,
        out_shape=[jax.ShapeDtypeStruct((np_, d), F32), jax.ShapeDtypeStruct((ns_, d), F32)],
        compiler_params=_cparams("arbitrary"),
        name="combine_norm",
    )(rows, yb, gates, h, gf)


def kernel(x_prompt, x_sample, state_pool, cache_k, cache_v, page_table, norm1_g, w_in, pool_w, pool_scale,
           lambda_q1, lambda_k1, lambda_q2, lambda_k2, subln_g, w_out, norm2_g, router_w, router_b,
           w1, b1, w2, b2, normf_g):
    batch, seq, d = x_prompt.shape
    bs, n_new, _ = x_sample.shape
    depth = w_in.shape[0]
    assert depth == 1, "single-layer trunk"
    n_heads = cache_k.shape[3]
    page_size = cache_k.shape[2]
    pw = pool_w.shape[1] * pool_w.shape[2]
    qk_w = n_heads * HEAD_W
    n_experts = router_w.shape[2]
    np_, ns_ = batch * seq, bs * n_new
    n_total = np_ + ns_
    lam_init = 0.8 - 0.6 * math.exp(-0.3 * 0)

    g1 = norm1_g[0][None]
    w_in_b = w_in[0].astype(BF16)
    wvt_b = w_in_b[:, pw + 2 * qk_w:].T
    pool_w_b = pool_w[0].astype(BF16)
    ps = pool_scale[0][None]
    lams = [a[0][None] for a in (lambda_q1, lambda_k1, lambda_q2, lambda_k2)]
    sg = subln_g[0][None]
    slopes = jnp.exp2(-8.0 * (jnp.arange(n_heads, dtype=F32) + 1.0) / n_heads)
    w_out_b = w_out[0].astype(BF16)
    g2 = norm2_g[0][None]
    rw_b = router_w[0].astype(BF16)
    rb = router_b[0][None]
    w1g, w1l = _split_w1(w1.reshape(w1.shape[1:]))
    b1g = b1[0][:, None, 0::2]
    b1l = b1[0][:, None, 1::2]
    w2b = w2.reshape(w2.shape[1:])
    b2e = b2[0][:, None, :]
    gf = normf_g[None]

    xp = x_prompt.reshape(np_, d)
    tq = min(ATTN_TILE, seq)
    u_p, q_p, k_p, v_p, kb_p, vt_p = _inproj(xp, g1, w_in_b, wvt_b, pw, qk_w, key_tile=tq)
    pool_p = _pool_prompt(u_p, pool_w_b, ps, batch, seq)
    o_p = _attn_prompt(q_p, kb_p, vt_p, slopes, lams, sg, batch, seq, n_heads, lam_init, tq)

    xs = x_sample.reshape(ns_, d)
    u_s, q_s, k_s, v_s = _inproj(xs, g1, w_in_b, wvt_b, pw, qk_w)
    pool_s, new_pool_s = _pool_sample(state_pool[0], u_s.reshape(bs, n_new, pw), pool_w_b, ps)
    kc = cache_k.reshape(cache_k.shape[1], page_size * n_heads, HEAD_W)
    vc = cache_v.reshape(cache_v.shape[1], page_size * n_heads, HEAD_W)
    o_s = _attn_sample(q_s.reshape(bs, n_new, qk_w), k_s.reshape(bs, n_new, qk_w), v_s.reshape(bs, n_new, qk_w),
                       kc, vc, page_table, slopes, lams, sg, n_heads, lam_init)

    h, hn, code, gates, cnt = _outproj((pool_p, o_p, xp), (pool_s.reshape(ns_, pw), o_s.reshape(ns_, qk_w), xs),
                                       w_out_b, g2, rw_b, rb)
    code = code.reshape(-1)

    counts = cnt[0].astype(I32)
    n_blocks = -(-n_total * TOP_K // EXPERT_ROWS) + n_experts
    padded = (counts + EXPERT_ROWS - 1) // EXPERT_ROWS * EXPERT_ROWS
    pad_ends = jnp.cumsum(padded)
    pad_starts = pad_ends - padded
    n_used = (pad_ends[-1:] // EXPERT_ROWS).astype(I32)
    block_start = jnp.arange(n_blocks, dtype=I32) * EXPERT_ROWS
    block_e = jnp.minimum(jnp.sum((pad_ends[None, :] <= block_start[:, None]).astype(I32), axis=1), n_experts - 1)

    xs_rows, sorted_rows = _dispatch(code, pad_starts, pad_starts + counts, pad_ends, n_used, hn, n_blocks)
    yb = _experts(block_e, n_used, xs_rows, w1g, w1l, b1g, b1l, w2b, b2e, n_blocks)
    y_p, y_s = _combine(sorted_rows, yb, gates, h, gf, np_)

    return (y_p.reshape(batch, seq, d), y_s.reshape(bs, n_new, d),
            u_p.reshape(batch, seq, pw)[None, :, seq - state_pool.shape[2]:],
            new_pool_s[None],
            k_p.reshape(1, batch, seq, n_heads, HEAD_W), v_p.reshape(1, batch, seq, n_heads, HEAD_W),
            k_s.reshape(1, bs, n_new, n_heads, HEAD_W), v_s.reshape(1, bs, n_new, n_heads, HEAD_W))
```

```python
import functools
import math

import jax
import jax.numpy as jnp
from jax import lax
from jax.experimental import pallas as pl
from jax.experimental.pallas import tpu as pltpu

F32 = jnp.float32
BF16 = jnp.bfloat16
I32 = jnp.int32

RMS_EPS = 1e-6
NEG_INF = -1e30
HEAD_DIM = 64
HEAD_W = 2 * HEAD_DIM
POOL_WINDOWS = (2, 4, 8, 16)
POOL_HALO = 16
TOP_K = 4
RANK_BITS = 20
SWIGLU_LIMIT = 7.0
SWIGLU_ALPHA = 1.702
LANES = 128
SUBLANES = 8
ROW_TILE = 512
ATTN_TILE = 512
ATTN_HEADS = 2
LOG2E = 1.4426950408889634
N_SPLIT = 3
POS_LOW = 16
BF16_ROWS = 16
EXPERT_ROWS = 512
COMBINE_TILE = 256
VMEM_LIMIT = 48 * 1024 * 1024


def _cparams(*sem):
    return pltpu.CompilerParams(dimension_semantics=sem, vmem_limit_bytes=VMEM_LIMIT)


def _dot(a, b):
    return jnp.dot(a, b, preferred_element_type=F32)


def _dot_nt(a, b):
    return lax.dot_general(a, b, (((1,), (1,)), ((), ())), preferred_element_type=F32)


def _rms(x, g):
    return x * lax.rsqrt(jnp.mean(x * x, axis=-1, keepdims=True) + RMS_EPS) * g


def _inproj_kernel(x_ref, g_ref, w_ref, wvt_ref, u_ref, q_ref, k_ref, v_ref, *attn_refs, pool_w, qk_w, key_tile):
    xb = _rms(x_ref[...], g_ref[...]).astype(BF16)
    c0, c1, c2 = pool_w, pool_w + qk_w, pool_w + 2 * qk_w
    u_ref[...] = _dot(xb, w_ref[:, 0:c0])
    q_ref[...] = _dot(xb, w_ref[:, c0:c1]) * (HEAD_DIM ** -0.5)
    k = _dot(xb, w_ref[:, c1:c2])
    v = _dot(xb, w_ref[:, c2:])
    for h in range(k_ref.shape[1]):
        k_ref[:, h, :] = k[:, h * HEAD_W:(h + 1) * HEAD_W]
        v_ref[:, h, :] = v[:, h * HEAD_W:(h + 1) * HEAD_W]
    if attn_refs:
        kb_ref, vt_ref = attn_refs
        kb_ref[...] = k.astype(BF16)
        vt = _dot_nt(wvt_ref[...], xb)
        for c in range(vt_ref.shape[0]):
            vt_ref[c] = vt[:, c * key_tile:(c + 1) * key_tile].astype(BF16)


def _inproj(x, g1, w_in_b, wvt_b, pool_w, qk_w, key_tile=None):
    n, d = x.shape
    val_w = wvt_b.shape[0]
    tm = min(ROW_TILE, n)
    row = lambda w: pl.BlockSpec((tm, w), lambda i: (i, 0))
    full = lambda a: pl.BlockSpec(a.shape, lambda i: (0,) * a.ndim)
    n_heads = qk_w // HEAD_W
    assert val_w == qk_w
    heads = pl.BlockSpec((tm, n_heads, HEAD_W), lambda i: (i, 0, 0))
    out_specs = [row(pool_w), row(qk_w), heads, heads]
    out_shape = [jax.ShapeDtypeStruct((n, pool_w), F32), jax.ShapeDtypeStruct((n, qk_w), F32),
                 jax.ShapeDtypeStruct((n, n_heads, HEAD_W), F32), jax.ShapeDtypeStruct((n, n_heads, HEAD_W), F32)]
    if key_tile is not None:
        assert tm % key_tile == 0
        out_specs += [row(qk_w), pl.BlockSpec((tm // key_tile, val_w, key_tile), lambda i: (i, 0, 0))]
        out_shape += [jax.ShapeDtypeStruct((n, qk_w), BF16), jax.ShapeDtypeStruct((n // key_tile, val_w, key_tile), BF16)]
    return pl.pallas_call(
        functools.partial(_inproj_kernel, pool_w=pool_w, qk_w=qk_w, key_tile=key_tile),
        grid=(n // tm,),
        in_specs=[row(d), full(g1), full(w_in_b), full(wvt_b)],
        out_specs=out_specs,
        out_shape=out_shape,
        compiler_params=_cparams("parallel"),
        name="inproj",
    )(x, g1, w_in_b, wvt_b)


def _pool_group(s_win, tok, cnt, pw, ps):
    d = s_win / cnt - tok
    return _dot(d.astype(BF16), pw) * ps


def _pool_prompt_kernel(u_ref, pw_ref, ps_ref, o_ref, carry_ref, *, tt):
    j = pl.program_id(1)

    @pl.when(j == 0)
    def _():
        carry_ref[...] = jnp.zeros_like(carry_ref)

    cur = u_ref[...]
    ext = jnp.concatenate([carry_ref[...], cur], axis=0)
    carry_ref[...] = cur[tt - POOL_HALO:, :]
    pos = (j * tt + lax.broadcasted_iota(I32, (tt, 1), 0)).astype(F32)
    gw = cur.shape[1] // len(POOL_WINDOWS)
    for g, w in enumerate(POOL_WINDOWS):
        cols = slice(g * gw, (g + 1) * gw)
        e = ext[:, cols]
        s, span = e, 1
        while span < w:
            s = s + pltpu.roll(s, span, axis=0)
            span *= 2
        cnt = jnp.minimum(float(w), pos + 1.0)
        out = _pool_group(s[POOL_HALO:, :], e[POOL_HALO:, :], cnt, pw_ref[g], ps_ref[:, cols])
        o_ref[:, cols] = out.astype(o_ref.dtype)


def _pool_prompt(u, pool_w_b, pool_scale, batch, seq):
    pw = u.shape[1]
    tt = min(ROW_TILE, seq)
    u3 = u.reshape(batch, seq, pw)
    out = pl.pallas_call(
        functools.partial(_pool_prompt_kernel, tt=tt),
        grid=(batch, seq // tt),
        in_specs=[pl.BlockSpec((None, tt, pw), lambda b, j: (b, j, 0)),
                  pl.BlockSpec(pool_w_b.shape, lambda b, j: (0, 0, 0)),
                  pl.BlockSpec(pool_scale.shape, lambda b, j: (0, 0))],
        out_specs=pl.BlockSpec((None, tt, pw), lambda b, j: (b, j, 0)),
        out_shape=jax.ShapeDtypeStruct((batch, seq, pw), BF16),
        scratch_shapes=[pltpu.VMEM((POOL_HALO, pw), F32)],
        compiler_params=_cparams("arbitrary", "arbitrary"),
        name="pool_prompt",
    )(u3, pool_w_b, pool_scale)
    return out.reshape(batch * seq, pw)


def _pool_sample_kernel(sp_ref, u_ref, pw_ref, ps_ref, o_ref, np_ref, *, n_new, n_buf):
    rows = [sp_ref[:, r, :] for r in range(n_buf)] + [u_ref[:, r, :] for r in range(n_new)]
    for r in range(n_buf):
        np_ref[:, r, :] = rows[n_new + r]
    gw = rows[0].shape[1] // len(POOL_WINDOWS)
    for t in range(n_new):
        i = n_buf + t
        for g, w in enumerate(POOL_WINDOWS):
            cols = slice(g * gw, (g + 1) * gw)
            s = rows[i][:, cols]
            for back in range(1, w):
                s = s + rows[i - back][:, cols]
            out = _pool_group(s, rows[i][:, cols], float(w), pw_ref[g], ps_ref[:, cols])
            o_ref[:, t, cols] = out


def _pool_sample(state_pool, u3, pool_w_b, pool_scale):
    bs, n_buf, pw = state_pool.shape
    n_new = u3.shape[1]
    full = lambda a: pl.BlockSpec(a.shape, lambda i: (0,) * a.ndim)
    return pl.pallas_call(
        functools.partial(_pool_sample_kernel, n_new=n_new, n_buf=n_buf),
        grid=(1,),
        in_specs=[full(state_pool), full(u3), full(pool_w_b), full(pool_scale)],
        out_specs=[pl.BlockSpec((bs, n_new, pw), lambda i: (0, 0, 0)),
                   pl.BlockSpec((bs, n_buf, pw), lambda i: (0, 0, 0))],
        out_shape=[jax.ShapeDtypeStruct((bs, n_new, pw), F32),
                   jax.ShapeDtypeStruct((bs, n_buf, pw), F32)],
        compiler_params=_cparams("arbitrary"),
        name="pool_sample",
    )(state_pool, u3, pool_w_b, pool_scale)


def _lambda_full(lq1, lk1, lq2, lk2, lam_init):
    e1 = jnp.exp(jnp.sum(lq1 * lk1, axis=-1, keepdims=True))
    e2 = jnp.exp(jnp.sum(lq2 * lk2, axis=-1, keepdims=True))
    return e1 - e2 + lam_init


def _split_halves(q):
    lane = lax.broadcasted_iota(I32, q.shape, 1)
    zero = jnp.zeros_like(q)
    return jnp.concatenate([jnp.where(lane < HEAD_DIM, q, zero), jnp.where(lane >= HEAD_DIM, q, zero)], axis=0)


def _subln(o, g, lam_init):
    return _rms(o, g) * (1.0 - lam_init)


def _attn_prompt_kernel(slopes_ref, q_ref, k_ref, vt_ref, lq1_ref, lk1_ref, lq2_ref, lk2_ref, g_ref, o_ref,
                        *, tq, lam_init, heads):
    hg = pl.program_id(1)
    i = pl.program_id(2)
    lam = _lambda_full(lq1_ref[...], lk1_ref[...], lq2_ref[...], lk2_ref[...], lam_init)
    head_cols = [slice(hh * HEAD_W, (hh + 1) * HEAD_W) for hh in range(heads)]
    row = lax.broadcasted_iota(I32, (tq, LANES), 0)
    lane = lax.broadcasted_iota(I32, (tq, LANES), 1)
    pos_feat = jnp.where(lane < 2 * N_SPLIT, jnp.where(lane % 2 == 0, row & ~(POS_LOW - 1), row & (POS_LOW - 1)), 0)
    pos_feat = pos_feat.astype(F32).astype(BF16)
    qlane = lax.broadcasted_iota(I32, (2 * tq, LANES), 1)
    slopes2, qqs = [], []
    for hh in range(heads):
        s2 = slopes_ref[hg * heads + hh] * LOG2E
        slopes2.append(s2)
        rest = jnp.full((2 * tq, LANES), s2, F32)
        feat = jnp.zeros((2 * tq, LANES), F32)
        for part in range(N_SPLIT):
            piece = rest.astype(BF16).astype(F32)
            rest = rest - piece
            feat = jnp.where(qlane // 2 == part, piece, feat)
        qq = _split_halves((q_ref[:, head_cols[hh]] * LOG2E).astype(BF16))
        qqs.append(jnp.concatenate([qq, feat.astype(BF16)], axis=1))
    ones = jnp.ones((BF16_ROWS, tq), BF16)

    def head_step(hh, j, carry, diag):
        m, acc = carry
        start = pl.multiple_of(j * tq, tq)
        keys = jnp.concatenate([k_ref[pl.ds(start, tq), head_cols[hh]], pos_feat], axis=1)
        st = _dot_nt(keys, qqs[hh])
        if diag:
            kr = lax.broadcasted_iota(I32, st.shape, 0)
            qc = lax.broadcasted_iota(I32, st.shape, 1)
            qc = jnp.where(qc >= tq, qc - tq, qc)
            st = jnp.where(qc >= kr, st, NEG_INF)
        offset = slopes2[hh] * ((j - i) * tq).astype(F32)
        m_new = jnp.maximum(m, jnp.max(st, axis=0, keepdims=True) + offset)
        alpha = jnp.exp2(m - m_new)
        p = jnp.exp2(st - (m_new - offset)).astype(BF16)
        values = jnp.concatenate([vt_ref[j, head_cols[hh], :], ones], axis=0)
        acc = alpha * acc + _dot(values, p)
        return m_new, acc

    def step(j, carries, diag):
        return tuple(head_step(hh, j, carries[hh], diag) for hh in range(heads))

    init = (jnp.full((1, 2 * tq), NEG_INF, F32), jnp.zeros((HEAD_W + BF16_ROWS, 2 * tq), F32))
    carries = lax.fori_loop(0, i, lambda j, cr: step(j, cr, False), (init,) * heads)
    carries = step(i, carries, True)
    for hh, (_, acc) in enumerate(carries):
        o = acc[:HEAD_W] / acc[HEAD_W:HEAD_W + 1]
        o = (o[:, :tq] - lam * o[:, tq:]).T
        o_ref[:, head_cols[hh]] = _subln(o, g_ref[...], lam_init).astype(o_ref.dtype)


def _attn_prompt(q, kb, vt, slopes, lams, subln_g, batch, seq, n_heads, lam_init, tq):
    heads = math.gcd(n_heads, ATTN_HEADS)
    width = heads * HEAD_W
    q3, k3 = (a.reshape(batch, seq, n_heads * HEAD_W) for a in (q, kb))
    small = lambda a: pl.BlockSpec(a.shape, lambda b, h, i, s: (0,) * a.ndim)
    grid_spec = pltpu.PrefetchScalarGridSpec(
        num_scalar_prefetch=1,
        grid=(batch, n_heads // heads, seq // tq),
        in_specs=[pl.BlockSpec((None, tq, width), lambda b, h, i, s: (b, i, h)),
                  pl.BlockSpec((None, seq, width), lambda b, h, i, s: (b, 0, h)),
                  pl.BlockSpec((seq // tq, width, tq), lambda b, h, i, s: (b, h, 0)),
                  *[small(a) for a in lams], small(subln_g)],
        out_specs=pl.BlockSpec((None, tq, width), lambda b, h, i, s: (b, i, h)),
    )
    out = pl.pallas_call(
        functools.partial(_attn_prompt_kernel, tq=tq, lam_init=lam_init, heads=heads),
        grid_spec=grid_spec,
        out_shape=jax.ShapeDtypeStruct((batch, seq, n_heads * HEAD_W), BF16),
        compiler_params=_cparams("parallel", "parallel", "arbitrary"),
        name="attn_prompt",
    )(slopes, q3, k3, vt, *lams, subln_g)
    return out.reshape(batch * seq, n_heads * HEAD_W)


def _attn_sample_kernel(pt_ref, slopes_ref, q_ref, kn_ref, vn_ref, kc_ref, vc_ref,
                        lq1_ref, lk1_ref, lq2_ref, lk2_ref, g_ref, o_ref, kbuf, vbuf, sem,
                        *, n_pages, page_rows, n_heads, n_new, lam_init):
    b = pl.program_id(0)
    nb = pl.num_programs(0)
    slot = b % 2
    past = n_pages * page_rows // n_heads

    def page_copies(seq, sl, pg):
        p = pt_ref[seq * n_pages + pg]
        dst = pl.ds(pg * page_rows, page_rows)
        return (pltpu.make_async_copy(kc_ref.at[p], kbuf.at[sl, dst, :], sem.at[sl, 0]),
                pltpu.make_async_copy(vc_ref.at[p], vbuf.at[sl, dst, :], sem.at[sl, 1]))

    def start_seq(seq, sl):
        for pg in range(n_pages):
            for cp in page_copies(seq, sl, pg):
                cp.start()

    @pl.when(b == 0)
    def _():
        start_seq(0, 0)

    @pl.when(b + 1 < nb)
    def _():
        start_seq(b + 1, 1 - slot)

    for pg in range(n_pages):
        for cp in page_copies(b, slot, pg):
            cp.wait()

    lam = _lambda_full(lq1_ref[...], lk1_ref[...], lq2_ref[...], lk2_ref[...], lam_init)
    rows = 2 * SUBLANES
    r = lax.broadcasted_iota(I32, (rows, past), 0)
    t_past = jnp.where(r >= n_new, r - n_new, r)
    dist_past = (past + t_past - lax.broadcasted_iota(I32, (rows, past), 1)).astype(F32)
    r1 = lax.broadcasted_iota(I32, (rows, 1), 0)
    t_new = jnp.where(r1 >= n_new, r1 - n_new, r1)
    for h in range(n_heads):
        slope = slopes_ref[h]
        cols = slice(h * HEAD_W, (h + 1) * HEAD_W)
        q = q_ref[:, cols]
        qq = jnp.concatenate([_split_halves(q), jnp.zeros((rows - 2 * n_new, HEAD_W), F32)], axis=0)
        kh = kbuf[slot, pl.ds(h, past, stride=n_heads), :].astype(BF16)
        vh = vbuf[slot, pl.ds(h, past, stride=n_heads), :].astype(BF16)
        s = _dot_nt(qq.astype(BF16), kh) - slope * dist_past
        kn = kn_ref[:, cols]
        vn = vn_ref[:, cols]
        s_new = []
        for c in range(n_new):
            sc = jnp.sum(qq * kn[c:c + 1, :], axis=-1, keepdims=True) - slope * (t_new - c).astype(F32)
            s_new.append(jnp.where(t_new >= c, sc, NEG_INF))
        m = jnp.max(s, axis=-1, keepdims=True)
        for sc in s_new:
            m = jnp.maximum(m, sc)
        p = jnp.exp(s - m)
        l = jnp.sum(p, axis=-1, keepdims=True)
        acc = _dot(p.astype(BF16), vh)
        for c, sc in enumerate(s_new):
            pc = jnp.exp(sc - m)
            l = l + pc
            acc = acc + pc * vn[c:c + 1, :]
        o = acc / l
        o = o[0:n_new] - lam * o[n_new:2 * n_new]
        o_ref[:, cols] = _subln(o, g_ref[...], lam_init)


def _attn_sample(q3, kn3, vn3, kc, vc, page_table, slopes, lams, subln_g, n_heads, lam_init):
    bs, n_new, qk_w = q3.shape
    n_pages = page_table.shape[1]
    page_rows = kc.shape[1]
    small = lambda a: pl.BlockSpec(a.shape, lambda b, pt, s: (0,) * a.ndim)
    tok = pl.BlockSpec((None, n_new, qk_w), lambda b, pt, s: (b, 0, 0))
    grid_spec = pltpu.PrefetchScalarGridSpec(
        num_scalar_prefetch=2,
        grid=(bs,),
        in_specs=[tok, tok, tok, pl.BlockSpec(memory_space=pl.ANY), pl.BlockSpec(memory_space=pl.ANY),
                  *[small(a) for a in lams], small(subln_g)],
        out_specs=tok,
        scratch_shapes=[pltpu.VMEM((2, n_pages * page_rows, LANES), F32),
                        pltpu.VMEM((2, n_pages * page_rows, LANES), F32),
                        pltpu.SemaphoreType.DMA((2, 2))],
    )
    return pl.pallas_call(
        functools.partial(_attn_sample_kernel, n_pages=n_pages, page_rows=page_rows, n_heads=n_heads,
                          n_new=n_new, lam_init=lam_init),
        grid_spec=grid_spec,
        out_shape=jax.ShapeDtypeStruct((bs, n_new, qk_w), F32),
        compiler_params=_cparams("arbitrary"),
        name="attn_sample",
    )(page_table.reshape(-1), slopes, q3, kn3, vn3, kc, vc, *lams, subln_g)


def _outproj_kernel(pool_p_ref, o_p_ref, x_p_ref, pool_s_ref, o_s_ref, x_s_ref, wo_ref, g2_ref, rwt_ref, rb_ref,
                    earlier_ref, h_ref, hn_ref, code_ref, gate_ref, cnt_out_ref, run_ref, *, tm, n_experts,
                    prompt_tiles):
    i = pl.program_id(0)

    @pl.when(i == 0)
    def _():
        run_ref[...] = jnp.zeros_like(run_ref)

    tile = functools.partial(_outproj_tile, wo_ref=wo_ref, g2_ref=g2_ref, rwt_ref=rwt_ref, rb_ref=rb_ref,
                             earlier_ref=earlier_ref, h_ref=h_ref, hn_ref=hn_ref, code_ref=code_ref, gate_ref=gate_ref,
                             cnt_out_ref=cnt_out_ref, run_ref=run_ref, tm=tm, n_experts=n_experts)
    pl.when(i < prompt_tiles)(functools.partial(tile, pool_p_ref, o_p_ref, x_p_ref))
    pl.when(i >= prompt_tiles)(functools.partial(tile, pool_s_ref, o_s_ref, x_s_ref))


def _outproj_tile(pool_ref, o_ref, x_ref, *, wo_ref, g2_ref, rwt_ref, rb_ref, earlier_ref, h_ref, hn_ref, code_ref,
                  gate_ref, cnt_out_ref, run_ref, tm, n_experts):
    pw = pool_ref.shape[1]
    mix = _dot(pool_ref[...].astype(BF16), wo_ref[0:pw, :]) + _dot(o_ref[...].astype(BF16), wo_ref[pw:, :])
    h = x_ref[...] + mix
    h_ref[...] = h
    hn = _rms(h, g2_ref[...])
    for s in range(hn.shape[1] // LANES):
        hn_ref[pl.ds(s, tm, stride=SUBLANES), :] = hn[:, s * LANES:(s + 1) * LANES]
    logits = _dot_nt(rwt_ref[...], hn.astype(BF16)) + rb_ref[...]
    expert = lax.broadcasted_iota(I32, logits.shape, 0)
    vals, idxs = [], []
    for _ in range(TOP_K):
        m = jnp.max(logits, axis=0, keepdims=True)
        idx = jnp.min(jnp.where(logits == m, expert, n_experts), axis=0, keepdims=True)
        vals.append(m)
        idxs.append(idx)
        logits = jnp.where(expert == idx, -jnp.inf, logits)
    ex = [jnp.exp(v - vals[0]) for v in vals]
    den = ex[0]
    for e in ex[1:]:
        den = den + e
    chosen = jnp.zeros(logits.shape, F32)
    for idx in idxs:
        chosen = chosen + jnp.where(expert == idx, 1.0, 0.0)
    before = _dot(chosen.astype(BF16), earlier_ref[...]) + run_ref[...]
    for k in range(TOP_K):
        gate_ref[k:k + 1, :] = ex[k] / den
        rank = jnp.sum(jnp.where(expert == idxs[k], before, 0.0), axis=0, keepdims=True).astype(I32)
        code_ref[k:k + 1, :] = idxs[k] * (1 << RANK_BITS) + rank
    gate_ref[TOP_K:, :] = jnp.zeros((gate_ref.shape[0] - TOP_K, tm), F32)
    run_ref[...] = run_ref[...] + jnp.sum(chosen, axis=1, keepdims=True)
    cnt_out_ref[...] = run_ref[...]


def _outproj(prompt, sample, w_out_b, g2, rwt_b, rb):
    np_, d = prompt[2].shape
    ns_ = sample[2].shape[0]
    n_total = np_ + ns_
    n_experts = rwt_b.shape[0]
    tm = min(ROW_TILE, math.gcd(np_, ns_))
    prompt_tiles = np_ // tm
    assert d == SUBLANES * LANES and n_total < (1 << RANK_BITS)
    first = lambda a: pl.BlockSpec((tm, a.shape[1]), lambda i: (jnp.minimum(i, prompt_tiles - 1), 0))
    second = lambda a: pl.BlockSpec((tm, a.shape[1]), lambda i: (jnp.maximum(i - prompt_tiles, 0), 0))
    row = lambda w: pl.BlockSpec((tm, w), lambda i: (i, 0))
    col = lambda r: pl.BlockSpec((r, tm), lambda i: (0, i))
    full = lambda a: pl.BlockSpec(a.shape, lambda i: (0,) * a.ndim)
    counts = jax.ShapeDtypeStruct((n_experts, 1), F32)
    earlier = jnp.triu(jnp.ones((tm, tm), BF16), 1)
    return pl.pallas_call(
        functools.partial(_outproj_kernel, tm=tm, n_experts=n_experts, prompt_tiles=prompt_tiles),
        grid=(n_total // tm,),
        in_specs=[*[first(a) for a in prompt], *[second(a) for a in sample],
                  full(w_out_b), full(g2), full(rwt_b), full(rb), full(earlier)],
        out_specs=[row(d), pl.BlockSpec((tm * SUBLANES, LANES), lambda i: (i, 0)),
                   col(TOP_K), col(SUBLANES), pl.BlockSpec(counts.shape, lambda i: (0, 0))],
        out_shape=[jax.ShapeDtypeStruct((n_total, d), F32), jax.ShapeDtypeStruct((n_total * SUBLANES, LANES), F32),
                   jax.ShapeDtypeStruct((TOP_K, n_total), I32), jax.ShapeDtypeStruct((SUBLANES, n_total), F32),
                   counts],
        scratch_shapes=[pltpu.VMEM(counts.shape, F32)],
        compiler_params=_cparams("arbitrary"),
        name="outproj_router",
    )(*prompt, *sample, w_out_b, g2, rwt_b, rb, earlier)


def _sorted_rows_kernel(start_ref, code_ref, rows_ref, *, n_experts):
    code = code_ref[...]
    expert = code >> RANK_BITS
    rows = code & ((1 << RANK_BITS) - 1)
    for e in range(n_experts):
        rows = rows + jnp.where(expert == e, start_ref[e], 0)
    rows_ref[...] = rows


def _sorted_rows(starts, code):
    return pl.pallas_call(
        functools.partial(_sorted_rows_kernel, n_experts=starts.shape[0]),
        grid=(1,),
        in_specs=[pl.BlockSpec(memory_space=pltpu.SMEM), pl.BlockSpec(code.shape, lambda i: (0, 0))],
        out_specs=pl.BlockSpec(code.shape, lambda i: (0, 0)),
        out_shape=jax.ShapeDtypeStruct(code.shape, I32),
        compiler_params=_cparams("arbitrary"),
        name="sorted_rows",
    )(starts, code)


MXU_DIM = 256


def _split_w1_tile(w_ref, g_ref, l_ref):
    r = lax.broadcasted_iota(I32, (MXU_DIM, MXU_DIM), 0)
    c = lax.broadcasted_iota(I32, (MXU_DIM, MXU_DIM), 1)
    half = MXU_DIM // 2
    source = jnp.where(c < half, 2 * c, 2 * (c - half) + 1)
    perm = jnp.where(r == source, 1.0, 0.0).astype(BF16)
    for blk in range(w_ref.shape[1] // MXU_DIM):
        res = _dot(w_ref[:, blk * MXU_DIM:(blk + 1) * MXU_DIM].astype(BF16), perm)
        g_ref[:, blk * half:(blk + 1) * half] = res[:, :half].astype(BF16)
        l_ref[:, blk * half:(blk + 1) * half] = res[:, half:].astype(BF16)


def _dispatch_kernel(row_ref, zlo_ref, zhi_ref, nu_ref, hn_ref, w1_ref, xs_ref, w1g_ref, w1l_ref, zero_ref, sem,
                     *, tm, n_total, rows, n_experts, n_blocks):
    i = pl.program_id(0)
    tile = tm * SUBLANES
    block = rows * SUBLANES

    def pad_pieces(e):
        lo, n = zlo_ref[e], zhi_ref[e] - zlo_ref[e]
        for bit in range(rows.bit_length() - 1):
            size = 1 << bit
            below = n & (size - 1)
            start = pl.multiple_of((lo + below) * SUBLANES, SUBLANES)
            cp = pltpu.make_async_copy(zero_ref.at[pl.ds(0, size * SUBLANES), :],
                                       xs_ref.at[pl.ds(start, size * SUBLANES), :], sem.at[1])
            yield (n & size) != 0, cp

    def tail_copy(blk):
        start = pl.multiple_of(blk * block, block)
        return pltpu.make_async_copy(zero_ref, xs_ref.at[pl.ds(start, block), :], sem.at[1])

    def for_zero_copies(fn):
        def per_expert(e, carry):
            for used, cp in pad_pieces(e):
                pl.when(used)(functools.partial(fn, cp))
            return carry
        lax.fori_loop(0, n_experts, per_expert, 0)

        def per_block(blk, carry):
            fn(tail_copy(blk))
            return carry
        lax.fori_loop(nu_ref[0], n_blocks, per_block, 0)

    @pl.when(i == 0)
    def _():
        zero_ref[...] = jnp.zeros_like(zero_ref)
        for_zero_copies(lambda cp: cp.start())

    def body(t, carry):
        src = pl.multiple_of(t * SUBLANES, SUBLANES)
        for k in range(TOP_K):
            dst = pl.multiple_of(row_ref[k * n_total + i * tm + t] * SUBLANES, SUBLANES)
            pltpu.make_async_copy(hn_ref.at[pl.ds(src, SUBLANES), :], xs_ref.at[pl.ds(dst, SUBLANES), :],
                                  sem.at[0]).start()
        return carry
    lax.fori_loop(0, tm, body, 0)

    _split_w1_tile(w1_ref, w1g_ref, w1l_ref)

    for _ in range(TOP_K):
        pltpu.make_async_copy(hn_ref, xs_ref.at[pl.ds(0, tile), :], sem.at[0]).wait()

    @pl.when(i == pl.num_programs(0) - 1)
    def _():
        for_zero_copies(lambda cp: cp.wait())


def _dispatch(sorted_rows, zlo, zhi, n_used, hn, w1, n_blocks):
    n_total = hn.shape[0] // SUBLANES
    n_exp, d, ff2 = w1.shape
    tr = next(t for t in (ROW_TILE, 2 * ROW_TILE, d) if d % t == 0 and n_total % (n_exp * (d // t)) == 0)
    row_tiles = d // tr
    steps = n_exp * row_tiles
    tm = n_total // steps
    rows = EXPERT_ROWS
    assert rows & (rows - 1) == 0
    half = pl.BlockSpec((None, tr, ff2 // 2), lambda i, *_: (i // row_tiles, i % row_tiles, 0))
    grid_spec = pltpu.PrefetchScalarGridSpec(
        num_scalar_prefetch=4,
        grid=(steps,),
        in_specs=[pl.BlockSpec((tm * SUBLANES, LANES), lambda i, *_: (i, 0)),
                  pl.BlockSpec((None, tr, ff2), lambda i, *_: (i // row_tiles, i % row_tiles, 0))],
        out_specs=[pl.BlockSpec(memory_space=pl.ANY), half, half],
        scratch_shapes=[pltpu.VMEM((rows * SUBLANES, LANES), F32), pltpu.SemaphoreType.DMA((2,))],
    )
    return pl.pallas_call(
        functools.partial(_dispatch_kernel, tm=tm, n_total=n_total, rows=rows, n_experts=n_exp, n_blocks=n_blocks),
        grid_spec=grid_spec,
        out_shape=[jax.ShapeDtypeStruct((n_blocks * rows * SUBLANES, LANES), F32),
                   jax.ShapeDtypeStruct((n_exp, d, ff2 // 2), BF16), jax.ShapeDtypeStruct((n_exp, d, ff2 // 2), BF16)],
        compiler_params=_cparams("arbitrary"),
        name="dispatch_split_w1",
    )(sorted_rows, zlo, zhi, n_used, hn, w1)


def _expert_kernel(be_ref, nu_ref, xs_ref, w1g_ref, w1l_ref, b1g_ref, b1l_ref, w2_ref, b2_ref, y_ref, *, rows):
    j = pl.program_id(0)
    n_used = nu_ref[0]

    @pl.when(j < n_used)
    def _():
        x = jnp.concatenate([xs_ref[pl.ds(s, rows, stride=SUBLANES), :] for s in range(SUBLANES)],
                            axis=-1).astype(BF16)
        glu = jnp.minimum(_dot(x, w1g_ref[...]) + b1g_ref[...], SWIGLU_LIMIT)
        lin = jnp.clip(_dot(x, w1l_ref[...]) + b1l_ref[...], -SWIGLU_LIMIT, SWIGLU_LIMIT)
        act = glu * jax.nn.sigmoid(SWIGLU_ALPHA * glu) * (lin + 1.0)
        y = _dot(act.astype(BF16), w2_ref[...].astype(BF16)) + b2_ref[...]
        for s in range(SUBLANES):
            y_ref[pl.ds(s, rows, stride=SUBLANES), :] = y[:, s * LANES:(s + 1) * LANES]

    @pl.when(j >= n_used)
    def _():
        y_ref[...] = jnp.zeros_like(y_ref)


def _experts(block_e, n_used, xs, w1g, w1l, b1g, b1l, w2, b2, n_blocks):
    rows = EXPERT_ROWS
    tile = rows * SUBLANES
    wspec = lambda a: pl.BlockSpec((None,) + a.shape[1:], lambda j, be, nu: (be[j], 0, 0))
    grid_spec = pltpu.PrefetchScalarGridSpec(
        num_scalar_prefetch=2,
        grid=(n_blocks,),
        in_specs=[pl.BlockSpec((tile, LANES), lambda j, be, nu: (jnp.minimum(j, nu[0] - 1), 0)),
                  wspec(w1g), wspec(w1l), wspec(b1g), wspec(b1l), wspec(w2), wspec(b2)],
        out_specs=pl.BlockSpec((tile, LANES), lambda j, be, nu: (j, 0)),
    )
    return pl.pallas_call(
        functools.partial(_expert_kernel, rows=rows),
        grid_spec=grid_spec,
        out_shape=jax.ShapeDtypeStruct((n_blocks * tile, LANES), F32),
        compiler_params=_cparams("arbitrary"),
        name="experts",
    )(block_e, n_used, xs, w1g, w1l, b1g, b1l, w2, b2)


def _combine_kernel(row_ref, yb_ref, gate_ref, h_ref, g_ref, yp_ref, ys_ref, cbuf, sem, *, tm, n_total,
                    prompt_tiles):
    i = pl.program_id(0)
    n = pl.num_programs(0)
    slot = i % 2
    tile = tm * SUBLANES

    def gather(blk, sl):
        def body(t, carry):
            for k in range(TOP_K):
                src = pl.multiple_of(row_ref[k * n_total + blk * tm + t] * SUBLANES, SUBLANES)
                dst = pl.multiple_of((k * tm + t) * SUBLANES, SUBLANES)
                pltpu.make_async_copy(yb_ref.at[pl.ds(src, SUBLANES), :], cbuf.at[sl, pl.ds(dst, SUBLANES), :],
                                      sem.at[sl]).start()
            return carry
        lax.fori_loop(0, tm, body, 0)

    @pl.when(i == 0)
    def _():
        gather(0, 0)

    @pl.when(i + 1 < n)
    def _():
        gather(i + 1, 1 - slot)

    pltpu.make_async_copy(yb_ref.at[pl.ds(0, TOP_K * tile), :], cbuf.at[slot], sem.at[slot]).wait()
    gates = gate_ref[...].T
    slabs = []
    for s in range(SUBLANES):
        acc = None
        for k in range(TOP_K):
            part = gates[:, k:k + 1] * cbuf[slot, pl.ds(k * tile + s, tm, stride=SUBLANES), :]
            acc = part if acc is None else acc + part
        slabs.append(acc)
    hf = h_ref[...] + jnp.concatenate(slabs, axis=-1)
    y = _rms(hf, g_ref[...])

    @pl.when(i < prompt_tiles)
    def _():
        yp_ref[...] = y

    @pl.when(i >= prompt_tiles)
    def _():
        ys_ref[...] = y


def _combine(rows, yb, gates, h, gf, np_):
    n, d = h.shape
    ns_ = n - np_
    tm = min(COMBINE_TILE, math.gcd(np_, ns_))
    prompt_tiles = np_ // tm
    row = lambda w: pl.BlockSpec((tm, w), lambda i, *_: (i, 0))
    grid_spec = pltpu.PrefetchScalarGridSpec(
        num_scalar_prefetch=1,
        grid=(n // tm,),
        in_specs=[pl.BlockSpec(memory_space=pl.ANY), pl.BlockSpec((gates.shape[0], tm), lambda i, *_: (0, i)), row(d),
                  pl.BlockSpec(gf.shape, lambda i, *_: (0, 0))],
        out_specs=[pl.BlockSpec((tm, d), lambda i, *_: (jnp.minimum(i, prompt_tiles - 1), 0)),
                   pl.BlockSpec((tm, d), lambda i, *_: (jnp.maximum(i - prompt_tiles, 0), 0))],
        scratch_shapes=[pltpu.VMEM((2, TOP_K * tm * SUBLANES, LANES), F32), pltpu.SemaphoreType.DMA((2,))],
    )
    return pl.pallas_call(
        functools.partial(_combine_kernel, tm=tm, n_total=n, prompt_tiles=prompt_tiles),
        grid_spec=grid_spec,
        out_shape=[jax.ShapeDtypeStruct((np_, d), F32), jax.ShapeDtypeStruct((ns_, d), F32)],
        compiler_params=_cparams("arbitrary"),
        name="combine_norm",
    )(rows, yb, gates, h, gf)


def kernel(x_prompt, x_sample, state_pool, cache_k, cache_v, page_table, norm1_g, w_in, pool_w, pool_scale,
           lambda_q1, lambda_k1, lambda_q2, lambda_k2, subln_g, w_out, norm2_g, router_w, router_b,
           w1, b1, w2, b2, normf_g):
    batch, seq, d = x_prompt.shape
    bs, n_new, _ = x_sample.shape
    depth = w_in.shape[0]
    assert depth == 1, "single-layer trunk"
    n_heads = cache_k.shape[3]
    page_size = cache_k.shape[2]
    pw = pool_w.shape[1] * pool_w.shape[2]
    qk_w = n_heads * HEAD_W
    n_experts = router_w.shape[2]
    np_, ns_ = batch * seq, bs * n_new
    n_total = np_ + ns_
    lam_init = 0.8 - 0.6 * math.exp(-0.3 * 0)

    g1 = norm1_g[0][None]
    w_in_b = w_in[0].astype(BF16)
    wvt_b = w_in_b[:, pw + 2 * qk_w:].T
    pool_w_b = pool_w[0].astype(BF16)
    ps = pool_scale[0][None]
    lams = [a[0][None] for a in (lambda_q1, lambda_k1, lambda_q2, lambda_k2)]
    sg = subln_g[0][None]
    slopes = jnp.exp2(-8.0 * (jnp.arange(n_heads, dtype=F32) + 1.0) / n_heads)
    w_out_b = w_out[0].astype(BF16)
    g2 = norm2_g[0][None]
    rwt_b = router_w[0].T.astype(BF16)
    rb = router_b[0][:, None]
    b1g = b1[0][:, None, 0::2]
    b1l = b1[0][:, None, 1::2]
    w2b = w2.reshape(w2.shape[1:])
    b2e = b2[0][:, None, :]
    gf = normf_g[None]

    xp = x_prompt.reshape(np_, d)
    tq = min(ATTN_TILE, seq)
    u_p, q_p, k_p, v_p, kb_p, vt_p = _inproj(xp, g1, w_in_b, wvt_b, pw, qk_w, key_tile=tq)
    pool_p = _pool_prompt(u_p, pool_w_b, ps, batch, seq)
    o_p = _attn_prompt(q_p, kb_p, vt_p, slopes, lams, sg, batch, seq, n_heads, lam_init, tq)

    xs = x_sample.reshape(ns_, d)
    u_s, q_s, k_s, v_s = _inproj(xs, g1, w_in_b, wvt_b, pw, qk_w)
    pool_s, new_pool_s = _pool_sample(state_pool[0], u_s.reshape(bs, n_new, pw), pool_w_b, ps)
    kc = cache_k.reshape(cache_k.shape[1], page_size * n_heads, HEAD_W)
    vc = cache_v.reshape(cache_v.shape[1], page_size * n_heads, HEAD_W)
    o_s = _attn_sample(q_s.reshape(bs, n_new, qk_w), k_s.reshape(bs, n_new, qk_w), v_s.reshape(bs, n_new, qk_w),
                       kc, vc, page_table, slopes, lams, sg, n_heads, lam_init)

    h, hn, code, gates, cnt = _outproj((pool_p, o_p, xp), (pool_s.reshape(ns_, pw), o_s.reshape(ns_, qk_w), xs),
                                       w_out_b, g2, rwt_b, rb)

    counts = cnt[:, 0].astype(I32)
    n_blocks = -(-n_total * TOP_K // EXPERT_ROWS) + n_experts
    padded = (counts + EXPERT_ROWS - 1) // EXPERT_ROWS * EXPERT_ROWS
    pad_ends = jnp.cumsum(padded)
    pad_starts = pad_ends - padded
    n_used = (pad_ends[-1:] // EXPERT_ROWS).astype(I32)
    block_start = jnp.arange(n_blocks, dtype=I32) * EXPERT_ROWS
    block_e = jnp.minimum(jnp.sum((pad_ends[None, :] <= block_start[:, None]).astype(I32), axis=1), n_experts - 1)

    sorted_rows = _sorted_rows(pad_starts, code).reshape(-1)
    xs_rows, w1g, w1l = _dispatch(sorted_rows, pad_starts + counts, pad_ends, n_used, hn, w1.reshape(w1.shape[1:]),
                                  n_blocks)
    yb = _experts(block_e, n_used, xs_rows, w1g, w1l, b1g, b1l, w2b, b2e, n_blocks)
    y_p, y_s = _combine(sorted_rows, yb, gates, h, gf, np_)

    return (y_p.reshape(batch, seq, d), y_s.reshape(bs, n_new, d),
            u_p.reshape(batch, seq, pw)[None, :, seq - state_pool.shape[2]:],
            new_pool_s[None],
            k_p.reshape(1, batch, seq, n_heads, HEAD_W), v_p.reshape(1, batch, seq, n_heads, HEAD_W),
            k_s.reshape(1, bs, n_new, n_heads, HEAD_W), v_s.reshape(1, bs, n_new, n_heads, HEAD_W))
```

```python
import functools
import math

import jax
import jax.numpy as jnp
from jax import lax
from jax.experimental import pallas as pl
from jax.experimental.pallas import tpu as pltpu

F32 = jnp.float32
BF16 = jnp.bfloat16
I32 = jnp.int32

RMS_EPS = 1e-6
NEG_INF = -1e30
HEAD_DIM = 64
HEAD_W = 2 * HEAD_DIM
POOL_WINDOWS = (2, 4, 8, 16)
POOL_HALO = 16
TOP_K = 4
RANK_BITS = 20
SWIGLU_LIMIT = 7.0
SWIGLU_ALPHA = 1.702
LANES = 128
SUBLANES = 8
ROW_TILE = 512
ATTN_TILE = 512
ATTN_HEADS = 4
LOG2E = 1.4426950408889634
N_SPLIT = 3
POS_LOW = 16
BF16_ROWS = 16
EXPERT_ROWS = 512
EXPERT_CHUNK = 128
COMBINE_TILE = 512
VMEM_LIMIT = 48 * 1024 * 1024


def _cparams(*sem):
    return pltpu.CompilerParams(dimension_semantics=sem, vmem_limit_bytes=VMEM_LIMIT)


def _dot(a, b):
    return jnp.dot(a, b, preferred_element_type=F32)


def _dot_nt(a, b):
    return lax.dot_general(a, b, (((1,), (1,)), ((), ())), preferred_element_type=F32)


def _rms(x, g):
    return x * lax.rsqrt(jnp.mean(x * x, axis=-1, keepdims=True) + RMS_EPS) * g


def _inproj_kernel(x_ref, g_ref, w_ref, wvt_ref, u_ref, q_ref, k_ref, v_ref, *attn_refs, pool_w, qk_w, key_tile):
    xb = _rms(x_ref[...], g_ref[...]).astype(BF16)
    c0, c1, c2 = pool_w, pool_w + qk_w, pool_w + 2 * qk_w
    u_ref[...] = _dot(xb, w_ref[:, 0:c0])
    q_ref[...] = _dot(xb, w_ref[:, c0:c1]) * (HEAD_DIM ** -0.5)
    k = _dot(xb, w_ref[:, c1:c2])
    v = _dot(xb, w_ref[:, c2:])
    for h in range(k_ref.shape[1]):
        k_ref[:, h, :] = k[:, h * HEAD_W:(h + 1) * HEAD_W]
        v_ref[:, h, :] = v[:, h * HEAD_W:(h + 1) * HEAD_W]
    if attn_refs:
        kb_ref, vt_ref = attn_refs
        kb_ref[...] = k.astype(BF16)
        vt = _dot_nt(wvt_ref[...], xb)
        for c in range(vt_ref.shape[0]):
            vt_ref[c] = vt[:, c * key_tile:(c + 1) * key_tile].astype(BF16)


def _inproj(x, g1, w_in_b, wvt_b, pool_w, qk_w, key_tile=None):
    n, d = x.shape
    val_w = wvt_b.shape[0]
    tm = min(ROW_TILE, n)
    row = lambda w: pl.BlockSpec((tm, w), lambda i: (i, 0))
    full = lambda a: pl.BlockSpec(a.shape, lambda i: (0,) * a.ndim)
    n_heads = qk_w // HEAD_W
    assert val_w == qk_w
    heads = pl.BlockSpec((tm, n_heads, HEAD_W), lambda i: (i, 0, 0))
    out_specs = [row(pool_w), row(qk_w), heads, heads]
    out_shape = [jax.ShapeDtypeStruct((n, pool_w), F32), jax.ShapeDtypeStruct((n, qk_w), F32),
                 jax.ShapeDtypeStruct((n, n_heads, HEAD_W), F32), jax.ShapeDtypeStruct((n, n_heads, HEAD_W), F32)]
    if key_tile is not None:
        assert tm % key_tile == 0
        out_specs += [row(qk_w), pl.BlockSpec((tm // key_tile, val_w, key_tile), lambda i: (i, 0, 0))]
        out_shape += [jax.ShapeDtypeStruct((n, qk_w), BF16), jax.ShapeDtypeStruct((n // key_tile, val_w, key_tile), BF16)]
    return pl.pallas_call(
        functools.partial(_inproj_kernel, pool_w=pool_w, qk_w=qk_w, key_tile=key_tile),
        grid=(n // tm,),
        in_specs=[row(d), full(g1), full(w_in_b), full(wvt_b)],
        out_specs=out_specs,
        out_shape=out_shape,
        compiler_params=_cparams("parallel"),
        name="inproj",
    )(x, g1, w_in_b, wvt_b)


def _pool_group(s_win, tok, cnt, pw, ps):
    d = s_win / cnt - tok
    return _dot(d.astype(BF16), pw) * ps


def _pool_prompt_kernel(u_ref, pw_ref, ps_ref, o_ref, carry_ref, *, tt):
    j = pl.program_id(1)

    @pl.when(j == 0)
    def _():
        carry_ref[...] = jnp.zeros_like(carry_ref)

    cur = u_ref[...]
    ext = jnp.concatenate([carry_ref[...], cur], axis=0)
    carry_ref[...] = cur[tt - POOL_HALO:, :]
    pos = (j * tt + lax.broadcasted_iota(I32, (tt, 1), 0)).astype(F32)
    gw = cur.shape[1] // len(POOL_WINDOWS)
    for g, w in enumerate(POOL_WINDOWS):
        cols = slice(g * gw, (g + 1) * gw)
        e = ext[:, cols]
        s, span = e, 1
        while span < w:
            s = s + pltpu.roll(s, span, axis=0)
            span *= 2
        cnt = jnp.minimum(float(w), pos + 1.0)
        out = _pool_group(s[POOL_HALO:, :], e[POOL_HALO:, :], cnt, pw_ref[g], ps_ref[:, cols])
        o_ref[:, cols] = out.astype(o_ref.dtype)


def _pool_prompt(u, pool_w_b, pool_scale, batch, seq):
    pw = u.shape[1]
    tt = min(ROW_TILE, seq)
    u3 = u.reshape(batch, seq, pw)
    out = pl.pallas_call(
        functools.partial(_pool_prompt_kernel, tt=tt),
        grid=(batch, seq // tt),
        in_specs=[pl.BlockSpec((None, tt, pw), lambda b, j: (b, j, 0)),
                  pl.BlockSpec(pool_w_b.shape, lambda b, j: (0, 0, 0)),
                  pl.BlockSpec(pool_scale.shape, lambda b, j: (0, 0))],
        out_specs=pl.BlockSpec((None, tt, pw), lambda b, j: (b, j, 0)),
        out_shape=jax.ShapeDtypeStruct((batch, seq, pw), BF16),
        scratch_shapes=[pltpu.VMEM((POOL_HALO, pw), F32)],
        compiler_params=_cparams("arbitrary", "arbitrary"),
        name="pool_prompt",
    )(u3, pool_w_b, pool_scale)
    return out.reshape(batch * seq, pw)


def _pool_sample_kernel(sp_ref, u_ref, pw_ref, ps_ref, o_ref, np_ref, *, n_new, n_buf):
    rows = [sp_ref[:, r, :] for r in range(n_buf)] + [u_ref[:, r, :] for r in range(n_new)]
    for r in range(n_buf):
        np_ref[:, r, :] = rows[n_new + r]
    gw = rows[0].shape[1] // len(POOL_WINDOWS)
    for t in range(n_new):
        i = n_buf + t
        for g, w in enumerate(POOL_WINDOWS):
            cols = slice(g * gw, (g + 1) * gw)
            s = rows[i][:, cols]
            for back in range(1, w):
                s = s + rows[i - back][:, cols]
            out = _pool_group(s, rows[i][:, cols], float(w), pw_ref[g], ps_ref[:, cols])
            o_ref[:, t, cols] = out


def _pool_sample(state_pool, u3, pool_w_b, pool_scale):
    bs, n_buf, pw = state_pool.shape
    n_new = u3.shape[1]
    full = lambda a: pl.BlockSpec(a.shape, lambda i: (0,) * a.ndim)
    return pl.pallas_call(
        functools.partial(_pool_sample_kernel, n_new=n_new, n_buf=n_buf),
        grid=(1,),
        in_specs=[full(state_pool), full(u3), full(pool_w_b), full(pool_scale)],
        out_specs=[pl.BlockSpec((bs, n_new, pw), lambda i: (0, 0, 0)),
                   pl.BlockSpec((bs, n_buf, pw), lambda i: (0, 0, 0))],
        out_shape=[jax.ShapeDtypeStruct((bs, n_new, pw), F32),
                   jax.ShapeDtypeStruct((bs, n_buf, pw), F32)],
        compiler_params=_cparams("arbitrary"),
        name="pool_sample",
    )(state_pool, u3, pool_w_b, pool_scale)


def _lambda_full(lq1, lk1, lq2, lk2, lam_init):
    e1 = jnp.exp(jnp.sum(lq1 * lk1, axis=-1, keepdims=True))
    e2 = jnp.exp(jnp.sum(lq2 * lk2, axis=-1, keepdims=True))
    return e1 - e2 + lam_init


def _split_halves(q):
    lane = lax.broadcasted_iota(I32, q.shape, 1)
    zero = jnp.zeros_like(q)
    return jnp.concatenate([jnp.where(lane < HEAD_DIM, q, zero), jnp.where(lane >= HEAD_DIM, q, zero)], axis=0)


def _subln(o, g, lam_init):
    return _rms(o, g) * (1.0 - lam_init)


def _attn_prompt_kernel(slopes_ref, q_ref, k_ref, vt_ref, lq1_ref, lk1_ref, lq2_ref, lk2_ref, g_ref, o_ref,
                        *, tq, lam_init, heads):
    hg = pl.program_id(1)
    i = pl.program_id(2)
    lam = _lambda_full(lq1_ref[...], lk1_ref[...], lq2_ref[...], lk2_ref[...], lam_init)
    head_cols = [slice(hh * HEAD_W, (hh + 1) * HEAD_W) for hh in range(heads)]
    row = lax.broadcasted_iota(I32, (tq, LANES), 0)
    lane = lax.broadcasted_iota(I32, (tq, LANES), 1)
    pos_feat = jnp.where(lane < 2 * N_SPLIT, jnp.where(lane % 2 == 0, row & ~(POS_LOW - 1), row & (POS_LOW - 1)), 0)
    pos_feat = pos_feat.astype(F32).astype(BF16)
    qlane = lax.broadcasted_iota(I32, (2 * tq, LANES), 1)
    slopes2, qqs = [], []
    for hh in range(heads):
        s2 = slopes_ref[hg * heads + hh] * LOG2E
        slopes2.append(s2)
        rest = jnp.full((2 * tq, LANES), s2, F32)
        feat = jnp.zeros((2 * tq, LANES), F32)
        for part in range(N_SPLIT):
            piece = rest.astype(BF16).astype(F32)
            rest = rest - piece
            feat = jnp.where(qlane // 2 == part, piece, feat)
        qq = _split_halves((q_ref[:, head_cols[hh]] * LOG2E).astype(BF16))
        qqs.append(jnp.concatenate([qq, feat.astype(BF16)], axis=1))
    ones = jnp.ones((BF16_ROWS, tq), BF16)

    def head_step(hh, j, carry, diag):
        m, acc = carry
        start = pl.multiple_of(j * tq, tq)
        keys = jnp.concatenate([k_ref[pl.ds(start, tq), head_cols[hh]], pos_feat], axis=1)
        st = _dot_nt(keys, qqs[hh])
        if diag:
            kr = lax.broadcasted_iota(I32, st.shape, 0)
            qc = lax.broadcasted_iota(I32, st.shape, 1)
            qc = jnp.where(qc >= tq, qc - tq, qc)
            st = jnp.where(qc >= kr, st, NEG_INF)
        offset = slopes2[hh] * ((j - i) * tq).astype(F32)
        m_new = jnp.maximum(m, jnp.max(st, axis=0, keepdims=True) + offset)
        alpha = jnp.exp2(m - m_new)
        p = jnp.exp2(st - (m_new - offset)).astype(BF16)
        values = jnp.concatenate([vt_ref[j, head_cols[hh], :], ones], axis=0)
        acc = alpha * acc + _dot(values, p)
        return m_new, acc

    def step(j, carries, diag):
        return tuple(head_step(hh, j, carries[hh], diag) for hh in range(heads))

    init = (jnp.full((1, 2 * tq), NEG_INF, F32), jnp.zeros((HEAD_W + BF16_ROWS, 2 * tq), F32))
    carries = lax.fori_loop(0, i, lambda j, cr: step(j, cr, False), (init,) * heads)
    carries = step(i, carries, True)
    for hh, (_, acc) in enumerate(carries):
        o = acc[:HEAD_W] / acc[HEAD_W:HEAD_W + 1]
        o = (o[:, :tq] - lam * o[:, tq:]).T
        o_ref[:, head_cols[hh]] = _subln(o, g_ref[...], lam_init).astype(o_ref.dtype)


def _attn_prompt(q, kb, vt, slopes, lams, subln_g, batch, seq, n_heads, lam_init, tq):
    heads = math.gcd(n_heads, ATTN_HEADS)
    width = heads * HEAD_W
    q3, k3 = (a.reshape(batch, seq, n_heads * HEAD_W) for a in (q, kb))
    small = lambda a: pl.BlockSpec(a.shape, lambda b, h, i, s: (0,) * a.ndim)
    grid_spec = pltpu.PrefetchScalarGridSpec(
        num_scalar_prefetch=1,
        grid=(batch, n_heads // heads, seq // tq),
        in_specs=[pl.BlockSpec((None, tq, width), lambda b, h, i, s: (b, i, h)),
                  pl.BlockSpec((None, seq, width), lambda b, h, i, s: (b, 0, h)),
                  pl.BlockSpec((seq // tq, width, tq), lambda b, h, i, s: (b, h, 0)),
                  *[small(a) for a in lams], small(subln_g)],
        out_specs=pl.BlockSpec((None, tq, width), lambda b, h, i, s: (b, i, h)),
    )
    out = pl.pallas_call(
        functools.partial(_attn_prompt_kernel, tq=tq, lam_init=lam_init, heads=heads),
        grid_spec=grid_spec,
        out_shape=jax.ShapeDtypeStruct((batch, seq, n_heads * HEAD_W), BF16),
        compiler_params=_cparams("parallel", "parallel", "arbitrary"),
        name="attn_prompt",
    )(slopes, q3, k3, vt, *lams, subln_g)
    return out.reshape(batch * seq, n_heads * HEAD_W)


def _attn_sample_kernel(pt_ref, slopes_ref, q_ref, kn_ref, vn_ref, kc_ref, vc_ref,
                        lq1_ref, lk1_ref, lq2_ref, lk2_ref, g_ref, o_ref, kbuf, vbuf, sem,
                        *, n_pages, page_rows, n_heads, n_new, lam_init):
    b = pl.program_id(0)
    nb = pl.num_programs(0)
    slot = b % 2
    past = n_pages * page_rows // n_heads

    def page_copies(seq, sl, pg):
        p = pt_ref[seq * n_pages + pg]
        dst = pl.ds(pg * page_rows, page_rows)
        return (pltpu.make_async_copy(kc_ref.at[p], kbuf.at[sl, dst, :], sem.at[sl, 0]),
                pltpu.make_async_copy(vc_ref.at[p], vbuf.at[sl, dst, :], sem.at[sl, 1]))

    def start_seq(seq, sl):
        for pg in range(n_pages):
            for cp in page_copies(seq, sl, pg):
                cp.start()

    @pl.when(b == 0)
    def _():
        start_seq(0, 0)

    @pl.when(b + 1 < nb)
    def _():
        start_seq(b + 1, 1 - slot)

    for pg in range(n_pages):
        for cp in page_copies(b, slot, pg):
            cp.wait()

    lam = _lambda_full(lq1_ref[...], lk1_ref[...], lq2_ref[...], lk2_ref[...], lam_init)
    rows = 2 * SUBLANES
    r = lax.broadcasted_iota(I32, (rows, past), 0)
    t_past = jnp.where(r >= n_new, r - n_new, r)
    dist_past = (past + t_past - lax.broadcasted_iota(I32, (rows, past), 1)).astype(F32)
    r1 = lax.broadcasted_iota(I32, (rows, 1), 0)
    t_new = jnp.where(r1 >= n_new, r1 - n_new, r1)
    for h in range(n_heads):
        slope = slopes_ref[h]
        cols = slice(h * HEAD_W, (h + 1) * HEAD_W)
        q = q_ref[:, cols]
        qq = jnp.concatenate([_split_halves(q), jnp.zeros((rows - 2 * n_new, HEAD_W), F32)], axis=0)
        kh = kbuf[slot, pl.ds(h, past, stride=n_heads), :].astype(BF16)
        vh = vbuf[slot, pl.ds(h, past, stride=n_heads), :].astype(BF16)
        s = _dot_nt(qq.astype(BF16), kh) - slope * dist_past
        kn = kn_ref[:, cols]
        vn = vn_ref[:, cols]
        s_new = []
        for c in range(n_new):
            sc = jnp.sum(qq * kn[c:c + 1, :], axis=-1, keepdims=True) - slope * (t_new - c).astype(F32)
            s_new.append(jnp.where(t_new >= c, sc, NEG_INF))
        m = jnp.max(s, axis=-1, keepdims=True)
        for sc in s_new:
            m = jnp.maximum(m, sc)
        p = jnp.exp(s - m)
        l = jnp.sum(p, axis=-1, keepdims=True)
        acc = _dot(p.astype(BF16), vh)
        for c, sc in enumerate(s_new):
            pc = jnp.exp(sc - m)
            l = l + pc
            acc = acc + pc * vn[c:c + 1, :]
        o = acc / l
        o = o[0:n_new] - lam * o[n_new:2 * n_new]
        o_ref[:, cols] = _subln(o, g_ref[...], lam_init)


def _attn_sample(q3, kn3, vn3, kc, vc, page_table, slopes, lams, subln_g, n_heads, lam_init):
    bs, n_new, qk_w = q3.shape
    n_pages = page_table.shape[1]
    page_rows = kc.shape[1]
    small = lambda a: pl.BlockSpec(a.shape, lambda b, pt, s: (0,) * a.ndim)
    tok = pl.BlockSpec((None, n_new, qk_w), lambda b, pt, s: (b, 0, 0))
    grid_spec = pltpu.PrefetchScalarGridSpec(
        num_scalar_prefetch=2,
        grid=(bs,),
        in_specs=[tok, tok, tok, pl.BlockSpec(memory_space=pl.ANY), pl.BlockSpec(memory_space=pl.ANY),
                  *[small(a) for a in lams], small(subln_g)],
        out_specs=tok,
        scratch_shapes=[pltpu.VMEM((2, n_pages * page_rows, LANES), F32),
                        pltpu.VMEM((2, n_pages * page_rows, LANES), F32),
                        pltpu.SemaphoreType.DMA((2, 2))],
    )
    return pl.pallas_call(
        functools.partial(_attn_sample_kernel, n_pages=n_pages, page_rows=page_rows, n_heads=n_heads,
                          n_new=n_new, lam_init=lam_init),
        grid_spec=grid_spec,
        out_shape=jax.ShapeDtypeStruct((bs, n_new, qk_w), F32),
        compiler_params=_cparams("arbitrary"),
        name="attn_sample",
    )(page_table.reshape(-1), slopes, q3, kn3, vn3, kc, vc, *lams, subln_g)


def _outproj_kernel(pool_p_ref, o_p_ref, x_p_ref, pool_s_ref, o_s_ref, x_s_ref, wo_ref, g2_ref, rwt_ref, rb_ref,
                    earlier_ref, h_ref, hn_ref, code_ref, gate_ref, cnt_out_ref, run_ref, *, tm, n_experts,
                    prompt_tiles):
    i = pl.program_id(0)

    @pl.when(i == 0)
    def _():
        run_ref[...] = jnp.zeros_like(run_ref)

    tile = functools.partial(_outproj_tile, wo_ref=wo_ref, g2_ref=g2_ref, rwt_ref=rwt_ref, rb_ref=rb_ref,
                             earlier_ref=earlier_ref, h_ref=h_ref, hn_ref=hn_ref, code_ref=code_ref, gate_ref=gate_ref,
                             cnt_out_ref=cnt_out_ref, run_ref=run_ref, tm=tm, n_experts=n_experts)
    pl.when(i < prompt_tiles)(functools.partial(tile, pool_p_ref, o_p_ref, x_p_ref))
    pl.when(i >= prompt_tiles)(functools.partial(tile, pool_s_ref, o_s_ref, x_s_ref))


def _outproj_tile(pool_ref, o_ref, x_ref, *, wo_ref, g2_ref, rwt_ref, rb_ref, earlier_ref, h_ref, hn_ref, code_ref,
                  gate_ref, cnt_out_ref, run_ref, tm, n_experts):
    pw = pool_ref.shape[1]
    mix = _dot(pool_ref[...].astype(BF16), wo_ref[0:pw, :]) + _dot(o_ref[...].astype(BF16), wo_ref[pw:, :])
    h = x_ref[...] + mix
    h_ref[...] = h
    hn = _rms(h, g2_ref[...])
    for s in range(hn.shape[1] // LANES):
        hn_ref[pl.ds(s, tm, stride=SUBLANES), :] = hn[:, s * LANES:(s + 1) * LANES]
    logits = _dot_nt(rwt_ref[...], hn.astype(BF16)) + rb_ref[...]
    expert = lax.broadcasted_iota(I32, logits.shape, 0)
    vals, idxs = [], []
    for _ in range(TOP_K):
        m = jnp.max(logits, axis=0, keepdims=True)
        idx = jnp.min(jnp.where(logits == m, expert, n_experts), axis=0, keepdims=True)
        vals.append(m)
        idxs.append(idx)
        logits = jnp.where(expert == idx, -jnp.inf, logits)
    ex = [jnp.exp(v - vals[0]) for v in vals]
    den = ex[0]
    for e in ex[1:]:
        den = den + e
    chosen = jnp.zeros(logits.shape, F32)
    for idx in idxs:
        chosen = chosen + jnp.where(expert == idx, 1.0, 0.0)
    before = _dot(chosen.astype(BF16), earlier_ref[...]) + run_ref[...]
    for k in range(TOP_K):
        gate_ref[k:k + 1, :] = ex[k] / den
        rank = jnp.sum(jnp.where(expert == idxs[k], before, 0.0), axis=0, keepdims=True).astype(I32)
        code_ref[k:k + 1, :] = idxs[k] * (1 << RANK_BITS) + rank
    gate_ref[TOP_K:, :] = jnp.zeros((gate_ref.shape[0] - TOP_K, tm), F32)
    run_ref[...] = run_ref[...] + jnp.sum(chosen, axis=1, keepdims=True)
    cnt_out_ref[...] = run_ref[...]


def _outproj(prompt, sample, w_out_b, g2, rwt_b, rb):
    np_, d = prompt[2].shape
    ns_ = sample[2].shape[0]
    n_total = np_ + ns_
    n_experts = rwt_b.shape[0]
    tm = min(ROW_TILE, math.gcd(np_, ns_))
    prompt_tiles = np_ // tm
    assert d == SUBLANES * LANES and n_total < (1 << RANK_BITS)
    first = lambda a: pl.BlockSpec((tm, a.shape[1]), lambda i: (jnp.minimum(i, prompt_tiles - 1), 0))
    second = lambda a: pl.BlockSpec((tm, a.shape[1]), lambda i: (jnp.maximum(i - prompt_tiles, 0), 0))
    row = lambda w: pl.BlockSpec((tm, w), lambda i: (i, 0))
    col = lambda r: pl.BlockSpec((r, tm), lambda i: (0, i))
    full = lambda a: pl.BlockSpec(a.shape, lambda i: (0,) * a.ndim)
    counts = jax.ShapeDtypeStruct((n_experts, 1), F32)
    earlier = jnp.triu(jnp.ones((tm, tm), BF16), 1)
    return pl.pallas_call(
        functools.partial(_outproj_kernel, tm=tm, n_experts=n_experts, prompt_tiles=prompt_tiles),
        grid=(n_total // tm,),
        in_specs=[*[first(a) for a in prompt], *[second(a) for a in sample],
                  full(w_out_b), full(g2), full(rwt_b), full(rb), full(earlier)],
        out_specs=[row(d), pl.BlockSpec((tm * SUBLANES, LANES), lambda i: (i, 0)),
                   col(TOP_K), col(SUBLANES), pl.BlockSpec(counts.shape, lambda i: (0, 0))],
        out_shape=[jax.ShapeDtypeStruct((n_total, d), F32), jax.ShapeDtypeStruct((n_total * SUBLANES, LANES), F32),
                   jax.ShapeDtypeStruct((TOP_K, n_total), I32), jax.ShapeDtypeStruct((SUBLANES, n_total), F32),
                   counts],
        scratch_shapes=[pltpu.VMEM(counts.shape, F32)],
        compiler_params=_cparams("arbitrary"),
        name="outproj_router",
    )(*prompt, *sample, w_out_b, g2, rwt_b, rb, earlier)


def _sorted_rows_kernel(start_ref, code_ref, rows_ref, *, n_experts):
    code = code_ref[...]
    expert = code >> RANK_BITS
    rows = code & ((1 << RANK_BITS) - 1)
    for e in range(n_experts):
        rows = rows + jnp.where(expert == e, start_ref[e], 0)
    rows_ref[...] = rows


def _sorted_rows(starts, code):
    return pl.pallas_call(
        functools.partial(_sorted_rows_kernel, n_experts=starts.shape[0]),
        grid=(1,),
        in_specs=[pl.BlockSpec(memory_space=pltpu.SMEM), pl.BlockSpec(code.shape, lambda i: (0, 0))],
        out_specs=pl.BlockSpec(code.shape, lambda i: (0, 0)),
        out_shape=jax.ShapeDtypeStruct(code.shape, I32),
        compiler_params=_cparams("arbitrary"),
        name="sorted_rows",
    )(starts, code)


MXU_DIM = 256


def _split_w1_tile(w_ref, g_ref, l_ref):
    r = lax.broadcasted_iota(I32, (MXU_DIM, MXU_DIM), 0)
    c = lax.broadcasted_iota(I32, (MXU_DIM, MXU_DIM), 1)
    half = MXU_DIM // 2
    source = jnp.where(c < half, 2 * c, 2 * (c - half) + 1)
    perm = jnp.where(r == source, 1.0, 0.0).astype(BF16)
    for blk in range(w_ref.shape[1] // MXU_DIM):
        res = _dot(w_ref[:, blk * MXU_DIM:(blk + 1) * MXU_DIM].astype(BF16), perm)
        g_ref[:, blk * half:(blk + 1) * half] = res[:, :half].astype(BF16)
        l_ref[:, blk * half:(blk + 1) * half] = res[:, half:].astype(BF16)


def _dispatch_kernel(row_ref, zlo_ref, zhi_ref, nu_ref, hn_ref, w1_ref, xs_ref, w1g_ref, w1l_ref, zero_ref, sem,
                     *, tm, n_total, rows, n_experts, n_blocks):
    i = pl.program_id(0)
    tile = tm * SUBLANES
    block = rows * SUBLANES

    def pad_pieces(e):
        lo, n = zlo_ref[e], zhi_ref[e] - zlo_ref[e]
        for bit in range(rows.bit_length() - 1):
            size = 1 << bit
            below = n & (size - 1)
            start = pl.multiple_of((lo + below) * SUBLANES, SUBLANES)
            cp = pltpu.make_async_copy(zero_ref.at[pl.ds(0, size * SUBLANES), :],
                                       xs_ref.at[pl.ds(start, size * SUBLANES), :], sem.at[1])
            yield (n & size) != 0, cp

    def tail_copy(blk):
        start = pl.multiple_of(blk * block, block)
        return pltpu.make_async_copy(zero_ref, xs_ref.at[pl.ds(start, block), :], sem.at[1])

    def for_zero_copies(fn):
        def per_expert(e, carry):
            for used, cp in pad_pieces(e):
                pl.when(used)(functools.partial(fn, cp))
            return carry
        lax.fori_loop(0, n_experts, per_expert, 0)

        def per_block(blk, carry):
            fn(tail_copy(blk))
            return carry
        lax.fori_loop(nu_ref[0], n_blocks, per_block, 0)

    @pl.when(i == 0)
    def _():
        zero_ref[...] = jnp.zeros_like(zero_ref)
        for_zero_copies(lambda cp: cp.start())

    def body(t, carry):
        src = pl.multiple_of(t * SUBLANES, SUBLANES)
        for k in range(TOP_K):
            dst = pl.multiple_of(row_ref[k * n_total + i * tm + t] * SUBLANES, SUBLANES)
            pltpu.make_async_copy(hn_ref.at[pl.ds(src, SUBLANES), :], xs_ref.at[pl.ds(dst, SUBLANES), :],
                                  sem.at[0]).start()
        return carry
    lax.fori_loop(0, tm, body, 0)

    _split_w1_tile(w1_ref, w1g_ref, w1l_ref)

    for _ in range(TOP_K):
        pltpu.make_async_copy(hn_ref, xs_ref.at[pl.ds(0, tile), :], sem.at[0]).wait()

    @pl.when(i == pl.num_programs(0) - 1)
    def _():
        for_zero_copies(lambda cp: cp.wait())


def _dispatch(sorted_rows, zlo, zhi, n_used, hn, w1, n_blocks):
    n_total = hn.shape[0] // SUBLANES
    n_exp, d, ff2 = w1.shape
    tr = next(t for t in (ROW_TILE, 2 * ROW_TILE, d) if d % t == 0 and n_total % (n_exp * (d // t)) == 0)
    row_tiles = d // tr
    steps = n_exp * row_tiles
    tm = n_total // steps
    rows = EXPERT_ROWS
    assert rows & (rows - 1) == 0
    half = pl.BlockSpec((None, tr, ff2 // 2), lambda i, *_: (i // row_tiles, i % row_tiles, 0))
    grid_spec = pltpu.PrefetchScalarGridSpec(
        num_scalar_prefetch=4,
        grid=(steps,),
        in_specs=[pl.BlockSpec((tm * SUBLANES, LANES), lambda i, *_: (i, 0)),
                  pl.BlockSpec((None, tr, ff2), lambda i, *_: (i // row_tiles, i % row_tiles, 0))],
        out_specs=[pl.BlockSpec(memory_space=pl.ANY), half, half],
        scratch_shapes=[pltpu.VMEM((rows * SUBLANES, LANES), F32), pltpu.SemaphoreType.DMA((2,))],
    )
    return pl.pallas_call(
        functools.partial(_dispatch_kernel, tm=tm, n_total=n_total, rows=rows, n_experts=n_exp, n_blocks=n_blocks),
        grid_spec=grid_spec,
        out_shape=[jax.ShapeDtypeStruct((n_blocks * rows * SUBLANES, LANES), F32),
                   jax.ShapeDtypeStruct((n_exp, d, ff2 // 2), BF16), jax.ShapeDtypeStruct((n_exp, d, ff2 // 2), BF16)],
        compiler_params=_cparams("arbitrary"),
        name="dispatch_split_w1",
    )(sorted_rows, zlo, zhi, n_used, hn, w1)


def _expert_kernel(be_ref, valid_ref, nu_ref, xs_ref, w1g_ref, w1l_ref, b1g_ref, b1l_ref, w2_ref, b2_ref, y_ref,
                   *, rows):
    del be_ref, nu_ref
    valid = valid_ref[pl.program_id(0)]

    def mlp(first, n):
        x = jnp.concatenate([xs_ref[pl.ds(first * SUBLANES + s, n, stride=SUBLANES), :] for s in range(SUBLANES)],
                            axis=-1).astype(BF16)
        glu = jnp.minimum(_dot(x, w1g_ref[...]) + b1g_ref[...], SWIGLU_LIMIT)
        lin = jnp.clip(_dot(x, w1l_ref[...]) + b1l_ref[...], -SWIGLU_LIMIT, SWIGLU_LIMIT)
        act = glu * jax.nn.sigmoid(SWIGLU_ALPHA * glu) * (lin + 1.0)
        y = _dot(act.astype(BF16), w2_ref[...].astype(BF16)) + b2_ref[...]
        for s in range(SUBLANES):
            y_ref[pl.ds(first * SUBLANES + s, n, stride=SUBLANES), :] = y[:, s * LANES:(s + 1) * LANES]

    @pl.when(valid == rows)
    def _():
        mlp(0, rows)

    @pl.when(valid < rows)
    def _():
        y_ref[...] = jnp.zeros_like(y_ref)
        for first in range(0, rows, EXPERT_CHUNK):
            pl.when(first < valid)(functools.partial(mlp, first, EXPERT_CHUNK))


def _experts(block_e, valid, n_used, xs, w1g, w1l, b1g, b1l, w2, b2, n_blocks):
    rows = EXPERT_ROWS
    assert rows % EXPERT_CHUNK == 0
    tile = rows * SUBLANES
    wspec = lambda a: pl.BlockSpec((None,) + a.shape[1:], lambda j, be, va, nu: (be[j], 0, 0))
    grid_spec = pltpu.PrefetchScalarGridSpec(
        num_scalar_prefetch=3,
        grid=(n_blocks,),
        in_specs=[pl.BlockSpec((tile, LANES), lambda j, be, va, nu: (jnp.minimum(j, nu[0] - 1), 0)),
                  wspec(w1g), wspec(w1l), wspec(b1g), wspec(b1l), wspec(w2), wspec(b2)],
        out_specs=pl.BlockSpec((tile, LANES), lambda j, be, va, nu: (j, 0)),
    )
    return pl.pallas_call(
        functools.partial(_expert_kernel, rows=rows),
        grid_spec=grid_spec,
        out_shape=jax.ShapeDtypeStruct((n_blocks * tile, LANES), F32),
        compiler_params=_cparams("arbitrary"),
        name="experts",
    )(block_e, valid, n_used, xs, w1g, w1l, b1g, b1l, w2, b2)


def _combine_kernel(row_ref, yb_ref, gate_ref, h_ref, g_ref, yp_ref, ys_ref, cbuf, sem, *, tm, n_total,
                    prompt_tiles):
    i = pl.program_id(0)
    n = pl.num_programs(0)
    slot = i % 2
    tile = tm * SUBLANES

    def gather(blk, sl):
        def body(t, carry):
            for k in range(TOP_K):
                src = pl.multiple_of(row_ref[k * n_total + blk * tm + t] * SUBLANES, SUBLANES)
                dst = pl.multiple_of((k * tm + t) * SUBLANES, SUBLANES)
                pltpu.make_async_copy(yb_ref.at[pl.ds(src, SUBLANES), :], cbuf.at[sl, pl.ds(dst, SUBLANES), :],
                                      sem.at[sl]).start()
            return carry
        lax.fori_loop(0, tm, body, 0)

    @pl.when(i == 0)
    def _():
        gather(0, 0)

    @pl.when(i + 1 < n)
    def _():
        gather(i + 1, 1 - slot)

    pltpu.make_async_copy(yb_ref.at[pl.ds(0, TOP_K * tile), :], cbuf.at[slot], sem.at[slot]).wait()
    gates = gate_ref[...].T
    slabs = []
    for s in range(SUBLANES):
        acc = None
        for k in range(TOP_K):
            part = gates[:, k:k + 1] * cbuf[slot, pl.ds(k * tile + s, tm, stride=SUBLANES), :]
            acc = part if acc is None else acc + part
        slabs.append(acc)
    hf = h_ref[...] + jnp.concatenate(slabs, axis=-1)
    y = _rms(hf, g_ref[...])

    @pl.when(i < prompt_tiles)
    def _():
        yp_ref[...] = y

    @pl.when(i >= prompt_tiles)
    def _():
        ys_ref[...] = y


def _combine(rows, yb, gates, h, gf, np_):
    n, d = h.shape
    ns_ = n - np_
    tm = min(COMBINE_TILE, math.gcd(np_, ns_))
    prompt_tiles = np_ // tm
    row = lambda w: pl.BlockSpec((tm, w), lambda i, *_: (i, 0))
    grid_spec = pltpu.PrefetchScalarGridSpec(
        num_scalar_prefetch=1,
        grid=(n // tm,),
        in_specs=[pl.BlockSpec(memory_space=pl.ANY), pl.BlockSpec((gates.shape[0], tm), lambda i, *_: (0, i)), row(d),
                  pl.BlockSpec(gf.shape, lambda i, *_: (0, 0))],
        out_specs=[pl.BlockSpec((tm, d), lambda i, *_: (jnp.minimum(i, prompt_tiles - 1), 0)),
                   pl.BlockSpec((tm, d), lambda i, *_: (jnp.maximum(i - prompt_tiles, 0), 0))],
        scratch_shapes=[pltpu.VMEM((2, TOP_K * tm * SUBLANES, LANES), F32), pltpu.SemaphoreType.DMA((2,))],
    )
    return pl.pallas_call(
        functools.partial(_combine_kernel, tm=tm, n_total=n, prompt_tiles=prompt_tiles),
        grid_spec=grid_spec,
        out_shape=[jax.ShapeDtypeStruct((np_, d), F32), jax.ShapeDtypeStruct((ns_, d), F32)],
        compiler_params=_cparams("arbitrary"),
        name="combine_norm",
    )(rows, yb, gates, h, gf)


def kernel(x_prompt, x_sample, state_pool, cache_k, cache_v, page_table, norm1_g, w_in, pool_w, pool_scale,
           lambda_q1, lambda_k1, lambda_q2, lambda_k2, subln_g, w_out, norm2_g, router_w, router_b,
           w1, b1, w2, b2, normf_g):
    batch, seq, d = x_prompt.shape
    bs, n_new, _ = x_sample.shape
    depth = w_in.shape[0]
    assert depth == 1, "single-layer trunk"
    n_heads = cache_k.shape[3]
    page_size = cache_k.shape[2]
    pw = pool_w.shape[1] * pool_w.shape[2]
    qk_w = n_heads * HEAD_W
    n_experts = router_w.shape[2]
    np_, ns_ = batch * seq, bs * n_new
    n_total = np_ + ns_
    lam_init = 0.8 - 0.6 * math.exp(-0.3 * 0)

    g1 = norm1_g[0][None]
    w_in_b = w_in[0].astype(BF16)
    wvt_b = w_in_b[:, pw + 2 * qk_w:].T
    pool_w_b = pool_w[0].astype(BF16)
    ps = pool_scale[0][None]
    lams = [a[0][None] for a in (lambda_q1, lambda_k1, lambda_q2, lambda_k2)]
    sg = subln_g[0][None]
    slopes = jnp.exp2(-8.0 * (jnp.arange(n_heads, dtype=F32) + 1.0) / n_heads)
    w_out_b = w_out[0].astype(BF16)
    g2 = norm2_g[0][None]
    rwt_b = router_w[0].T.astype(BF16)
    rb = router_b[0][:, None]
    b1g = b1[0][:, None, 0::2]
    b1l = b1[0][:, None, 1::2]
    w2b = w2.reshape(w2.shape[1:])
    b2e = b2[0][:, None, :]
    gf = normf_g[None]

    xp = x_prompt.reshape(np_, d)
    tq = min(ATTN_TILE, seq)
    u_p, q_p, k_p, v_p, kb_p, vt_p = _inproj(xp, g1, w_in_b, wvt_b, pw, qk_w, key_tile=tq)
    pool_p = _pool_prompt(u_p, pool_w_b, ps, batch, seq)
    o_p = _attn_prompt(q_p, kb_p, vt_p, slopes, lams, sg, batch, seq, n_heads, lam_init, tq)

    xs = x_sample.reshape(ns_, d)
    u_s, q_s, k_s, v_s = _inproj(xs, g1, w_in_b, wvt_b, pw, qk_w)
    pool_s, new_pool_s = _pool_sample(state_pool[0], u_s.reshape(bs, n_new, pw), pool_w_b, ps)
    kc = cache_k.reshape(cache_k.shape[1], page_size * n_heads, HEAD_W)
    vc = cache_v.reshape(cache_v.shape[1], page_size * n_heads, HEAD_W)
    o_s = _attn_sample(q_s.reshape(bs, n_new, qk_w), k_s.reshape(bs, n_new, qk_w), v_s.reshape(bs, n_new, qk_w),
                       kc, vc, page_table, slopes, lams, sg, n_heads, lam_init)

    h, hn, code, gates, cnt = _outproj((pool_p, o_p, xp), (pool_s.reshape(ns_, pw), o_s.reshape(ns_, qk_w), xs),
                                       w_out_b, g2, rwt_b, rb)

    counts = cnt[:, 0].astype(I32)
    n_blocks = -(-n_total * TOP_K // EXPERT_ROWS) + n_experts
    padded = (counts + EXPERT_ROWS - 1) // EXPERT_ROWS * EXPERT_ROWS
    pad_ends = jnp.cumsum(padded)
    pad_starts = pad_ends - padded
    n_used = (pad_ends[-1:] // EXPERT_ROWS).astype(I32)
    block_start = jnp.arange(n_blocks, dtype=I32) * EXPERT_ROWS
    block_e = jnp.minimum(jnp.sum((pad_ends[None, :] <= block_start[:, None]).astype(I32), axis=1), n_experts - 1)
    of_block = (block_e[:, None] == jnp.arange(n_experts, dtype=I32)[None, :]).astype(I32)
    real_end = jnp.sum(of_block * (pad_starts + counts)[None, :], axis=1)
    block_valid = jnp.clip(real_end - block_start, 0, EXPERT_ROWS)

    sorted_rows = _sorted_rows(pad_starts, code).reshape(-1)
    xs_rows, w1g, w1l = _dispatch(sorted_rows, pad_starts + counts, pad_ends, n_used, hn, w1.reshape(w1.shape[1:]),
                                  n_blocks)
    yb = _experts(block_e, block_valid, n_used, xs_rows, w1g, w1l, b1g, b1l, w2b, b2e, n_blocks)
    y_p, y_s = _combine(sorted_rows, yb, gates, h, gf, np_)

    return (y_p.reshape(batch, seq, d), y_s.reshape(bs, n_new, d),
            u_p.reshape(batch, seq, pw)[None, :, seq - state_pool.shape[2]:],
            new_pool_s[None],
            k_p.reshape(1, batch, seq, n_heads, HEAD_W), v_p.reshape(1, batch, seq, n_heads, HEAD_W),
            k_s.reshape(1, bs, n_new, n_heads, HEAD_W), v_s.reshape(1, bs, n_new, n_heads, HEAD_W))
```

```python
import functools
import math

import jax
import jax.numpy as jnp
from jax import lax
from jax.experimental import pallas as pl
from jax.experimental.pallas import tpu as pltpu

F32 = jnp.float32
BF16 = jnp.bfloat16
I32 = jnp.int32

RMS_EPS = 1e-6
NEG_INF = -1e30
HEAD_DIM = 64
HEAD_W = 2 * HEAD_DIM
POOL_WINDOWS = (2, 4, 8, 16)
POOL_HALO = 16
TOP_K = 4
RANK_BITS = 20
SWIGLU_LIMIT = 7.0
SWIGLU_ALPHA = 1.702
LANES = 128
SUBLANES = 8
ROW_TILE = 512
ATTN_TILE = 512
SAMPLE_HEADS = 2
ATTN_HEADS = 4
LOG2E = 1.4426950408889634
N_SPLIT = 3
POS_LOW = 16
BF16_ROWS = 16
EXPERT_ROWS = 512
COMBINE_TILE = 512
VMEM_LIMIT = 48 * 1024 * 1024


def _cparams(*sem):
    return pltpu.CompilerParams(dimension_semantics=sem, vmem_limit_bytes=VMEM_LIMIT)


def _dot(a, b):
    return jnp.dot(a, b, preferred_element_type=F32)


def _dot_nt(a, b):
    return lax.dot_general(a, b, (((1,), (1,)), ((), ())), preferred_element_type=F32)


def _rms(x, g):
    return x * lax.rsqrt(jnp.mean(x * x, axis=-1, keepdims=True) + RMS_EPS) * g


def _inproj_kernel(x_ref, g_ref, w_ref, wvt_ref, u_ref, q_ref, k_ref, v_ref, *attn_refs, pool_w, qk_w, key_tile):
    xb = _rms(x_ref[...], g_ref[...]).astype(BF16)
    c0, c1, c2 = pool_w, pool_w + qk_w, pool_w + 2 * qk_w
    u_ref[...] = _dot(xb, w_ref[:, 0:c0])
    q_ref[...] = _dot(xb, w_ref[:, c0:c1]) * (HEAD_DIM ** -0.5)
    k = _dot(xb, w_ref[:, c1:c2])
    v = _dot(xb, w_ref[:, c2:])
    n_heads = qk_w // HEAD_W
    tm = k.shape[0]
    for h in range(n_heads):
        k_ref[pl.ds(h, tm, stride=n_heads), :] = k[:, h * HEAD_W:(h + 1) * HEAD_W]
        v_ref[pl.ds(h, tm, stride=n_heads), :] = v[:, h * HEAD_W:(h + 1) * HEAD_W]
    if attn_refs:
        kb_ref, vt_ref = attn_refs
        kb_ref[...] = k.astype(BF16)
        vt = _dot_nt(wvt_ref[...], xb)
        for c in range(vt_ref.shape[0]):
            vt_ref[c] = vt[:, c * key_tile:(c + 1) * key_tile].astype(BF16)


def _inproj(x, g1, w_in_b, wvt_b, pool_w, qk_w, key_tile=None):
    n, d = x.shape
    val_w = wvt_b.shape[0]
    tm = min(ROW_TILE, n)
    row = lambda w: pl.BlockSpec((tm, w), lambda i: (i, 0))
    full = lambda a: pl.BlockSpec(a.shape, lambda i: (0,) * a.ndim)
    n_heads = qk_w // HEAD_W
    assert val_w == qk_w
    heads = pl.BlockSpec((tm * n_heads, HEAD_W), lambda i: (i, 0))
    out_specs = [row(pool_w), row(qk_w), heads, heads]
    out_shape = [jax.ShapeDtypeStruct((n, pool_w), F32), jax.ShapeDtypeStruct((n, qk_w), F32),
                 jax.ShapeDtypeStruct((n * n_heads, HEAD_W), F32), jax.ShapeDtypeStruct((n * n_heads, HEAD_W), F32)]
    if key_tile is not None:
        assert tm % key_tile == 0
        out_specs += [row(qk_w), pl.BlockSpec((tm // key_tile, val_w, key_tile), lambda i: (i, 0, 0))]
        out_shape += [jax.ShapeDtypeStruct((n, qk_w), BF16), jax.ShapeDtypeStruct((n // key_tile, val_w, key_tile), BF16)]
    return pl.pallas_call(
        functools.partial(_inproj_kernel, pool_w=pool_w, qk_w=qk_w, key_tile=key_tile),
        grid=(n // tm,),
        in_specs=[row(d), full(g1), full(w_in_b), full(wvt_b)],
        out_specs=out_specs,
        out_shape=out_shape,
        compiler_params=_cparams("parallel"),
        name="inproj",
    )(x, g1, w_in_b, wvt_b)


def _pool_group(s_win, tok, cnt, pw, ps):
    d = s_win / cnt - tok
    return _dot(d.astype(BF16), pw) * ps


def _pool_prompt_kernel(u_ref, pw_ref, ps_ref, o_ref, carry_ref, *, tt):
    j = pl.program_id(1)

    @pl.when(j == 0)
    def _():
        carry_ref[...] = jnp.zeros_like(carry_ref)

    cur = u_ref[...]
    ext = jnp.concatenate([carry_ref[...], cur], axis=0)
    carry_ref[...] = cur[tt - POOL_HALO:, :]
    pos = (j * tt + lax.broadcasted_iota(I32, (tt, 1), 0)).astype(F32)
    gw = cur.shape[1] // len(POOL_WINDOWS)
    for g, w in enumerate(POOL_WINDOWS):
        cols = slice(g * gw, (g + 1) * gw)
        e = ext[:, cols]
        s, span = e, 1
        while span < w:
            s = s + pltpu.roll(s, span, axis=0)
            span *= 2
        cnt = jnp.minimum(float(w), pos + 1.0)
        out = _pool_group(s[POOL_HALO:, :], e[POOL_HALO:, :], cnt, pw_ref[g], ps_ref[:, cols])
        o_ref[:, cols] = out.astype(o_ref.dtype)


def _pool_prompt(u, pool_w_b, pool_scale, batch, seq):
    pw = u.shape[1]
    tt = min(ROW_TILE, seq)
    u3 = u.reshape(batch, seq, pw)
    out = pl.pallas_call(
        functools.partial(_pool_prompt_kernel, tt=tt),
        grid=(batch, seq // tt),
        in_specs=[pl.BlockSpec((None, tt, pw), lambda b, j: (b, j, 0)),
                  pl.BlockSpec(pool_w_b.shape, lambda b, j: (0, 0, 0)),
                  pl.BlockSpec(pool_scale.shape, lambda b, j: (0, 0))],
        out_specs=pl.BlockSpec((None, tt, pw), lambda b, j: (b, j, 0)),
        out_shape=jax.ShapeDtypeStruct((batch, seq, pw), BF16),
        scratch_shapes=[pltpu.VMEM((POOL_HALO, pw), F32)],
        compiler_params=_cparams("arbitrary", "arbitrary"),
        name="pool_prompt",
    )(u3, pool_w_b, pool_scale)
    return out.reshape(batch * seq, pw)


def _pool_sample_kernel(sp_ref, u_ref, pw_ref, ps_ref, o_ref, np_ref, *, n_new, n_buf):
    rows = [sp_ref[:, r, :] for r in range(n_buf)] + [u_ref[:, r, :] for r in range(n_new)]
    for r in range(n_buf):
        np_ref[:, r, :] = rows[n_new + r]
    gw = rows[0].shape[1] // len(POOL_WINDOWS)
    for t in range(n_new):
        i = n_buf + t
        for g, w in enumerate(POOL_WINDOWS):
            cols = slice(g * gw, (g + 1) * gw)
            s = rows[i][:, cols]
            for back in range(1, w):
                s = s + rows[i - back][:, cols]
            out = _pool_group(s, rows[i][:, cols], float(w), pw_ref[g], ps_ref[:, cols])
            o_ref[:, t, cols] = out


def _pool_sample(state_pool, u3, pool_w_b, pool_scale):
    bs, n_buf, pw = state_pool.shape
    n_new = u3.shape[1]
    full = lambda a: pl.BlockSpec(a.shape, lambda i: (0,) * a.ndim)
    return pl.pallas_call(
        functools.partial(_pool_sample_kernel, n_new=n_new, n_buf=n_buf),
        grid=(1,),
        in_specs=[full(state_pool), full(u3), full(pool_w_b), full(pool_scale)],
        out_specs=[pl.BlockSpec((bs, n_new, pw), lambda i: (0, 0, 0)),
                   pl.BlockSpec((bs, n_buf, pw), lambda i: (0, 0, 0))],
        out_shape=[jax.ShapeDtypeStruct((bs, n_new, pw), F32),
                   jax.ShapeDtypeStruct((bs, n_buf, pw), F32)],
        compiler_params=_cparams("arbitrary"),
        name="pool_sample",
    )(state_pool, u3, pool_w_b, pool_scale)


def _lambda_full(lq1, lk1, lq2, lk2, lam_init):
    e1 = jnp.exp(jnp.sum(lq1 * lk1, axis=-1, keepdims=True))
    e2 = jnp.exp(jnp.sum(lq2 * lk2, axis=-1, keepdims=True))
    return e1 - e2 + lam_init


def _split_halves(q):
    lane = lax.broadcasted_iota(I32, q.shape, 1)
    zero = jnp.zeros_like(q)
    return jnp.concatenate([jnp.where(lane < HEAD_DIM, q, zero), jnp.where(lane >= HEAD_DIM, q, zero)], axis=0)


def _subln(o, g, lam_init):
    return _rms(o, g) * (1.0 - lam_init)


def _attn_prompt_kernel(slopes_ref, q_ref, k_ref, vt_ref, lq1_ref, lk1_ref, lq2_ref, lk2_ref, g_ref, o_ref,
                        *, tq, lam_init, heads):
    hg = pl.program_id(1)
    i = pl.program_id(2)
    lam = _lambda_full(lq1_ref[...], lk1_ref[...], lq2_ref[...], lk2_ref[...], lam_init)
    head_cols = [slice(hh * HEAD_W, (hh + 1) * HEAD_W) for hh in range(heads)]
    row = lax.broadcasted_iota(I32, (tq, LANES), 0)
    lane = lax.broadcasted_iota(I32, (tq, LANES), 1)
    pos_feat = jnp.where(lane < 2 * N_SPLIT, jnp.where(lane % 2 == 0, row & ~(POS_LOW - 1), row & (POS_LOW - 1)), 0)
    pos_feat = pos_feat.astype(F32).astype(BF16)
    qlane = lax.broadcasted_iota(I32, (2 * tq, LANES), 1)
    slopes2, qqs = [], []
    for hh in range(heads):
        s2 = slopes_ref[hg * heads + hh] * LOG2E
        slopes2.append(s2)
        rest = jnp.full((2 * tq, LANES), s2, F32)
        feat = jnp.zeros((2 * tq, LANES), F32)
        for part in range(N_SPLIT):
            piece = rest.astype(BF16).astype(F32)
            rest = rest - piece
            feat = jnp.where(qlane // 2 == part, piece, feat)
        qq = _split_halves((q_ref[:, head_cols[hh]] * LOG2E).astype(BF16))
        qqs.append(jnp.concatenate([qq, feat.astype(BF16)], axis=1))
    ones = jnp.ones((BF16_ROWS, tq), BF16)

    def head_step(hh, j, carry, diag):
        m, acc = carry
        start = pl.multiple_of(j * tq, tq)
        keys = jnp.concatenate([k_ref[pl.ds(start, tq), head_cols[hh]], pos_feat], axis=1)
        st = _dot_nt(keys, qqs[hh])
        if diag:
            kr = lax.broadcasted_iota(I32, st.shape, 0)
            qc = lax.broadcasted_iota(I32, st.shape, 1)
            qc = jnp.where(qc >= tq, qc - tq, qc)
            st = jnp.where(qc >= kr, st, NEG_INF)
        offset = slopes2[hh] * ((j - i) * tq).astype(F32)
        m_new = jnp.maximum(m, jnp.max(st, axis=0, keepdims=True) + offset)
        alpha = jnp.exp2(m - m_new)
        p = jnp.exp2(st - (m_new - offset)).astype(BF16)
        values = jnp.concatenate([vt_ref[j, head_cols[hh], :], ones], axis=0)
        acc = alpha * acc + _dot(values, p)
        return m_new, acc

    def step(j, carries, diag):
        return tuple(head_step(hh, j, carries[hh], diag) for hh in range(heads))

    init = (jnp.full((1, 2 * tq), NEG_INF, F32), jnp.zeros((HEAD_W + BF16_ROWS, 2 * tq), F32))
    carries = lax.fori_loop(0, i, lambda j, cr: step(j, cr, False), (init,) * heads)
    carries = step(i, carries, True)
    for hh, (_, acc) in enumerate(carries):
        o = acc[:HEAD_W] / acc[HEAD_W:HEAD_W + 1]
        o = (o[:, :tq] - lam * o[:, tq:]).T
        o_ref[:, head_cols[hh]] = _subln(o, g_ref[...], lam_init).astype(o_ref.dtype)


def _attn_prompt(q, kb, vt, slopes, lams, subln_g, batch, seq, n_heads, lam_init, tq):
    heads = math.gcd(n_heads, ATTN_HEADS)
    width = heads * HEAD_W
    q3, k3 = (a.reshape(batch, seq, n_heads * HEAD_W) for a in (q, kb))
    small = lambda a: pl.BlockSpec(a.shape, lambda b, h, i, s: (0,) * a.ndim)
    grid_spec = pltpu.PrefetchScalarGridSpec(
        num_scalar_prefetch=1,
        grid=(batch, n_heads // heads, seq // tq),
        in_specs=[pl.BlockSpec((None, tq, width), lambda b, h, i, s: (b, i, h)),
                  pl.BlockSpec((None, seq, width), lambda b, h, i, s: (b, 0, h)),
                  pl.BlockSpec((seq // tq, width, tq), lambda b, h, i, s: (b, h, 0)),
                  *[small(a) for a in lams], small(subln_g)],
        out_specs=pl.BlockSpec((None, tq, width), lambda b, h, i, s: (b, i, h)),
    )
    out = pl.pallas_call(
        functools.partial(_attn_prompt_kernel, tq=tq, lam_init=lam_init, heads=heads),
        grid_spec=grid_spec,
        out_shape=jax.ShapeDtypeStruct((batch, seq, n_heads * HEAD_W), BF16),
        compiler_params=_cparams("parallel", "parallel", "arbitrary"),
        name="attn_prompt",
    )(slopes, q3, k3, vt, *lams, subln_g)
    return out.reshape(batch * seq, n_heads * HEAD_W)


def _attn_sample_kernel(pt_ref, slopes_ref, q_ref, kn_ref, vn_ref, kc_ref, vc_ref,
                        lq1_ref, lk1_ref, lq2_ref, lk2_ref, g_ref, o_ref, kbuf, vbuf, sem,
                        *, n_pages, page_rows, n_heads, n_new, lam_init):
    b = pl.program_id(0)
    nb = pl.num_programs(0)
    slot = b % 2
    past = n_pages * page_rows // n_heads

    def page_copies(seq, sl, pg):
        p = pt_ref[seq * n_pages + pg]
        dst = pl.ds(pg * page_rows, page_rows)
        return (pltpu.make_async_copy(kc_ref.at[p], kbuf.at[sl, dst, :], sem.at[sl, 0]),
                pltpu.make_async_copy(vc_ref.at[p], vbuf.at[sl, dst, :], sem.at[sl, 1]))

    def start_seq(seq, sl):
        for pg in range(n_pages):
            for cp in page_copies(seq, sl, pg):
                cp.start()

    @pl.when(b == 0)
    def _():
        start_seq(0, 0)

    @pl.when(b + 1 < nb)
    def _():
        start_seq(b + 1, 1 - slot)

    for pg in range(n_pages):
        for cp in page_copies(b, slot, pg):
            cp.wait()

    lam = _lambda_full(lq1_ref[...], lk1_ref[...], lq2_ref[...], lk2_ref[...], lam_init)
    grp = 2 * SUBLANES
    rows = SAMPLE_HEADS * grp
    r = lax.broadcasted_iota(I32, (rows, past), 0) % grp
    t_past = jnp.where(r >= n_new, r - n_new, r)
    dist_past = (past + t_past - lax.broadcasted_iota(I32, (rows, past), 1)).astype(F32)
    r1 = lax.broadcasted_iota(I32, (rows, 1), 0)
    t_new = jnp.where(r1 % grp >= n_new, r1 % grp - n_new, r1 % grp)
    pad = jnp.zeros((grp - 2 * n_new, HEAD_W), F32)
    for h0 in range(0, n_heads, SAMPLE_HEADS):
        heads = range(h0, h0 + SAMPLE_HEADS)
        cols = slice(h0 * HEAD_W, (h0 + SAMPLE_HEADS) * HEAD_W)
        slope = jnp.concatenate([jnp.full((grp, 1), slopes_ref[h], F32) for h in heads], axis=0)
        blocks = []
        for i, h in enumerate(heads):
            qh = jnp.concatenate([_split_halves(q_ref[:, h * HEAD_W:(h + 1) * HEAD_W]), pad], axis=0)
            zero = jnp.zeros_like(qh)
            blocks.append(jnp.concatenate([qh if j == i else zero for j in range(SAMPLE_HEADS)], axis=1))
        qq = jnp.concatenate(blocks, axis=0)
        kh = jnp.concatenate([kbuf[slot, pl.ds(h, past, stride=n_heads), :] for h in heads], axis=1).astype(BF16)
        vh = jnp.concatenate([vbuf[slot, pl.ds(h, past, stride=n_heads), :] for h in heads], axis=1).astype(BF16)
        s = _dot_nt(qq.astype(BF16), kh) - slope * dist_past
        kn = kn_ref[:, cols]
        vn = vn_ref[:, cols]
        s_new = []
        for c in range(n_new):
            sc = jnp.sum(qq * kn[c:c + 1, :], axis=-1, keepdims=True) - slope * (t_new - c).astype(F32)
            s_new.append(jnp.where(t_new >= c, sc, NEG_INF))
        m = jnp.max(s, axis=-1, keepdims=True)
        for sc in s_new:
            m = jnp.maximum(m, sc)
        p = jnp.exp(s - m)
        l = jnp.sum(p, axis=-1, keepdims=True)
        acc = _dot(p.astype(BF16), vh)
        for c, sc in enumerate(s_new):
            pc = jnp.exp(sc - m)
            l = l + pc
            acc = acc + pc * vn[c:c + 1, :]
        o = acc / l
        for i, h in enumerate(heads):
            oh = o[i * grp:(i + 1) * grp, i * HEAD_W:(i + 1) * HEAD_W]
            oh = oh[0:n_new] - lam * oh[n_new:2 * n_new]
            o_ref[:, h * HEAD_W:(h + 1) * HEAD_W] = _subln(oh, g_ref[...], lam_init)


def _attn_sample(q3, kn3, vn3, kc, vc, page_table, slopes, lams, subln_g, n_heads, lam_init):
    bs, n_new, qk_w = q3.shape
    n_pages = page_table.shape[1]
    page_rows = kc.shape[1]
    small = lambda a: pl.BlockSpec(a.shape, lambda b, pt, s: (0,) * a.ndim)
    tok = pl.BlockSpec((None, n_new, qk_w), lambda b, pt, s: (b, 0, 0))
    grid_spec = pltpu.PrefetchScalarGridSpec(
        num_scalar_prefetch=2,
        grid=(bs,),
        in_specs=[tok, tok, tok, pl.BlockSpec(memory_space=pl.ANY), pl.BlockSpec(memory_space=pl.ANY),
                  *[small(a) for a in lams], small(subln_g)],
        out_specs=tok,
        scratch_shapes=[pltpu.VMEM((2, n_pages * page_rows, LANES), F32),
                        pltpu.VMEM((2, n_pages * page_rows, LANES), F32),
                        pltpu.SemaphoreType.DMA((2, 2))],
    )
    return pl.pallas_call(
        functools.partial(_attn_sample_kernel, n_pages=n_pages, page_rows=page_rows, n_heads=n_heads,
                          n_new=n_new, lam_init=lam_init),
        grid_spec=grid_spec,
        out_shape=jax.ShapeDtypeStruct((bs, n_new, qk_w), F32),
        compiler_params=_cparams("arbitrary"),
        name="attn_sample",
    )(page_table.reshape(-1), slopes, q3, kn3, vn3, kc, vc, *lams, subln_g)


def _outproj_kernel(pool_p_ref, o_p_ref, x_p_ref, pool_s_ref, o_s_ref, x_s_ref, wo_ref, g2_ref, rwt_ref, rb_ref,
                    earlier_ref, h_ref, hn_ref, code_ref, gate_ref, cnt_out_ref, run_ref, *, tm, n_experts,
                    prompt_tiles):
    i = pl.program_id(0)

    @pl.when(i == 0)
    def _():
        run_ref[...] = jnp.zeros_like(run_ref)

    tile = functools.partial(_outproj_tile, wo_ref=wo_ref, g2_ref=g2_ref, rwt_ref=rwt_ref, rb_ref=rb_ref,
                             earlier_ref=earlier_ref, h_ref=h_ref, hn_ref=hn_ref, code_ref=code_ref, gate_ref=gate_ref,
                             cnt_out_ref=cnt_out_ref, run_ref=run_ref, tm=tm, n_experts=n_experts)
    pl.when(i < prompt_tiles)(functools.partial(tile, pool_p_ref, o_p_ref, x_p_ref))
    pl.when(i >= prompt_tiles)(functools.partial(tile, pool_s_ref, o_s_ref, x_s_ref))


def _outproj_tile(pool_ref, o_ref, x_ref, *, wo_ref, g2_ref, rwt_ref, rb_ref, earlier_ref, h_ref, hn_ref, code_ref,
                  gate_ref, cnt_out_ref, run_ref, tm, n_experts):
    pw = pool_ref.shape[1]
    mix = _dot(pool_ref[...].astype(BF16), wo_ref[0:pw, :]) + _dot(o_ref[...].astype(BF16), wo_ref[pw:, :])
    h = x_ref[...] + mix
    h_ref[...] = h
    hn = _rms(h, g2_ref[...])
    for s in range(hn.shape[1] // LANES):
        hn_ref[pl.ds(s, tm, stride=SUBLANES), :] = hn[:, s * LANES:(s + 1) * LANES]
    logits = _dot_nt(rwt_ref[...], hn.astype(BF16)) + rb_ref[...]
    expert = lax.broadcasted_iota(I32, logits.shape, 0)
    vals, idxs = [], []
    for _ in range(TOP_K):
        m = jnp.max(logits, axis=0, keepdims=True)
        idx = jnp.min(jnp.where(logits == m, expert, n_experts), axis=0, keepdims=True)
        vals.append(m)
        idxs.append(idx)
        logits = jnp.where(expert == idx, -jnp.inf, logits)
    ex = [jnp.exp(v - vals[0]) for v in vals]
    den = ex[0]
    for e in ex[1:]:
        den = den + e
    chosen = jnp.zeros(logits.shape, F32)
    for idx in idxs:
        chosen = chosen + jnp.where(expert == idx, 1.0, 0.0)
    before = _dot(chosen.astype(BF16), earlier_ref[...]) + run_ref[...]
    for k in range(TOP_K):
        gate_ref[k:k + 1, :] = ex[k] / den
        rank = jnp.sum(jnp.where(expert == idxs[k], before, 0.0), axis=0, keepdims=True).astype(I32)
        code_ref[k:k + 1, :] = idxs[k] * (1 << RANK_BITS) + rank
    gate_ref[TOP_K:, :] = jnp.zeros((gate_ref.shape[0] - TOP_K, tm), F32)
    run_ref[...] = run_ref[...] + jnp.sum(chosen, axis=1, keepdims=True)
    cnt_out_ref[...] = run_ref[...]


def _outproj(prompt, sample, w_out_b, g2, rwt_b, rb):
    np_, d = prompt[2].shape
    ns_ = sample[2].shape[0]
    n_total = np_ + ns_
    n_experts = rwt_b.shape[0]
    tm = min(ROW_TILE, math.gcd(np_, ns_))
    prompt_tiles = np_ // tm
    assert d == SUBLANES * LANES and n_total < (1 << RANK_BITS)
    first = lambda a: pl.BlockSpec((tm, a.shape[1]), lambda i: (jnp.minimum(i, prompt_tiles - 1), 0))
    second = lambda a: pl.BlockSpec((tm, a.shape[1]), lambda i: (jnp.maximum(i - prompt_tiles, 0), 0))
    row = lambda w: pl.BlockSpec((tm, w), lambda i: (i, 0))
    col = lambda r: pl.BlockSpec((r, tm), lambda i: (0, i))
    full = lambda a: pl.BlockSpec(a.shape, lambda i: (0,) * a.ndim)
    counts = jax.ShapeDtypeStruct((n_experts, 1), F32)
    earlier = jnp.triu(jnp.ones((tm, tm), BF16), 1)
    return pl.pallas_call(
        functools.partial(_outproj_kernel, tm=tm, n_experts=n_experts, prompt_tiles=prompt_tiles),
        grid=(n_total // tm,),
        in_specs=[*[first(a) for a in prompt], *[second(a) for a in sample],
                  full(w_out_b), full(g2), full(rwt_b), full(rb), full(earlier)],
        out_specs=[row(d), pl.BlockSpec((tm * SUBLANES, LANES), lambda i: (i, 0)),
                   col(TOP_K), col(SUBLANES), pl.BlockSpec(counts.shape, lambda i: (0, 0))],
        out_shape=[jax.ShapeDtypeStruct((n_total, d), F32), jax.ShapeDtypeStruct((n_total * SUBLANES, LANES), F32),
                   jax.ShapeDtypeStruct((TOP_K, n_total), I32), jax.ShapeDtypeStruct((SUBLANES, n_total), F32),
                   counts],
        scratch_shapes=[pltpu.VMEM(counts.shape, F32)],
        compiler_params=_cparams("arbitrary"),
        name="outproj_router",
    )(*prompt, *sample, w_out_b, g2, rwt_b, rb, earlier)


def _sorted_rows_kernel(start_ref, code_ref, rows_ref, *, n_experts):
    code = code_ref[...]
    expert = code >> RANK_BITS
    rows = code & ((1 << RANK_BITS) - 1)
    for e in range(n_experts):
        rows = rows + jnp.where(expert == e, start_ref[e], 0)
    rows_ref[...] = rows


def _sorted_rows(starts, code):
    return pl.pallas_call(
        functools.partial(_sorted_rows_kernel, n_experts=starts.shape[0]),
        grid=(1,),
        in_specs=[pl.BlockSpec(memory_space=pltpu.SMEM), pl.BlockSpec(code.shape, lambda i: (0, 0))],
        out_specs=pl.BlockSpec(code.shape, lambda i: (0, 0)),
        out_shape=jax.ShapeDtypeStruct(code.shape, I32),
        compiler_params=_cparams("arbitrary"),
        name="sorted_rows",
    )(starts, code)


MXU_DIM = 256


def _split_w1_tile(w_ref, g_ref, l_ref):
    r = lax.broadcasted_iota(I32, (MXU_DIM, MXU_DIM), 0)
    c = lax.broadcasted_iota(I32, (MXU_DIM, MXU_DIM), 1)
    half = MXU_DIM // 2
    source = jnp.where(c < half, 2 * c, 2 * (c - half) + 1)
    perm = jnp.where(r == source, 1.0, 0.0).astype(BF16)
    for blk in range(w_ref.shape[1] // MXU_DIM):
        res = _dot(w_ref[:, blk * MXU_DIM:(blk + 1) * MXU_DIM].astype(BF16), perm)
        g_ref[:, blk * half:(blk + 1) * half] = res[:, :half].astype(BF16)
        l_ref[:, blk * half:(blk + 1) * half] = res[:, half:].astype(BF16)


def _dispatch_kernel(row_ref, zlo_ref, zhi_ref, nu_ref, hn_ref, w1_ref, xs_ref, w1g_ref, w1l_ref, zero_ref, sem,
                     *, tm, n_total, rows, n_experts, n_blocks):
    i = pl.program_id(0)
    tile = tm * SUBLANES
    block = rows * SUBLANES

    def pad_pieces(e):
        lo, n = zlo_ref[e], zhi_ref[e] - zlo_ref[e]
        for bit in range(rows.bit_length() - 1):
            size = 1 << bit
            below = n & (size - 1)
            start = pl.multiple_of((lo + below) * SUBLANES, SUBLANES)
            cp = pltpu.make_async_copy(zero_ref.at[pl.ds(0, size * SUBLANES), :],
                                       xs_ref.at[pl.ds(start, size * SUBLANES), :], sem.at[1])
            yield (n & size) != 0, cp

    def tail_copy(blk):
        start = pl.multiple_of(blk * block, block)
        return pltpu.make_async_copy(zero_ref, xs_ref.at[pl.ds(start, block), :], sem.at[1])

    def for_zero_copies(fn):
        def per_expert(e, carry):
            for used, cp in pad_pieces(e):
                pl.when(used)(functools.partial(fn, cp))
            return carry
        lax.fori_loop(0, n_experts, per_expert, 0)

        def per_block(blk, carry):
            fn(tail_copy(blk))
            return carry
        lax.fori_loop(nu_ref[0], n_blocks, per_block, 0)

    @pl.when(i == 0)
    def _():
        zero_ref[...] = jnp.zeros_like(zero_ref)
        for_zero_copies(lambda cp: cp.start())

    def body(t, carry):
        src = pl.multiple_of(t * SUBLANES, SUBLANES)
        for k in range(TOP_K):
            dst = pl.multiple_of(row_ref[k * n_total + i * tm + t] * SUBLANES, SUBLANES)
            pltpu.make_async_copy(hn_ref.at[pl.ds(src, SUBLANES), :], xs_ref.at[pl.ds(dst, SUBLANES), :],
                                  sem.at[0]).start()
        return carry
    lax.fori_loop(0, tm, body, 0)

    _split_w1_tile(w1_ref, w1g_ref, w1l_ref)

    for _ in range(TOP_K):
        pltpu.make_async_copy(hn_ref, xs_ref.at[pl.ds(0, tile), :], sem.at[0]).wait()

    @pl.when(i == pl.num_programs(0) - 1)
    def _():
        for_zero_copies(lambda cp: cp.wait())


def _dispatch(sorted_rows, zlo, zhi, n_used, hn, w1, n_blocks):
    n_total = hn.shape[0] // SUBLANES
    n_exp, d, ff2 = w1.shape
    tr = next(t for t in (ROW_TILE, 2 * ROW_TILE, d) if d % t == 0 and n_total % (n_exp * (d // t)) == 0)
    row_tiles = d // tr
    steps = n_exp * row_tiles
    tm = n_total // steps
    rows = EXPERT_ROWS
    assert rows & (rows - 1) == 0
    half = pl.BlockSpec((None, tr, ff2 // 2), lambda i, *_: (i // row_tiles, i % row_tiles, 0))
    grid_spec = pltpu.PrefetchScalarGridSpec(
        num_scalar_prefetch=4,
        grid=(steps,),
        in_specs=[pl.BlockSpec((tm * SUBLANES, LANES), lambda i, *_: (i, 0)),
                  pl.BlockSpec((None, tr, ff2), lambda i, *_: (i // row_tiles, i % row_tiles, 0))],
        out_specs=[pl.BlockSpec(memory_space=pl.ANY), half, half],
        scratch_shapes=[pltpu.VMEM((rows * SUBLANES, LANES), F32), pltpu.SemaphoreType.DMA((2,))],
    )
    return pl.pallas_call(
        functools.partial(_dispatch_kernel, tm=tm, n_total=n_total, rows=rows, n_experts=n_exp, n_blocks=n_blocks),
        grid_spec=grid_spec,
        out_shape=[jax.ShapeDtypeStruct((n_blocks * rows * SUBLANES, LANES), F32),
                   jax.ShapeDtypeStruct((n_exp, d, ff2 // 2), BF16), jax.ShapeDtypeStruct((n_exp, d, ff2 // 2), BF16)],
        compiler_params=_cparams("arbitrary"),
        name="dispatch_split_w1",
    )(sorted_rows, zlo, zhi, n_used, hn, w1)


def _expert_kernel(be_ref, nu_ref, xs_ref, w1g_ref, w1l_ref, b1g_ref, b1l_ref, w2_ref, b2_ref, y_ref, *, rows):
    j = pl.program_id(0)
    n_used = nu_ref[0]

    @pl.when(j < n_used)
    def _():
        x = jnp.concatenate([xs_ref[pl.ds(s, rows, stride=SUBLANES), :] for s in range(SUBLANES)],
                            axis=-1).astype(BF16)
        glu = jnp.minimum(_dot(x, w1g_ref[...]) + b1g_ref[...], SWIGLU_LIMIT)
        lin = jnp.clip(_dot(x, w1l_ref[...]) + b1l_ref[...], -SWIGLU_LIMIT, SWIGLU_LIMIT)
        act = glu * jax.nn.sigmoid(SWIGLU_ALPHA * glu) * (lin + 1.0)
        y = _dot(act.astype(BF16), w2_ref[...].astype(BF16)) + b2_ref[...]
        for s in range(SUBLANES):
            y_ref[pl.ds(s, rows, stride=SUBLANES), :] = y[:, s * LANES:(s + 1) * LANES]

    @pl.when(j >= n_used)
    def _():
        y_ref[...] = jnp.zeros_like(y_ref)


def _experts(block_e, n_used, xs, w1g, w1l, b1g, b1l, w2, b2, n_blocks):
    rows = EXPERT_ROWS
    tile = rows * SUBLANES
    wspec = lambda a: pl.BlockSpec((None,) + a.shape[1:], lambda j, be, nu: (be[j], 0, 0))
    grid_spec = pltpu.PrefetchScalarGridSpec(
        num_scalar_prefetch=2,
        grid=(n_blocks,),
        in_specs=[pl.BlockSpec((tile, LANES), lambda j, be, nu: (jnp.minimum(j, nu[0] - 1), 0)),
                  wspec(w1g), wspec(w1l), wspec(b1g), wspec(b1l), wspec(w2), wspec(b2)],
        out_specs=pl.BlockSpec((tile, LANES), lambda j, be, nu: (j, 0)),
    )
    return pl.pallas_call(
        functools.partial(_expert_kernel, rows=rows),
        grid_spec=grid_spec,
        out_shape=jax.ShapeDtypeStruct((n_blocks * tile, LANES), F32),
        compiler_params=_cparams("arbitrary"),
        name="experts",
    )(block_e, n_used, xs, w1g, w1l, b1g, b1l, w2, b2)


def _combine_kernel(row_ref, yb_ref, gate_ref, h_ref, g_ref, yp_ref, ys_ref, cbuf, sem, *, tm, n_total,
                    prompt_tiles):
    i = pl.program_id(0)
    n = pl.num_programs(0)
    slot = i % 2
    tile = tm * SUBLANES

    def gather(blk, sl):
        def body(t, carry):
            for k in range(TOP_K):
                src = pl.multiple_of(row_ref[k * n_total + blk * tm + t] * SUBLANES, SUBLANES)
                dst = pl.multiple_of((k * tm + t) * SUBLANES, SUBLANES)
                pltpu.make_async_copy(yb_ref.at[pl.ds(src, SUBLANES), :], cbuf.at[sl, pl.ds(dst, SUBLANES), :],
                                      sem.at[sl]).start()
            return carry
        lax.fori_loop(0, tm, body, 0)

    @pl.when(i == 0)
    def _():
        gather(0, 0)

    @pl.when(i + 1 < n)
    def _():
        gather(i + 1, 1 - slot)

    pltpu.make_async_copy(yb_ref.at[pl.ds(0, TOP_K * tile), :], cbuf.at[slot], sem.at[slot]).wait()
    gates = gate_ref[...].T
    slabs = []
    for s in range(SUBLANES):
        acc = None
        for k in range(TOP_K):
            part = gates[:, k:k + 1] * cbuf[slot, pl.ds(k * tile + s, tm, stride=SUBLANES), :]
            acc = part if acc is None else acc + part
        slabs.append(acc)
    hf = h_ref[...] + jnp.concatenate(slabs, axis=-1)
    y = _rms(hf, g_ref[...])

    @pl.when(i < prompt_tiles)
    def _():
        yp_ref[...] = y

    @pl.when(i >= prompt_tiles)
    def _():
        ys_ref[...] = y


def _combine(rows, yb, gates, h, gf, np_):
    n, d = h.shape
    ns_ = n - np_
    tm = min(COMBINE_TILE, math.gcd(np_, ns_))
    prompt_tiles = np_ // tm
    row = lambda w: pl.BlockSpec((tm, w), lambda i, *_: (i, 0))
    grid_spec = pltpu.PrefetchScalarGridSpec(
        num_scalar_prefetch=1,
        grid=(n // tm,),
        in_specs=[pl.BlockSpec(memory_space=pl.ANY), pl.BlockSpec((gates.shape[0], tm), lambda i, *_: (0, i)), row(d),
                  pl.BlockSpec(gf.shape, lambda i, *_: (0, 0))],
        out_specs=[pl.BlockSpec((tm, d), lambda i, *_: (jnp.minimum(i, prompt_tiles - 1), 0)),
                   pl.BlockSpec((tm, d), lambda i, *_: (jnp.maximum(i - prompt_tiles, 0), 0))],
        scratch_shapes=[pltpu.VMEM((2, TOP_K * tm * SUBLANES, LANES), F32), pltpu.SemaphoreType.DMA((2,))],
    )
    return pl.pallas_call(
        functools.partial(_combine_kernel, tm=tm, n_total=n, prompt_tiles=prompt_tiles),
        grid_spec=grid_spec,
        out_shape=[jax.ShapeDtypeStruct((np_, d), F32), jax.ShapeDtypeStruct((ns_, d), F32)],
        compiler_params=_cparams("arbitrary"),
        name="combine_norm",
    )(rows, yb, gates, h, gf)


def kernel(x_prompt, x_sample, state_pool, cache_k, cache_v, page_table, norm1_g, w_in, pool_w, pool_scale,
           lambda_q1, lambda_k1, lambda_q2, lambda_k2, subln_g, w_out, norm2_g, router_w, router_b,
           w1, b1, w2, b2, normf_g):
    batch, seq, d = x_prompt.shape
    bs, n_new, _ = x_sample.shape
    depth = w_in.shape[0]
    assert depth == 1, "single-layer trunk"
    n_heads = cache_k.shape[3]
    page_size = cache_k.shape[2]
    pw = pool_w.shape[1] * pool_w.shape[2]
    qk_w = n_heads * HEAD_W
    n_experts = router_w.shape[2]
    np_, ns_ = batch * seq, bs * n_new
    n_total = np_ + ns_
    lam_init = 0.8 - 0.6 * math.exp(-0.3 * 0)

    g1 = norm1_g[0][None]
    w_in_b = w_in[0].astype(BF16)
    wvt_b = w_in_b[:, pw + 2 * qk_w:].T
    pool_w_b = pool_w[0].astype(BF16)
    ps = pool_scale[0][None]
    lams = [a[0][None] for a in (lambda_q1, lambda_k1, lambda_q2, lambda_k2)]
    sg = subln_g[0][None]
    slopes = jnp.exp2(-8.0 * (jnp.arange(n_heads, dtype=F32) + 1.0) / n_heads)
    w_out_b = w_out[0].astype(BF16)
    g2 = norm2_g[0][None]
    rwt_b = router_w[0].T.astype(BF16)
    rb = router_b[0][:, None]
    b1g = b1[0][:, None, 0::2]
    b1l = b1[0][:, None, 1::2]
    w2b = w2.reshape(w2.shape[1:])
    b2e = b2[0][:, None, :]
    gf = normf_g[None]

    xp = x_prompt.reshape(np_, d)
    tq = min(ATTN_TILE, seq)
    u_p, q_p, k_p, v_p, kb_p, vt_p = _inproj(xp, g1, w_in_b, wvt_b, pw, qk_w, key_tile=tq)
    pool_p = _pool_prompt(u_p, pool_w_b, ps, batch, seq)
    o_p = _attn_prompt(q_p, kb_p, vt_p, slopes, lams, sg, batch, seq, n_heads, lam_init, tq)

    xs = x_sample.reshape(ns_, d)
    u_s, q_s, k_s, v_s = _inproj(xs, g1, w_in_b, wvt_b, pw, qk_w)
    pool_s, new_pool_s = _pool_sample(state_pool[0], u_s.reshape(bs, n_new, pw), pool_w_b, ps)
    kc = cache_k.reshape(cache_k.shape[1], page_size * n_heads, HEAD_W)
    vc = cache_v.reshape(cache_v.shape[1], page_size * n_heads, HEAD_W)
    o_s = _attn_sample(q_s.reshape(bs, n_new, qk_w), k_s.reshape(bs, n_new, qk_w), v_s.reshape(bs, n_new, qk_w),
                       kc, vc, page_table, slopes, lams, sg, n_heads, lam_init)

    h, hn, code, gates, cnt = _outproj((pool_p, o_p, xp), (pool_s.reshape(ns_, pw), o_s.reshape(ns_, qk_w), xs),
                                       w_out_b, g2, rwt_b, rb)

    counts = cnt[:, 0].astype(I32)
    n_blocks = -(-n_total * TOP_K // EXPERT_ROWS) + n_experts
    padded = (counts + EXPERT_ROWS - 1) // EXPERT_ROWS * EXPERT_ROWS
    pad_ends = jnp.cumsum(padded)
    pad_starts = pad_ends - padded
    n_used = (pad_ends[-1:] // EXPERT_ROWS).astype(I32)
    block_start = jnp.arange(n_blocks, dtype=I32) * EXPERT_ROWS
    block_e = jnp.minimum(jnp.sum((pad_ends[None, :] <= block_start[:, None]).astype(I32), axis=1), n_experts - 1)

    sorted_rows = _sorted_rows(pad_starts, code).reshape(-1)
    xs_rows, w1g, w1l = _dispatch(sorted_rows, pad_starts + counts, pad_ends, n_used, hn, w1.reshape(w1.shape[1:]),
                                  n_blocks)
    yb = _experts(block_e, n_used, xs_rows, w1g, w1l, b1g, b1l, w2b, b2e, n_blocks)
    y_p, y_s = _combine(sorted_rows, yb, gates, h, gf, np_)

    return (y_p.reshape(batch, seq, d), y_s.reshape(bs, n_new, d),
            u_p.reshape(batch, seq, pw)[None, :, seq - state_pool.shape[2]:],
            new_pool_s[None],
            k_p.reshape(1, batch, seq, n_heads, HEAD_W), v_p.reshape(1, batch, seq, n_heads, HEAD_W),
            k_s.reshape(1, bs, n_new, n_heads, HEAD_W), v_s.reshape(1, bs, n_new, n_heads, HEAD_W))
```

```python
import functools
import math

import jax
import jax.numpy as jnp
from jax import lax
from jax.experimental import pallas as pl
from jax.experimental.pallas import tpu as pltpu

F32 = jnp.float32
BF16 = jnp.bfloat16
I32 = jnp.int32

RMS_EPS = 1e-6
NEG_INF = -1e30
HEAD_DIM = 64
HEAD_W = 2 * HEAD_DIM
POOL_WINDOWS = (2, 4, 8, 16)
POOL_HALO = 16
TOP_K = 4
RANK_BITS = 20
SWIGLU_LIMIT = 7.0
SWIGLU_ALPHA = 1.702
LANES = 128
SUBLANES = 8
ROW_TILE = 512
ATTN_TILE = 512
SAMPLE_HEADS = 2
ATTN_HEADS = 4
LOG2E = 1.4426950408889634
N_SPLIT = 3
POS_LOW = 16
BF16_ROWS = 16
EXPERT_ROWS = 512
COMBINE_TILE = 512
VMEM_LIMIT = 48 * 1024 * 1024


def _cparams(*sem):
    return pltpu.CompilerParams(dimension_semantics=sem, vmem_limit_bytes=VMEM_LIMIT)


def _dot(a, b):
    return jnp.dot(a, b, preferred_element_type=F32)


def _dot_nt(a, b):
    return lax.dot_general(a, b, (((1,), (1,)), ((), ())), preferred_element_type=F32)


def _rms(x, g):
    return x * lax.rsqrt(jnp.mean(x * x, axis=-1, keepdims=True) + RMS_EPS) * g


def _inproj_kernel(x_ref, g_ref, w_ref, wvt_ref, u_ref, q_ref, k_ref, v_ref, *attn_refs, pool_w, qk_w, key_tile):
    xb = _rms(x_ref[...], g_ref[...]).astype(BF16)
    c0, c1, c2 = pool_w, pool_w + qk_w, pool_w + 2 * qk_w
    u_ref[...] = _dot(xb, w_ref[:, 0:c0])
    q_ref[...] = _dot(xb, w_ref[:, c0:c1]) * (HEAD_DIM ** -0.5)
    k = _dot(xb, w_ref[:, c1:c2])
    v = _dot(xb, w_ref[:, c2:])
    n_heads = qk_w // HEAD_W
    tm = k.shape[0]
    for h in range(n_heads):
        k_ref[pl.ds(h, tm, stride=n_heads), :] = k[:, h * HEAD_W:(h + 1) * HEAD_W]
        v_ref[pl.ds(h, tm, stride=n_heads), :] = v[:, h * HEAD_W:(h + 1) * HEAD_W]
    if attn_refs:
        kb_ref, vt_ref = attn_refs
        kb_ref[...] = k.astype(BF16)
        vt = _dot_nt(wvt_ref[...], xb)
        for c in range(vt_ref.shape[0]):
            vt_ref[c] = vt[:, c * key_tile:(c + 1) * key_tile].astype(BF16)


def _inproj(x, g1, w_in_b, wvt_b, pool_w, qk_w, key_tile=None):
    n, d = x.shape
    val_w = wvt_b.shape[0]
    tm = min(ROW_TILE, n)
    row = lambda w: pl.BlockSpec((tm, w), lambda i: (i, 0))
    full = lambda a: pl.BlockSpec(a.shape, lambda i: (0,) * a.ndim)
    n_heads = qk_w // HEAD_W
    assert val_w == qk_w
    heads = pl.BlockSpec((tm * n_heads, HEAD_W), lambda i: (i, 0))
    out_specs = [row(pool_w), row(qk_w), heads, heads]
    out_shape = [jax.ShapeDtypeStruct((n, pool_w), F32), jax.ShapeDtypeStruct((n, qk_w), F32),
                 jax.ShapeDtypeStruct((n * n_heads, HEAD_W), F32), jax.ShapeDtypeStruct((n * n_heads, HEAD_W), F32)]
    if key_tile is not None:
        assert tm % key_tile == 0
        out_specs += [row(qk_w), pl.BlockSpec((tm // key_tile, val_w, key_tile), lambda i: (i, 0, 0))]
        out_shape += [jax.ShapeDtypeStruct((n, qk_w), BF16), jax.ShapeDtypeStruct((n // key_tile, val_w, key_tile), BF16)]
    return pl.pallas_call(
        functools.partial(_inproj_kernel, pool_w=pool_w, qk_w=qk_w, key_tile=key_tile),
        grid=(n // tm,),
        in_specs=[row(d), full(g1), full(w_in_b), full(wvt_b)],
        out_specs=out_specs,
        out_shape=out_shape,
        compiler_params=_cparams("parallel"),
        name="inproj",
    )(x, g1, w_in_b, wvt_b)


def _pool_group(s_win, tok, cnt, pw, ps):
    d = s_win / cnt - tok
    return _dot(d.astype(BF16), pw) * ps


def _pool_prompt_kernel(u_ref, pw_ref, ps_ref, o_ref, carry_ref, *, tt):
    j = pl.program_id(1)

    @pl.when(j == 0)
    def _():
        carry_ref[...] = jnp.zeros_like(carry_ref)

    cur = u_ref[...]
    ext = jnp.concatenate([carry_ref[...], cur], axis=0)
    carry_ref[...] = cur[tt - POOL_HALO:, :]
    pos = (j * tt + lax.broadcasted_iota(I32, (tt, 1), 0)).astype(F32)
    gw = cur.shape[1] // len(POOL_WINDOWS)
    for g, w in enumerate(POOL_WINDOWS):
        cols = slice(g * gw, (g + 1) * gw)
        e = ext[:, cols]
        s, span = e, 1
        while span < w:
            s = s + pltpu.roll(s, span, axis=0)
            span *= 2
        cnt = jnp.minimum(float(w), pos + 1.0)
        out = _pool_group(s[POOL_HALO:, :], e[POOL_HALO:, :], cnt, pw_ref[g], ps_ref[:, cols])
        o_ref[:, cols] = out.astype(o_ref.dtype)


def _pool_prompt(u, pool_w_b, pool_scale, batch, seq):
    pw = u.shape[1]
    tt = min(ROW_TILE, seq)
    u3 = u.reshape(batch, seq, pw)
    out = pl.pallas_call(
        functools.partial(_pool_prompt_kernel, tt=tt),
        grid=(batch, seq // tt),
        in_specs=[pl.BlockSpec((None, tt, pw), lambda b, j: (b, j, 0)),
                  pl.BlockSpec(pool_w_b.shape, lambda b, j: (0, 0, 0)),
                  pl.BlockSpec(pool_scale.shape, lambda b, j: (0, 0))],
        out_specs=pl.BlockSpec((None, tt, pw), lambda b, j: (b, j, 0)),
        out_shape=jax.ShapeDtypeStruct((batch, seq, pw), BF16),
        scratch_shapes=[pltpu.VMEM((POOL_HALO, pw), F32)],
        compiler_params=_cparams("arbitrary", "arbitrary"),
        name="pool_prompt",
    )(u3, pool_w_b, pool_scale)
    return out.reshape(batch * seq, pw)


def _pool_sample_kernel(sp_ref, u_ref, pw_ref, ps_ref, o_ref, np_ref, *, n_new, n_buf):
    rows = [sp_ref[:, r, :] for r in range(n_buf)] + [u_ref[:, r, :] for r in range(n_new)]
    for r in range(n_buf):
        np_ref[:, r, :] = rows[n_new + r]
    gw = rows[0].shape[1] // len(POOL_WINDOWS)
    for t in range(n_new):
        i = n_buf + t
        for g, w in enumerate(POOL_WINDOWS):
            cols = slice(g * gw, (g + 1) * gw)
            s = rows[i][:, cols]
            for back in range(1, w):
                s = s + rows[i - back][:, cols]
            out = _pool_group(s, rows[i][:, cols], float(w), pw_ref[g], ps_ref[:, cols])
            o_ref[:, t, cols] = out


def _pool_sample(state_pool, u3, pool_w_b, pool_scale):
    bs, n_buf, pw = state_pool.shape
    n_new = u3.shape[1]
    full = lambda a: pl.BlockSpec(a.shape, lambda i: (0,) * a.ndim)
    return pl.pallas_call(
        functools.partial(_pool_sample_kernel, n_new=n_new, n_buf=n_buf),
        grid=(1,),
        in_specs=[full(state_pool), full(u3), full(pool_w_b), full(pool_scale)],
        out_specs=[pl.BlockSpec((bs, n_new, pw), lambda i: (0, 0, 0)),
                   pl.BlockSpec((bs, n_buf, pw), lambda i: (0, 0, 0))],
        out_shape=[jax.ShapeDtypeStruct((bs, n_new, pw), F32),
                   jax.ShapeDtypeStruct((bs, n_buf, pw), F32)],
        compiler_params=_cparams("arbitrary"),
        name="pool_sample",
    )(state_pool, u3, pool_w_b, pool_scale)


def _lambda_full(lq1, lk1, lq2, lk2, lam_init):
    e1 = jnp.exp(jnp.sum(lq1 * lk1, axis=-1, keepdims=True))
    e2 = jnp.exp(jnp.sum(lq2 * lk2, axis=-1, keepdims=True))
    return e1 - e2 + lam_init


def _split_halves(q):
    lane = lax.broadcasted_iota(I32, q.shape, 1)
    zero = jnp.zeros_like(q)
    return jnp.concatenate([jnp.where(lane < HEAD_DIM, q, zero), jnp.where(lane >= HEAD_DIM, q, zero)], axis=0)


def _subln(o, g, lam_init):
    return _rms(o, g) * (1.0 - lam_init)


def _prompt_unit(hg, i, lam, slopes_ref, q_ref, k_ref, vt_ref, g_ref, o_ref, *, tq, lam_init, heads):
    head_cols = [slice(hh * HEAD_W, (hh + 1) * HEAD_W) for hh in range(heads)]
    row = lax.broadcasted_iota(I32, (tq, LANES), 0)
    lane = lax.broadcasted_iota(I32, (tq, LANES), 1)
    pos_feat = jnp.where(lane < 2 * N_SPLIT, jnp.where(lane % 2 == 0, row & ~(POS_LOW - 1), row & (POS_LOW - 1)), 0)
    pos_feat = pos_feat.astype(F32).astype(BF16)
    qlane = lax.broadcasted_iota(I32, (2 * tq, LANES), 1)
    slopes2, qqs = [], []
    for hh in range(heads):
        s2 = slopes_ref[hg * heads + hh] * LOG2E
        slopes2.append(s2)
        rest = jnp.full((2 * tq, LANES), s2, F32)
        feat = jnp.zeros((2 * tq, LANES), F32)
        for part in range(N_SPLIT):
            piece = rest.astype(BF16).astype(F32)
            rest = rest - piece
            feat = jnp.where(qlane // 2 == part, piece, feat)
        qq = _split_halves((q_ref[:, head_cols[hh]] * LOG2E).astype(BF16))
        qqs.append(jnp.concatenate([qq, feat.astype(BF16)], axis=1))
    ones = jnp.ones((BF16_ROWS, tq), BF16)

    def head_step(hh, j, carry, diag):
        m, acc = carry
        start = pl.multiple_of(j * tq, tq)
        keys = jnp.concatenate([k_ref[pl.ds(start, tq), head_cols[hh]], pos_feat], axis=1)
        st = _dot_nt(keys, qqs[hh])
        if diag:
            kr = lax.broadcasted_iota(I32, st.shape, 0)
            qc = lax.broadcasted_iota(I32, st.shape, 1)
            qc = jnp.where(qc >= tq, qc - tq, qc)
            st = jnp.where(qc >= kr, st, NEG_INF)
        offset = slopes2[hh] * ((j - i) * tq).astype(F32)
        m_new = jnp.maximum(m, jnp.max(st, axis=0, keepdims=True) + offset)
        alpha = jnp.exp2(m - m_new)
        p = jnp.exp2(st - (m_new - offset)).astype(BF16)
        values = jnp.concatenate([vt_ref[j, head_cols[hh], :], ones], axis=0)
        acc = alpha * acc + _dot(values, p)
        return m_new, acc

    def step(j, carries, diag):
        return tuple(head_step(hh, j, carries[hh], diag) for hh in range(heads))

    init = (jnp.full((1, 2 * tq), NEG_INF, F32), jnp.zeros((HEAD_W + BF16_ROWS, 2 * tq), F32))
    carries = lax.fori_loop(0, i, lambda j, cr: step(j, cr, False), (init,) * heads)
    carries = step(i, carries, True)
    for hh, (_, acc) in enumerate(carries):
        o = acc[:HEAD_W] / acc[HEAD_W:HEAD_W + 1]
        o = (o[:, :tq] - lam * o[:, tq:]).T
        o_ref[:, head_cols[hh]] = _subln(o, g_ref[...], lam_init).astype(o_ref.dtype)


def _attention_kernel(pt_ref, slopes_ref, qs_ref, kn_ref, vn_ref, kc_ref, vc_ref, qp_ref, kp_ref, vt_ref,
                      lq1_ref, lk1_ref, lq2_ref, lk2_ref, g_ref, os_ref, op_ref, kbuf, vbuf, sem,
                      *, n_seqs, seqs_per_step, n_pages, page_rows, n_heads, n_new, lam_init, tq, heads, head_groups,
                      q_tiles):
    step = pl.program_id(0)
    lam = _lambda_full(lq1_ref[...], lk1_ref[...], lq2_ref[...], lk2_ref[...], lam_init)

    def page_copies(seq, sl, pg):
        p = pt_ref[seq * n_pages + pg]
        dst = pl.ds(pg * page_rows, page_rows)
        return (pltpu.make_async_copy(kc_ref.at[p], kbuf.at[sl, dst, :], sem.at[sl, 0]),
                pltpu.make_async_copy(vc_ref.at[p], vbuf.at[sl, dst, :], sem.at[sl, 1]))

    def start_seq(seq, sl):
        for pg in range(n_pages):
            for cp in page_copies(seq, sl, pg):
                cp.start()

    @pl.when(step == 0)
    def _():
        start_seq(0, 0)

    for u in range(seqs_per_step):
        seq = step * seqs_per_step + u
        slot = seq % 2

        @pl.when(seq + 1 < n_seqs)
        def _():
            start_seq(seq + 1, 1 - slot)

        @pl.when(seq < n_seqs)
        def _():
            for pg in range(n_pages):
                for cp in page_copies(seq, slot, pg):
                    cp.wait()
            _sample_seq(slot, lam, slopes_ref, qs_ref.at[u], kn_ref.at[u], vn_ref.at[u], kbuf, vbuf, g_ref,
                        os_ref.at[u], past=n_pages * page_rows // n_heads, n_heads=n_heads, n_new=n_new,
                        lam_init=lam_init)

    _prompt_unit((step // q_tiles) % head_groups, step % q_tiles, lam, slopes_ref, qp_ref, kp_ref, vt_ref, g_ref, op_ref,
                 tq=tq, lam_init=lam_init, heads=heads)


def _sample_seq(slot, lam, slopes_ref, q_ref, kn_ref, vn_ref, kbuf, vbuf, g_ref, o_ref, *, past, n_heads, n_new,
                lam_init):
    grp = 2 * SUBLANES
    rows = SAMPLE_HEADS * grp
    r = lax.broadcasted_iota(I32, (rows, past), 0) % grp
    t_past = jnp.where(r >= n_new, r - n_new, r)
    dist_past = (past + t_past - lax.broadcasted_iota(I32, (rows, past), 1)).astype(F32)
    r1 = lax.broadcasted_iota(I32, (rows, 1), 0)
    t_new = jnp.where(r1 % grp >= n_new, r1 % grp - n_new, r1 % grp)
    pad = jnp.zeros((grp - 2 * n_new, HEAD_W), F32)
    for h0 in range(0, n_heads, SAMPLE_HEADS):
        heads = range(h0, h0 + SAMPLE_HEADS)
        cols = slice(h0 * HEAD_W, (h0 + SAMPLE_HEADS) * HEAD_W)
        slope = jnp.concatenate([jnp.full((grp, 1), slopes_ref[h], F32) for h in heads], axis=0)
        blocks = []
        for i, h in enumerate(heads):
            qh = jnp.concatenate([_split_halves(q_ref[:, h * HEAD_W:(h + 1) * HEAD_W]), pad], axis=0)
            zero = jnp.zeros_like(qh)
            blocks.append(jnp.concatenate([qh if j == i else zero for j in range(SAMPLE_HEADS)], axis=1))
        qq = jnp.concatenate(blocks, axis=0)
        kh = jnp.concatenate([kbuf[slot, pl.ds(h, past, stride=n_heads), :] for h in heads], axis=1).astype(BF16)
        vh = jnp.concatenate([vbuf[slot, pl.ds(h, past, stride=n_heads), :] for h in heads], axis=1).astype(BF16)
        s = _dot_nt(qq.astype(BF16), kh) - slope * dist_past
        kn = kn_ref[:, cols]
        vn = vn_ref[:, cols]
        s_new = []
        for c in range(n_new):
            sc = jnp.sum(qq * kn[c:c + 1, :], axis=-1, keepdims=True) - slope * (t_new - c).astype(F32)
            s_new.append(jnp.where(t_new >= c, sc, NEG_INF))
        m = jnp.max(s, axis=-1, keepdims=True)
        for sc in s_new:
            m = jnp.maximum(m, sc)
        p = jnp.exp(s - m)
        l = jnp.sum(p, axis=-1, keepdims=True)
        acc = _dot(p.astype(BF16), vh)
        for c, sc in enumerate(s_new):
            pc = jnp.exp(sc - m)
            l = l + pc
            acc = acc + pc * vn[c:c + 1, :]
        o = acc / l
        for i, h in enumerate(heads):
            oh = o[i * grp:(i + 1) * grp, i * HEAD_W:(i + 1) * HEAD_W]
            oh = oh[0:n_new] - lam * oh[n_new:2 * n_new]
            o_ref[:, h * HEAD_W:(h + 1) * HEAD_W] = _subln(oh, g_ref[...], lam_init)


def _attention(sample, caches, page_table, prompt, slopes, lams, subln_g, batch, seq, n_heads, lam_init, tq):
    q3, kn3, vn3 = sample
    kc, vc = caches
    qp, kb, vt = prompt
    bs, n_new, qk_w = q3.shape
    n_pages = page_table.shape[1]
    page_rows = kc.shape[1]
    heads = math.gcd(n_heads, ATTN_HEADS)
    width = heads * HEAD_W
    head_groups, q_tiles = n_heads // heads, seq // tq
    steps = batch * head_groups * q_tiles
    seqs_per_step = -(-bs // steps)
    assert bs % seqs_per_step == 0
    seq_blocks = bs // seqs_per_step
    qp3, kb3 = (a.reshape(batch, seq, n_heads * HEAD_W) for a in (qp, kb))
    unit = lambda s: (s // (head_groups * q_tiles), (s // q_tiles) % head_groups, s % q_tiles)
    small = lambda a: pl.BlockSpec(a.shape, lambda s, *_: (0,) * a.ndim)
    tok = pl.BlockSpec((seqs_per_step, n_new, qk_w), lambda s, *_: (jnp.minimum(s, seq_blocks - 1), 0, 0))
    grid_spec = pltpu.PrefetchScalarGridSpec(
        num_scalar_prefetch=2,
        grid=(steps,),
        in_specs=[tok, tok, tok, pl.BlockSpec(memory_space=pl.ANY), pl.BlockSpec(memory_space=pl.ANY),
                  pl.BlockSpec((None, tq, width), lambda s, *_: (unit(s)[0], unit(s)[2], unit(s)[1])),
                  pl.BlockSpec((None, seq, width), lambda s, *_: (unit(s)[0], 0, unit(s)[1])),
                  pl.BlockSpec((q_tiles, width, tq), lambda s, *_: (unit(s)[0], unit(s)[1], 0)),
                  *[small(a) for a in lams], small(subln_g)],
        out_specs=[tok, pl.BlockSpec((None, tq, width), lambda s, *_: (unit(s)[0], unit(s)[2], unit(s)[1]))],
        scratch_shapes=[pltpu.VMEM((2, n_pages * page_rows, LANES), F32),
                        pltpu.VMEM((2, n_pages * page_rows, LANES), F32),
                        pltpu.SemaphoreType.DMA((2, 2))],
    )
    o_s, o_p = pl.pallas_call(
        functools.partial(_attention_kernel, n_seqs=bs, seqs_per_step=seqs_per_step, n_pages=n_pages,
                          page_rows=page_rows, n_heads=n_heads, n_new=n_new, lam_init=lam_init, tq=tq, heads=heads,
                          head_groups=head_groups, q_tiles=q_tiles),
        grid_spec=grid_spec,
        out_shape=[jax.ShapeDtypeStruct((bs, n_new, qk_w), F32),
                   jax.ShapeDtypeStruct((batch, seq, n_heads * HEAD_W), BF16)],
        compiler_params=_cparams("arbitrary"),
        name="attention",
    )(page_table.reshape(-1), slopes, q3, kn3, vn3, kc, vc, qp3, kb3, vt, *lams, subln_g)
    return o_s, o_p.reshape(batch * seq, n_heads * HEAD_W)


def _outproj_kernel(pool_p_ref, o_p_ref, x_p_ref, pool_s_ref, o_s_ref, x_s_ref, wo_ref, g2_ref, rwt_ref, rb_ref,
                    earlier_ref, h_ref, hn_ref, code_ref, gate_ref, cnt_out_ref, run_ref, *, tm, n_experts,
                    prompt_tiles):
    i = pl.program_id(0)

    @pl.when(i == 0)
    def _():
        run_ref[...] = jnp.zeros_like(run_ref)

    tile = functools.partial(_outproj_tile, wo_ref=wo_ref, g2_ref=g2_ref, rwt_ref=rwt_ref, rb_ref=rb_ref,
                             earlier_ref=earlier_ref, h_ref=h_ref, hn_ref=hn_ref, code_ref=code_ref, gate_ref=gate_ref,
                             cnt_out_ref=cnt_out_ref, run_ref=run_ref, tm=tm, n_experts=n_experts)
    pl.when(i < prompt_tiles)(functools.partial(tile, pool_p_ref, o_p_ref, x_p_ref))
    pl.when(i >= prompt_tiles)(functools.partial(tile, pool_s_ref, o_s_ref, x_s_ref))


def _outproj_tile(pool_ref, o_ref, x_ref, *, wo_ref, g2_ref, rwt_ref, rb_ref, earlier_ref, h_ref, hn_ref, code_ref,
                  gate_ref, cnt_out_ref, run_ref, tm, n_experts):
    pw = pool_ref.shape[1]
    mix = _dot(pool_ref[...].astype(BF16), wo_ref[0:pw, :]) + _dot(o_ref[...].astype(BF16), wo_ref[pw:, :])
    h = x_ref[...] + mix
    h_ref[...] = h
    hn = _rms(h, g2_ref[...])
    for s in range(hn.shape[1] // LANES):
        hn_ref[pl.ds(s, tm, stride=SUBLANES), :] = hn[:, s * LANES:(s + 1) * LANES]
    logits = _dot_nt(rwt_ref[...], hn.astype(BF16)) + rb_ref[...]
    expert = lax.broadcasted_iota(I32, logits.shape, 0)
    vals, idxs = [], []
    for _ in range(TOP_K):
        m = jnp.max(logits, axis=0, keepdims=True)
        idx = jnp.min(jnp.where(logits == m, expert, n_experts), axis=0, keepdims=True)
        vals.append(m)
        idxs.append(idx)
        logits = jnp.where(expert == idx, -jnp.inf, logits)
    ex = [jnp.exp(v - vals[0]) for v in vals]
    den = ex[0]
    for e in ex[1:]:
        den = den + e
    chosen = jnp.zeros(logits.shape, F32)
    for idx in idxs:
        chosen = chosen + jnp.where(expert == idx, 1.0, 0.0)
    before = _dot(chosen.astype(BF16), earlier_ref[...]) + run_ref[...]
    for k in range(TOP_K):
        gate_ref[k:k + 1, :] = ex[k] / den
        rank = jnp.sum(jnp.where(expert == idxs[k], before, 0.0), axis=0, keepdims=True).astype(I32)
        code_ref[k:k + 1, :] = idxs[k] * (1 << RANK_BITS) + rank
    gate_ref[TOP_K:, :] = jnp.zeros((gate_ref.shape[0] - TOP_K, tm), F32)
    run_ref[...] = run_ref[...] + jnp.sum(chosen, axis=1, keepdims=True)
    cnt_out_ref[...] = run_ref[...]


def _outproj(prompt, sample, w_out_b, g2, rwt_b, rb):
    np_, d = prompt[2].shape
    ns_ = sample[2].shape[0]
    n_total = np_ + ns_
    n_experts = rwt_b.shape[0]
    tm = min(ROW_TILE, math.gcd(np_, ns_))
    prompt_tiles = np_ // tm
    assert d == SUBLANES * LANES and n_total < (1 << RANK_BITS)
    first = lambda a: pl.BlockSpec((tm, a.shape[1]), lambda i: (jnp.minimum(i, prompt_tiles - 1), 0))
    second = lambda a: pl.BlockSpec((tm, a.shape[1]), lambda i: (jnp.maximum(i - prompt_tiles, 0), 0))
    row = lambda w: pl.BlockSpec((tm, w), lambda i: (i, 0))
    col = lambda r: pl.BlockSpec((r, tm), lambda i: (0, i))
    full = lambda a: pl.BlockSpec(a.shape, lambda i: (0,) * a.ndim)
    counts = jax.ShapeDtypeStruct((n_experts, 1), F32)
    earlier = jnp.triu(jnp.ones((tm, tm), BF16), 1)
    return pl.pallas_call(
        functools.partial(_outproj_kernel, tm=tm, n_experts=n_experts, prompt_tiles=prompt_tiles),
        grid=(n_total // tm,),
        in_specs=[*[first(a) for a in prompt], *[second(a) for a in sample],
                  full(w_out_b), full(g2), full(rwt_b), full(rb), full(earlier)],
        out_specs=[row(d), pl.BlockSpec((tm * SUBLANES, LANES), lambda i: (i, 0)),
                   col(TOP_K), col(SUBLANES), pl.BlockSpec(counts.shape, lambda i: (0, 0))],
        out_shape=[jax.ShapeDtypeStruct((n_total, d), F32), jax.ShapeDtypeStruct((n_total * SUBLANES, LANES), F32),
                   jax.ShapeDtypeStruct((TOP_K, n_total), I32), jax.ShapeDtypeStruct((SUBLANES, n_total), F32),
                   counts],
        scratch_shapes=[pltpu.VMEM(counts.shape, F32)],
        compiler_params=_cparams("arbitrary"),
        name="outproj_router",
    )(*prompt, *sample, w_out_b, g2, rwt_b, rb, earlier)


def _sorted_rows_kernel(start_ref, code_ref, rows_ref, *, n_experts):
    code = code_ref[...]
    expert = code >> RANK_BITS
    rows = code & ((1 << RANK_BITS) - 1)
    for e in range(n_experts):
        rows = rows + jnp.where(expert == e, start_ref[e], 0)
    rows_ref[...] = rows


def _sorted_rows(starts, code):
    return pl.pallas_call(
        functools.partial(_sorted_rows_kernel, n_experts=starts.shape[0]),
        grid=(1,),
        in_specs=[pl.BlockSpec(memory_space=pltpu.SMEM), pl.BlockSpec(code.shape, lambda i: (0, 0))],
        out_specs=pl.BlockSpec(code.shape, lambda i: (0, 0)),
        out_shape=jax.ShapeDtypeStruct(code.shape, I32),
        compiler_params=_cparams("arbitrary"),
        name="sorted_rows",
    )(starts, code)


MXU_DIM = 256


def _split_w1_tile(w_ref, g_ref, l_ref):
    r = lax.broadcasted_iota(I32, (MXU_DIM, MXU_DIM), 0)
    c = lax.broadcasted_iota(I32, (MXU_DIM, MXU_DIM), 1)
    half = MXU_DIM // 2
    source = jnp.where(c < half, 2 * c, 2 * (c - half) + 1)
    perm = jnp.where(r == source, 1.0, 0.0).astype(BF16)
    for blk in range(w_ref.shape[1] // MXU_DIM):
        res = _dot(w_ref[:, blk * MXU_DIM:(blk + 1) * MXU_DIM].astype(BF16), perm)
        g_ref[:, blk * half:(blk + 1) * half] = res[:, :half].astype(BF16)
        l_ref[:, blk * half:(blk + 1) * half] = res[:, half:].astype(BF16)


def _dispatch_kernel(row_ref, zlo_ref, zhi_ref, nu_ref, hn_ref, w1_ref, xs_ref, w1g_ref, w1l_ref, zero_ref, sem,
                     *, tm, n_total, rows, n_experts, n_blocks):
    i = pl.program_id(0)
    tile = tm * SUBLANES
    block = rows * SUBLANES

    def pad_pieces(e):
        lo, n = zlo_ref[e], zhi_ref[e] - zlo_ref[e]
        for bit in range(rows.bit_length() - 1):
            size = 1 << bit
            below = n & (size - 1)
            start = pl.multiple_of((lo + below) * SUBLANES, SUBLANES)
            cp = pltpu.make_async_copy(zero_ref.at[pl.ds(0, size * SUBLANES), :],
                                       xs_ref.at[pl.ds(start, size * SUBLANES), :], sem.at[1])
            yield (n & size) != 0, cp

    def tail_copy(blk):
        start = pl.multiple_of(blk * block, block)
        return pltpu.make_async_copy(zero_ref, xs_ref.at[pl.ds(start, block), :], sem.at[1])

    def for_zero_copies(fn):
        def per_expert(e, carry):
            for used, cp in pad_pieces(e):
                pl.when(used)(functools.partial(fn, cp))
            return carry
        lax.fori_loop(0, n_experts, per_expert, 0)

        def per_block(blk, carry):
            fn(tail_copy(blk))
            return carry
        lax.fori_loop(nu_ref[0], n_blocks, per_block, 0)

    @pl.when(i == 0)
    def _():
        zero_ref[...] = jnp.zeros_like(zero_ref)
        for_zero_copies(lambda cp: cp.start())

    def body(t, carry):
        src = pl.multiple_of(t * SUBLANES, SUBLANES)
        for k in range(TOP_K):
            dst = pl.multiple_of(row_ref[k * n_total + i * tm + t] * SUBLANES, SUBLANES)
            pltpu.make_async_copy(hn_ref.at[pl.ds(src, SUBLANES), :], xs_ref.at[pl.ds(dst, SUBLANES), :],
                                  sem.at[0]).start()
        return carry
    lax.fori_loop(0, tm, body, 0)

    _split_w1_tile(w1_ref, w1g_ref, w1l_ref)

    for _ in range(TOP_K):
        pltpu.make_async_copy(hn_ref, xs_ref.at[pl.ds(0, tile), :], sem.at[0]).wait()

    @pl.when(i == pl.num_programs(0) - 1)
    def _():
        for_zero_copies(lambda cp: cp.wait())


def _dispatch(sorted_rows, zlo, zhi, n_used, hn, w1, n_blocks):
    n_total = hn.shape[0] // SUBLANES
    n_exp, d, ff2 = w1.shape
    tr = next(t for t in (ROW_TILE, 2 * ROW_TILE, d) if d % t == 0 and n_total % (n_exp * (d // t)) == 0)
    row_tiles = d // tr
    steps = n_exp * row_tiles
    tm = n_total // steps
    rows = EXPERT_ROWS
    assert rows & (rows - 1) == 0
    half = pl.BlockSpec((None, tr, ff2 // 2), lambda i, *_: (i // row_tiles, i % row_tiles, 0))
    grid_spec = pltpu.PrefetchScalarGridSpec(
        num_scalar_prefetch=4,
        grid=(steps,),
        in_specs=[pl.BlockSpec((tm * SUBLANES, LANES), lambda i, *_: (i, 0)),
                  pl.BlockSpec((None, tr, ff2), lambda i, *_: (i // row_tiles, i % row_tiles, 0))],
        out_specs=[pl.BlockSpec(memory_space=pl.ANY), half, half],
        scratch_shapes=[pltpu.VMEM((rows * SUBLANES, LANES), F32), pltpu.SemaphoreType.DMA((2,))],
    )
    return pl.pallas_call(
        functools.partial(_dispatch_kernel, tm=tm, n_total=n_total, rows=rows, n_experts=n_exp, n_blocks=n_blocks),
        grid_spec=grid_spec,
        out_shape=[jax.ShapeDtypeStruct((n_blocks * rows * SUBLANES, LANES), F32),
                   jax.ShapeDtypeStruct((n_exp, d, ff2 // 2), BF16), jax.ShapeDtypeStruct((n_exp, d, ff2 // 2), BF16)],
        compiler_params=_cparams("arbitrary"),
        name="dispatch_split_w1",
    )(sorted_rows, zlo, zhi, n_used, hn, w1)


def _expert_kernel(be_ref, nu_ref, xs_ref, w1g_ref, w1l_ref, b1g_ref, b1l_ref, w2_ref, b2_ref, y_ref, *, rows):
    j = pl.program_id(0)
    n_used = nu_ref[0]

    @pl.when(j < n_used)
    def _():
        x = jnp.concatenate([xs_ref[pl.ds(s, rows, stride=SUBLANES), :] for s in range(SUBLANES)],
                            axis=-1).astype(BF16)
        glu = jnp.minimum(_dot(x, w1g_ref[...]) + b1g_ref[...], SWIGLU_LIMIT)
        lin = jnp.clip(_dot(x, w1l_ref[...]) + b1l_ref[...], -SWIGLU_LIMIT, SWIGLU_LIMIT)
        act = glu * jax.nn.sigmoid(SWIGLU_ALPHA * glu) * (lin + 1.0)
        y = _dot(act.astype(BF16), w2_ref[...].astype(BF16)) + b2_ref[...]
        for s in range(SUBLANES):
            y_ref[pl.ds(s, rows, stride=SUBLANES), :] = y[:, s * LANES:(s + 1) * LANES]

    @pl.when(j >= n_used)
    def _():
        y_ref[...] = jnp.zeros_like(y_ref)


def _experts(block_e, n_used, xs, w1g, w1l, b1g, b1l, w2, b2, n_blocks):
    rows = EXPERT_ROWS
    tile = rows * SUBLANES
    wspec = lambda a: pl.BlockSpec((None,) + a.shape[1:], lambda j, be, nu: (be[j], 0, 0))
    grid_spec = pltpu.PrefetchScalarGridSpec(
        num_scalar_prefetch=2,
        grid=(n_blocks,),
        in_specs=[pl.BlockSpec((tile, LANES), lambda j, be, nu: (jnp.minimum(j, nu[0] - 1), 0)),
                  wspec(w1g), wspec(w1l), wspec(b1g), wspec(b1l), wspec(w2), wspec(b2)],
        out_specs=pl.BlockSpec((tile, LANES), lambda j, be, nu: (j, 0)),
    )
    return pl.pallas_call(
        functools.partial(_expert_kernel, rows=rows),
        grid_spec=grid_spec,
        out_shape=jax.ShapeDtypeStruct((n_blocks * tile, LANES), F32),
        compiler_params=_cparams("arbitrary"),
        name="experts",
    )(block_e, n_used, xs, w1g, w1l, b1g, b1l, w2, b2)


def _combine_kernel(row_ref, yb_ref, gate_ref, h_ref, g_ref, yp_ref, ys_ref, cbuf, sem, *, tm, n_total,
                    prompt_tiles):
    i = pl.program_id(0)
    n = pl.num_programs(0)
    slot = i % 2
    tile = tm * SUBLANES

    def gather(blk, sl):
        def body(t, carry):
            for k in range(TOP_K):
                src = pl.multiple_of(row_ref[k * n_total + blk * tm + t] * SUBLANES, SUBLANES)
                dst = pl.multiple_of((k * tm + t) * SUBLANES, SUBLANES)
                pltpu.make_async_copy(yb_ref.at[pl.ds(src, SUBLANES), :], cbuf.at[sl, pl.ds(dst, SUBLANES), :],
                                      sem.at[sl]).start()
            return carry
        lax.fori_loop(0, tm, body, 0)

    @pl.when(i == 0)
    def _():
        gather(0, 0)

    @pl.when(i + 1 < n)
    def _():
        gather(i + 1, 1 - slot)

    pltpu.make_async_copy(yb_ref.at[pl.ds(0, TOP_K * tile), :], cbuf.at[slot], sem.at[slot]).wait()
    gates = gate_ref[...].T
    slabs = []
    for s in range(SUBLANES):
        acc = None
        for k in range(TOP_K):
            part = gates[:, k:k + 1] * cbuf[slot, pl.ds(k * tile + s, tm, stride=SUBLANES), :]
            acc = part if acc is None else acc + part
        slabs.append(acc)
    hf = h_ref[...] + jnp.concatenate(slabs, axis=-1)
    y = _rms(hf, g_ref[...])

    @pl.when(i < prompt_tiles)
    def _():
        yp_ref[...] = y

    @pl.when(i >= prompt_tiles)
    def _():
        ys_ref[...] = y


def _combine(rows, yb, gates, h, gf, np_):
    n, d = h.shape
    ns_ = n - np_
    tm = min(COMBINE_TILE, math.gcd(np_, ns_))
    prompt_tiles = np_ // tm
    row = lambda w: pl.BlockSpec((tm, w), lambda i, *_: (i, 0))
    grid_spec = pltpu.PrefetchScalarGridSpec(
        num_scalar_prefetch=1,
        grid=(n // tm,),
        in_specs=[pl.BlockSpec(memory_space=pl.ANY), pl.BlockSpec((gates.shape[0], tm), lambda i, *_: (0, i)), row(d),
                  pl.BlockSpec(gf.shape, lambda i, *_: (0, 0))],
        out_specs=[pl.BlockSpec((tm, d), lambda i, *_: (jnp.minimum(i, prompt_tiles - 1), 0)),
                   pl.BlockSpec((tm, d), lambda i, *_: (jnp.maximum(i - prompt_tiles, 0), 0))],
        scratch_shapes=[pltpu.VMEM((2, TOP_K * tm * SUBLANES, LANES), F32), pltpu.SemaphoreType.DMA((2,))],
    )
    return pl.pallas_call(
        functools.partial(_combine_kernel, tm=tm, n_total=n, prompt_tiles=prompt_tiles),
        grid_spec=grid_spec,
        out_shape=[jax.ShapeDtypeStruct((np_, d), F32), jax.ShapeDtypeStruct((ns_, d), F32)],
        compiler_params=_cparams("arbitrary"),
        name="combine_norm",
    )(rows, yb, gates, h, gf)


def kernel(x_prompt, x_sample, state_pool, cache_k, cache_v, page_table, norm1_g, w_in, pool_w, pool_scale,
           lambda_q1, lambda_k1, lambda_q2, lambda_k2, subln_g, w_out, norm2_g, router_w, router_b,
           w1, b1, w2, b2, normf_g):
    batch, seq, d = x_prompt.shape
    bs, n_new, _ = x_sample.shape
    depth = w_in.shape[0]
    assert depth == 1, "single-layer trunk"
    n_heads = cache_k.shape[3]
    page_size = cache_k.shape[2]
    pw = pool_w.shape[1] * pool_w.shape[2]
    qk_w = n_heads * HEAD_W
    n_experts = router_w.shape[2]
    np_, ns_ = batch * seq, bs * n_new
    n_total = np_ + ns_
    lam_init = 0.8 - 0.6 * math.exp(-0.3 * 0)

    g1 = norm1_g[0][None]
    w_in_b = w_in[0].astype(BF16)
    wvt_b = w_in_b[:, pw + 2 * qk_w:].T
    pool_w_b = pool_w[0].astype(BF16)
    ps = pool_scale[0][None]
    lams = [a[0][None] for a in (lambda_q1, lambda_k1, lambda_q2, lambda_k2)]
    sg = subln_g[0][None]
    slopes = jnp.exp2(-8.0 * (jnp.arange(n_heads, dtype=F32) + 1.0) / n_heads)
    w_out_b = w_out[0].astype(BF16)
    g2 = norm2_g[0][None]
    rwt_b = router_w[0].T.astype(BF16)
    rb = router_b[0][:, None]
    b1g = b1[0][:, None, 0::2]
    b1l = b1[0][:, None, 1::2]
    w2b = w2.reshape(w2.shape[1:])
    b2e = b2[0][:, None, :]
    gf = normf_g[None]

    xp = x_prompt.reshape(np_, d)
    tq = min(ATTN_TILE, seq)
    u_p, q_p, k_p, v_p, kb_p, vt_p = _inproj(xp, g1, w_in_b, wvt_b, pw, qk_w, key_tile=tq)
    pool_p = _pool_prompt(u_p, pool_w_b, ps, batch, seq)

    xs = x_sample.reshape(ns_, d)
    u_s, q_s, k_s, v_s = _inproj(xs, g1, w_in_b, wvt_b, pw, qk_w)
    pool_s, new_pool_s = _pool_sample(state_pool[0], u_s.reshape(bs, n_new, pw), pool_w_b, ps)
    kc = cache_k.reshape(cache_k.shape[1], page_size * n_heads, HEAD_W)
    vc = cache_v.reshape(cache_v.shape[1], page_size * n_heads, HEAD_W)
    o_s, o_p = _attention((q_s.reshape(bs, n_new, qk_w), k_s.reshape(bs, n_new, qk_w), v_s.reshape(bs, n_new, qk_w)),
                          (kc, vc), page_table, (q_p, kb_p, vt_p), slopes, lams, sg, batch, seq, n_heads, lam_init, tq)

    h, hn, code, gates, cnt = _outproj((pool_p, o_p, xp), (pool_s.reshape(ns_, pw), o_s.reshape(ns_, qk_w), xs),
                                       w_out_b, g2, rwt_b, rb)

    counts = cnt[:, 0].astype(I32)
    n_blocks = -(-n_total * TOP_K // EXPERT_ROWS) + n_experts
    padded = (counts + EXPERT_ROWS - 1) // EXPERT_ROWS * EXPERT_ROWS
    pad_ends = jnp.cumsum(padded)
    pad_starts = pad_ends - padded
    n_used = (pad_ends[-1:] // EXPERT_ROWS).astype(I32)
    block_start = jnp.arange(n_blocks, dtype=I32) * EXPERT_ROWS
    block_e = jnp.minimum(jnp.sum((pad_ends[None, :] <= block_start[:, None]).astype(I32), axis=1), n_experts - 1)

    sorted_rows = _sorted_rows(pad_starts, code).reshape(-1)
    xs_rows, w1g, w1l = _dispatch(sorted_rows, pad_starts + counts, pad_ends, n_used, hn, w1.reshape(w1.shape[1:]),
                                  n_blocks)
    yb = _experts(block_e, n_used, xs_rows, w1g, w1l, b1g, b1l, w2b, b2e, n_blocks)
    y_p, y_s = _combine(sorted_rows, yb, gates, h, gf, np_)

    return (y_p.reshape(batch, seq, d), y_s.reshape(bs, n_new, d),
            u_p.reshape(batch, seq, pw)[None, :, seq - state_pool.shape[2]:],
            new_pool_s[None],
            k_p.reshape(1, batch, seq, n_heads, HEAD_W), v_p.reshape(1, batch, seq, n_heads, HEAD_W),
            k_s.reshape(1, bs, n_new, n_heads, HEAD_W), v_s.reshape(1, bs, n_new, n_heads, HEAD_W))
```

```python
import functools
import math

import jax
import jax.numpy as jnp
from jax import lax
from jax.experimental import pallas as pl
from jax.experimental.pallas import tpu as pltpu

F32 = jnp.float32
BF16 = jnp.bfloat16
I32 = jnp.int32

RMS_EPS = 1e-6
NEG_INF = -1e30
HEAD_DIM = 64
HEAD_W = 2 * HEAD_DIM
POOL_WINDOWS = (2, 4, 8, 16)
POOL_HALO = 16
TOP_K = 4
RANK_BITS = 20
SWIGLU_LIMIT = 7.0
SWIGLU_ALPHA = 1.702
LANES = 128
SUBLANES = 8
ROW_TILE = 512
ATTN_TILE = 512
SAMPLE_HEADS = 2
ATTN_HEADS = 4
LOG2E = 1.4426950408889634
N_SPLIT = 3
POS_LOW = 16
BF16_ROWS = 16
EXPERT_ROWS = 512
COMBINE_TILE = 512
VMEM_LIMIT = 48 * 1024 * 1024


def _cparams(*sem):
    return pltpu.CompilerParams(dimension_semantics=sem, vmem_limit_bytes=VMEM_LIMIT)


def _dot(a, b):
    return jnp.dot(a, b, preferred_element_type=F32)


def _dot_nt(a, b):
    return lax.dot_general(a, b, (((1,), (1,)), ((), ())), preferred_element_type=F32)


def _rms(x, g):
    return x * lax.rsqrt(jnp.mean(x * x, axis=-1, keepdims=True) + RMS_EPS) * g


def _inproj_sample_kernel(x_ref, g_ref, w_ref, wvt_ref, u_ref, q_ref, k_ref, v_ref, *, pool_w, qk_w):
    del wvt_ref
    u = _inproj_tile(x_ref, g_ref, w_ref, q_ref, k_ref, v_ref, pool_w=pool_w, qk_w=qk_w)[0]
    u_ref[...] = u


def _inproj_prompt_kernel(x_ref, g_ref, w_ref, wvt_ref, pw_ref, ps_ref, pool_ref, tail_ref, q_ref, k_ref, v_ref, kb_ref,
                          vt_ref, carry_ref, *, pool_w, qk_w, key_tile, seq_tiles):
    u, k, xb = _inproj_tile(x_ref, g_ref, w_ref, q_ref, k_ref, v_ref, pool_w=pool_w, qk_w=qk_w)
    _pool_tile(u, pl.program_id(0) % seq_tiles, pw_ref, ps_ref, pool_ref, carry_ref)
    tail_ref[...] = u[u.shape[0] - POOL_HALO:, :]
    kb_ref[...] = k.astype(BF16)
    vt = _dot_nt(wvt_ref[...], xb)
    for c in range(vt_ref.shape[0]):
        vt_ref[c] = vt[:, c * key_tile:(c + 1) * key_tile].astype(BF16)


def _inproj_tile(x_ref, g_ref, w_ref, q_ref, k_ref, v_ref, *, pool_w, qk_w):
    xb = _rms(x_ref[...], g_ref[...]).astype(BF16)
    c0, c1, c2 = pool_w, pool_w + qk_w, pool_w + 2 * qk_w
    u = _dot(xb, w_ref[:, 0:c0])
    q_ref[...] = _dot(xb, w_ref[:, c0:c1]) * (HEAD_DIM ** -0.5)
    k = _dot(xb, w_ref[:, c1:c2])
    v = _dot(xb, w_ref[:, c2:])
    n_heads = qk_w // HEAD_W
    tm = k.shape[0]
    for h in range(n_heads):
        k_ref[pl.ds(h, tm, stride=n_heads), :] = k[:, h * HEAD_W:(h + 1) * HEAD_W]
        v_ref[pl.ds(h, tm, stride=n_heads), :] = v[:, h * HEAD_W:(h + 1) * HEAD_W]
    return u, k, xb


def _inproj(x, g1, w_in_b, wvt_b, pool_w, qk_w, prompt=None):
    n, d = x.shape
    val_w = wvt_b.shape[0]
    tm = min(ROW_TILE, n)
    row = lambda w: pl.BlockSpec((tm, w), lambda i: (i, 0))
    full = lambda a: pl.BlockSpec(a.shape, lambda i: (0,) * a.ndim)
    n_heads = qk_w // HEAD_W
    assert val_w == qk_w
    heads = pl.BlockSpec((tm * n_heads, HEAD_W), lambda i: (i, 0))
    qkv_specs = [row(qk_w), heads, heads]
    qkv_shape = [jax.ShapeDtypeStruct((n, qk_w), F32), jax.ShapeDtypeStruct((n * n_heads, HEAD_W), F32),
                 jax.ShapeDtypeStruct((n * n_heads, HEAD_W), F32)]
    if prompt is None:
        return pl.pallas_call(
            functools.partial(_inproj_sample_kernel, pool_w=pool_w, qk_w=qk_w),
            grid=(n // tm,),
            in_specs=[row(d), full(g1), full(w_in_b), full(wvt_b)],
            out_specs=[row(pool_w)] + qkv_specs,
            out_shape=[jax.ShapeDtypeStruct((n, pool_w), F32)] + qkv_shape,
            compiler_params=_cparams("parallel"),
            name="inproj_sample",
        )(x, g1, w_in_b, wvt_b)
    pool_w_b, pool_scale, seq, key_tile = prompt
    assert tm % key_tile == 0 and seq % tm == 0 and tm >= POOL_HALO
    seq_tiles = seq // tm
    return pl.pallas_call(
        functools.partial(_inproj_prompt_kernel, pool_w=pool_w, qk_w=qk_w, key_tile=key_tile, seq_tiles=seq_tiles),
        grid=(n // tm,),
        in_specs=[row(d), full(g1), full(w_in_b), full(wvt_b), full(pool_w_b), full(pool_scale)],
        out_specs=[row(pool_w), pl.BlockSpec((None, POOL_HALO, pool_w), lambda i: (i // seq_tiles, 0, 0))] + qkv_specs
        + [row(qk_w), pl.BlockSpec((tm // key_tile, val_w, key_tile), lambda i: (i, 0, 0))],
        out_shape=[jax.ShapeDtypeStruct((n, pool_w), BF16), jax.ShapeDtypeStruct((n // seq, POOL_HALO, pool_w), F32)]
        + qkv_shape + [jax.ShapeDtypeStruct((n, qk_w), BF16),
                       jax.ShapeDtypeStruct((n // key_tile, val_w, key_tile), BF16)],
        scratch_shapes=[pltpu.VMEM((POOL_HALO, pool_w), F32)],
        compiler_params=_cparams("arbitrary"),
        name="inproj_pool_prompt",
    )(x, g1, w_in_b, wvt_b, pool_w_b, pool_scale)


def _pool_group(s_win, tok, cnt, pw, ps):
    d = s_win / cnt - tok
    return _dot(d.astype(BF16), pw) * ps


def _pool_tile(cur, j, pw_ref, ps_ref, o_ref, carry_ref):
    tt = cur.shape[0]

    @pl.when(j == 0)
    def _():
        carry_ref[...] = jnp.zeros_like(carry_ref)

    ext = jnp.concatenate([carry_ref[...], cur], axis=0)
    carry_ref[...] = cur[tt - POOL_HALO:, :]
    pos = (j * tt + lax.broadcasted_iota(I32, (tt, 1), 0)).astype(F32)
    gw = cur.shape[1] // len(POOL_WINDOWS)
    for g, w in enumerate(POOL_WINDOWS):
        cols = slice(g * gw, (g + 1) * gw)
        e = ext[:, cols]
        s, span = e, 1
        while span < w:
            s = s + pltpu.roll(s, span, axis=0)
            span *= 2
        cnt = jnp.minimum(float(w), pos + 1.0)
        out = _pool_group(s[POOL_HALO:, :], e[POOL_HALO:, :], cnt, pw_ref[g], ps_ref[:, cols])
        o_ref[:, cols] = out.astype(o_ref.dtype)


def _pool_sample_kernel(sp_ref, u_ref, pw_ref, ps_ref, o_ref, np_ref, *, n_new, n_buf):
    rows = [sp_ref[r] for r in range(n_buf)] + [u_ref[r] for r in range(n_new)]
    for r in range(n_buf):
        np_ref[r] = rows[n_new + r]
    gw = rows[0].shape[1] // len(POOL_WINDOWS)
    for t in range(n_new):
        i = n_buf + t
        for g, w in enumerate(POOL_WINDOWS):
            cols = slice(g * gw, (g + 1) * gw)
            s = rows[i][:, cols]
            for back in range(1, w):
                s = s + rows[i - back][:, cols]
            out = _pool_group(s, rows[i][:, cols], float(w), pw_ref[g], ps_ref[:, cols])
            o_ref[t, :, cols] = out


def _pool_sample(state_pool_t, u_t, pool_w_b, pool_scale):
    n_buf, bs, pw = state_pool_t.shape
    n_new = u_t.shape[0]
    full = lambda a: pl.BlockSpec(a.shape, lambda i: (0,) * a.ndim)
    return pl.pallas_call(
        functools.partial(_pool_sample_kernel, n_new=n_new, n_buf=n_buf),
        grid=(1,),
        in_specs=[full(state_pool_t), full(u_t), full(pool_w_b), full(pool_scale)],
        out_specs=[pl.BlockSpec((n_new, bs, pw), lambda i: (0, 0, 0)),
                   pl.BlockSpec((n_buf, bs, pw), lambda i: (0, 0, 0))],
        out_shape=[jax.ShapeDtypeStruct((n_new, bs, pw), F32),
                   jax.ShapeDtypeStruct((n_buf, bs, pw), F32)],
        compiler_params=_cparams("arbitrary"),
        name="pool_sample",
    )(state_pool_t, u_t, pool_w_b, pool_scale)


def _lambda_full(lq1, lk1, lq2, lk2, lam_init):
    e1 = jnp.exp(jnp.sum(lq1 * lk1, axis=-1, keepdims=True))
    e2 = jnp.exp(jnp.sum(lq2 * lk2, axis=-1, keepdims=True))
    return e1 - e2 + lam_init


def _split_halves(q):
    lane = lax.broadcasted_iota(I32, q.shape, 1)
    zero = jnp.zeros_like(q)
    return jnp.concatenate([jnp.where(lane < HEAD_DIM, q, zero), jnp.where(lane >= HEAD_DIM, q, zero)], axis=0)


def _subln(o, g, lam_init):
    return _rms(o, g) * (1.0 - lam_init)


def _attn_prompt_kernel(slopes_ref, q_ref, k_ref, vt_ref, lq1_ref, lk1_ref, lq2_ref, lk2_ref, g_ref, o_ref,
                        *, tq, lam_init, heads):
    hg = pl.program_id(1)
    i = pl.program_id(2)
    lam = _lambda_full(lq1_ref[...], lk1_ref[...], lq2_ref[...], lk2_ref[...], lam_init)
    head_cols = [slice(hh * HEAD_W, (hh + 1) * HEAD_W) for hh in range(heads)]
    row = lax.broadcasted_iota(I32, (tq, LANES), 0)
    lane = lax.broadcasted_iota(I32, (tq, LANES), 1)
    pos_feat = jnp.where(lane < 2 * N_SPLIT, jnp.where(lane % 2 == 0, row & ~(POS_LOW - 1), row & (POS_LOW - 1)), 0)
    pos_feat = pos_feat.astype(F32).astype(BF16)
    qlane = lax.broadcasted_iota(I32, (2 * tq, LANES), 1)
    slopes2, qqs = [], []
    for hh in range(heads):
        s2 = slopes_ref[hg * heads + hh] * LOG2E
        slopes2.append(s2)
        rest = jnp.full((2 * tq, LANES), s2, F32)
        feat = jnp.zeros((2 * tq, LANES), F32)
        for part in range(N_SPLIT):
            piece = rest.astype(BF16).astype(F32)
            rest = rest - piece
            feat = jnp.where(qlane // 2 == part, piece, feat)
        qq = _split_halves((q_ref[:, head_cols[hh]] * LOG2E).astype(BF16))
        qqs.append(jnp.concatenate([qq, feat.astype(BF16)], axis=1))
    ones = jnp.ones((BF16_ROWS, tq), BF16)

    def head_step(hh, j, carry, diag):
        m, acc = carry
        start = pl.multiple_of(j * tq, tq)
        keys = jnp.concatenate([k_ref[pl.ds(start, tq), head_cols[hh]], pos_feat], axis=1)
        st = _dot_nt(keys, qqs[hh])
        if diag:
            kr = lax.broadcasted_iota(I32, st.shape, 0)
            qc = lax.broadcasted_iota(I32, st.shape, 1)
            qc = jnp.where(qc >= tq, qc - tq, qc)
            st = jnp.where(qc >= kr, st, NEG_INF)
        offset = slopes2[hh] * ((j - i) * tq).astype(F32)
        m_new = jnp.maximum(m, jnp.max(st, axis=0, keepdims=True) + offset)
        alpha = jnp.exp2(m - m_new)
        p = jnp.exp2(st - (m_new - offset)).astype(BF16)
        values = jnp.concatenate([vt_ref[j, head_cols[hh], :], ones], axis=0)
        acc = alpha * acc + _dot(values, p)
        return m_new, acc

    def step(j, carries, diag):
        return tuple(head_step(hh, j, carries[hh], diag) for hh in range(heads))

    init = (jnp.full((1, 2 * tq), NEG_INF, F32), jnp.zeros((HEAD_W + BF16_ROWS, 2 * tq), F32))
    carries = lax.fori_loop(0, i, lambda j, cr: step(j, cr, False), (init,) * heads)
    carries = step(i, carries, True)
    for hh, (_, acc) in enumerate(carries):
        o = acc[:HEAD_W] / acc[HEAD_W:HEAD_W + 1]
        o = (o[:, :tq] - lam * o[:, tq:]).T
        o_ref[:, head_cols[hh]] = _subln(o, g_ref[...], lam_init).astype(o_ref.dtype)


def _attn_prompt(q, kb, vt, slopes, lams, subln_g, batch, seq, n_heads, lam_init, tq):
    heads = math.gcd(n_heads, ATTN_HEADS)
    width = heads * HEAD_W
    q3, k3 = (a.reshape(batch, seq, n_heads * HEAD_W) for a in (q, kb))
    small = lambda a: pl.BlockSpec(a.shape, lambda b, h, i, s: (0,) * a.ndim)
    grid_spec = pltpu.PrefetchScalarGridSpec(
        num_scalar_prefetch=1,
        grid=(batch, n_heads // heads, seq // tq),
        in_specs=[pl.BlockSpec((None, tq, width), lambda b, h, i, s: (b, i, h)),
                  pl.BlockSpec((None, seq, width), lambda b, h, i, s: (b, 0, h)),
                  pl.BlockSpec((seq // tq, width, tq), lambda b, h, i, s: (b, h, 0)),
                  *[small(a) for a in lams], small(subln_g)],
        out_specs=pl.BlockSpec((None, tq, width), lambda b, h, i, s: (b, i, h)),
    )
    out = pl.pallas_call(
        functools.partial(_attn_prompt_kernel, tq=tq, lam_init=lam_init, heads=heads),
        grid_spec=grid_spec,
        out_shape=jax.ShapeDtypeStruct((batch, seq, n_heads * HEAD_W), BF16),
        compiler_params=_cparams("parallel", "parallel", "arbitrary"),
        name="attn_prompt",
    )(slopes, q3, k3, vt, *lams, subln_g)
    return out.reshape(batch * seq, n_heads * HEAD_W)


def _attn_sample_kernel(pt_ref, slopes_ref, q_ref, kn_ref, vn_ref, kc_ref, vc_ref,
                        lq1_ref, lk1_ref, lq2_ref, lk2_ref, g_ref, o_ref, kbuf, vbuf, sem,
                        *, n_pages, page_rows, n_heads, n_new, lam_init):
    b = pl.program_id(0)
    nb = pl.num_programs(0)
    slot = b % 2
    past = n_pages * page_rows // n_heads

    def page_copies(seq, sl, pg):
        p = pt_ref[seq * n_pages + pg]
        dst = pl.ds(pg * page_rows, page_rows)
        return (pltpu.make_async_copy(kc_ref.at[p], kbuf.at[sl, dst, :], sem.at[sl, 0]),
                pltpu.make_async_copy(vc_ref.at[p], vbuf.at[sl, dst, :], sem.at[sl, 1]))

    def start_seq(seq, sl):
        for pg in range(n_pages):
            for cp in page_copies(seq, sl, pg):
                cp.start()

    @pl.when(b == 0)
    def _():
        start_seq(0, 0)

    @pl.when(b + 1 < nb)
    def _():
        start_seq(b + 1, 1 - slot)

    for pg in range(n_pages):
        for cp in page_copies(b, slot, pg):
            cp.wait()

    lam = _lambda_full(lq1_ref[...], lk1_ref[...], lq2_ref[...], lk2_ref[...], lam_init)
    grp = 2 * SUBLANES
    rows = SAMPLE_HEADS * grp
    r = lax.broadcasted_iota(I32, (rows, past), 0) % grp
    t_past = jnp.where(r >= n_new, r - n_new, r)
    dist_past = (past + t_past - lax.broadcasted_iota(I32, (rows, past), 1)).astype(F32)
    r1 = lax.broadcasted_iota(I32, (rows, 1), 0)
    t_new = jnp.where(r1 % grp >= n_new, r1 % grp - n_new, r1 % grp)
    pad = jnp.zeros((grp - 2 * n_new, HEAD_W), F32)
    for h0 in range(0, n_heads, SAMPLE_HEADS):
        heads = range(h0, h0 + SAMPLE_HEADS)
        cols = slice(h0 * HEAD_W, (h0 + SAMPLE_HEADS) * HEAD_W)
        slope = jnp.concatenate([jnp.full((grp, 1), slopes_ref[h], F32) for h in heads], axis=0)
        blocks = []
        for i, h in enumerate(heads):
            qh = jnp.concatenate([_split_halves(q_ref[:, h * HEAD_W:(h + 1) * HEAD_W]), pad], axis=0)
            zero = jnp.zeros_like(qh)
            blocks.append(jnp.concatenate([qh if j == i else zero for j in range(SAMPLE_HEADS)], axis=1))
        qq = jnp.concatenate(blocks, axis=0)
        kh = jnp.concatenate([kbuf[slot, pl.ds(h, past, stride=n_heads), :] for h in heads], axis=1).astype(BF16)
        vh = jnp.concatenate([vbuf[slot, pl.ds(h, past, stride=n_heads), :] for h in heads], axis=1).astype(BF16)
        s = _dot_nt(qq.astype(BF16), kh) - slope * dist_past
        kn = kn_ref[:, cols]
        vn = vn_ref[:, cols]
        s_new = []
        for c in range(n_new):
            sc = jnp.sum(qq * kn[c:c + 1, :], axis=-1, keepdims=True) - slope * (t_new - c).astype(F32)
            s_new.append(jnp.where(t_new >= c, sc, NEG_INF))
        m = jnp.max(s, axis=-1, keepdims=True)
        for sc in s_new:
            m = jnp.maximum(m, sc)
        p = jnp.exp(s - m)
        l = jnp.sum(p, axis=-1, keepdims=True)
        acc = _dot(p.astype(BF16), vh)
        for c, sc in enumerate(s_new):
            pc = jnp.exp(sc - m)
            l = l + pc
            acc = acc + pc * vn[c:c + 1, :]
        o = acc / l
        for i, h in enumerate(heads):
            oh = o[i * grp:(i + 1) * grp, i * HEAD_W:(i + 1) * HEAD_W]
            oh = oh[0:n_new] - lam * oh[n_new:2 * n_new]
            o_ref[:, h * HEAD_W:(h + 1) * HEAD_W] = _subln(oh, g_ref[...], lam_init)


def _attn_sample(q3, kn3, vn3, kc, vc, page_table, slopes, lams, subln_g, n_heads, lam_init):
    bs, n_new, qk_w = q3.shape
    n_pages = page_table.shape[1]
    page_rows = kc.shape[1]
    small = lambda a: pl.BlockSpec(a.shape, lambda b, pt, s: (0,) * a.ndim)
    tok = pl.BlockSpec((None, n_new, qk_w), lambda b, pt, s: (b, 0, 0))
    grid_spec = pltpu.PrefetchScalarGridSpec(
        num_scalar_prefetch=2,
        grid=(bs,),
        in_specs=[tok, tok, tok, pl.BlockSpec(memory_space=pl.ANY), pl.BlockSpec(memory_space=pl.ANY),
                  *[small(a) for a in lams], small(subln_g)],
        out_specs=tok,
        scratch_shapes=[pltpu.VMEM((2, n_pages * page_rows, LANES), F32),
                        pltpu.VMEM((2, n_pages * page_rows, LANES), F32),
                        pltpu.SemaphoreType.DMA((2, 2))],
    )
    return pl.pallas_call(
        functools.partial(_attn_sample_kernel, n_pages=n_pages, page_rows=page_rows, n_heads=n_heads,
                          n_new=n_new, lam_init=lam_init),
        grid_spec=grid_spec,
        out_shape=jax.ShapeDtypeStruct((bs, n_new, qk_w), F32),
        compiler_params=_cparams("arbitrary"),
        name="attn_sample",
    )(page_table.reshape(-1), slopes, q3, kn3, vn3, kc, vc, *lams, subln_g)


def _outproj_kernel(pool_p_ref, o_p_ref, x_p_ref, pool_s_ref, o_s_ref, x_s_ref, wo_ref, g2_ref, rwt_ref, rb_ref,
                    earlier_ref, h_ref, hn_ref, code_ref, gate_ref, cnt_out_ref, run_ref, *, tm, n_experts,
                    prompt_tiles):
    i = pl.program_id(0)

    @pl.when(i == 0)
    def _():
        run_ref[...] = jnp.zeros_like(run_ref)

    tile = functools.partial(_outproj_tile, wo_ref=wo_ref, g2_ref=g2_ref, rwt_ref=rwt_ref, rb_ref=rb_ref,
                             earlier_ref=earlier_ref, h_ref=h_ref, hn_ref=hn_ref, code_ref=code_ref, gate_ref=gate_ref,
                             cnt_out_ref=cnt_out_ref, run_ref=run_ref, tm=tm, n_experts=n_experts)
    pl.when(i < prompt_tiles)(functools.partial(tile, pool_p_ref, o_p_ref, x_p_ref))
    pl.when(i >= prompt_tiles)(functools.partial(tile, pool_s_ref, o_s_ref, x_s_ref))


def _outproj_tile(pool_ref, o_ref, x_ref, *, wo_ref, g2_ref, rwt_ref, rb_ref, earlier_ref, h_ref, hn_ref, code_ref,
                  gate_ref, cnt_out_ref, run_ref, tm, n_experts):
    pw = pool_ref.shape[1]
    mix = _dot(pool_ref[...].astype(BF16), wo_ref[0:pw, :]) + _dot(o_ref[...].astype(BF16), wo_ref[pw:, :])
    h = x_ref[...] + mix
    h_ref[...] = h
    hn = _rms(h, g2_ref[...])
    for s in range(hn.shape[1] // LANES):
        hn_ref[pl.ds(s, tm, stride=SUBLANES), :] = hn[:, s * LANES:(s + 1) * LANES]
    logits = _dot_nt(rwt_ref[...], hn.astype(BF16)) + rb_ref[...]
    expert = lax.broadcasted_iota(I32, logits.shape, 0)
    vals, idxs = [], []
    for _ in range(TOP_K):
        m = jnp.max(logits, axis=0, keepdims=True)
        idx = jnp.min(jnp.where(logits == m, expert, n_experts), axis=0, keepdims=True)
        vals.append(m)
        idxs.append(idx)
        logits = jnp.where(expert == idx, -jnp.inf, logits)
    ex = [jnp.exp(v - vals[0]) for v in vals]
    den = ex[0]
    for e in ex[1:]:
        den = den + e
    chosen = jnp.zeros(logits.shape, F32)
    for idx in idxs:
        chosen = chosen + jnp.where(expert == idx, 1.0, 0.0)
    before = _dot(chosen.astype(BF16), earlier_ref[...]) + run_ref[...]
    for k in range(TOP_K):
        gate_ref[k:k + 1, :] = ex[k] / den
        rank = jnp.sum(jnp.where(expert == idxs[k], before, 0.0), axis=0, keepdims=True).astype(I32)
        code_ref[k:k + 1, :] = idxs[k] * (1 << RANK_BITS) + rank
    gate_ref[TOP_K:, :] = jnp.zeros((gate_ref.shape[0] - TOP_K, tm), F32)
    run_ref[...] = run_ref[...] + jnp.sum(chosen, axis=1, keepdims=True)
    cnt_out_ref[...] = run_ref[...]


def _outproj(prompt, sample, w_out_b, g2, rwt_b, rb):
    np_, d = prompt[2].shape
    ns_ = sample[2].shape[0]
    n_total = np_ + ns_
    n_experts = rwt_b.shape[0]
    tm = min(ROW_TILE, math.gcd(np_, ns_))
    prompt_tiles = np_ // tm
    assert d == SUBLANES * LANES and n_total < (1 << RANK_BITS)
    first = lambda a: pl.BlockSpec((tm, a.shape[1]), lambda i: (jnp.minimum(i, prompt_tiles - 1), 0))
    second = lambda a: pl.BlockSpec((tm, a.shape[1]), lambda i: (jnp.maximum(i - prompt_tiles, 0), 0))
    row = lambda w: pl.BlockSpec((tm, w), lambda i: (i, 0))
    col = lambda r: pl.BlockSpec((r, tm), lambda i: (0, i))
    full = lambda a: pl.BlockSpec(a.shape, lambda i: (0,) * a.ndim)
    counts = jax.ShapeDtypeStruct((n_experts, 1), F32)
    earlier = jnp.triu(jnp.ones((tm, tm), BF16), 1)
    return pl.pallas_call(
        functools.partial(_outproj_kernel, tm=tm, n_experts=n_experts, prompt_tiles=prompt_tiles),
        grid=(n_total // tm,),
        in_specs=[*[first(a) for a in prompt], *[second(a) for a in sample],
                  full(w_out_b), full(g2), full(rwt_b), full(rb), full(earlier)],
        out_specs=[row(d), pl.BlockSpec((tm * SUBLANES, LANES), lambda i: (i, 0)),
                   col(TOP_K), col(SUBLANES), pl.BlockSpec(counts.shape, lambda i: (0, 0))],
        out_shape=[jax.ShapeDtypeStruct((n_total, d), F32), jax.ShapeDtypeStruct((n_total * SUBLANES, LANES), F32),
                   jax.ShapeDtypeStruct((TOP_K, n_total), I32), jax.ShapeDtypeStruct((SUBLANES, n_total), F32),
                   counts],
        scratch_shapes=[pltpu.VMEM(counts.shape, F32)],
        compiler_params=_cparams("arbitrary"),
        name="outproj_router",
    )(*prompt, *sample, w_out_b, g2, rwt_b, rb, earlier)


def _sorted_rows_kernel(start_ref, code_ref, rows_ref, *, n_experts):
    code = code_ref[...]
    expert = code >> RANK_BITS
    rows = code & ((1 << RANK_BITS) - 1)
    for e in range(n_experts):
        rows = rows + jnp.where(expert == e, start_ref[e], 0)
    rows_ref[...] = rows


def _sorted_rows(starts, code):
    return pl.pallas_call(
        functools.partial(_sorted_rows_kernel, n_experts=starts.shape[0]),
        grid=(1,),
        in_specs=[pl.BlockSpec(memory_space=pltpu.SMEM), pl.BlockSpec(code.shape, lambda i: (0, 0))],
        out_specs=pl.BlockSpec(code.shape, lambda i: (0, 0)),
        out_shape=jax.ShapeDtypeStruct(code.shape, I32),
        compiler_params=_cparams("arbitrary"),
        name="sorted_rows",
    )(starts, code)


MXU_DIM = 256


def _split_w1_tile(w_ref, g_ref, l_ref):
    r = lax.broadcasted_iota(I32, (MXU_DIM, MXU_DIM), 0)
    c = lax.broadcasted_iota(I32, (MXU_DIM, MXU_DIM), 1)
    half = MXU_DIM // 2
    source = jnp.where(c < half, 2 * c, 2 * (c - half) + 1)
    perm = jnp.where(r == source, 1.0, 0.0).astype(BF16)
    for blk in range(w_ref.shape[1] // MXU_DIM):
        res = _dot(w_ref[:, blk * MXU_DIM:(blk + 1) * MXU_DIM].astype(BF16), perm)
        g_ref[:, blk * half:(blk + 1) * half] = res[:, :half].astype(BF16)
        l_ref[:, blk * half:(blk + 1) * half] = res[:, half:].astype(BF16)


def _dispatch_kernel(row_ref, zlo_ref, zhi_ref, nu_ref, hn_ref, w1_ref, xs_ref, w1g_ref, w1l_ref, zero_ref, sem,
                     *, tm, n_total, rows, n_experts, n_blocks):
    i = pl.program_id(0)
    tile = tm * SUBLANES
    block = rows * SUBLANES

    def pad_pieces(e):
        lo, n = zlo_ref[e], zhi_ref[e] - zlo_ref[e]
        for bit in range(rows.bit_length() - 1):
            size = 1 << bit
            below = n & (size - 1)
            start = pl.multiple_of((lo + below) * SUBLANES, SUBLANES)
            cp = pltpu.make_async_copy(zero_ref.at[pl.ds(0, size * SUBLANES), :],
                                       xs_ref.at[pl.ds(start, size * SUBLANES), :], sem.at[1])
            yield (n & size) != 0, cp

    def tail_copy(blk):
        start = pl.multiple_of(blk * block, block)
        return pltpu.make_async_copy(zero_ref, xs_ref.at[pl.ds(start, block), :], sem.at[1])

    def for_zero_copies(fn):
        def per_expert(e, carry):
            for used, cp in pad_pieces(e):
                pl.when(used)(functools.partial(fn, cp))
            return carry
        lax.fori_loop(0, n_experts, per_expert, 0)

        def per_block(blk, carry):
            fn(tail_copy(blk))
            return carry
        lax.fori_loop(nu_ref[0], n_blocks, per_block, 0)

    @pl.when(i == 0)
    def _():
        zero_ref[...] = jnp.zeros_like(zero_ref)
        for_zero_copies(lambda cp: cp.start())

    def body(t, carry):
        src = pl.multiple_of(t * SUBLANES, SUBLANES)
        for k in range(TOP_K):
            dst = pl.multiple_of(row_ref[k * n_total + i * tm + t] * SUBLANES, SUBLANES)
            pltpu.make_async_copy(hn_ref.at[pl.ds(src, SUBLANES), :], xs_ref.at[pl.ds(dst, SUBLANES), :],
                                  sem.at[0]).start()
        return carry
    lax.fori_loop(0, tm, body, 0)

    _split_w1_tile(w1_ref, w1g_ref, w1l_ref)

    for _ in range(TOP_K):
        pltpu.make_async_copy(hn_ref, xs_ref.at[pl.ds(0, tile), :], sem.at[0]).wait()

    @pl.when(i == pl.num_programs(0) - 1)
    def _():
        for_zero_copies(lambda cp: cp.wait())


def _dispatch(sorted_rows, zlo, zhi, n_used, hn, w1, n_blocks):
    n_total = hn.shape[0] // SUBLANES
    n_exp, d, ff2 = w1.shape
    tr = next(t for t in (ROW_TILE, 2 * ROW_TILE, d) if d % t == 0 and n_total % (n_exp * (d // t)) == 0)
    row_tiles = d // tr
    steps = n_exp * row_tiles
    tm = n_total // steps
    rows = EXPERT_ROWS
    assert rows & (rows - 1) == 0
    half = pl.BlockSpec((None, tr, ff2 // 2), lambda i, *_: (i // row_tiles, i % row_tiles, 0))
    grid_spec = pltpu.PrefetchScalarGridSpec(
        num_scalar_prefetch=4,
        grid=(steps,),
        in_specs=[pl.BlockSpec((tm * SUBLANES, LANES), lambda i, *_: (i, 0)),
                  pl.BlockSpec((None, tr, ff2), lambda i, *_: (i // row_tiles, i % row_tiles, 0))],
        out_specs=[pl.BlockSpec(memory_space=pl.ANY), half, half],
        scratch_shapes=[pltpu.VMEM((rows * SUBLANES, LANES), F32), pltpu.SemaphoreType.DMA((2,))],
    )
    return pl.pallas_call(
        functools.partial(_dispatch_kernel, tm=tm, n_total=n_total, rows=rows, n_experts=n_exp, n_blocks=n_blocks),
        grid_spec=grid_spec,
        out_shape=[jax.ShapeDtypeStruct((n_blocks * rows * SUBLANES, LANES), F32),
                   jax.ShapeDtypeStruct((n_exp, d, ff2 // 2), BF16), jax.ShapeDtypeStruct((n_exp, d, ff2 // 2), BF16)],
        compiler_params=_cparams("arbitrary"),
        name="dispatch_split_w1",
    )(sorted_rows, zlo, zhi, n_used, hn, w1)


def _expert_kernel(be_ref, nu_ref, xs_ref, w1g_ref, w1l_ref, b1g_ref, b1l_ref, w2_ref, b2_ref, y_ref, *, rows):
    j = pl.program_id(0)
    n_used = nu_ref[0]

    @pl.when(j < n_used)
    def _():
        x = jnp.concatenate([xs_ref[pl.ds(s, rows, stride=SUBLANES), :] for s in range(SUBLANES)],
                            axis=-1).astype(BF16)
        glu = jnp.minimum(_dot(x, w1g_ref[...]) + b1g_ref[...], SWIGLU_LIMIT)
        lin = jnp.clip(_dot(x, w1l_ref[...]) + b1l_ref[...], -SWIGLU_LIMIT, SWIGLU_LIMIT)
        act = glu * jax.nn.sigmoid(SWIGLU_ALPHA * glu) * (lin + 1.0)
        y = _dot(act.astype(BF16), w2_ref[...].astype(BF16)) + b2_ref[...]
        for s in range(SUBLANES):
            y_ref[pl.ds(s, rows, stride=SUBLANES), :] = y[:, s * LANES:(s + 1) * LANES]

    @pl.when(j >= n_used)
    def _():
        y_ref[...] = jnp.zeros_like(y_ref)


def _experts(block_e, n_used, xs, w1g, w1l, b1g, b1l, w2, b2, n_blocks):
    rows = EXPERT_ROWS
    tile = rows * SUBLANES
    wspec = lambda a: pl.BlockSpec((None,) + a.shape[1:], lambda j, be, nu: (be[j], 0, 0))
    grid_spec = pltpu.PrefetchScalarGridSpec(
        num_scalar_prefetch=2,
        grid=(n_blocks,),
        in_specs=[pl.BlockSpec((tile, LANES), lambda j, be, nu: (jnp.minimum(j, nu[0] - 1), 0)),
                  wspec(w1g), wspec(w1l), wspec(b1g), wspec(b1l), wspec(w2), wspec(b2)],
        out_specs=pl.BlockSpec((tile, LANES), lambda j, be, nu: (j, 0)),
    )
    return pl.pallas_call(
        functools.partial(_expert_kernel, rows=rows),
        grid_spec=grid_spec,
        out_shape=jax.ShapeDtypeStruct((n_blocks * tile, LANES), F32),
        compiler_params=_cparams("arbitrary"),
        name="experts",
    )(block_e, n_used, xs, w1g, w1l, b1g, b1l, w2, b2)


def _combine_kernel(row_ref, yb_ref, gate_ref, h_ref, g_ref, yp_ref, ys_ref, cbuf, sem, *, tm, n_total,
                    prompt_tiles):
    i = pl.program_id(0)
    n = pl.num_programs(0)
    slot = i % 2
    tile = tm * SUBLANES

    def gather(blk, sl):
        def body(t, carry):
            for k in range(TOP_K):
                src = pl.multiple_of(row_ref[k * n_total + blk * tm + t] * SUBLANES, SUBLANES)
                dst = pl.multiple_of((k * tm + t) * SUBLANES, SUBLANES)
                pltpu.make_async_copy(yb_ref.at[pl.ds(src, SUBLANES), :], cbuf.at[sl, pl.ds(dst, SUBLANES), :],
                                      sem.at[sl]).start()
            return carry
        lax.fori_loop(0, tm, body, 0)

    @pl.when(i == 0)
    def _():
        gather(0, 0)

    @pl.when(i + 1 < n)
    def _():
        gather(i + 1, 1 - slot)

    pltpu.make_async_copy(yb_ref.at[pl.ds(0, TOP_K * tile), :], cbuf.at[slot], sem.at[slot]).wait()
    gates = gate_ref[...].T
    slabs = []
    for s in range(SUBLANES):
        acc = None
        for k in range(TOP_K):
            part = gates[:, k:k + 1] * cbuf[slot, pl.ds(k * tile + s, tm, stride=SUBLANES), :]
            acc = part if acc is None else acc + part
        slabs.append(acc)
    hf = h_ref[...] + jnp.concatenate(slabs, axis=-1)
    y = _rms(hf, g_ref[...])

    @pl.when(i < prompt_tiles)
    def _():
        yp_ref[...] = y

    @pl.when(i >= prompt_tiles)
    def _():
        ys_ref[...] = y


def _combine(rows, yb, gates, h, gf, np_):
    n, d = h.shape
    ns_ = n - np_
    tm = min(COMBINE_TILE, math.gcd(np_, ns_))
    prompt_tiles = np_ // tm
    row = lambda w: pl.BlockSpec((tm, w), lambda i, *_: (i, 0))
    grid_spec = pltpu.PrefetchScalarGridSpec(
        num_scalar_prefetch=1,
        grid=(n // tm,),
        in_specs=[pl.BlockSpec(memory_space=pl.ANY), pl.BlockSpec((gates.shape[0], tm), lambda i, *_: (0, i)), row(d),
                  pl.BlockSpec(gf.shape, lambda i, *_: (0, 0))],
        out_specs=[pl.BlockSpec((tm, d), lambda i, *_: (jnp.minimum(i, prompt_tiles - 1), 0)),
                   pl.BlockSpec((tm, d), lambda i, *_: (jnp.maximum(i - prompt_tiles, 0), 0))],
        scratch_shapes=[pltpu.VMEM((2, TOP_K * tm * SUBLANES, LANES), F32), pltpu.SemaphoreType.DMA((2,))],
    )
    return pl.pallas_call(
        functools.partial(_combine_kernel, tm=tm, n_total=n, prompt_tiles=prompt_tiles),
        grid_spec=grid_spec,
        out_shape=[jax.ShapeDtypeStruct((np_, d), F32), jax.ShapeDtypeStruct((ns_, d), F32)],
        compiler_params=_cparams("arbitrary"),
        name="combine_norm",
    )(rows, yb, gates, h, gf)


def kernel(x_prompt, x_sample, state_pool, cache_k, cache_v, page_table, norm1_g, w_in, pool_w, pool_scale,
           lambda_q1, lambda_k1, lambda_q2, lambda_k2, subln_g, w_out, norm2_g, router_w, router_b,
           w1, b1, w2, b2, normf_g):
    batch, seq, d = x_prompt.shape
    bs, n_new, _ = x_sample.shape
    depth = w_in.shape[0]
    assert depth == 1, "single-layer trunk"
    n_heads = cache_k.shape[3]
    page_size = cache_k.shape[2]
    pw = pool_w.shape[1] * pool_w.shape[2]
    qk_w = n_heads * HEAD_W
    n_experts = router_w.shape[2]
    np_, ns_ = batch * seq, bs * n_new
    n_total = np_ + ns_
    lam_init = 0.8 - 0.6 * math.exp(-0.3 * 0)

    g1 = norm1_g[0][None]
    w_in_b = w_in[0].astype(BF16)
    wvt_b = w_in_b[:, pw + 2 * qk_w:].T
    pool_w_b = pool_w[0].astype(BF16)
    ps = pool_scale[0][None]
    lams = [a[0][None] for a in (lambda_q1, lambda_k1, lambda_q2, lambda_k2)]
    sg = subln_g[0][None]
    slopes = jnp.exp2(-8.0 * (jnp.arange(n_heads, dtype=F32) + 1.0) / n_heads)
    w_out_b = w_out[0].astype(BF16)
    g2 = norm2_g[0][None]
    rwt_b = router_w[0].T.astype(BF16)
    rb = router_b[0][:, None]
    b1_pairs = b1[0].T.reshape(-1, 2, n_experts)
    b1g = b1_pairs[:, 0, :].T[:, None, :]
    b1l = b1_pairs[:, 1, :].T[:, None, :]
    w2b = w2.reshape(w2.shape[1:])
    b2e = b2[0][:, None, :]
    gf = normf_g[None]

    xp = x_prompt.reshape(np_, d)
    tq = min(ATTN_TILE, seq)
    pool_p, tail_p, q_p, k_p, v_p, kb_p, vt_p = _inproj(xp, g1, w_in_b, wvt_b, pw, qk_w,
                                                        prompt=(pool_w_b, ps, seq, tq))
    o_p = _attn_prompt(q_p, kb_p, vt_p, slopes, lams, sg, batch, seq, n_heads, lam_init, tq)

    xs = x_sample.reshape(ns_, d)
    u_s, q_s, k_s, v_s = _inproj(xs, g1, w_in_b, wvt_b, pw, qk_w)
    pool_s, new_pool_s = _pool_sample(jnp.transpose(state_pool[0], (1, 0, 2)),
                                      jnp.transpose(u_s.reshape(bs, n_new, pw), (1, 0, 2)), pool_w_b, ps)
    pool_s = jnp.transpose(pool_s, (1, 0, 2))
    new_pool_s = jnp.transpose(new_pool_s, (1, 0, 2))
    kc = cache_k.reshape(cache_k.shape[1], page_size * n_heads, HEAD_W)
    vc = cache_v.reshape(cache_v.shape[1], page_size * n_heads, HEAD_W)
    o_s = _attn_sample(q_s.reshape(bs, n_new, qk_w), k_s.reshape(bs, n_new, qk_w), v_s.reshape(bs, n_new, qk_w),
                       kc, vc, page_table, slopes, lams, sg, n_heads, lam_init)

    h, hn, code, gates, cnt = _outproj((pool_p, o_p, xp), (pool_s.reshape(ns_, pw), o_s.reshape(ns_, qk_w), xs),
                                       w_out_b, g2, rwt_b, rb)

    counts = cnt[:, 0].astype(I32)
    n_blocks = -(-n_total * TOP_K // EXPERT_ROWS) + n_experts
    padded = (counts + EXPERT_ROWS - 1) // EXPERT_ROWS * EXPERT_ROWS
    pad_ends = jnp.cumsum(padded)
    pad_starts = pad_ends - padded
    n_used = (pad_ends[-1:] // EXPERT_ROWS).astype(I32)
    block_start = jnp.arange(n_blocks, dtype=I32) * EXPERT_ROWS
    block_e = jnp.minimum(jnp.sum((pad_ends[None, :] <= block_start[:, None]).astype(I32), axis=1), n_experts - 1)

    sorted_rows = _sorted_rows(pad_starts, code).reshape(-1)
    xs_rows, w1g, w1l = _dispatch(sorted_rows, pad_starts + counts, pad_ends, n_used, hn, w1.reshape(w1.shape[1:]),
                                  n_blocks)
    yb = _experts(block_e, n_used, xs_rows, w1g, w1l, b1g, b1l, w2b, b2e, n_blocks)
    y_p, y_s = _combine(sorted_rows, yb, gates, h, gf, np_)

    return (y_p.reshape(batch, seq, d), y_s.reshape(bs, n_new, d),
            tail_p[None, :, POOL_HALO - state_pool.shape[2]:],
            new_pool_s[None],
            k_p.reshape(1, batch, seq, n_heads, HEAD_W), v_p.reshape(1, batch, seq, n_heads, HEAD_W),
            k_s.reshape(1, bs, n_new, n_heads, HEAD_W), v_s.reshape(1, bs, n_new, n_heads, HEAD_W))
```

```python
import functools
import math

import jax
import jax.numpy as jnp
from jax import lax
from jax.experimental import pallas as pl
from jax.experimental.pallas import tpu as pltpu

F32 = jnp.float32
BF16 = jnp.bfloat16
I32 = jnp.int32

RMS_EPS = 1e-6
NEG_INF = -1e30
HEAD_DIM = 64
HEAD_W = 2 * HEAD_DIM
POOL_WINDOWS = (2, 4, 8, 16)
POOL_HALO = 16
TOP_K = 4
RANK_BITS = 20
SWIGLU_LIMIT = 7.0
SWIGLU_ALPHA = 1.702
LANES = 128
SUBLANES = 8
ROW_TILE = 512
ATTN_TILE = 512
SAMPLE_HEADS = 2
KEY_TILES = 2
ATTN_HEADS = 4
LOG2E = 1.4426950408889634
N_SPLIT = 3
POS_LOW = 16
BF16_ROWS = 16
EXPERT_ROWS = 512
COMBINE_TILE = 512
VMEM_LIMIT = 48 * 1024 * 1024


def _cparams(*sem):
    return pltpu.CompilerParams(dimension_semantics=sem, vmem_limit_bytes=VMEM_LIMIT)


def _dot(a, b):
    return jnp.dot(a, b, preferred_element_type=F32)


def _dot_nt(a, b):
    return lax.dot_general(a, b, (((1,), (1,)), ((), ())), preferred_element_type=F32)


def _rms(x, g):
    return x * lax.rsqrt(jnp.mean(x * x, axis=-1, keepdims=True) + RMS_EPS) * g


def _inproj_sample_kernel(x_ref, g_ref, w_ref, wvt_ref, u_ref, q_ref, k_ref, v_ref, *, pool_w, qk_w):
    del wvt_ref
    u = _inproj_tile(x_ref, g_ref, w_ref, q_ref, k_ref, v_ref, pool_w=pool_w, qk_w=qk_w)[0]
    u_ref[...] = u


def _inproj_prompt_kernel(x_ref, g_ref, w_ref, wvt_ref, pw_ref, ps_ref, pool_ref, tail_ref, q_ref, k_ref, v_ref, kb_ref,
                          vt_ref, carry_ref, *, pool_w, qk_w, key_tile, seq_tiles):
    u, k, xb = _inproj_tile(x_ref, g_ref, w_ref, q_ref, k_ref, v_ref, pool_w=pool_w, qk_w=qk_w)
    _pool_tile(u, pl.program_id(0) % seq_tiles, pw_ref, ps_ref, pool_ref, carry_ref)
    tail_ref[...] = u[u.shape[0] - POOL_HALO:, :]
    kb_ref[...] = k.astype(BF16)
    vt = _dot_nt(wvt_ref[...], xb)
    for c in range(vt_ref.shape[0]):
        vt_ref[c] = vt[:, c * key_tile:(c + 1) * key_tile].astype(BF16)


def _inproj_tile(x_ref, g_ref, w_ref, q_ref, k_ref, v_ref, *, pool_w, qk_w):
    xb = _rms(x_ref[...], g_ref[...]).astype(BF16)
    c0, c1, c2 = pool_w, pool_w + qk_w, pool_w + 2 * qk_w
    u = _dot(xb, w_ref[:, 0:c0])
    q_ref[...] = _dot(xb, w_ref[:, c0:c1]) * (HEAD_DIM ** -0.5)
    k = _dot(xb, w_ref[:, c1:c2])
    v = _dot(xb, w_ref[:, c2:])
    n_heads = qk_w // HEAD_W
    tm = k.shape[0]
    for h in range(n_heads):
        k_ref[pl.ds(h, tm, stride=n_heads), :] = k[:, h * HEAD_W:(h + 1) * HEAD_W]
        v_ref[pl.ds(h, tm, stride=n_heads), :] = v[:, h * HEAD_W:(h + 1) * HEAD_W]
    return u, k, xb


def _inproj(x, g1, w_in_b, wvt_b, pool_w, qk_w, prompt=None):
    n, d = x.shape
    val_w = wvt_b.shape[0]
    tm = min(ROW_TILE, n)
    row = lambda w: pl.BlockSpec((tm, w), lambda i: (i, 0))
    full = lambda a: pl.BlockSpec(a.shape, lambda i: (0,) * a.ndim)
    n_heads = qk_w // HEAD_W
    assert val_w == qk_w
    heads = pl.BlockSpec((tm * n_heads, HEAD_W), lambda i: (i, 0))
    qkv_specs = [row(qk_w), heads, heads]
    qkv_shape = [jax.ShapeDtypeStruct((n, qk_w), F32), jax.ShapeDtypeStruct((n * n_heads, HEAD_W), F32),
                 jax.ShapeDtypeStruct((n * n_heads, HEAD_W), F32)]
    if prompt is None:
        return pl.pallas_call(
            functools.partial(_inproj_sample_kernel, pool_w=pool_w, qk_w=qk_w),
            grid=(n // tm,),
            in_specs=[row(d), full(g1), full(w_in_b), full(wvt_b)],
            out_specs=[row(pool_w)] + qkv_specs,
            out_shape=[jax.ShapeDtypeStruct((n, pool_w), F32)] + qkv_shape,
            compiler_params=_cparams("parallel"),
            name="inproj_sample",
        )(x, g1, w_in_b, wvt_b)
    pool_w_b, pool_scale, seq, key_tile = prompt
    assert tm % key_tile == 0 and seq % tm == 0 and tm >= POOL_HALO
    seq_tiles = seq // tm
    return pl.pallas_call(
        functools.partial(_inproj_prompt_kernel, pool_w=pool_w, qk_w=qk_w, key_tile=key_tile, seq_tiles=seq_tiles),
        grid=(n // tm,),
        in_specs=[row(d), full(g1), full(w_in_b), full(wvt_b), full(pool_w_b), full(pool_scale)],
        out_specs=[row(pool_w), pl.BlockSpec((None, POOL_HALO, pool_w), lambda i: (i // seq_tiles, 0, 0))] + qkv_specs
        + [row(qk_w), pl.BlockSpec((tm // key_tile, val_w, key_tile), lambda i: (i, 0, 0))],
        out_shape=[jax.ShapeDtypeStruct((n, pool_w), BF16), jax.ShapeDtypeStruct((n // seq, POOL_HALO, pool_w), F32)]
        + qkv_shape + [jax.ShapeDtypeStruct((n, qk_w), BF16),
                       jax.ShapeDtypeStruct((n // key_tile, val_w, key_tile), BF16)],
        scratch_shapes=[pltpu.VMEM((POOL_HALO, pool_w), F32)],
        compiler_params=_cparams("arbitrary"),
        name="inproj_pool_prompt",
    )(x, g1, w_in_b, wvt_b, pool_w_b, pool_scale)


def _pool_group(s_win, tok, cnt, pw, ps):
    d = s_win / cnt - tok
    return _dot(d.astype(BF16), pw) * ps


def _pool_tile(cur, j, pw_ref, ps_ref, o_ref, carry_ref):
    tt = cur.shape[0]

    @pl.when(j == 0)
    def _():
        carry_ref[...] = jnp.zeros_like(carry_ref)

    ext = jnp.concatenate([carry_ref[...], cur], axis=0)
    carry_ref[...] = cur[tt - POOL_HALO:, :]
    pos = (j * tt + lax.broadcasted_iota(I32, (tt, 1), 0)).astype(F32)
    gw = cur.shape[1] // len(POOL_WINDOWS)
    for g, w in enumerate(POOL_WINDOWS):
        cols = slice(g * gw, (g + 1) * gw)
        e = ext[:, cols]
        s, span = e, 1
        while span < w:
            s = s + pltpu.roll(s, span, axis=0)
            span *= 2
        cnt = jnp.minimum(float(w), pos + 1.0)
        out = _pool_group(s[POOL_HALO:, :], e[POOL_HALO:, :], cnt, pw_ref[g], ps_ref[:, cols])
        o_ref[:, cols] = out.astype(o_ref.dtype)


def _pool_sample_kernel(sp_ref, u_ref, pw_ref, ps_ref, o_ref, np_ref, *, n_new, n_buf):
    rows = [sp_ref[r] for r in range(n_buf)] + [u_ref[r] for r in range(n_new)]
    for r in range(n_buf):
        np_ref[r] = rows[n_new + r]
    gw = rows[0].shape[1] // len(POOL_WINDOWS)
    for t in range(n_new):
        i = n_buf + t
        for g, w in enumerate(POOL_WINDOWS):
            cols = slice(g * gw, (g + 1) * gw)
            s = rows[i][:, cols]
            for back in range(1, w):
                s = s + rows[i - back][:, cols]
            out = _pool_group(s, rows[i][:, cols], float(w), pw_ref[g], ps_ref[:, cols])
            o_ref[t, :, cols] = out


def _pool_sample(state_pool_t, u_t, pool_w_b, pool_scale):
    n_buf, bs, pw = state_pool_t.shape
    n_new = u_t.shape[0]
    full = lambda a: pl.BlockSpec(a.shape, lambda i: (0,) * a.ndim)
    return pl.pallas_call(
        functools.partial(_pool_sample_kernel, n_new=n_new, n_buf=n_buf),
        grid=(1,),
        in_specs=[full(state_pool_t), full(u_t), full(pool_w_b), full(pool_scale)],
        out_specs=[pl.BlockSpec((n_new, bs, pw), lambda i: (0, 0, 0)),
                   pl.BlockSpec((n_buf, bs, pw), lambda i: (0, 0, 0))],
        out_shape=[jax.ShapeDtypeStruct((n_new, bs, pw), F32),
                   jax.ShapeDtypeStruct((n_buf, bs, pw), F32)],
        compiler_params=_cparams("arbitrary"),
        name="pool_sample",
    )(state_pool_t, u_t, pool_w_b, pool_scale)


def _lambda_full(lq1, lk1, lq2, lk2, lam_init):
    e1 = jnp.exp(jnp.sum(lq1 * lk1, axis=-1, keepdims=True))
    e2 = jnp.exp(jnp.sum(lq2 * lk2, axis=-1, keepdims=True))
    return e1 - e2 + lam_init


def _split_halves(q):
    lane = lax.broadcasted_iota(I32, q.shape, 1)
    zero = jnp.zeros_like(q)
    return jnp.concatenate([jnp.where(lane < HEAD_DIM, q, zero), jnp.where(lane >= HEAD_DIM, q, zero)], axis=0)


def _subln(o, g, lam_init):
    return _rms(o, g) * (1.0 - lam_init)


def _attn_prompt_kernel(slopes_ref, q_ref, k_ref, vt_ref, lq1_ref, lk1_ref, lq2_ref, lk2_ref, g_ref, o_ref,
                        *, tq, lam_init, heads):
    hg = pl.program_id(1)
    i = pl.program_id(2)
    lam = _lambda_full(lq1_ref[...], lk1_ref[...], lq2_ref[...], lk2_ref[...], lam_init)
    head_cols = [slice(hh * HEAD_W, (hh + 1) * HEAD_W) for hh in range(heads)]
    row = lax.broadcasted_iota(I32, (KEY_TILES * tq, LANES), 0)
    lane = lax.broadcasted_iota(I32, (KEY_TILES * tq, LANES), 1)
    pos_feat = jnp.where(lane < 2 * N_SPLIT, jnp.where(lane % 2 == 0, row & ~(POS_LOW - 1), row & (POS_LOW - 1)), 0)
    pos_feat = pos_feat.astype(F32).astype(BF16)
    qlane = lax.broadcasted_iota(I32, (2 * tq, LANES), 1)
    slopes2, qqs = [], []
    for hh in range(heads):
        s2 = slopes_ref[hg * heads + hh] * LOG2E
        slopes2.append(s2)
        rest = jnp.full((2 * tq, LANES), s2, F32)
        feat = jnp.zeros((2 * tq, LANES), F32)
        for part in range(N_SPLIT):
            piece = rest.astype(BF16).astype(F32)
            rest = rest - piece
            feat = jnp.where(qlane // 2 == part, piece, feat)
        qq = _split_halves((q_ref[:, head_cols[hh]] * LOG2E).astype(BF16))
        qqs.append(jnp.concatenate([qq, feat.astype(BF16)], axis=1))
    def head_step(hh, j, tiles, carry, diag_tile):
        m, acc = carry
        n_keys = tiles * tq
        start = pl.multiple_of(j * tq, tq)
        keys = jnp.concatenate([k_ref[pl.ds(start, n_keys), head_cols[hh]], pos_feat[:n_keys]], axis=1)
        st = _dot_nt(keys, qqs[hh])
        if diag_tile is not None:
            kr = lax.broadcasted_iota(I32, st.shape, 0) - diag_tile * tq
            qc = lax.broadcasted_iota(I32, st.shape, 1)
            qc = jnp.where(qc >= tq, qc - tq, qc)
            st = jnp.where(qc >= kr, st, NEG_INF)
        offset = slopes2[hh] * ((j - i) * tq).astype(F32)
        m_new = jnp.maximum(m, jnp.max(st, axis=0, keepdims=True) + offset)
        alpha = jnp.exp2(m - m_new)
        p = jnp.exp2(st - (m_new - offset)).astype(BF16)
        values = jnp.concatenate([vt_ref[j + t, head_cols[hh], :] for t in range(tiles)], axis=1)
        values = jnp.concatenate([values, jnp.ones((BF16_ROWS, n_keys), BF16)], axis=0)
        acc = alpha * acc + _dot(values, p)
        return m_new, acc

    def step(j, tiles, carries, diag_tile):
        return tuple(head_step(hh, j, tiles, carries[hh], diag_tile) for hh in range(heads))

    init = (jnp.full((1, 2 * tq), NEG_INF, F32), jnp.zeros((HEAD_W + BF16_ROWS, 2 * tq), F32))
    carries = lax.fori_loop(0, i // KEY_TILES, lambda g, cr: step(g * KEY_TILES, KEY_TILES, cr, None),
                            (init,) * heads)
    rest = i % KEY_TILES
    carries = lax.switch(rest, [functools.partial(step, i - r, r + 1, diag_tile=r) for r in range(KEY_TILES)],
                         carries)
    for hh, (_, acc) in enumerate(carries):
        o = acc[:HEAD_W] / acc[HEAD_W:HEAD_W + 1]
        o = (o[:, :tq] - lam * o[:, tq:]).T
        o_ref[:, head_cols[hh]] = _subln(o, g_ref[...], lam_init).astype(o_ref.dtype)


def _attn_prompt(q, kb, vt, slopes, lams, subln_g, batch, seq, n_heads, lam_init, tq):
    heads = math.gcd(n_heads, ATTN_HEADS)
    width = heads * HEAD_W
    q3, k3 = (a.reshape(batch, seq, n_heads * HEAD_W) for a in (q, kb))
    small = lambda a: pl.BlockSpec(a.shape, lambda b, h, i, s: (0,) * a.ndim)
    grid_spec = pltpu.PrefetchScalarGridSpec(
        num_scalar_prefetch=1,
        grid=(batch, n_heads // heads, seq // tq),
        in_specs=[pl.BlockSpec((None, tq, width), lambda b, h, i, s: (b, i, h)),
                  pl.BlockSpec((None, seq, width), lambda b, h, i, s: (b, 0, h)),
                  pl.BlockSpec((seq // tq, width, tq), lambda b, h, i, s: (b, h, 0)),
                  *[small(a) for a in lams], small(subln_g)],
        out_specs=pl.BlockSpec((None, tq, width), lambda b, h, i, s: (b, i, h)),
    )
    out = pl.pallas_call(
        functools.partial(_attn_prompt_kernel, tq=tq, lam_init=lam_init, heads=heads),
        grid_spec=grid_spec,
        out_shape=jax.ShapeDtypeStruct((batch, seq, n_heads * HEAD_W), BF16),
        compiler_params=_cparams("parallel", "parallel", "arbitrary"),
        name="attn_prompt",
    )(slopes, q3, k3, vt, *lams, subln_g)
    return out.reshape(batch * seq, n_heads * HEAD_W)


def _attn_sample_kernel(pt_ref, slopes_ref, q_ref, kn_ref, vn_ref, kc_ref, vc_ref,
                        lq1_ref, lk1_ref, lq2_ref, lk2_ref, g_ref, o_ref, kbuf, vbuf, sem,
                        *, n_pages, page_rows, n_heads, n_new, lam_init):
    b = pl.program_id(0)
    nb = pl.num_programs(0)
    slot = b % 2
    past = n_pages * page_rows // n_heads

    def page_copies(seq, sl, pg):
        p = pt_ref[seq * n_pages + pg]
        dst = pl.ds(pg * page_rows, page_rows)
        return (pltpu.make_async_copy(kc_ref.at[p], kbuf.at[sl, dst, :], sem.at[sl, 0]),
                pltpu.make_async_copy(vc_ref.at[p], vbuf.at[sl, dst, :], sem.at[sl, 1]))

    def start_seq(seq, sl):
        for pg in range(n_pages):
            for cp in page_copies(seq, sl, pg):
                cp.start()

    @pl.when(b == 0)
    def _():
        start_seq(0, 0)

    @pl.when(b + 1 < nb)
    def _():
        start_seq(b + 1, 1 - slot)

    for pg in range(n_pages):
        for cp in page_copies(b, slot, pg):
            cp.wait()

    lam = _lambda_full(lq1_ref[...], lk1_ref[...], lq2_ref[...], lk2_ref[...], lam_init)
    grp = 2 * SUBLANES
    rows = SAMPLE_HEADS * grp
    r = lax.broadcasted_iota(I32, (rows, past), 0) % grp
    t_past = jnp.where(r >= n_new, r - n_new, r)
    dist_past = (past + t_past - lax.broadcasted_iota(I32, (rows, past), 1)).astype(F32)
    r1 = lax.broadcasted_iota(I32, (rows, 1), 0)
    t_new = jnp.where(r1 % grp >= n_new, r1 % grp - n_new, r1 % grp)
    pad = jnp.zeros((grp - 2 * n_new, HEAD_W), F32)
    for h0 in range(0, n_heads, SAMPLE_HEADS):
        heads = range(h0, h0 + SAMPLE_HEADS)
        cols = slice(h0 * HEAD_W, (h0 + SAMPLE_HEADS) * HEAD_W)
        slope = jnp.concatenate([jnp.full((grp, 1), slopes_ref[h], F32) for h in heads], axis=0)
        blocks = []
        for i, h in enumerate(heads):
            qh = jnp.concatenate([_split_halves(q_ref[:, h * HEAD_W:(h + 1) * HEAD_W]), pad], axis=0)
            zero = jnp.zeros_like(qh)
            blocks.append(jnp.concatenate([qh if j == i else zero for j in range(SAMPLE_HEADS)], axis=1))
        qq = jnp.concatenate(blocks, axis=0)
        kh = jnp.concatenate([kbuf[slot, pl.ds(h, past, stride=n_heads), :] for h in heads], axis=1).astype(BF16)
        vh = jnp.concatenate([vbuf[slot, pl.ds(h, past, stride=n_heads), :] for h in heads], axis=1).astype(BF16)
        s = _dot_nt(qq.astype(BF16), kh) - slope * dist_past
        kn = kn_ref[:, cols]
        vn = vn_ref[:, cols]
        s_new = []
        for c in range(n_new):
            sc = jnp.sum(qq * kn[c:c + 1, :], axis=-1, keepdims=True) - slope * (t_new - c).astype(F32)
            s_new.append(jnp.where(t_new >= c, sc, NEG_INF))
        m = jnp.max(s, axis=-1, keepdims=True)
        for sc in s_new:
            m = jnp.maximum(m, sc)
        p = jnp.exp(s - m)
        l = jnp.sum(p, axis=-1, keepdims=True)
        acc = _dot(p.astype(BF16), vh)
        for c, sc in enumerate(s_new):
            pc = jnp.exp(sc - m)
            l = l + pc
            acc = acc + pc * vn[c:c + 1, :]
        o = acc / l
        for i, h in enumerate(heads):
            oh = o[i * grp:(i + 1) * grp, i * HEAD_W:(i + 1) * HEAD_W]
            oh = oh[0:n_new] - lam * oh[n_new:2 * n_new]
            o_ref[:, h * HEAD_W:(h + 1) * HEAD_W] = _subln(oh, g_ref[...], lam_init)


def _attn_sample(q3, kn3, vn3, kc, vc, page_table, slopes, lams, subln_g, n_heads, lam_init):
    bs, n_new, qk_w = q3.shape
    n_pages = page_table.shape[1]
    page_rows = kc.shape[1]
    small = lambda a: pl.BlockSpec(a.shape, lambda b, pt, s: (0,) * a.ndim)
    tok = pl.BlockSpec((None, n_new, qk_w), lambda b, pt, s: (b, 0, 0))
    grid_spec = pltpu.PrefetchScalarGridSpec(
        num_scalar_prefetch=2,
        grid=(bs,),
        in_specs=[tok, tok, tok, pl.BlockSpec(memory_space=pl.ANY), pl.BlockSpec(memory_space=pl.ANY),
                  *[small(a) for a in lams], small(subln_g)],
        out_specs=tok,
        scratch_shapes=[pltpu.VMEM((2, n_pages * page_rows, LANES), F32),
                        pltpu.VMEM((2, n_pages * page_rows, LANES), F32),
                        pltpu.SemaphoreType.DMA((2, 2))],
    )
    return pl.pallas_call(
        functools.partial(_attn_sample_kernel, n_pages=n_pages, page_rows=page_rows, n_heads=n_heads,
                          n_new=n_new, lam_init=lam_init),
        grid_spec=grid_spec,
        out_shape=jax.ShapeDtypeStruct((bs, n_new, qk_w), F32),
        compiler_params=_cparams("arbitrary"),
        name="attn_sample",
    )(page_table.reshape(-1), slopes, q3, kn3, vn3, kc, vc, *lams, subln_g)


def _outproj_kernel(pool_p_ref, o_p_ref, x_p_ref, pool_s_ref, o_s_ref, x_s_ref, wo_ref, g2_ref, rwt_ref, rb_ref,
                    earlier_ref, h_ref, hn_ref, code_ref, gate_ref, cnt_out_ref, run_ref, *, tm, n_experts,
                    prompt_tiles):
    i = pl.program_id(0)

    @pl.when(i == 0)
    def _():
        run_ref[...] = jnp.zeros_like(run_ref)

    tile = functools.partial(_outproj_tile, wo_ref=wo_ref, g2_ref=g2_ref, rwt_ref=rwt_ref, rb_ref=rb_ref,
                             earlier_ref=earlier_ref, h_ref=h_ref, hn_ref=hn_ref, code_ref=code_ref, gate_ref=gate_ref,
                             cnt_out_ref=cnt_out_ref, run_ref=run_ref, tm=tm, n_experts=n_experts)
    pl.when(i < prompt_tiles)(functools.partial(tile, pool_p_ref, o_p_ref, x_p_ref))
    pl.when(i >= prompt_tiles)(functools.partial(tile, pool_s_ref, o_s_ref, x_s_ref))


def _outproj_tile(pool_ref, o_ref, x_ref, *, wo_ref, g2_ref, rwt_ref, rb_ref, earlier_ref, h_ref, hn_ref, code_ref,
                  gate_ref, cnt_out_ref, run_ref, tm, n_experts):
    pw = pool_ref.shape[1]
    mix = _dot(pool_ref[...].astype(BF16), wo_ref[0:pw, :]) + _dot(o_ref[...].astype(BF16), wo_ref[pw:, :])
    h = x_ref[...] + mix
    h_ref[...] = h
    hn = _rms(h, g2_ref[...])
    for s in range(hn.shape[1] // LANES):
        hn_ref[pl.ds(s, tm, stride=SUBLANES), :] = hn[:, s * LANES:(s + 1) * LANES]
    logits = _dot_nt(rwt_ref[...], hn.astype(BF16)) + rb_ref[...]
    expert = lax.broadcasted_iota(I32, logits.shape, 0)
    vals, idxs = [], []
    for _ in range(TOP_K):
        m = jnp.max(logits, axis=0, keepdims=True)
        idx = jnp.min(jnp.where(logits == m, expert, n_experts), axis=0, keepdims=True)
        vals.append(m)
        idxs.append(idx)
        logits = jnp.where(expert == idx, -jnp.inf, logits)
    ex = [jnp.exp(v - vals[0]) for v in vals]
    den = ex[0]
    for e in ex[1:]:
        den = den + e
    chosen = jnp.zeros(logits.shape, F32)
    for idx in idxs:
        chosen = chosen + jnp.where(expert == idx, 1.0, 0.0)
    before = _dot(chosen.astype(BF16), earlier_ref[...]) + run_ref[...]
    for k in range(TOP_K):
        gate_ref[k:k + 1, :] = ex[k] / den
        rank = jnp.sum(jnp.where(expert == idxs[k], before, 0.0), axis=0, keepdims=True).astype(I32)
        code_ref[k:k + 1, :] = idxs[k] * (1 << RANK_BITS) + rank
    gate_ref[TOP_K:, :] = jnp.zeros((gate_ref.shape[0] - TOP_K, tm), F32)
    run_ref[...] = run_ref[...] + jnp.sum(chosen, axis=1, keepdims=True)
    cnt_out_ref[...] = run_ref[...]


def _outproj(prompt, sample, w_out_b, g2, rwt_b, rb):
    np_, d = prompt[2].shape
    ns_ = sample[2].shape[0]
    n_total = np_ + ns_
    n_experts = rwt_b.shape[0]
    tm = min(ROW_TILE, math.gcd(np_, ns_))
    prompt_tiles = np_ // tm
    assert d == SUBLANES * LANES and n_total < (1 << RANK_BITS)
    first = lambda a: pl.BlockSpec((tm, a.shape[1]), lambda i: (jnp.minimum(i, prompt_tiles - 1), 0))
    second = lambda a: pl.BlockSpec((tm, a.shape[1]), lambda i: (jnp.maximum(i - prompt_tiles, 0), 0))
    row = lambda w: pl.BlockSpec((tm, w), lambda i: (i, 0))
    col = lambda r: pl.BlockSpec((r, tm), lambda i: (0, i))
    full = lambda a: pl.BlockSpec(a.shape, lambda i: (0,) * a.ndim)
    counts = jax.ShapeDtypeStruct((n_experts, 1), F32)
    earlier = jnp.triu(jnp.ones((tm, tm), BF16), 1)
    return pl.pallas_call(
        functools.partial(_outproj_kernel, tm=tm, n_experts=n_experts, prompt_tiles=prompt_tiles),
        grid=(n_total // tm,),
        in_specs=[*[first(a) for a in prompt], *[second(a) for a in sample],
                  full(w_out_b), full(g2), full(rwt_b), full(rb), full(earlier)],
        out_specs=[row(d), pl.BlockSpec((tm * SUBLANES, LANES), lambda i: (i, 0)),
                   col(TOP_K), col(SUBLANES), pl.BlockSpec(counts.shape, lambda i: (0, 0))],
        out_shape=[jax.ShapeDtypeStruct((n_total, d), F32), jax.ShapeDtypeStruct((n_total * SUBLANES, LANES), F32),
                   jax.ShapeDtypeStruct((TOP_K, n_total), I32), jax.ShapeDtypeStruct((SUBLANES, n_total), F32),
                   counts],
        scratch_shapes=[pltpu.VMEM(counts.shape, F32)],
        compiler_params=_cparams("arbitrary"),
        name="outproj_router",
    )(*prompt, *sample, w_out_b, g2, rwt_b, rb, earlier)


def _sorted_rows_kernel(start_ref, code_ref, rows_ref, *, n_experts):
    code = code_ref[...]
    expert = code >> RANK_BITS
    rows = code & ((1 << RANK_BITS) - 1)
    for e in range(n_experts):
        rows = rows + jnp.where(expert == e, start_ref[e], 0)
    rows_ref[...] = rows


def _sorted_rows(starts, code):
    return pl.pallas_call(
        functools.partial(_sorted_rows_kernel, n_experts=starts.shape[0]),
        grid=(1,),
        in_specs=[pl.BlockSpec(memory_space=pltpu.SMEM), pl.BlockSpec(code.shape, lambda i: (0, 0))],
        out_specs=pl.BlockSpec(code.shape, lambda i: (0, 0)),
        out_shape=jax.ShapeDtypeStruct(code.shape, I32),
        compiler_params=_cparams("arbitrary"),
        name="sorted_rows",
    )(starts, code)


MXU_DIM = 256


def _split_w1_tile(w_ref, g_ref, l_ref):
    r = lax.broadcasted_iota(I32, (MXU_DIM, MXU_DIM), 0)
    c = lax.broadcasted_iota(I32, (MXU_DIM, MXU_DIM), 1)
    half = MXU_DIM // 2
    source = jnp.where(c < half, 2 * c, 2 * (c - half) + 1)
    perm = jnp.where(r == source, 1.0, 0.0).astype(BF16)
    for blk in range(w_ref.shape[1] // MXU_DIM):
        res = _dot(w_ref[:, blk * MXU_DIM:(blk + 1) * MXU_DIM].astype(BF16), perm)
        g_ref[:, blk * half:(blk + 1) * half] = res[:, :half].astype(BF16)
        l_ref[:, blk * half:(blk + 1) * half] = res[:, half:].astype(BF16)


def _dispatch_kernel(row_ref, zlo_ref, zhi_ref, nu_ref, hn_ref, w1_ref, xs_ref, w1g_ref, w1l_ref, zero_ref, sem,
                     *, tm, n_total, rows, n_experts, n_blocks):
    i = pl.program_id(0)
    tile = tm * SUBLANES
    block = rows * SUBLANES

    def pad_pieces(e):
        lo, n = zlo_ref[e], zhi_ref[e] - zlo_ref[e]
        for bit in range(rows.bit_length() - 1):
            size = 1 << bit
            below = n & (size - 1)
            start = pl.multiple_of((lo + below) * SUBLANES, SUBLANES)
            cp = pltpu.make_async_copy(zero_ref.at[pl.ds(0, size * SUBLANES), :],
                                       xs_ref.at[pl.ds(start, size * SUBLANES), :], sem.at[1])
            yield (n & size) != 0, cp

    def tail_copy(blk):
        start = pl.multiple_of(blk * block, block)
        return pltpu.make_async_copy(zero_ref, xs_ref.at[pl.ds(start, block), :], sem.at[1])

    def for_zero_copies(fn):
        def per_expert(e, carry):
            for used, cp in pad_pieces(e):
                pl.when(used)(functools.partial(fn, cp))
            return carry
        lax.fori_loop(0, n_experts, per_expert, 0)

        def per_block(blk, carry):
            fn(tail_copy(blk))
            return carry
        lax.fori_loop(nu_ref[0], n_blocks, per_block, 0)

    @pl.when(i == 0)
    def _():
        zero_ref[...] = jnp.zeros_like(zero_ref)
        for_zero_copies(lambda cp: cp.start())

    def body(t, carry):
        src = pl.multiple_of(t * SUBLANES, SUBLANES)
        for k in range(TOP_K):
            dst = pl.multiple_of(row_ref[k * n_total + i * tm + t] * SUBLANES, SUBLANES)
            pltpu.make_async_copy(hn_ref.at[pl.ds(src, SUBLANES), :], xs_ref.at[pl.ds(dst, SUBLANES), :],
                                  sem.at[0]).start()
        return carry
    lax.fori_loop(0, tm, body, 0)

    _split_w1_tile(w1_ref, w1g_ref, w1l_ref)

    for _ in range(TOP_K):
        pltpu.make_async_copy(hn_ref, xs_ref.at[pl.ds(0, tile), :], sem.at[0]).wait()

    @pl.when(i == pl.num_programs(0) - 1)
    def _():
        for_zero_copies(lambda cp: cp.wait())


def _dispatch(sorted_rows, zlo, zhi, n_used, hn, w1, n_blocks):
    n_total = hn.shape[0] // SUBLANES
    n_exp, d, ff2 = w1.shape
    tr = next(t for t in (ROW_TILE, 2 * ROW_TILE, d) if d % t == 0 and n_total % (n_exp * (d // t)) == 0)
    row_tiles = d // tr
    steps = n_exp * row_tiles
    tm = n_total // steps
    rows = EXPERT_ROWS
    assert rows & (rows - 1) == 0
    half = pl.BlockSpec((None, tr, ff2 // 2), lambda i, *_: (i // row_tiles, i % row_tiles, 0))
    grid_spec = pltpu.PrefetchScalarGridSpec(
        num_scalar_prefetch=4,
        grid=(steps,),
        in_specs=[pl.BlockSpec((tm * SUBLANES, LANES), lambda i, *_: (i, 0)),
                  pl.BlockSpec((None, tr, ff2), lambda i, *_: (i // row_tiles, i % row_tiles, 0))],
        out_specs=[pl.BlockSpec(memory_space=pl.ANY), half, half],
        scratch_shapes=[pltpu.VMEM((rows * SUBLANES, LANES), F32), pltpu.SemaphoreType.DMA((2,))],
    )
    return pl.pallas_call(
        functools.partial(_dispatch_kernel, tm=tm, n_total=n_total, rows=rows, n_experts=n_exp, n_blocks=n_blocks),
        grid_spec=grid_spec,
        out_shape=[jax.ShapeDtypeStruct((n_blocks * rows * SUBLANES, LANES), F32),
                   jax.ShapeDtypeStruct((n_exp, d, ff2 // 2), BF16), jax.ShapeDtypeStruct((n_exp, d, ff2 // 2), BF16)],
        compiler_params=_cparams("arbitrary"),
        name="dispatch_split_w1",
    )(sorted_rows, zlo, zhi, n_used, hn, w1)


def _expert_kernel(be_ref, nu_ref, xs_ref, w1g_ref, w1l_ref, b1g_ref, b1l_ref, w2_ref, b2_ref, y_ref, *, rows):
    j = pl.program_id(0)
    n_used = nu_ref[0]

    @pl.when(j < n_used)
    def _():
        x = jnp.concatenate([xs_ref[pl.ds(s, rows, stride=SUBLANES), :] for s in range(SUBLANES)],
                            axis=-1).astype(BF16)
        glu = jnp.minimum(_dot(x, w1g_ref[...]) + b1g_ref[...], SWIGLU_LIMIT)
        lin = jnp.clip(_dot(x, w1l_ref[...]) + b1l_ref[...], -SWIGLU_LIMIT, SWIGLU_LIMIT)
        act = glu * jax.nn.sigmoid(SWIGLU_ALPHA * glu) * (lin + 1.0)
        y = _dot(act.astype(BF16), w2_ref[...].astype(BF16)) + b2_ref[...]
        for s in range(SUBLANES):
            y_ref[pl.ds(s, rows, stride=SUBLANES), :] = y[:, s * LANES:(s + 1) * LANES]

    @pl.when(j >= n_used)
    def _():
        y_ref[...] = jnp.zeros_like(y_ref)


def _experts(block_e, n_used, xs, w1g, w1l, b1g, b1l, w2, b2, n_blocks):
    rows = EXPERT_ROWS
    tile = rows * SUBLANES
    wspec = lambda a: pl.BlockSpec((None,) + a.shape[1:], lambda j, be, nu: (be[j], 0, 0))
    grid_spec = pltpu.PrefetchScalarGridSpec(
        num_scalar_prefetch=2,
        grid=(n_blocks,),
        in_specs=[pl.BlockSpec((tile, LANES), lambda j, be, nu: (jnp.minimum(j, nu[0] - 1), 0)),
                  wspec(w1g), wspec(w1l), wspec(b1g), wspec(b1l), wspec(w2), wspec(b2)],
        out_specs=pl.BlockSpec((tile, LANES), lambda j, be, nu: (j, 0)),
    )
    return pl.pallas_call(
        functools.partial(_expert_kernel, rows=rows),
        grid_spec=grid_spec,
        out_shape=jax.ShapeDtypeStruct((n_blocks * tile, LANES), F32),
        compiler_params=_cparams("arbitrary"),
        name="experts",
    )(block_e, n_used, xs, w1g, w1l, b1g, b1l, w2, b2)


def _combine_kernel(row_ref, yb_ref, gate_ref, h_ref, g_ref, yp_ref, ys_ref, cbuf, sem, *, tm, n_total,
                    prompt_tiles):
    i = pl.program_id(0)
    n = pl.num_programs(0)
    slot = i % 2
    tile = tm * SUBLANES

    def gather(blk, sl):
        def body(t2, carry):
            for t in (2 * t2, 2 * t2 + 1):
                for k in range(TOP_K):
                    src = pl.multiple_of(row_ref[k * n_total + blk * tm + t] * SUBLANES, SUBLANES)
                    dst = pl.multiple_of((k * tm + t) * SUBLANES, SUBLANES)
                    pltpu.make_async_copy(yb_ref.at[pl.ds(src, SUBLANES), :], cbuf.at[sl, pl.ds(dst, SUBLANES), :],
                                          sem.at[sl]).start()
            return carry
        lax.fori_loop(0, tm // 2, body, 0)

    @pl.when(i == 0)
    def _():
        gather(0, 0)

    @pl.when(i + 1 < n)
    def _():
        gather(i + 1, 1 - slot)

    pltpu.make_async_copy(yb_ref.at[pl.ds(0, TOP_K * tile), :], cbuf.at[slot], sem.at[slot]).wait()
    gates = gate_ref[...].T
    slabs = []
    for s in range(SUBLANES):
        acc = None
        for k in range(TOP_K):
            part = gates[:, k:k + 1] * cbuf[slot, pl.ds(k * tile + s, tm, stride=SUBLANES), :]
            acc = part if acc is None else acc + part
        slabs.append(acc)
    hf = h_ref[...] + jnp.concatenate(slabs, axis=-1)
    y = _rms(hf, g_ref[...])

    @pl.when(i < prompt_tiles)
    def _():
        yp_ref[...] = y

    @pl.when(i >= prompt_tiles)
    def _():
        ys_ref[...] = y


def _combine(rows, yb, gates, h, gf, np_):
    n, d = h.shape
    ns_ = n - np_
    tm = min(COMBINE_TILE, math.gcd(np_, ns_))
    prompt_tiles = np_ // tm
    row = lambda w: pl.BlockSpec((tm, w), lambda i, *_: (i, 0))
    grid_spec = pltpu.PrefetchScalarGridSpec(
        num_scalar_prefetch=1,
        grid=(n // tm,),
        in_specs=[pl.BlockSpec(memory_space=pl.ANY), pl.BlockSpec((gates.shape[0], tm), lambda i, *_: (0, i)), row(d),
                  pl.BlockSpec(gf.shape, lambda i, *_: (0, 0))],
        out_specs=[pl.BlockSpec((tm, d), lambda i, *_: (jnp.minimum(i, prompt_tiles - 1), 0)),
                   pl.BlockSpec((tm, d), lambda i, *_: (jnp.maximum(i - prompt_tiles, 0), 0))],
        scratch_shapes=[pltpu.VMEM((2, TOP_K * tm * SUBLANES, LANES), F32), pltpu.SemaphoreType.DMA((2,))],
    )
    return pl.pallas_call(
        functools.partial(_combine_kernel, tm=tm, n_total=n, prompt_tiles=prompt_tiles),
        grid_spec=grid_spec,
        out_shape=[jax.ShapeDtypeStruct((np_, d), F32), jax.ShapeDtypeStruct((ns_, d), F32)],
        compiler_params=_cparams("arbitrary"),
        name="combine_norm",
    )(rows, yb, gates, h, gf)


def kernel(x_prompt, x_sample, state_pool, cache_k, cache_v, page_table, norm1_g, w_in, pool_w, pool_scale,
           lambda_q1, lambda_k1, lambda_q2, lambda_k2, subln_g, w_out, norm2_g, router_w, router_b,
           w1, b1, w2, b2, normf_g):
    batch, seq, d = x_prompt.shape
    bs, n_new, _ = x_sample.shape
    depth = w_in.shape[0]
    assert depth == 1, "single-layer trunk"
    n_heads = cache_k.shape[3]
    page_size = cache_k.shape[2]
    pw = pool_w.shape[1] * pool_w.shape[2]
    qk_w = n_heads * HEAD_W
    n_experts = router_w.shape[2]
    np_, ns_ = batch * seq, bs * n_new
    n_total = np_ + ns_
    lam_init = 0.8 - 0.6 * math.exp(-0.3 * 0)

    g1 = norm1_g[0][None]
    w_in_b = w_in[0].astype(BF16)
    wvt_b = w_in_b[:, pw + 2 * qk_w:].T
    pool_w_b = pool_w[0].astype(BF16)
    ps = pool_scale[0][None]
    lams = [a[0][None] for a in (lambda_q1, lambda_k1, lambda_q2, lambda_k2)]
    sg = subln_g[0][None]
    slopes = jnp.exp2(-8.0 * (jnp.arange(n_heads, dtype=F32) + 1.0) / n_heads)
    w_out_b = w_out[0].astype(BF16)
    g2 = norm2_g[0][None]
    rwt_b = router_w[0].T.astype(BF16)
    rb = router_b[0][:, None]
    b1_pairs = b1[0].T.reshape(-1, 2, n_experts)
    b1g = b1_pairs[:, 0, :].T[:, None, :]
    b1l = b1_pairs[:, 1, :].T[:, None, :]
    w2b = w2.reshape(w2.shape[1:])
    b2e = b2[0][:, None, :]
    gf = normf_g[None]

    xp = x_prompt.reshape(np_, d)
    tq = min(ATTN_TILE, seq)
    pool_p, tail_p, q_p, k_p, v_p, kb_p, vt_p = _inproj(xp, g1, w_in_b, wvt_b, pw, qk_w,
                                                        prompt=(pool_w_b, ps, seq, tq))
    o_p = _attn_prompt(q_p, kb_p, vt_p, slopes, lams, sg, batch, seq, n_heads, lam_init, tq)

    xs = x_sample.reshape(ns_, d)
    u_s, q_s, k_s, v_s = _inproj(xs, g1, w_in_b, wvt_b, pw, qk_w)
    pool_s, new_pool_s = _pool_sample(jnp.transpose(state_pool[0], (1, 0, 2)),
                                      jnp.transpose(u_s.reshape(bs, n_new, pw), (1, 0, 2)), pool_w_b, ps)
    pool_s = jnp.transpose(pool_s, (1, 0, 2))
    new_pool_s = jnp.transpose(new_pool_s, (1, 0, 2))
    kc = cache_k.reshape(cache_k.shape[1], page_size * n_heads, HEAD_W)
    vc = cache_v.reshape(cache_v.shape[1], page_size * n_heads, HEAD_W)
    o_s = _attn_sample(q_s.reshape(bs, n_new, qk_w), k_s.reshape(bs, n_new, qk_w), v_s.reshape(bs, n_new, qk_w),
                       kc, vc, page_table, slopes, lams, sg, n_heads, lam_init)

    h, hn, code, gates, cnt = _outproj((pool_p, o_p, xp), (pool_s.reshape(ns_, pw), o_s.reshape(ns_, qk_w), xs),
                                       w_out_b, g2, rwt_b, rb)

    counts = cnt[:, 0].astype(I32)
    n_blocks = -(-n_total * TOP_K // EXPERT_ROWS) + n_experts
    padded = (counts + EXPERT_ROWS - 1) // EXPERT_ROWS * EXPERT_ROWS
    pad_ends = jnp.cumsum(padded)
    pad_starts = pad_ends - padded
    n_used = (pad_ends[-1:] // EXPERT_ROWS).astype(I32)
    block_start = jnp.arange(n_blocks, dtype=I32) * EXPERT_ROWS
    block_e = jnp.minimum(jnp.sum((pad_ends[None, :] <= block_start[:, None]).astype(I32), axis=1), n_experts - 1)

    sorted_rows = _sorted_rows(pad_starts, code).reshape(-1)
    xs_rows, w1g, w1l = _dispatch(sorted_rows, pad_starts + counts, pad_ends, n_used, hn, w1.reshape(w1.shape[1:]),
                                  n_blocks)
    yb = _experts(block_e, n_used, xs_rows, w1g, w1l, b1g, b1l, w2b, b2e, n_blocks)
    y_p, y_s = _combine(sorted_rows, yb, gates, h, gf, np_)

    return (y_p.reshape(batch, seq, d), y_s.reshape(bs, n_new, d),
            tail_p[None, :, POOL_HALO - state_pool.shape[2]:],
            new_pool_s[None],
            k_p.reshape(1, batch, seq, n_heads, HEAD_W), v_p.reshape(1, batch, seq, n_heads, HEAD_W),
            k_s.reshape(1, bs, n_new, n_heads, HEAD_W), v_s.reshape(1, bs, n_new, n_heads, HEAD_W))
```

```python
import functools
import math

import jax
import jax.numpy as jnp
from jax import lax
from jax.experimental import pallas as pl
from jax.experimental.pallas import tpu as pltpu

F32 = jnp.float32
BF16 = jnp.bfloat16
I32 = jnp.int32

RMS_EPS = 1e-6
NEG_INF = -1e30
HEAD_DIM = 64
HEAD_W = 2 * HEAD_DIM
POOL_WINDOWS = (2, 4, 8, 16)
POOL_HALO = 16
TOP_K = 4
RANK_BITS = 20
SWIGLU_LIMIT = 7.0
SWIGLU_ALPHA = 1.702
LANES = 128
SUBLANES = 8
ROW_TILE = 512
INPROJ_TILE = 1024
ATTN_TILE = 512
SAMPLE_HEADS = 2
KEY_TILES = 1
ATTN_HEADS = 4
LOG2E = 1.4426950408889634
N_SPLIT = 3
POS_LOW = 16
BF16_ROWS = 16
EXPERT_ROWS = 512
COMBINE_TILE = 512
VMEM_LIMIT = 48 * 1024 * 1024


def _cparams(*sem):
    return pltpu.CompilerParams(dimension_semantics=sem, vmem_limit_bytes=VMEM_LIMIT)


def _dot(a, b):
    return jnp.dot(a, b, preferred_element_type=F32)


def _dot_nt(a, b):
    return lax.dot_general(a, b, (((1,), (1,)), ((), ())), preferred_element_type=F32)


def _rms(x, g):
    return x * lax.rsqrt(jnp.mean(x * x, axis=-1, keepdims=True) + RMS_EPS) * g


def _inproj_sample_kernel(x_ref, g_ref, w_ref, wvt_ref, u_ref, q_ref, k_ref, v_ref, *, pool_w, qk_w):
    del wvt_ref
    u = _inproj_tile(x_ref, g_ref, w_ref, q_ref, k_ref, v_ref, pool_w=pool_w, qk_w=qk_w)[0]
    u_ref[...] = u


def _inproj_prompt_kernel(x_ref, g_ref, w_ref, wvt_ref, pw_ref, ps_ref, pool_ref, tail_ref, q_ref, k_ref, v_ref, kb_ref,
                          vt_ref, carry_ref, *, pool_w, qk_w, key_tile, seq_tiles):
    u, k, xb = _inproj_tile(x_ref, g_ref, w_ref, q_ref, k_ref, v_ref, pool_w=pool_w, qk_w=qk_w)
    _pool_tile(u, pl.program_id(0) % seq_tiles, pw_ref, ps_ref, pool_ref, carry_ref)
    tail_ref[...] = u[u.shape[0] - POOL_HALO:, :]
    kb_ref[...] = k.astype(BF16)
    vt = _dot_nt(wvt_ref[...], xb)
    for c in range(vt_ref.shape[0]):
        vt_ref[c] = vt[:, c * key_tile:(c + 1) * key_tile].astype(BF16)


def _inproj_tile(x_ref, g_ref, w_ref, q_ref, k_ref, v_ref, *, pool_w, qk_w):
    xb = _rms(x_ref[...], g_ref[...]).astype(BF16)
    c0, c1, c2 = pool_w, pool_w + qk_w, pool_w + 2 * qk_w
    u = _dot(xb, w_ref[:, 0:c0])
    q_ref[...] = _dot(xb, w_ref[:, c0:c1]) * (HEAD_DIM ** -0.5)
    k = _dot(xb, w_ref[:, c1:c2])
    v = _dot(xb, w_ref[:, c2:])
    n_heads = qk_w // HEAD_W
    tm = k.shape[0]
    for h in range(n_heads):
        k_ref[pl.ds(h, tm, stride=n_heads), :] = k[:, h * HEAD_W:(h + 1) * HEAD_W]
        v_ref[pl.ds(h, tm, stride=n_heads), :] = v[:, h * HEAD_W:(h + 1) * HEAD_W]
    return u, k, xb


def _inproj(x, g1, w_in_b, wvt_b, pool_w, qk_w, prompt=None):
    n, d = x.shape
    val_w = wvt_b.shape[0]
    tm = min(INPROJ_TILE, n if prompt is None else prompt[2])
    row = lambda w: pl.BlockSpec((tm, w), lambda i: (i, 0))
    full = lambda a: pl.BlockSpec(a.shape, lambda i: (0,) * a.ndim)
    n_heads = qk_w // HEAD_W
    assert val_w == qk_w and n % tm == 0
    heads = pl.BlockSpec((tm * n_heads, HEAD_W), lambda i: (i, 0))
    qkv_specs = [row(qk_w), heads, heads]
    qkv_shape = [jax.ShapeDtypeStruct((n, qk_w), F32), jax.ShapeDtypeStruct((n * n_heads, HEAD_W), F32),
                 jax.ShapeDtypeStruct((n * n_heads, HEAD_W), F32)]
    if prompt is None:
        return pl.pallas_call(
            functools.partial(_inproj_sample_kernel, pool_w=pool_w, qk_w=qk_w),
            grid=(n // tm,),
            in_specs=[row(d), full(g1), full(w_in_b), full(wvt_b)],
            out_specs=[row(pool_w)] + qkv_specs,
            out_shape=[jax.ShapeDtypeStruct((n, pool_w), F32)] + qkv_shape,
            compiler_params=_cparams("parallel"),
            name="inproj_sample",
        )(x, g1, w_in_b, wvt_b)
    pool_w_b, pool_scale, seq, key_tile = prompt
    assert tm % key_tile == 0 and seq % tm == 0 and tm >= POOL_HALO
    seq_tiles = seq // tm
    return pl.pallas_call(
        functools.partial(_inproj_prompt_kernel, pool_w=pool_w, qk_w=qk_w, key_tile=key_tile, seq_tiles=seq_tiles),
        grid=(n // tm,),
        in_specs=[row(d), full(g1), full(w_in_b), full(wvt_b), full(pool_w_b), full(pool_scale)],
        out_specs=[row(pool_w), pl.BlockSpec((None, POOL_HALO, pool_w), lambda i: (i // seq_tiles, 0, 0))] + qkv_specs
        + [row(qk_w), pl.BlockSpec((tm // key_tile, val_w, key_tile), lambda i: (i, 0, 0))],
        out_shape=[jax.ShapeDtypeStruct((n, pool_w), BF16), jax.ShapeDtypeStruct((n // seq, POOL_HALO, pool_w), F32)]
        + qkv_shape + [jax.ShapeDtypeStruct((n, qk_w), BF16),
                       jax.ShapeDtypeStruct((n // key_tile, val_w, key_tile), BF16)],
        scratch_shapes=[pltpu.VMEM((POOL_HALO, pool_w), F32)],
        compiler_params=_cparams("arbitrary"),
        name="inproj_pool_prompt",
    )(x, g1, w_in_b, wvt_b, pool_w_b, pool_scale)


def _pool_group(s_win, tok, cnt, pw, ps):
    d = s_win / cnt - tok
    return _dot(d.astype(BF16), pw) * ps


def _pool_tile(cur, j, pw_ref, ps_ref, o_ref, carry_ref):
    tt = cur.shape[0]

    @pl.when(j == 0)
    def _():
        carry_ref[...] = jnp.zeros_like(carry_ref)

    ext = jnp.concatenate([carry_ref[...], cur], axis=0)
    carry_ref[...] = cur[tt - POOL_HALO:, :]
    pos = (j * tt + lax.broadcasted_iota(I32, (tt, 1), 0)).astype(F32)
    gw = cur.shape[1] // len(POOL_WINDOWS)
    for g, w in enumerate(POOL_WINDOWS):
        cols = slice(g * gw, (g + 1) * gw)
        e = ext[:, cols]
        s, span = e, 1
        while span < w:
            s = s + pltpu.roll(s, span, axis=0)
            span *= 2
        cnt = jnp.minimum(float(w), pos + 1.0)
        out = _pool_group(s[POOL_HALO:, :], e[POOL_HALO:, :], cnt, pw_ref[g], ps_ref[:, cols])
        o_ref[:, cols] = out.astype(o_ref.dtype)


def _pool_sample_kernel(sp_ref, u_ref, pw_ref, ps_ref, o_ref, np_ref, *, n_new, n_buf):
    rows = [sp_ref[r] for r in range(n_buf)] + [u_ref[r] for r in range(n_new)]
    for r in range(n_buf):
        np_ref[r] = rows[n_new + r]
    gw = rows[0].shape[1] // len(POOL_WINDOWS)
    for t in range(n_new):
        i = n_buf + t
        for g, w in enumerate(POOL_WINDOWS):
            cols = slice(g * gw, (g + 1) * gw)
            s = rows[i][:, cols]
            for back in range(1, w):
                s = s + rows[i - back][:, cols]
            out = _pool_group(s, rows[i][:, cols], float(w), pw_ref[g], ps_ref[:, cols])
            o_ref[t, :, cols] = out


def _pool_sample(state_pool_t, u_t, pool_w_b, pool_scale):
    n_buf, bs, pw = state_pool_t.shape
    n_new = u_t.shape[0]
    full = lambda a: pl.BlockSpec(a.shape, lambda i: (0,) * a.ndim)
    return pl.pallas_call(
        functools.partial(_pool_sample_kernel, n_new=n_new, n_buf=n_buf),
        grid=(1,),
        in_specs=[full(state_pool_t), full(u_t), full(pool_w_b), full(pool_scale)],
        out_specs=[pl.BlockSpec((n_new, bs, pw), lambda i: (0, 0, 0)),
                   pl.BlockSpec((n_buf, bs, pw), lambda i: (0, 0, 0))],
        out_shape=[jax.ShapeDtypeStruct((n_new, bs, pw), F32),
                   jax.ShapeDtypeStruct((n_buf, bs, pw), F32)],
        compiler_params=_cparams("arbitrary"),
        name="pool_sample",
    )(state_pool_t, u_t, pool_w_b, pool_scale)


def _lambda_full(lq1, lk1, lq2, lk2, lam_init):
    e1 = jnp.exp(jnp.sum(lq1 * lk1, axis=-1, keepdims=True))
    e2 = jnp.exp(jnp.sum(lq2 * lk2, axis=-1, keepdims=True))
    return e1 - e2 + lam_init


def _split_halves(q):
    lane = lax.broadcasted_iota(I32, q.shape, 1)
    zero = jnp.zeros_like(q)
    return jnp.concatenate([jnp.where(lane < HEAD_DIM, q, zero), jnp.where(lane >= HEAD_DIM, q, zero)], axis=0)


def _subln(o, g, lam_init):
    return _rms(o, g) * (1.0 - lam_init)


def _attn_prompt_kernel(slopes_ref, q_ref, k_ref, vt_ref, lq1_ref, lk1_ref, lq2_ref, lk2_ref, g_ref, o_ref,
                        *, tq, lam_init, heads):
    hg = pl.program_id(1)
    i = pl.program_id(2)
    lam = _lambda_full(lq1_ref[...], lk1_ref[...], lq2_ref[...], lk2_ref[...], lam_init)
    head_cols = [slice(hh * HEAD_W, (hh + 1) * HEAD_W) for hh in range(heads)]
    row = lax.broadcasted_iota(I32, (KEY_TILES * tq, LANES), 0)
    lane = lax.broadcasted_iota(I32, (KEY_TILES * tq, LANES), 1)
    pos_feat = jnp.where(lane < 2 * N_SPLIT, jnp.where(lane % 2 == 0, row & ~(POS_LOW - 1), row & (POS_LOW - 1)), 0)
    pos_feat = pos_feat.astype(F32).astype(BF16)
    qlane = lax.broadcasted_iota(I32, (2 * tq, LANES), 1)
    slopes2, qqs = [], []
    for hh in range(heads):
        s2 = slopes_ref[hg * heads + hh] * LOG2E
        slopes2.append(s2)
        rest = jnp.full((2 * tq, LANES), s2, F32)
        feat = jnp.zeros((2 * tq, LANES), F32)
        for part in range(N_SPLIT):
            piece = rest.astype(BF16).astype(F32)
            rest = rest - piece
            feat = jnp.where(qlane // 2 == part, piece, feat)
        qq = _split_halves((q_ref[:, head_cols[hh]] * LOG2E).astype(BF16))
        qqs.append(jnp.concatenate([qq, feat.astype(BF16)], axis=1))
    def head_step(hh, j, tiles, carry, diag_tile):
        m, acc = carry
        n_keys = tiles * tq
        start = pl.multiple_of(j * tq, tq)
        keys = jnp.concatenate([k_ref[pl.ds(start, n_keys), head_cols[hh]], pos_feat[:n_keys]], axis=1)
        st = _dot_nt(keys, qqs[hh])
        if diag_tile is not None:
            kr = lax.broadcasted_iota(I32, st.shape, 0) - diag_tile * tq
            qc = lax.broadcasted_iota(I32, st.shape, 1)
            qc = jnp.where(qc >= tq, qc - tq, qc)
            st = jnp.where(qc >= kr, st, NEG_INF)
        offset = slopes2[hh] * ((j - i) * tq).astype(F32)
        m_new = jnp.maximum(m, jnp.max(st, axis=0, keepdims=True) + offset)
        alpha = jnp.exp2(m - m_new)
        p = jnp.exp2(st - (m_new - offset)).astype(BF16)
        values = jnp.concatenate([vt_ref[j + t, head_cols[hh], :] for t in range(tiles)], axis=1)
        values = jnp.concatenate([values, jnp.ones((BF16_ROWS, n_keys), BF16)], axis=0)
        acc = alpha * acc + _dot(values, p)
        return m_new, acc

    def step(j, tiles, carries, diag_tile):
        return tuple(head_step(hh, j, tiles, carries[hh], diag_tile) for hh in range(heads))

    init = (jnp.full((1, 2 * tq), NEG_INF, F32), jnp.zeros((HEAD_W + BF16_ROWS, 2 * tq), F32))
    carries = lax.fori_loop(0, i // KEY_TILES, lambda g, cr: step(g * KEY_TILES, KEY_TILES, cr, None),
                            (init,) * heads)
    rest = i % KEY_TILES
    carries = lax.switch(rest, [functools.partial(step, i - r, r + 1, diag_tile=r) for r in range(KEY_TILES)],
                         carries)
    for hh, (_, acc) in enumerate(carries):
        o = acc[:HEAD_W] / acc[HEAD_W:HEAD_W + 1]
        o = (o[:, :tq] - lam * o[:, tq:]).T
        o_ref[:, head_cols[hh]] = _subln(o, g_ref[...], lam_init).astype(o_ref.dtype)


def _attn_prompt(q, kb, vt, slopes, lams, subln_g, batch, seq, n_heads, lam_init, tq):
    heads = math.gcd(n_heads, ATTN_HEADS)
    width = heads * HEAD_W
    q3, k3 = (a.reshape(batch, seq, n_heads * HEAD_W) for a in (q, kb))
    small = lambda a: pl.BlockSpec(a.shape, lambda b, h, i, s: (0,) * a.ndim)
    grid_spec = pltpu.PrefetchScalarGridSpec(
        num_scalar_prefetch=1,
        grid=(batch, n_heads // heads, seq // tq),
        in_specs=[pl.BlockSpec((None, tq, width), lambda b, h, i, s: (b, i, h)),
                  pl.BlockSpec((None, seq, width), lambda b, h, i, s: (b, 0, h)),
                  pl.BlockSpec((seq // tq, width, tq), lambda b, h, i, s: (b, h, 0)),
                  *[small(a) for a in lams], small(subln_g)],
        out_specs=pl.BlockSpec((None, tq, width), lambda b, h, i, s: (b, i, h)),
    )
    out = pl.pallas_call(
        functools.partial(_attn_prompt_kernel, tq=tq, lam_init=lam_init, heads=heads),
        grid_spec=grid_spec,
        out_shape=jax.ShapeDtypeStruct((batch, seq, n_heads * HEAD_W), BF16),
        compiler_params=_cparams("parallel", "parallel", "arbitrary"),
        name="attn_prompt",
    )(slopes, q3, k3, vt, *lams, subln_g)
    return out.reshape(batch * seq, n_heads * HEAD_W)


def _attn_sample_kernel(pt_ref, slopes_ref, q_ref, kn_ref, vn_ref, kc_ref, vc_ref,
                        lq1_ref, lk1_ref, lq2_ref, lk2_ref, g_ref, o_ref, kbuf, vbuf, sem,
                        *, n_pages, page_rows, n_heads, n_new, lam_init):
    b = pl.program_id(0)
    nb = pl.num_programs(0)
    slot = b % 2
    past = n_pages * page_rows // n_heads

    def page_copies(seq, sl, pg):
        p = pt_ref[seq * n_pages + pg]
        dst = pl.ds(pg * page_rows, page_rows)
        return (pltpu.make_async_copy(kc_ref.at[p], kbuf.at[sl, dst, :], sem.at[sl, 0]),
                pltpu.make_async_copy(vc_ref.at[p], vbuf.at[sl, dst, :], sem.at[sl, 1]))

    def start_seq(seq, sl):
        for pg in range(n_pages):
            for cp in page_copies(seq, sl, pg):
                cp.start()

    @pl.when(b == 0)
    def _():
        start_seq(0, 0)

    @pl.when(b + 1 < nb)
    def _():
        start_seq(b + 1, 1 - slot)

    for pg in range(n_pages):
        for cp in page_copies(b, slot, pg):
            cp.wait()

    lam = _lambda_full(lq1_ref[...], lk1_ref[...], lq2_ref[...], lk2_ref[...], lam_init)
    grp = 2 * SUBLANES
    rows = SAMPLE_HEADS * grp
    r = lax.broadcasted_iota(I32, (rows, past), 0) % grp
    t_past = jnp.where(r >= n_new, r - n_new, r)
    dist_past = (past + t_past - lax.broadcasted_iota(I32, (rows, past), 1)).astype(F32)
    r1 = lax.broadcasted_iota(I32, (rows, 1), 0)
    t_new = jnp.where(r1 % grp >= n_new, r1 % grp - n_new, r1 % grp)
    pad = jnp.zeros((grp - 2 * n_new, HEAD_W), F32)
    for h0 in range(0, n_heads, SAMPLE_HEADS):
        heads = range(h0, h0 + SAMPLE_HEADS)
        cols = slice(h0 * HEAD_W, (h0 + SAMPLE_HEADS) * HEAD_W)
        slope = jnp.concatenate([jnp.full((grp, 1), slopes_ref[h], F32) for h in heads], axis=0)
        blocks = []
        for i, h in enumerate(heads):
            qh = jnp.concatenate([_split_halves(q_ref[:, h * HEAD_W:(h + 1) * HEAD_W]), pad], axis=0)
            zero = jnp.zeros_like(qh)
            blocks.append(jnp.concatenate([qh if j == i else zero for j in range(SAMPLE_HEADS)], axis=1))
        qq = jnp.concatenate(blocks, axis=0)
        kh = jnp.concatenate([kbuf[slot, pl.ds(h, past, stride=n_heads), :] for h in heads], axis=1).astype(BF16)
        vh = jnp.concatenate([vbuf[slot, pl.ds(h, past, stride=n_heads), :] for h in heads], axis=1).astype(BF16)
        s = _dot_nt(qq.astype(BF16), kh) - slope * dist_past
        kn = kn_ref[:, cols]
        vn = vn_ref[:, cols]
        s_new = []
        for c in range(n_new):
            sc = jnp.sum(qq * kn[c:c + 1, :], axis=-1, keepdims=True) - slope * (t_new - c).astype(F32)
            s_new.append(jnp.where(t_new >= c, sc, NEG_INF))
        m = jnp.max(s, axis=-1, keepdims=True)
        for sc in s_new:
            m = jnp.maximum(m, sc)
        p = jnp.exp(s - m)
        l = jnp.sum(p, axis=-1, keepdims=True)
        acc = _dot(p.astype(BF16), vh)
        for c, sc in enumerate(s_new):
            pc = jnp.exp(sc - m)
            l = l + pc
            acc = acc + pc * vn[c:c + 1, :]
        o = acc / l
        for i, h in enumerate(heads):
            oh = o[i * grp:(i + 1) * grp, i * HEAD_W:(i + 1) * HEAD_W]
            oh = oh[0:n_new] - lam * oh[n_new:2 * n_new]
            o_ref[:, h * HEAD_W:(h + 1) * HEAD_W] = _subln(oh, g_ref[...], lam_init)


def _attn_sample(q3, kn3, vn3, kc, vc, page_table, slopes, lams, subln_g, n_heads, lam_init):
    bs, n_new, qk_w = q3.shape
    n_pages = page_table.shape[1]
    page_rows = kc.shape[1]
    small = lambda a: pl.BlockSpec(a.shape, lambda b, pt, s: (0,) * a.ndim)
    tok = pl.BlockSpec((None, n_new, qk_w), lambda b, pt, s: (b, 0, 0))
    grid_spec = pltpu.PrefetchScalarGridSpec(
        num_scalar_prefetch=2,
        grid=(bs,),
        in_specs=[tok, tok, tok, pl.BlockSpec(memory_space=pl.ANY), pl.BlockSpec(memory_space=pl.ANY),
                  *[small(a) for a in lams], small(subln_g)],
        out_specs=tok,
        scratch_shapes=[pltpu.VMEM((2, n_pages * page_rows, LANES), F32),
                        pltpu.VMEM((2, n_pages * page_rows, LANES), F32),
                        pltpu.SemaphoreType.DMA((2, 2))],
    )
    return pl.pallas_call(
        functools.partial(_attn_sample_kernel, n_pages=n_pages, page_rows=page_rows, n_heads=n_heads,
                          n_new=n_new, lam_init=lam_init),
        grid_spec=grid_spec,
        out_shape=jax.ShapeDtypeStruct((bs, n_new, qk_w), F32),
        compiler_params=_cparams("arbitrary"),
        name="attn_sample",
    )(page_table.reshape(-1), slopes, q3, kn3, vn3, kc, vc, *lams, subln_g)


def _outproj_kernel(pool_p_ref, o_p_ref, x_p_ref, pool_s_ref, o_s_ref, x_s_ref, wo_ref, g2_ref, rwt_ref, rb_ref,
                    earlier_ref, h_ref, hn_ref, code_ref, gate_ref, cnt_out_ref, run_ref, *, tm, n_experts,
                    prompt_tiles):
    i = pl.program_id(0)

    @pl.when(i == 0)
    def _():
        run_ref[...] = jnp.zeros_like(run_ref)

    tile = functools.partial(_outproj_tile, wo_ref=wo_ref, g2_ref=g2_ref, rwt_ref=rwt_ref, rb_ref=rb_ref,
                             earlier_ref=earlier_ref, h_ref=h_ref, hn_ref=hn_ref, code_ref=code_ref, gate_ref=gate_ref,
                             cnt_out_ref=cnt_out_ref, run_ref=run_ref, tm=tm, n_experts=n_experts)
    pl.when(i < prompt_tiles)(functools.partial(tile, pool_p_ref, o_p_ref, x_p_ref))
    pl.when(i >= prompt_tiles)(functools.partial(tile, pool_s_ref, o_s_ref, x_s_ref))


def _outproj_tile(pool_ref, o_ref, x_ref, *, wo_ref, g2_ref, rwt_ref, rb_ref, earlier_ref, h_ref, hn_ref, code_ref,
                  gate_ref, cnt_out_ref, run_ref, tm, n_experts):
    pw = pool_ref.shape[1]
    mix = _dot(pool_ref[...].astype(BF16), wo_ref[0:pw, :]) + _dot(o_ref[...].astype(BF16), wo_ref[pw:, :])
    h = x_ref[...] + mix
    h_ref[...] = h
    hn = _rms(h, g2_ref[...])
    for s in range(hn.shape[1] // LANES):
        hn_ref[pl.ds(s, tm, stride=SUBLANES), :] = hn[:, s * LANES:(s + 1) * LANES]
    logits = _dot_nt(rwt_ref[...], hn.astype(BF16)) + rb_ref[...]
    expert = lax.broadcasted_iota(I32, logits.shape, 0)
    vals, idxs = [], []
    for _ in range(TOP_K):
        m = jnp.max(logits, axis=0, keepdims=True)
        idx = jnp.min(jnp.where(logits == m, expert, n_experts), axis=0, keepdims=True)
        vals.append(m)
        idxs.append(idx)
        logits = jnp.where(expert == idx, -jnp.inf, logits)
    ex = [jnp.exp(v - vals[0]) for v in vals]
    den = ex[0]
    for e in ex[1:]:
        den = den + e
    chosen = jnp.zeros(logits.shape, F32)
    for idx in idxs:
        chosen = chosen + jnp.where(expert == idx, 1.0, 0.0)
    before = _dot(chosen.astype(BF16), earlier_ref[...]) + run_ref[...]
    for k in range(TOP_K):
        gate_ref[k:k + 1, :] = ex[k] / den
        rank = jnp.sum(jnp.where(expert == idxs[k], before, 0.0), axis=0, keepdims=True).astype(I32)
        code_ref[k:k + 1, :] = idxs[k] * (1 << RANK_BITS) + rank
    gate_ref[TOP_K:, :] = jnp.zeros((gate_ref.shape[0] - TOP_K, tm), F32)
    run_ref[...] = run_ref[...] + jnp.sum(chosen, axis=1, keepdims=True)
    cnt_out_ref[...] = run_ref[...]


def _outproj(prompt, sample, w_out_b, g2, rwt_b, rb):
    np_, d = prompt[2].shape
    ns_ = sample[2].shape[0]
    n_total = np_ + ns_
    n_experts = rwt_b.shape[0]
    tm = min(ROW_TILE, math.gcd(np_, ns_))
    prompt_tiles = np_ // tm
    assert d == SUBLANES * LANES and n_total < (1 << RANK_BITS)
    first = lambda a: pl.BlockSpec((tm, a.shape[1]), lambda i: (jnp.minimum(i, prompt_tiles - 1), 0))
    second = lambda a: pl.BlockSpec((tm, a.shape[1]), lambda i: (jnp.maximum(i - prompt_tiles, 0), 0))
    row = lambda w: pl.BlockSpec((tm, w), lambda i: (i, 0))
    col = lambda r: pl.BlockSpec((r, tm), lambda i: (0, i))
    full = lambda a: pl.BlockSpec(a.shape, lambda i: (0,) * a.ndim)
    counts = jax.ShapeDtypeStruct((n_experts, 1), F32)
    earlier = jnp.triu(jnp.ones((tm, tm), BF16), 1)
    return pl.pallas_call(
        functools.partial(_outproj_kernel, tm=tm, n_experts=n_experts, prompt_tiles=prompt_tiles),
        grid=(n_total // tm,),
        in_specs=[*[first(a) for a in prompt], *[second(a) for a in sample],
                  full(w_out_b), full(g2), full(rwt_b), full(rb), full(earlier)],
        out_specs=[row(d), pl.BlockSpec((tm * SUBLANES, LANES), lambda i: (i, 0)),
                   col(TOP_K), col(SUBLANES), pl.BlockSpec(counts.shape, lambda i: (0, 0))],
        out_shape=[jax.ShapeDtypeStruct((n_total, d), F32), jax.ShapeDtypeStruct((n_total * SUBLANES, LANES), F32),
                   jax.ShapeDtypeStruct((TOP_K, n_total), I32), jax.ShapeDtypeStruct((SUBLANES, n_total), F32),
                   counts],
        scratch_shapes=[pltpu.VMEM(counts.shape, F32)],
        compiler_params=_cparams("arbitrary"),
        name="outproj_router",
    )(*prompt, *sample, w_out_b, g2, rwt_b, rb, earlier)


def _sorted_rows_kernel(start_ref, code_ref, rows_ref, *, n_experts):
    code = code_ref[...]
    expert = code >> RANK_BITS
    rows = code & ((1 << RANK_BITS) - 1)
    for e in range(n_experts):
        rows = rows + jnp.where(expert == e, start_ref[e], 0)
    rows_ref[...] = rows


def _sorted_rows(starts, code):
    return pl.pallas_call(
        functools.partial(_sorted_rows_kernel, n_experts=starts.shape[0]),
        grid=(1,),
        in_specs=[pl.BlockSpec(memory_space=pltpu.SMEM), pl.BlockSpec(code.shape, lambda i: (0, 0))],
        out_specs=pl.BlockSpec(code.shape, lambda i: (0, 0)),
        out_shape=jax.ShapeDtypeStruct(code.shape, I32),
        compiler_params=_cparams("arbitrary"),
        name="sorted_rows",
    )(starts, code)


MXU_DIM = 256


def _split_w1_tile(w_ref, g_ref, l_ref):
    r = lax.broadcasted_iota(I32, (MXU_DIM, MXU_DIM), 0)
    c = lax.broadcasted_iota(I32, (MXU_DIM, MXU_DIM), 1)
    half = MXU_DIM // 2
    source = jnp.where(c < half, 2 * c, 2 * (c - half) + 1)
    perm = jnp.where(r == source, 1.0, 0.0).astype(BF16)
    for blk in range(w_ref.shape[1] // MXU_DIM):
        res = _dot(w_ref[:, blk * MXU_DIM:(blk + 1) * MXU_DIM].astype(BF16), perm)
        g_ref[:, blk * half:(blk + 1) * half] = res[:, :half].astype(BF16)
        l_ref[:, blk * half:(blk + 1) * half] = res[:, half:].astype(BF16)


def _dispatch_kernel(row_ref, zlo_ref, zhi_ref, nu_ref, hn_ref, w1_ref, xs_ref, w1g_ref, w1l_ref, zero_ref, sem,
                     *, tm, n_total, rows, n_experts, n_blocks):
    i = pl.program_id(0)
    tile = tm * SUBLANES
    block = rows * SUBLANES

    def pad_pieces(e):
        lo, n = zlo_ref[e], zhi_ref[e] - zlo_ref[e]
        for bit in range(rows.bit_length() - 1):
            size = 1 << bit
            below = n & (size - 1)
            start = pl.multiple_of((lo + below) * SUBLANES, SUBLANES)
            cp = pltpu.make_async_copy(zero_ref.at[pl.ds(0, size * SUBLANES), :],
                                       xs_ref.at[pl.ds(start, size * SUBLANES), :], sem.at[1])
            yield (n & size) != 0, cp

    def tail_copy(blk):
        start = pl.multiple_of(blk * block, block)
        return pltpu.make_async_copy(zero_ref, xs_ref.at[pl.ds(start, block), :], sem.at[1])

    def for_zero_copies(fn):
        def per_expert(e, carry):
            for used, cp in pad_pieces(e):
                pl.when(used)(functools.partial(fn, cp))
            return carry
        lax.fori_loop(0, n_experts, per_expert, 0)

        def per_block(blk, carry):
            fn(tail_copy(blk))
            return carry
        lax.fori_loop(nu_ref[0], n_blocks, per_block, 0)

    @pl.when(i == 0)
    def _():
        zero_ref[...] = jnp.zeros_like(zero_ref)
        for_zero_copies(lambda cp: cp.start())

    def body(t, carry):
        src = pl.multiple_of(t * SUBLANES, SUBLANES)
        for k in range(TOP_K):
            dst = pl.multiple_of(row_ref[k * n_total + i * tm + t] * SUBLANES, SUBLANES)
            pltpu.make_async_copy(hn_ref.at[pl.ds(src, SUBLANES), :], xs_ref.at[pl.ds(dst, SUBLANES), :],
                                  sem.at[0]).start()
        return carry
    lax.fori_loop(0, tm, body, 0)

    _split_w1_tile(w1_ref, w1g_ref, w1l_ref)

    for _ in range(TOP_K):
        pltpu.make_async_copy(hn_ref, xs_ref.at[pl.ds(0, tile), :], sem.at[0]).wait()

    @pl.when(i == pl.num_programs(0) - 1)
    def _():
        for_zero_copies(lambda cp: cp.wait())


def _dispatch(sorted_rows, zlo, zhi, n_used, hn, w1, n_blocks):
    n_total = hn.shape[0] // SUBLANES
    n_exp, d, ff2 = w1.shape
    tr = next(t for t in (ROW_TILE, 2 * ROW_TILE, d) if d % t == 0 and n_total % (n_exp * (d // t)) == 0)
    row_tiles = d // tr
    steps = n_exp * row_tiles
    tm = n_total // steps
    rows = EXPERT_ROWS
    assert rows & (rows - 1) == 0
    half = pl.BlockSpec((None, tr, ff2 // 2), lambda i, *_: (i // row_tiles, i % row_tiles, 0))
    grid_spec = pltpu.PrefetchScalarGridSpec(
        num_scalar_prefetch=4,
        grid=(steps,),
        in_specs=[pl.BlockSpec((tm * SUBLANES, LANES), lambda i, *_: (i, 0)),
                  pl.BlockSpec((None, tr, ff2), lambda i, *_: (i // row_tiles, i % row_tiles, 0))],
        out_specs=[pl.BlockSpec(memory_space=pl.ANY), half, half],
        scratch_shapes=[pltpu.VMEM((rows * SUBLANES, LANES), F32), pltpu.SemaphoreType.DMA((2,))],
    )
    return pl.pallas_call(
        functools.partial(_dispatch_kernel, tm=tm, n_total=n_total, rows=rows, n_experts=n_exp, n_blocks=n_blocks),
        grid_spec=grid_spec,
        out_shape=[jax.ShapeDtypeStruct((n_blocks * rows * SUBLANES, LANES), F32),
                   jax.ShapeDtypeStruct((n_exp, d, ff2 // 2), BF16), jax.ShapeDtypeStruct((n_exp, d, ff2 // 2), BF16)],
        compiler_params=_cparams("arbitrary"),
        name="dispatch_split_w1",
    )(sorted_rows, zlo, zhi, n_used, hn, w1)


def _expert_kernel(be_ref, nu_ref, xs_ref, w1g_ref, w1l_ref, b1g_ref, b1l_ref, w2_ref, b2_ref, y_ref, *, rows):
    j = pl.program_id(0)
    n_used = nu_ref[0]

    @pl.when(j < n_used)
    def _():
        x = jnp.concatenate([xs_ref[pl.ds(s, rows, stride=SUBLANES), :] for s in range(SUBLANES)],
                            axis=-1).astype(BF16)
        glu = jnp.minimum(_dot(x, w1g_ref[...]) + b1g_ref[...], SWIGLU_LIMIT)
        lin = jnp.clip(_dot(x, w1l_ref[...]) + b1l_ref[...], -SWIGLU_LIMIT, SWIGLU_LIMIT)
        act = glu * jax.nn.sigmoid(SWIGLU_ALPHA * glu) * (lin + 1.0)
        y = _dot(act.astype(BF16), w2_ref[...].astype(BF16)) + b2_ref[...]
        for s in range(SUBLANES):
            y_ref[pl.ds(s, rows, stride=SUBLANES), :] = y[:, s * LANES:(s + 1) * LANES]

    @pl.when(j >= n_used)
    def _():
        y_ref[...] = jnp.zeros_like(y_ref)


def _experts(block_e, n_used, xs, w1g, w1l, b1g, b1l, w2, b2, n_blocks):
    rows = EXPERT_ROWS
    tile = rows * SUBLANES
    wspec = lambda a: pl.BlockSpec((None,) + a.shape[1:], lambda j, be, nu: (be[j], 0, 0))
    grid_spec = pltpu.PrefetchScalarGridSpec(
        num_scalar_prefetch=2,
        grid=(n_blocks,),
        in_specs=[pl.BlockSpec((tile, LANES), lambda j, be, nu: (jnp.minimum(j, nu[0] - 1), 0)),
                  wspec(w1g), wspec(w1l), wspec(b1g), wspec(b1l), wspec(w2), wspec(b2)],
        out_specs=pl.BlockSpec((tile, LANES), lambda j, be, nu: (j, 0)),
    )
    return pl.pallas_call(
        functools.partial(_expert_kernel, rows=rows),
        grid_spec=grid_spec,
        out_shape=jax.ShapeDtypeStruct((n_blocks * tile, LANES), F32),
        compiler_params=_cparams("arbitrary"),
        name="experts",
    )(block_e, n_used, xs, w1g, w1l, b1g, b1l, w2, b2)


def _combine_kernel(row_ref, yb_ref, gate_ref, h_ref, g_ref, yp_ref, ys_ref, cbuf, sem, *, tm, n_total,
                    prompt_tiles):
    i = pl.program_id(0)
    n = pl.num_programs(0)
    slot = i % 2
    tile = tm * SUBLANES

    def gather(blk, sl):
        def body(t2, carry):
            for t in (2 * t2, 2 * t2 + 1):
                for k in range(TOP_K):
                    src = pl.multiple_of(row_ref[k * n_total + blk * tm + t] * SUBLANES, SUBLANES)
                    dst = pl.multiple_of((k * tm + t) * SUBLANES, SUBLANES)
                    pltpu.make_async_copy(yb_ref.at[pl.ds(src, SUBLANES), :], cbuf.at[sl, pl.ds(dst, SUBLANES), :],
                                          sem.at[sl]).start()
            return carry
        lax.fori_loop(0, tm // 2, body, 0)

    @pl.when(i == 0)
    def _():
        gather(0, 0)

    @pl.when(i + 1 < n)
    def _():
        gather(i + 1, 1 - slot)

    pltpu.make_async_copy(yb_ref.at[pl.ds(0, TOP_K * tile), :], cbuf.at[slot], sem.at[slot]).wait()
    gates = gate_ref[...].T
    slabs = []
    for s in range(SUBLANES):
        acc = None
        for k in range(TOP_K):
            part = gates[:, k:k + 1] * cbuf[slot, pl.ds(k * tile + s, tm, stride=SUBLANES), :]
            acc = part if acc is None else acc + part
        slabs.append(acc)
    hf = h_ref[...] + jnp.concatenate(slabs, axis=-1)
    y = _rms(hf, g_ref[...])

    @pl.when(i < prompt_tiles)
    def _():
        yp_ref[...] = y

    @pl.when(i >= prompt_tiles)
    def _():
        ys_ref[...] = y


def _combine(rows, yb, gates, h, gf, np_):
    n, d = h.shape
    ns_ = n - np_
    tm = min(COMBINE_TILE, math.gcd(np_, ns_))
    prompt_tiles = np_ // tm
    row = lambda w: pl.BlockSpec((tm, w), lambda i, *_: (i, 0))
    grid_spec = pltpu.PrefetchScalarGridSpec(
        num_scalar_prefetch=1,
        grid=(n // tm,),
        in_specs=[pl.BlockSpec(memory_space=pl.ANY), pl.BlockSpec((gates.shape[0], tm), lambda i, *_: (0, i)), row(d),
                  pl.BlockSpec(gf.shape, lambda i, *_: (0, 0))],
        out_specs=[pl.BlockSpec((tm, d), lambda i, *_: (jnp.minimum(i, prompt_tiles - 1), 0)),
                   pl.BlockSpec((tm, d), lambda i, *_: (jnp.maximum(i - prompt_tiles, 0), 0))],
        scratch_shapes=[pltpu.VMEM((2, TOP_K * tm * SUBLANES, LANES), F32), pltpu.SemaphoreType.DMA((2,))],
    )
    return pl.pallas_call(
        functools.partial(_combine_kernel, tm=tm, n_total=n, prompt_tiles=prompt_tiles),
        grid_spec=grid_spec,
        out_shape=[jax.ShapeDtypeStruct((np_, d), F32), jax.ShapeDtypeStruct((ns_, d), F32)],
        compiler_params=_cparams("arbitrary"),
        name="combine_norm",
    )(rows, yb, gates, h, gf)


def kernel(x_prompt, x_sample, state_pool, cache_k, cache_v, page_table, norm1_g, w_in, pool_w, pool_scale,
           lambda_q1, lambda_k1, lambda_q2, lambda_k2, subln_g, w_out, norm2_g, router_w, router_b,
           w1, b1, w2, b2, normf_g):
    batch, seq, d = x_prompt.shape
    bs, n_new, _ = x_sample.shape
    depth = w_in.shape[0]
    assert depth == 1, "single-layer trunk"
    n_heads = cache_k.shape[3]
    page_size = cache_k.shape[2]
    pw = pool_w.shape[1] * pool_w.shape[2]
    qk_w = n_heads * HEAD_W
    n_experts = router_w.shape[2]
    np_, ns_ = batch * seq, bs * n_new
    n_total = np_ + ns_
    lam_init = 0.8 - 0.6 * math.exp(-0.3 * 0)

    g1 = norm1_g[0][None]
    w_in_b = w_in[0].astype(BF16)
    wvt_b = w_in_b[:, pw + 2 * qk_w:].T
    pool_w_b = pool_w[0].astype(BF16)
    ps = pool_scale[0][None]
    lams = [a[0][None] for a in (lambda_q1, lambda_k1, lambda_q2, lambda_k2)]
    sg = subln_g[0][None]
    slopes = jnp.exp2(-8.0 * (jnp.arange(n_heads, dtype=F32) + 1.0) / n_heads)
    w_out_b = w_out[0].astype(BF16)
    g2 = norm2_g[0][None]
    rwt_b = router_w[0].T.astype(BF16)
    rb = router_b[0][:, None]
    b1_pairs = b1[0].T.reshape(-1, 2, n_experts)
    b1g = b1_pairs[:, 0, :].T[:, None, :]
    b1l = b1_pairs[:, 1, :].T[:, None, :]
    w2b = w2.reshape(w2.shape[1:])
    b2e = b2[0][:, None, :]
    gf = normf_g[None]

    xp = x_prompt.reshape(np_, d)
    tq = min(ATTN_TILE, seq)
    pool_p, tail_p, q_p, k_p, v_p, kb_p, vt_p = _inproj(xp, g1, w_in_b, wvt_b, pw, qk_w,
                                                        prompt=(pool_w_b, ps, seq, tq))
    o_p = _attn_prompt(q_p, kb_p, vt_p, slopes, lams, sg, batch, seq, n_heads, lam_init, tq)

    xs = x_sample.reshape(ns_, d)
    u_s, q_s, k_s, v_s = _inproj(xs, g1, w_in_b, wvt_b, pw, qk_w)
    pool_s, new_pool_s = _pool_sample(jnp.transpose(state_pool[0], (1, 0, 2)),
                                      jnp.transpose(u_s.reshape(bs, n_new, pw), (1, 0, 2)), pool_w_b, ps)
    pool_s = jnp.transpose(pool_s, (1, 0, 2))
    new_pool_s = jnp.transpose(new_pool_s, (1, 0, 2))
    kc = cache_k.reshape(cache_k.shape[1], page_size * n_heads, HEAD_W)
    vc = cache_v.reshape(cache_v.shape[1], page_size * n_heads, HEAD_W)
    o_s = _attn_sample(q_s.reshape(bs, n_new, qk_w), k_s.reshape(bs, n_new, qk_w), v_s.reshape(bs, n_new, qk_w),
                       kc, vc, page_table, slopes, lams, sg, n_heads, lam_init)

    h, hn, code, gates, cnt = _outproj((pool_p, o_p, xp), (pool_s.reshape(ns_, pw), o_s.reshape(ns_, qk_w), xs),
                                       w_out_b, g2, rwt_b, rb)

    counts = cnt[:, 0].astype(I32)
    n_blocks = -(-n_total * TOP_K // EXPERT_ROWS) + n_experts
    padded = (counts + EXPERT_ROWS - 1) // EXPERT_ROWS * EXPERT_ROWS
    pad_ends = jnp.cumsum(padded)
    pad_starts = pad_ends - padded
    n_used = (pad_ends[-1:] // EXPERT_ROWS).astype(I32)
    block_start = jnp.arange(n_blocks, dtype=I32) * EXPERT_ROWS
    block_e = jnp.minimum(jnp.sum((pad_ends[None, :] <= block_start[:, None]).astype(I32), axis=1), n_experts - 1)

    sorted_rows = _sorted_rows(pad_starts, code).reshape(-1)
    xs_rows, w1g, w1l = _dispatch(sorted_rows, pad_starts + counts, pad_ends, n_used, hn, w1.reshape(w1.shape[1:]),
                                  n_blocks)
    yb = _experts(block_e, n_used, xs_rows, w1g, w1l, b1g, b1l, w2b, b2e, n_blocks)
    y_p, y_s = _combine(sorted_rows, yb, gates, h, gf, np_)

    return (y_p.reshape(batch, seq, d), y_s.reshape(bs, n_new, d),
            tail_p[None, :, POOL_HALO - state_pool.shape[2]:],
            new_pool_s[None],
            k_p.reshape(1, batch, seq, n_heads, HEAD_W), v_p.reshape(1, batch, seq, n_heads, HEAD_W),
            k_s.reshape(1, bs, n_new, n_heads, HEAD_W), v_s.reshape(1, bs, n_new, n_heads, HEAD_W))
```

```python
import functools
import math

import jax
import jax.numpy as jnp
from jax import lax
from jax.experimental import pallas as pl
from jax.experimental.pallas import tpu as pltpu

F32 = jnp.float32
BF16 = jnp.bfloat16
I32 = jnp.int32

RMS_EPS = 1e-6
NEG_INF = -1e30
HEAD_DIM = 64
HEAD_W = 2 * HEAD_DIM
POOL_WINDOWS = (2, 4, 8, 16)
POOL_HALO = 16
TOP_K = 4
RANK_BITS = 20
SWIGLU_LIMIT = 7.0
SWIGLU_ALPHA = 1.702
LANES = 128
SUBLANES = 8
ROW_TILE = 512
INPROJ_TILE = 1024
ATTN_TILE = 512
SAMPLE_HEADS = 2
KEY_TILES = 1
ATTN_HEADS = 4
LOG2E = 1.4426950408889634
N_SPLIT = 3
POS_LOW = 16
BF16_ROWS = 16
EXPERT_ROWS = 768
COMBINE_TILE = 512
VMEM_LIMIT = 48 * 1024 * 1024


def _cparams(*sem):
    return pltpu.CompilerParams(dimension_semantics=sem, vmem_limit_bytes=VMEM_LIMIT)


def _dot(a, b):
    return jnp.dot(a, b, preferred_element_type=F32)


def _dot_nt(a, b):
    return lax.dot_general(a, b, (((1,), (1,)), ((), ())), preferred_element_type=F32)


def _rms(x, g):
    return x * lax.rsqrt(jnp.mean(x * x, axis=-1, keepdims=True) + RMS_EPS) * g


def _inproj_sample_kernel(x_ref, g_ref, w_ref, wvt_ref, u_ref, q_ref, k_ref, v_ref, *, pool_w, qk_w):
    del wvt_ref
    u = _inproj_tile(x_ref, g_ref, w_ref, q_ref, k_ref, v_ref, pool_w=pool_w, qk_w=qk_w)[0]
    u_ref[...] = u


def _inproj_prompt_kernel(x_ref, g_ref, w_ref, wvt_ref, pw_ref, ps_ref, pool_ref, tail_ref, q_ref, k_ref, v_ref, kb_ref,
                          vt_ref, carry_ref, *, pool_w, qk_w, key_tile, seq_tiles):
    u, k, xb = _inproj_tile(x_ref, g_ref, w_ref, q_ref, k_ref, v_ref, pool_w=pool_w, qk_w=qk_w)
    _pool_tile(u, pl.program_id(0) % seq_tiles, pw_ref, ps_ref, pool_ref, carry_ref)
    tail_ref[...] = u[u.shape[0] - POOL_HALO:, :]
    kb_ref[...] = k.astype(BF16)
    vt = _dot_nt(wvt_ref[...], xb)
    for c in range(vt_ref.shape[0]):
        vt_ref[c] = vt[:, c * key_tile:(c + 1) * key_tile].astype(BF16)


def _inproj_tile(x_ref, g_ref, w_ref, q_ref, k_ref, v_ref, *, pool_w, qk_w):
    xb = _rms(x_ref[...], g_ref[...]).astype(BF16)
    c0, c1, c2 = pool_w, pool_w + qk_w, pool_w + 2 * qk_w
    u = _dot(xb, w_ref[:, 0:c0])
    q_ref[...] = _dot(xb, w_ref[:, c0:c1]) * (HEAD_DIM ** -0.5)
    k = _dot(xb, w_ref[:, c1:c2])
    v = _dot(xb, w_ref[:, c2:])
    n_heads = qk_w // HEAD_W
    tm = k.shape[0]
    for h in range(n_heads):
        k_ref[pl.ds(h, tm, stride=n_heads), :] = k[:, h * HEAD_W:(h + 1) * HEAD_W]
        v_ref[pl.ds(h, tm, stride=n_heads), :] = v[:, h * HEAD_W:(h + 1) * HEAD_W]
    return u, k, xb


def _inproj(x, g1, w_in_b, wvt_b, pool_w, qk_w, prompt=None):
    n, d = x.shape
    val_w = wvt_b.shape[0]
    tm = min(INPROJ_TILE, n if prompt is None else prompt[2])
    row = lambda w: pl.BlockSpec((tm, w), lambda i: (i, 0))
    full = lambda a: pl.BlockSpec(a.shape, lambda i: (0,) * a.ndim)
    n_heads = qk_w // HEAD_W
    assert val_w == qk_w and n % tm == 0
    heads = pl.BlockSpec((tm * n_heads, HEAD_W), lambda i: (i, 0))
    qkv_specs = [row(qk_w), heads, heads]
    qkv_shape = [jax.ShapeDtypeStruct((n, qk_w), F32), jax.ShapeDtypeStruct((n * n_heads, HEAD_W), F32),
                 jax.ShapeDtypeStruct((n * n_heads, HEAD_W), F32)]
    if prompt is None:
        return pl.pallas_call(
            functools.partial(_inproj_sample_kernel, pool_w=pool_w, qk_w=qk_w),
            grid=(n // tm,),
            in_specs=[row(d), full(g1), full(w_in_b), full(wvt_b)],
            out_specs=[row(pool_w)] + qkv_specs,
            out_shape=[jax.ShapeDtypeStruct((n, pool_w), F32)] + qkv_shape,
            compiler_params=_cparams("parallel"),
            name="inproj_sample",
        )(x, g1, w_in_b, wvt_b)
    pool_w_b, pool_scale, seq, key_tile = prompt
    assert tm % key_tile == 0 and seq % tm == 0 and tm >= POOL_HALO
    seq_tiles = seq // tm
    return pl.pallas_call(
        functools.partial(_inproj_prompt_kernel, pool_w=pool_w, qk_w=qk_w, key_tile=key_tile, seq_tiles=seq_tiles),
        grid=(n // tm,),
        in_specs=[row(d), full(g1), full(w_in_b), full(wvt_b), full(pool_w_b), full(pool_scale)],
        out_specs=[row(pool_w), pl.BlockSpec((None, POOL_HALO, pool_w), lambda i: (i // seq_tiles, 0, 0))] + qkv_specs
        + [row(qk_w), pl.BlockSpec((tm // key_tile, val_w, key_tile), lambda i: (i, 0, 0))],
        out_shape=[jax.ShapeDtypeStruct((n, pool_w), BF16), jax.ShapeDtypeStruct((n // seq, POOL_HALO, pool_w), F32)]
        + qkv_shape + [jax.ShapeDtypeStruct((n, qk_w), BF16),
                       jax.ShapeDtypeStruct((n // key_tile, val_w, key_tile), BF16)],
        scratch_shapes=[pltpu.VMEM((POOL_HALO, pool_w), F32)],
        compiler_params=_cparams("arbitrary"),
        name="inproj_pool_prompt",
    )(x, g1, w_in_b, wvt_b, pool_w_b, pool_scale)


def _pool_group(s_win, tok, cnt, pw, ps):
    d = s_win / cnt - tok
    return _dot(d.astype(BF16), pw) * ps


def _pool_tile(cur, j, pw_ref, ps_ref, o_ref, carry_ref):
    tt = cur.shape[0]

    @pl.when(j == 0)
    def _():
        carry_ref[...] = jnp.zeros_like(carry_ref)

    ext = jnp.concatenate([carry_ref[...], cur], axis=0)
    carry_ref[...] = cur[tt - POOL_HALO:, :]
    pos = (j * tt + lax.broadcasted_iota(I32, (tt, 1), 0)).astype(F32)
    gw = cur.shape[1] // len(POOL_WINDOWS)
    for g, w in enumerate(POOL_WINDOWS):
        cols = slice(g * gw, (g + 1) * gw)
        e = ext[:, cols]
        s, span = e, 1
        while span < w:
            s = s + pltpu.roll(s, span, axis=0)
            span *= 2
        cnt = jnp.minimum(float(w), pos + 1.0)
        out = _pool_group(s[POOL_HALO:, :], e[POOL_HALO:, :], cnt, pw_ref[g], ps_ref[:, cols])
        o_ref[:, cols] = out.astype(o_ref.dtype)


def _pool_sample_kernel(sp_ref, u_ref, pw_ref, ps_ref, o_ref, np_ref, *, n_new, n_buf):
    rows = [sp_ref[r] for r in range(n_buf)] + [u_ref[r] for r in range(n_new)]
    for r in range(n_buf):
        np_ref[r] = rows[n_new + r]
    gw = rows[0].shape[1] // len(POOL_WINDOWS)
    for t in range(n_new):
        i = n_buf + t
        for g, w in enumerate(POOL_WINDOWS):
            cols = slice(g * gw, (g + 1) * gw)
            s = rows[i][:, cols]
            for back in range(1, w):
                s = s + rows[i - back][:, cols]
            out = _pool_group(s, rows[i][:, cols], float(w), pw_ref[g], ps_ref[:, cols])
            o_ref[t, :, cols] = out


def _pool_sample(state_pool_t, u_t, pool_w_b, pool_scale):
    n_buf, bs, pw = state_pool_t.shape
    n_new = u_t.shape[0]
    full = lambda a: pl.BlockSpec(a.shape, lambda i: (0,) * a.ndim)
    return pl.pallas_call(
        functools.partial(_pool_sample_kernel, n_new=n_new, n_buf=n_buf),
        grid=(1,),
        in_specs=[full(state_pool_t), full(u_t), full(pool_w_b), full(pool_scale)],
        out_specs=[pl.BlockSpec((n_new, bs, pw), lambda i: (0, 0, 0)),
                   pl.BlockSpec((n_buf, bs, pw), lambda i: (0, 0, 0))],
        out_shape=[jax.ShapeDtypeStruct((n_new, bs, pw), F32),
                   jax.ShapeDtypeStruct((n_buf, bs, pw), F32)],
        compiler_params=_cparams("arbitrary"),
        name="pool_sample",
    )(state_pool_t, u_t, pool_w_b, pool_scale)


def _lambda_full(lq1, lk1, lq2, lk2, lam_init):
    e1 = jnp.exp(jnp.sum(lq1 * lk1, axis=-1, keepdims=True))
    e2 = jnp.exp(jnp.sum(lq2 * lk2, axis=-1, keepdims=True))
    return e1 - e2 + lam_init


def _split_halves(q):
    lane = lax.broadcasted_iota(I32, q.shape, 1)
    zero = jnp.zeros_like(q)
    return jnp.concatenate([jnp.where(lane < HEAD_DIM, q, zero), jnp.where(lane >= HEAD_DIM, q, zero)], axis=0)


def _subln(o, g, lam_init):
    return _rms(o, g) * (1.0 - lam_init)


def _attn_prompt_kernel(slopes_ref, q_ref, k_ref, vt_ref, lq1_ref, lk1_ref, lq2_ref, lk2_ref, g_ref, o_ref,
                        *, tq, lam_init, heads):
    hg = pl.program_id(1)
    i = pl.program_id(2)
    lam = _lambda_full(lq1_ref[...], lk1_ref[...], lq2_ref[...], lk2_ref[...], lam_init)
    head_cols = [slice(hh * HEAD_W, (hh + 1) * HEAD_W) for hh in range(heads)]
    row = lax.broadcasted_iota(I32, (KEY_TILES * tq, LANES), 0)
    lane = lax.broadcasted_iota(I32, (KEY_TILES * tq, LANES), 1)
    pos_feat = jnp.where(lane < 2 * N_SPLIT, jnp.where(lane % 2 == 0, row & ~(POS_LOW - 1), row & (POS_LOW - 1)), 0)
    pos_feat = pos_feat.astype(F32).astype(BF16)
    qlane = lax.broadcasted_iota(I32, (2 * tq, LANES), 1)
    slopes2, qqs = [], []
    for hh in range(heads):
        s2 = slopes_ref[hg * heads + hh] * LOG2E
        slopes2.append(s2)
        rest = jnp.full((2 * tq, LANES), s2, F32)
        feat = jnp.zeros((2 * tq, LANES), F32)
        for part in range(N_SPLIT):
            piece = rest.astype(BF16).astype(F32)
            rest = rest - piece
            feat = jnp.where(qlane // 2 == part, piece, feat)
        qq = _split_halves((q_ref[:, head_cols[hh]] * LOG2E).astype(BF16))
        qqs.append(jnp.concatenate([qq, feat.astype(BF16)], axis=1))
    def head_step(hh, j, tiles, carry, diag_tile):
        m, acc = carry
        n_keys = tiles * tq
        start = pl.multiple_of(j * tq, tq)
        keys = jnp.concatenate([k_ref[pl.ds(start, n_keys), head_cols[hh]], pos_feat[:n_keys]], axis=1)
        st = _dot_nt(keys, qqs[hh])
        if diag_tile is not None:
            kr = lax.broadcasted_iota(I32, st.shape, 0) - diag_tile * tq
            qc = lax.broadcasted_iota(I32, st.shape, 1)
            qc = jnp.where(qc >= tq, qc - tq, qc)
            st = jnp.where(qc >= kr, st, NEG_INF)
        offset = slopes2[hh] * ((j - i) * tq).astype(F32)
        m_new = jnp.maximum(m, jnp.max(st, axis=0, keepdims=True) + offset)
        alpha = jnp.exp2(m - m_new)
        p = jnp.exp2(st - (m_new - offset)).astype(BF16)
        values = jnp.concatenate([vt_ref[j + t, head_cols[hh], :] for t in range(tiles)], axis=1)
        values = jnp.concatenate([values, jnp.ones((BF16_ROWS, n_keys), BF16)], axis=0)
        acc = alpha * acc + _dot(values, p)
        return m_new, acc

    def step(j, tiles, carries, diag_tile):
        return tuple(head_step(hh, j, tiles, carries[hh], diag_tile) for hh in range(heads))

    init = (jnp.full((1, 2 * tq), NEG_INF, F32), jnp.zeros((HEAD_W + BF16_ROWS, 2 * tq), F32))
    carries = lax.fori_loop(0, i // KEY_TILES, lambda g, cr: step(g * KEY_TILES, KEY_TILES, cr, None),
                            (init,) * heads)
    rest = i % KEY_TILES
    carries = lax.switch(rest, [functools.partial(step, i - r, r + 1, diag_tile=r) for r in range(KEY_TILES)],
                         carries)
    for hh, (_, acc) in enumerate(carries):
        o = acc[:HEAD_W] / acc[HEAD_W:HEAD_W + 1]
        o = (o[:, :tq] - lam * o[:, tq:]).T
        o_ref[:, head_cols[hh]] = _subln(o, g_ref[...], lam_init).astype(o_ref.dtype)


def _attn_prompt(q, kb, vt, slopes, lams, subln_g, batch, seq, n_heads, lam_init, tq):
    heads = math.gcd(n_heads, ATTN_HEADS)
    width = heads * HEAD_W
    q3, k3 = (a.reshape(batch, seq, n_heads * HEAD_W) for a in (q, kb))
    small = lambda a: pl.BlockSpec(a.shape, lambda b, h, i, s: (0,) * a.ndim)
    grid_spec = pltpu.PrefetchScalarGridSpec(
        num_scalar_prefetch=1,
        grid=(batch, n_heads // heads, seq // tq),
        in_specs=[pl.BlockSpec((None, tq, width), lambda b, h, i, s: (b, i, h)),
                  pl.BlockSpec((None, seq, width), lambda b, h, i, s: (b, 0, h)),
                  pl.BlockSpec((seq // tq, width, tq), lambda b, h, i, s: (b, h, 0)),
                  *[small(a) for a in lams], small(subln_g)],
        out_specs=pl.BlockSpec((None, tq, width), lambda b, h, i, s: (b, i, h)),
    )
    out = pl.pallas_call(
        functools.partial(_attn_prompt_kernel, tq=tq, lam_init=lam_init, heads=heads),
        grid_spec=grid_spec,
        out_shape=jax.ShapeDtypeStruct((batch, seq, n_heads * HEAD_W), BF16),
        compiler_params=_cparams("parallel", "parallel", "arbitrary"),
        name="attn_prompt",
    )(slopes, q3, k3, vt, *lams, subln_g)
    return out.reshape(batch * seq, n_heads * HEAD_W)


def _attn_sample_kernel(pt_ref, slopes_ref, q_ref, kn_ref, vn_ref, kc_ref, vc_ref,
                        lq1_ref, lk1_ref, lq2_ref, lk2_ref, g_ref, o_ref, kbuf, vbuf, sem,
                        *, n_pages, page_rows, n_heads, n_new, lam_init):
    b = pl.program_id(0)
    nb = pl.num_programs(0)
    slot = b % 2
    past = n_pages * page_rows // n_heads

    def page_copies(seq, sl, pg):
        p = pt_ref[seq * n_pages + pg]
        dst = pl.ds(pg * page_rows, page_rows)
        return (pltpu.make_async_copy(kc_ref.at[p], kbuf.at[sl, dst, :], sem.at[sl, 0]),
                pltpu.make_async_copy(vc_ref.at[p], vbuf.at[sl, dst, :], sem.at[sl, 1]))

    def start_seq(seq, sl):
        for pg in range(n_pages):
            for cp in page_copies(seq, sl, pg):
                cp.start()

    @pl.when(b == 0)
    def _():
        start_seq(0, 0)

    @pl.when(b + 1 < nb)
    def _():
        start_seq(b + 1, 1 - slot)

    for pg in range(n_pages):
        for cp in page_copies(b, slot, pg):
            cp.wait()

    lam = _lambda_full(lq1_ref[...], lk1_ref[...], lq2_ref[...], lk2_ref[...], lam_init)
    grp = 2 * SUBLANES
    rows = SAMPLE_HEADS * grp
    r = lax.broadcasted_iota(I32, (rows, past), 0) % grp
    t_past = jnp.where(r >= n_new, r - n_new, r)
    dist_past = (past + t_past - lax.broadcasted_iota(I32, (rows, past), 1)).astype(F32)
    r1 = lax.broadcasted_iota(I32, (rows, 1), 0)
    t_new = jnp.where(r1 % grp >= n_new, r1 % grp - n_new, r1 % grp)
    pad = jnp.zeros((grp - 2 * n_new, HEAD_W), F32)
    for h0 in range(0, n_heads, SAMPLE_HEADS):
        heads = range(h0, h0 + SAMPLE_HEADS)
        cols = slice(h0 * HEAD_W, (h0 + SAMPLE_HEADS) * HEAD_W)
        slope = jnp.concatenate([jnp.full((grp, 1), slopes_ref[h], F32) for h in heads], axis=0)
        blocks = []
        for i, h in enumerate(heads):
            qh = jnp.concatenate([_split_halves(q_ref[:, h * HEAD_W:(h + 1) * HEAD_W]), pad], axis=0)
            zero = jnp.zeros_like(qh)
            blocks.append(jnp.concatenate([qh if j == i else zero for j in range(SAMPLE_HEADS)], axis=1))
        qq = jnp.concatenate(blocks, axis=0)
        kh = jnp.concatenate([kbuf[slot, pl.ds(h, past, stride=n_heads), :] for h in heads], axis=1).astype(BF16)
        vh = jnp.concatenate([vbuf[slot, pl.ds(h, past, stride=n_heads), :] for h in heads], axis=1).astype(BF16)
        s = _dot_nt(qq.astype(BF16), kh) - slope * dist_past
        kn = kn_ref[:, cols]
        vn = vn_ref[:, cols]
        s_new = []
        for c in range(n_new):
            sc = jnp.sum(qq * kn[c:c + 1, :], axis=-1, keepdims=True) - slope * (t_new - c).astype(F32)
            s_new.append(jnp.where(t_new >= c, sc, NEG_INF))
        m = jnp.max(s, axis=-1, keepdims=True)
        for sc in s_new:
            m = jnp.maximum(m, sc)
        p = jnp.exp(s - m)
        l = jnp.sum(p, axis=-1, keepdims=True)
        acc = _dot(p.astype(BF16), vh)
        for c, sc in enumerate(s_new):
            pc = jnp.exp(sc - m)
            l = l + pc
            acc = acc + pc * vn[c:c + 1, :]
        o = acc / l
        for i, h in enumerate(heads):
            oh = o[i * grp:(i + 1) * grp, i * HEAD_W:(i + 1) * HEAD_W]
            oh = oh[0:n_new] - lam * oh[n_new:2 * n_new]
            o_ref[:, h * HEAD_W:(h + 1) * HEAD_W] = _subln(oh, g_ref[...], lam_init)


def _attn_sample(q3, kn3, vn3, kc, vc, page_table, slopes, lams, subln_g, n_heads, lam_init):
    bs, n_new, qk_w = q3.shape
    n_pages = page_table.shape[1]
    page_rows = kc.shape[1]
    small = lambda a: pl.BlockSpec(a.shape, lambda b, pt, s: (0,) * a.ndim)
    tok = pl.BlockSpec((None, n_new, qk_w), lambda b, pt, s: (b, 0, 0))
    grid_spec = pltpu.PrefetchScalarGridSpec(
        num_scalar_prefetch=2,
        grid=(bs,),
        in_specs=[tok, tok, tok, pl.BlockSpec(memory_space=pl.ANY), pl.BlockSpec(memory_space=pl.ANY),
                  *[small(a) for a in lams], small(subln_g)],
        out_specs=tok,
        scratch_shapes=[pltpu.VMEM((2, n_pages * page_rows, LANES), F32),
                        pltpu.VMEM((2, n_pages * page_rows, LANES), F32),
                        pltpu.SemaphoreType.DMA((2, 2))],
    )
    return pl.pallas_call(
        functools.partial(_attn_sample_kernel, n_pages=n_pages, page_rows=page_rows, n_heads=n_heads,
                          n_new=n_new, lam_init=lam_init),
        grid_spec=grid_spec,
        out_shape=jax.ShapeDtypeStruct((bs, n_new, qk_w), F32),
        compiler_params=_cparams("arbitrary"),
        name="attn_sample",
    )(page_table.reshape(-1), slopes, q3, kn3, vn3, kc, vc, *lams, subln_g)


def _outproj_kernel(pool_p_ref, o_p_ref, x_p_ref, pool_s_ref, o_s_ref, x_s_ref, wo_ref, g2_ref, rwt_ref, rb_ref,
                    earlier_ref, h_ref, hn_ref, code_ref, gate_ref, cnt_out_ref, run_ref, *, tm, n_experts,
                    prompt_tiles):
    i = pl.program_id(0)

    @pl.when(i == 0)
    def _():
        run_ref[...] = jnp.zeros_like(run_ref)

    tile = functools.partial(_outproj_tile, wo_ref=wo_ref, g2_ref=g2_ref, rwt_ref=rwt_ref, rb_ref=rb_ref,
                             earlier_ref=earlier_ref, h_ref=h_ref, hn_ref=hn_ref, code_ref=code_ref, gate_ref=gate_ref,
                             cnt_out_ref=cnt_out_ref, run_ref=run_ref, tm=tm, n_experts=n_experts)
    pl.when(i < prompt_tiles)(functools.partial(tile, pool_p_ref, o_p_ref, x_p_ref))
    pl.when(i >= prompt_tiles)(functools.partial(tile, pool_s_ref, o_s_ref, x_s_ref))


def _outproj_tile(pool_ref, o_ref, x_ref, *, wo_ref, g2_ref, rwt_ref, rb_ref, earlier_ref, h_ref, hn_ref, code_ref,
                  gate_ref, cnt_out_ref, run_ref, tm, n_experts):
    pw = pool_ref.shape[1]
    mix = _dot(pool_ref[...].astype(BF16), wo_ref[0:pw, :]) + _dot(o_ref[...].astype(BF16), wo_ref[pw:, :])
    h = x_ref[...] + mix
    h_ref[...] = h
    hn = _rms(h, g2_ref[...])
    for s in range(hn.shape[1] // LANES):
        hn_ref[pl.ds(s, tm, stride=SUBLANES), :] = hn[:, s * LANES:(s + 1) * LANES]
    logits = _dot_nt(rwt_ref[...], hn.astype(BF16)) + rb_ref[...]
    expert = lax.broadcasted_iota(I32, logits.shape, 0)
    vals, idxs = [], []
    for _ in range(TOP_K):
        m = jnp.max(logits, axis=0, keepdims=True)
        idx = jnp.min(jnp.where(logits == m, expert, n_experts), axis=0, keepdims=True)
        vals.append(m)
        idxs.append(idx)
        logits = jnp.where(expert == idx, -jnp.inf, logits)
    ex = [jnp.exp(v - vals[0]) for v in vals]
    den = ex[0]
    for e in ex[1:]:
        den = den + e
    chosen = jnp.zeros(logits.shape, F32)
    for idx in idxs:
        chosen = chosen + jnp.where(expert == idx, 1.0, 0.0)
    before = _dot(chosen.astype(BF16), earlier_ref[...]) + run_ref[...]
    for k in range(TOP_K):
        gate_ref[k:k + 1, :] = ex[k] / den
        rank = jnp.sum(jnp.where(expert == idxs[k], before, 0.0), axis=0, keepdims=True).astype(I32)
        code_ref[k:k + 1, :] = idxs[k] * (1 << RANK_BITS) + rank
    gate_ref[TOP_K:, :] = jnp.zeros((gate_ref.shape[0] - TOP_K, tm), F32)
    run_ref[...] = run_ref[...] + jnp.sum(chosen, axis=1, keepdims=True)
    cnt_out_ref[...] = run_ref[...]


def _outproj(prompt, sample, w_out_b, g2, rwt_b, rb):
    np_, d = prompt[2].shape
    ns_ = sample[2].shape[0]
    n_total = np_ + ns_
    n_experts = rwt_b.shape[0]
    tm = min(ROW_TILE, math.gcd(np_, ns_))
    prompt_tiles = np_ // tm
    assert d == SUBLANES * LANES and n_total < (1 << RANK_BITS)
    first = lambda a: pl.BlockSpec((tm, a.shape[1]), lambda i: (jnp.minimum(i, prompt_tiles - 1), 0))
    second = lambda a: pl.BlockSpec((tm, a.shape[1]), lambda i: (jnp.maximum(i - prompt_tiles, 0), 0))
    row = lambda w: pl.BlockSpec((tm, w), lambda i: (i, 0))
    col = lambda r: pl.BlockSpec((r, tm), lambda i: (0, i))
    full = lambda a: pl.BlockSpec(a.shape, lambda i: (0,) * a.ndim)
    counts = jax.ShapeDtypeStruct((n_experts, 1), F32)
    earlier = jnp.triu(jnp.ones((tm, tm), BF16), 1)
    return pl.pallas_call(
        functools.partial(_outproj_kernel, tm=tm, n_experts=n_experts, prompt_tiles=prompt_tiles),
        grid=(n_total // tm,),
        in_specs=[*[first(a) for a in prompt], *[second(a) for a in sample],
                  full(w_out_b), full(g2), full(rwt_b), full(rb), full(earlier)],
        out_specs=[row(d), pl.BlockSpec((tm * SUBLANES, LANES), lambda i: (i, 0)),
                   col(TOP_K), col(SUBLANES), pl.BlockSpec(counts.shape, lambda i: (0, 0))],
        out_shape=[jax.ShapeDtypeStruct((n_total, d), F32), jax.ShapeDtypeStruct((n_total * SUBLANES, LANES), F32),
                   jax.ShapeDtypeStruct((TOP_K, n_total), I32), jax.ShapeDtypeStruct((SUBLANES, n_total), F32),
                   counts],
        scratch_shapes=[pltpu.VMEM(counts.shape, F32)],
        compiler_params=_cparams("arbitrary"),
        name="outproj_router",
    )(*prompt, *sample, w_out_b, g2, rwt_b, rb, earlier)


def _sorted_rows_kernel(start_ref, code_ref, rows_ref, *, n_experts):
    code = code_ref[...]
    expert = code >> RANK_BITS
    rows = code & ((1 << RANK_BITS) - 1)
    for e in range(n_experts):
        rows = rows + jnp.where(expert == e, start_ref[e], 0)
    rows_ref[...] = rows


def _sorted_rows(starts, code):
    return pl.pallas_call(
        functools.partial(_sorted_rows_kernel, n_experts=starts.shape[0]),
        grid=(1,),
        in_specs=[pl.BlockSpec(memory_space=pltpu.SMEM), pl.BlockSpec(code.shape, lambda i: (0, 0))],
        out_specs=pl.BlockSpec(code.shape, lambda i: (0, 0)),
        out_shape=jax.ShapeDtypeStruct(code.shape, I32),
        compiler_params=_cparams("arbitrary"),
        name="sorted_rows",
    )(starts, code)


MXU_DIM = 256


def _split_w1_tile(w_ref, g_ref, l_ref):
    r = lax.broadcasted_iota(I32, (MXU_DIM, MXU_DIM), 0)
    c = lax.broadcasted_iota(I32, (MXU_DIM, MXU_DIM), 1)
    half = MXU_DIM // 2
    source = jnp.where(c < half, 2 * c, 2 * (c - half) + 1)
    perm = jnp.where(r == source, 1.0, 0.0).astype(BF16)
    for blk in range(w_ref.shape[1] // MXU_DIM):
        res = _dot(w_ref[:, blk * MXU_DIM:(blk + 1) * MXU_DIM].astype(BF16), perm)
        g_ref[:, blk * half:(blk + 1) * half] = res[:, :half].astype(BF16)
        l_ref[:, blk * half:(blk + 1) * half] = res[:, half:].astype(BF16)


def _dispatch_kernel(row_ref, zlo_ref, zhi_ref, nu_ref, hn_ref, w1_ref, xs_ref, w1g_ref, w1l_ref, zero_ref, sem,
                     *, tm, n_total, rows, n_experts, n_blocks):
    i = pl.program_id(0)
    tile = tm * SUBLANES
    block = rows * SUBLANES

    def pad_pieces(e):
        lo, n = zlo_ref[e], zhi_ref[e] - zlo_ref[e]
        for bit in range((rows - 1).bit_length()):
            size = 1 << bit
            below = n & (size - 1)
            start = pl.multiple_of((lo + below) * SUBLANES, SUBLANES)
            cp = pltpu.make_async_copy(zero_ref.at[pl.ds(0, size * SUBLANES), :],
                                       xs_ref.at[pl.ds(start, size * SUBLANES), :], sem.at[1])
            yield (n & size) != 0, cp

    def tail_copy(blk):
        start = pl.multiple_of(blk * block, block)
        return pltpu.make_async_copy(zero_ref, xs_ref.at[pl.ds(start, block), :], sem.at[1])

    def for_zero_copies(fn):
        def per_expert(e, carry):
            for used, cp in pad_pieces(e):
                pl.when(used)(functools.partial(fn, cp))
            return carry
        lax.fori_loop(0, n_experts, per_expert, 0)

        def per_block(blk, carry):
            fn(tail_copy(blk))
            return carry
        lax.fori_loop(nu_ref[0], n_blocks, per_block, 0)

    @pl.when(i == 0)
    def _():
        zero_ref[...] = jnp.zeros_like(zero_ref)
        for_zero_copies(lambda cp: cp.start())

    def body(t, carry):
        src = pl.multiple_of(t * SUBLANES, SUBLANES)
        for k in range(TOP_K):
            dst = pl.multiple_of(row_ref[k * n_total + i * tm + t] * SUBLANES, SUBLANES)
            pltpu.make_async_copy(hn_ref.at[pl.ds(src, SUBLANES), :], xs_ref.at[pl.ds(dst, SUBLANES), :],
                                  sem.at[0]).start()
        return carry
    lax.fori_loop(0, tm, body, 0)

    _split_w1_tile(w1_ref, w1g_ref, w1l_ref)

    for _ in range(TOP_K):
        pltpu.make_async_copy(hn_ref, xs_ref.at[pl.ds(0, tile), :], sem.at[0]).wait()

    @pl.when(i == pl.num_programs(0) - 1)
    def _():
        for_zero_copies(lambda cp: cp.wait())


def _dispatch(sorted_rows, zlo, zhi, n_used, hn, w1, n_blocks):
    n_total = hn.shape[0] // SUBLANES
    n_exp, d, ff2 = w1.shape
    tr = next(t for t in (ROW_TILE, 2 * ROW_TILE, d) if d % t == 0 and n_total % (n_exp * (d // t)) == 0)
    row_tiles = d // tr
    steps = n_exp * row_tiles
    tm = n_total // steps
    rows = EXPERT_ROWS
    assert rows % SUBLANES == 0
    half = pl.BlockSpec((None, tr, ff2 // 2), lambda i, *_: (i // row_tiles, i % row_tiles, 0))
    grid_spec = pltpu.PrefetchScalarGridSpec(
        num_scalar_prefetch=4,
        grid=(steps,),
        in_specs=[pl.BlockSpec((tm * SUBLANES, LANES), lambda i, *_: (i, 0)),
                  pl.BlockSpec((None, tr, ff2), lambda i, *_: (i // row_tiles, i % row_tiles, 0))],
        out_specs=[pl.BlockSpec(memory_space=pl.ANY), half, half],
        scratch_shapes=[pltpu.VMEM((rows * SUBLANES, LANES), F32), pltpu.SemaphoreType.DMA((2,))],
    )
    return pl.pallas_call(
        functools.partial(_dispatch_kernel, tm=tm, n_total=n_total, rows=rows, n_experts=n_exp, n_blocks=n_blocks),
        grid_spec=grid_spec,
        out_shape=[jax.ShapeDtypeStruct((n_blocks * rows * SUBLANES, LANES), F32),
                   jax.ShapeDtypeStruct((n_exp, d, ff2 // 2), BF16), jax.ShapeDtypeStruct((n_exp, d, ff2 // 2), BF16)],
        compiler_params=_cparams("arbitrary"),
        name="dispatch_split_w1",
    )(sorted_rows, zlo, zhi, n_used, hn, w1)


def _expert_kernel(be_ref, nu_ref, xs_ref, w1g_ref, w1l_ref, b1g_ref, b1l_ref, w2_ref, b2_ref, y_ref, *, rows):
    j = pl.program_id(0)
    n_used = nu_ref[0]

    @pl.when(j < n_used)
    def _():
        x = jnp.concatenate([xs_ref[pl.ds(s, rows, stride=SUBLANES), :] for s in range(SUBLANES)],
                            axis=-1).astype(BF16)
        glu = jnp.minimum(_dot(x, w1g_ref[...]) + b1g_ref[...], SWIGLU_LIMIT)
        lin = jnp.clip(_dot(x, w1l_ref[...]) + b1l_ref[...], -SWIGLU_LIMIT, SWIGLU_LIMIT)
        act = glu * jax.nn.sigmoid(SWIGLU_ALPHA * glu) * (lin + 1.0)
        y = _dot(act.astype(BF16), w2_ref[...].astype(BF16)) + b2_ref[...]
        for s in range(SUBLANES):
            y_ref[pl.ds(s, rows, stride=SUBLANES), :] = y[:, s * LANES:(s + 1) * LANES]

    @pl.when(j >= n_used)
    def _():
        y_ref[...] = jnp.zeros_like(y_ref)


def _experts(block_e, n_used, xs, w1g, w1l, b1g, b1l, w2, b2, n_blocks):
    rows = EXPERT_ROWS
    tile = rows * SUBLANES
    wspec = lambda a: pl.BlockSpec((None,) + a.shape[1:], lambda j, be, nu: (be[j], 0, 0))
    grid_spec = pltpu.PrefetchScalarGridSpec(
        num_scalar_prefetch=2,
        grid=(n_blocks,),
        in_specs=[pl.BlockSpec((tile, LANES), lambda j, be, nu: (jnp.minimum(j, nu[0] - 1), 0)),
                  wspec(w1g), wspec(w1l), wspec(b1g), wspec(b1l), wspec(w2), wspec(b2)],
        out_specs=pl.BlockSpec((tile, LANES), lambda j, be, nu: (j, 0)),
    )
    return pl.pallas_call(
        functools.partial(_expert_kernel, rows=rows),
        grid_spec=grid_spec,
        out_shape=jax.ShapeDtypeStruct((n_blocks * tile, LANES), F32),
        compiler_params=_cparams("arbitrary"),
        name="experts",
    )(block_e, n_used, xs, w1g, w1l, b1g, b1l, w2, b2)


def _combine_kernel(row_ref, yb_ref, gate_ref, h_ref, g_ref, yp_ref, ys_ref, cbuf, sem, *, tm, n_total,
                    prompt_tiles):
    i = pl.program_id(0)
    n = pl.num_programs(0)
    slot = i % 2
    tile = tm * SUBLANES

    def gather(blk, sl):
        def body(t2, carry):
            for t in (2 * t2, 2 * t2 + 1):
                for k in range(TOP_K):
                    src = pl.multiple_of(row_ref[k * n_total + blk * tm + t] * SUBLANES, SUBLANES)
                    dst = pl.multiple_of((k * tm + t) * SUBLANES, SUBLANES)
                    pltpu.make_async_copy(yb_ref.at[pl.ds(src, SUBLANES), :], cbuf.at[sl, pl.ds(dst, SUBLANES), :],
                                          sem.at[sl]).start()
            return carry
        lax.fori_loop(0, tm // 2, body, 0)

    @pl.when(i == 0)
    def _():
        gather(0, 0)

    @pl.when(i + 1 < n)
    def _():
        gather(i + 1, 1 - slot)

    pltpu.make_async_copy(yb_ref.at[pl.ds(0, TOP_K * tile), :], cbuf.at[slot], sem.at[slot]).wait()
    gates = gate_ref[...].T
    slabs = []
    for s in range(SUBLANES):
        acc = None
        for k in range(TOP_K):
            part = gates[:, k:k + 1] * cbuf[slot, pl.ds(k * tile + s, tm, stride=SUBLANES), :]
            acc = part if acc is None else acc + part
        slabs.append(acc)
    hf = h_ref[...] + jnp.concatenate(slabs, axis=-1)
    y = _rms(hf, g_ref[...])

    @pl.when(i < prompt_tiles)
    def _():
        yp_ref[...] = y

    @pl.when(i >= prompt_tiles)
    def _():
        ys_ref[...] = y


def _combine(rows, yb, gates, h, gf, np_):
    n, d = h.shape
    ns_ = n - np_
    tm = min(COMBINE_TILE, math.gcd(np_, ns_))
    prompt_tiles = np_ // tm
    row = lambda w: pl.BlockSpec((tm, w), lambda i, *_: (i, 0))
    grid_spec = pltpu.PrefetchScalarGridSpec(
        num_scalar_prefetch=1,
        grid=(n // tm,),
        in_specs=[pl.BlockSpec(memory_space=pl.ANY), pl.BlockSpec((gates.shape[0], tm), lambda i, *_: (0, i)), row(d),
                  pl.BlockSpec(gf.shape, lambda i, *_: (0, 0))],
        out_specs=[pl.BlockSpec((tm, d), lambda i, *_: (jnp.minimum(i, prompt_tiles - 1), 0)),
                   pl.BlockSpec((tm, d), lambda i, *_: (jnp.maximum(i - prompt_tiles, 0), 0))],
        scratch_shapes=[pltpu.VMEM((2, TOP_K * tm * SUBLANES, LANES), F32), pltpu.SemaphoreType.DMA((2,))],
    )
    return pl.pallas_call(
        functools.partial(_combine_kernel, tm=tm, n_total=n, prompt_tiles=prompt_tiles),
        grid_spec=grid_spec,
        out_shape=[jax.ShapeDtypeStruct((np_, d), F32), jax.ShapeDtypeStruct((ns_, d), F32)],
        compiler_params=_cparams("arbitrary"),
        name="combine_norm",
    )(rows, yb, gates, h, gf)


def kernel(x_prompt, x_sample, state_pool, cache_k, cache_v, page_table, norm1_g, w_in, pool_w, pool_scale,
           lambda_q1, lambda_k1, lambda_q2, lambda_k2, subln_g, w_out, norm2_g, router_w, router_b,
           w1, b1, w2, b2, normf_g):
    batch, seq, d = x_prompt.shape
    bs, n_new, _ = x_sample.shape
    depth = w_in.shape[0]
    assert depth == 1, "single-layer trunk"
    n_heads = cache_k.shape[3]
    page_size = cache_k.shape[2]
    pw = pool_w.shape[1] * pool_w.shape[2]
    qk_w = n_heads * HEAD_W
    n_experts = router_w.shape[2]
    np_, ns_ = batch * seq, bs * n_new
    n_total = np_ + ns_
    lam_init = 0.8 - 0.6 * math.exp(-0.3 * 0)

    g1 = norm1_g[0][None]
    w_in_b = w_in[0].astype(BF16)
    wvt_b = w_in_b[:, pw + 2 * qk_w:].T
    pool_w_b = pool_w[0].astype(BF16)
    ps = pool_scale[0][None]
    lams = [a[0][None] for a in (lambda_q1, lambda_k1, lambda_q2, lambda_k2)]
    sg = subln_g[0][None]
    slopes = jnp.exp2(-8.0 * (jnp.arange(n_heads, dtype=F32) + 1.0) / n_heads)
    w_out_b = w_out[0].astype(BF16)
    g2 = norm2_g[0][None]
    rwt_b = router_w[0].T.astype(BF16)
    rb = router_b[0][:, None]
    b1_pairs = b1[0].T.reshape(-1, 2, n_experts)
    b1g = b1_pairs[:, 0, :].T[:, None, :]
    b1l = b1_pairs[:, 1, :].T[:, None, :]
    w2b = w2.reshape(w2.shape[1:])
    b2e = b2[0][:, None, :]
    gf = normf_g[None]

    xp = x_prompt.reshape(np_, d)
    tq = min(ATTN_TILE, seq)
    pool_p, tail_p, q_p, k_p, v_p, kb_p, vt_p = _inproj(xp, g1, w_in_b, wvt_b, pw, qk_w,
                                                        prompt=(pool_w_b, ps, seq, tq))
    o_p = _attn_prompt(q_p, kb_p, vt_p, slopes, lams, sg, batch, seq, n_heads, lam_init, tq)

    xs = x_sample.reshape(ns_, d)
    u_s, q_s, k_s, v_s = _inproj(xs, g1, w_in_b, wvt_b, pw, qk_w)
    pool_s, new_pool_s = _pool_sample(jnp.transpose(state_pool[0], (1, 0, 2)),
                                      jnp.transpose(u_s.reshape(bs, n_new, pw), (1, 0, 2)), pool_w_b, ps)
    pool_s = jnp.transpose(pool_s, (1, 0, 2))
    new_pool_s = jnp.transpose(new_pool_s, (1, 0, 2))
    kc = cache_k.reshape(cache_k.shape[1], page_size * n_heads, HEAD_W)
    vc = cache_v.reshape(cache_v.shape[1], page_size * n_heads, HEAD_W)
    o_s = _attn_sample(q_s.reshape(bs, n_new, qk_w), k_s.reshape(bs, n_new, qk_w), v_s.reshape(bs, n_new, qk_w),
                       kc, vc, page_table, slopes, lams, sg, n_heads, lam_init)

    h, hn, code, gates, cnt = _outproj((pool_p, o_p, xp), (pool_s.reshape(ns_, pw), o_s.reshape(ns_, qk_w), xs),
                                       w_out_b, g2, rwt_b, rb)

    counts = cnt[:, 0].astype(I32)
    n_blocks = -(-n_total * TOP_K // EXPERT_ROWS) + n_experts
    padded = (counts + EXPERT_ROWS - 1) // EXPERT_ROWS * EXPERT_ROWS
    pad_ends = jnp.cumsum(padded)
    pad_starts = pad_ends - padded
    n_used = (pad_ends[-1:] // EXPERT_ROWS).astype(I32)
    block_start = jnp.arange(n_blocks, dtype=I32) * EXPERT_ROWS
    block_e = jnp.minimum(jnp.sum((pad_ends[None, :] <= block_start[:, None]).astype(I32), axis=1), n_experts - 1)

    sorted_rows = _sorted_rows(pad_starts, code).reshape(-1)
    xs_rows, w1g, w1l = _dispatch(sorted_rows, pad_starts + counts, pad_ends, n_used, hn, w1.reshape(w1.shape[1:]),
                                  n_blocks)
    yb = _experts(block_e, n_used, xs_rows, w1g, w1l, b1g, b1l, w2b, b2e, n_blocks)
    y_p, y_s = _combine(sorted_rows, yb, gates, h, gf, np_)

    return (y_p.reshape(batch, seq, d), y_s.reshape(bs, n_new, d),
            tail_p[None, :, POOL_HALO - state_pool.shape[2]:],
            new_pool_s[None],
            k_p.reshape(1, batch, seq, n_heads, HEAD_W), v_p.reshape(1, batch, seq, n_heads, HEAD_W),
            k_s.reshape(1, bs, n_new, n_heads, HEAD_W), v_s.reshape(1, bs, n_new, n_heads, HEAD_W))
```

```python
import functools
import math

import jax
import jax.numpy as jnp
from jax import lax
from jax.experimental import pallas as pl
from jax.experimental.pallas import tpu as pltpu

F32 = jnp.float32
BF16 = jnp.bfloat16
I32 = jnp.int32

RMS_EPS = 1e-6
NEG_INF = -1e30
HEAD_DIM = 64
HEAD_W = 2 * HEAD_DIM
POOL_WINDOWS = (2, 4, 8, 16)
POOL_HALO = 16
TOP_K = 4
RANK_BITS = 20
SWIGLU_LIMIT = 7.0
SWIGLU_ALPHA = 1.702
LANES = 128
SUBLANES = 8
ROW_TILE = 512
INPROJ_TILE = 1024
ATTN_TILE = 512
SAMPLE_HEADS = 2
SAMPLE_SEQS = 4
KEY_TILES = 1
ATTN_HEADS = 4
LOG2E = 1.4426950408889634
N_SPLIT = 3
POS_LOW = 16
BF16_ROWS = 16
EXPERT_ROWS = 768
COMBINE_TILE = 512
VMEM_LIMIT = 48 * 1024 * 1024


def _cparams(*sem):
    return pltpu.CompilerParams(dimension_semantics=sem, vmem_limit_bytes=VMEM_LIMIT)


def _dot(a, b):
    return jnp.dot(a, b, preferred_element_type=F32)


def _dot_nt(a, b):
    return lax.dot_general(a, b, (((1,), (1,)), ((), ())), preferred_element_type=F32)


def _rms(x, g):
    return x * lax.rsqrt(jnp.mean(x * x, axis=-1, keepdims=True) + RMS_EPS) * g


def _inproj_sample_kernel(x_ref, g_ref, w_ref, wvt_ref, u_ref, q_ref, k_ref, v_ref, *, pool_w, qk_w):
    del wvt_ref
    u = _inproj_tile(x_ref, g_ref, w_ref, q_ref, k_ref, v_ref, pool_w=pool_w, qk_w=qk_w)[0]
    u_ref[...] = u


def _inproj_prompt_kernel(x_ref, g_ref, w_ref, wvt_ref, pw_ref, ps_ref, pool_ref, tail_ref, q_ref, k_ref, v_ref, kb_ref,
                          vt_ref, carry_ref, *, pool_w, qk_w, key_tile, seq_tiles):
    u, k, xb = _inproj_tile(x_ref, g_ref, w_ref, q_ref, k_ref, v_ref, pool_w=pool_w, qk_w=qk_w)
    _pool_tile(u, pl.program_id(0) % seq_tiles, pw_ref, ps_ref, pool_ref, carry_ref)
    tail_ref[...] = u[u.shape[0] - POOL_HALO:, :]
    kb_ref[...] = k.astype(BF16)
    vt = _dot_nt(wvt_ref[...], xb)
    for c in range(vt_ref.shape[0]):
        vt_ref[c] = vt[:, c * key_tile:(c + 1) * key_tile].astype(BF16)


def _inproj_tile(x_ref, g_ref, w_ref, q_ref, k_ref, v_ref, *, pool_w, qk_w):
    xb = _rms(x_ref[...], g_ref[...]).astype(BF16)
    c0, c1, c2 = pool_w, pool_w + qk_w, pool_w + 2 * qk_w
    u = _dot(xb, w_ref[:, 0:c0])
    q_ref[...] = _dot(xb, w_ref[:, c0:c1]) * (HEAD_DIM ** -0.5)
    k = _dot(xb, w_ref[:, c1:c2])
    v = _dot(xb, w_ref[:, c2:])
    n_heads = qk_w // HEAD_W
    tm = k.shape[0]
    for h in range(n_heads):
        k_ref[pl.ds(h, tm, stride=n_heads), :] = k[:, h * HEAD_W:(h + 1) * HEAD_W]
        v_ref[pl.ds(h, tm, stride=n_heads), :] = v[:, h * HEAD_W:(h + 1) * HEAD_W]
    return u, k, xb


def _inproj(x, g1, w_in_b, wvt_b, pool_w, qk_w, prompt=None):
    n, d = x.shape
    val_w = wvt_b.shape[0]
    tm = min(INPROJ_TILE, n if prompt is None else prompt[2])
    row = lambda w: pl.BlockSpec((tm, w), lambda i: (i, 0))
    full = lambda a: pl.BlockSpec(a.shape, lambda i: (0,) * a.ndim)
    n_heads = qk_w // HEAD_W
    assert val_w == qk_w and n % tm == 0
    heads = pl.BlockSpec((tm * n_heads, HEAD_W), lambda i: (i, 0))
    qkv_specs = [row(qk_w), heads, heads]
    qkv_shape = [jax.ShapeDtypeStruct((n, qk_w), F32), jax.ShapeDtypeStruct((n * n_heads, HEAD_W), F32),
                 jax.ShapeDtypeStruct((n * n_heads, HEAD_W), F32)]
    if prompt is None:
        return pl.pallas_call(
            functools.partial(_inproj_sample_kernel, pool_w=pool_w, qk_w=qk_w),
            grid=(n // tm,),
            in_specs=[row(d), full(g1), full(w_in_b), full(wvt_b)],
            out_specs=[row(pool_w)] + qkv_specs,
            out_shape=[jax.ShapeDtypeStruct((n, pool_w), F32)] + qkv_shape,
            compiler_params=_cparams("parallel"),
            name="inproj_sample",
        )(x, g1, w_in_b, wvt_b)
    pool_w_b, pool_scale, seq, key_tile = prompt
    assert tm % key_tile == 0 and seq % tm == 0 and tm >= POOL_HALO
    seq_tiles = seq // tm
    return pl.pallas_call(
        functools.partial(_inproj_prompt_kernel, pool_w=pool_w, qk_w=qk_w, key_tile=key_tile, seq_tiles=seq_tiles),
        grid=(n // tm,),
        in_specs=[row(d), full(g1), full(w_in_b), full(wvt_b), full(pool_w_b), full(pool_scale)],
        out_specs=[row(pool_w), pl.BlockSpec((None, POOL_HALO, pool_w), lambda i: (i // seq_tiles, 0, 0))] + qkv_specs
        + [row(qk_w), pl.BlockSpec((tm // key_tile, val_w, key_tile), lambda i: (i, 0, 0))],
        out_shape=[jax.ShapeDtypeStruct((n, pool_w), BF16), jax.ShapeDtypeStruct((n // seq, POOL_HALO, pool_w), F32)]
        + qkv_shape + [jax.ShapeDtypeStruct((n, qk_w), BF16),
                       jax.ShapeDtypeStruct((n // key_tile, val_w, key_tile), BF16)],
        scratch_shapes=[pltpu.VMEM((POOL_HALO, pool_w), F32)],
        compiler_params=_cparams("arbitrary"),
        name="inproj_pool_prompt",
    )(x, g1, w_in_b, wvt_b, pool_w_b, pool_scale)


def _pool_group(s_win, tok, cnt, pw, ps):
    d = s_win / cnt - tok
    return _dot(d.astype(BF16), pw) * ps


def _pool_tile(cur, j, pw_ref, ps_ref, o_ref, carry_ref):
    tt = cur.shape[0]

    @pl.when(j == 0)
    def _():
        carry_ref[...] = jnp.zeros_like(carry_ref)

    ext = jnp.concatenate([carry_ref[...], cur], axis=0)
    carry_ref[...] = cur[tt - POOL_HALO:, :]
    pos = (j * tt + lax.broadcasted_iota(I32, (tt, 1), 0)).astype(F32)
    gw = cur.shape[1] // len(POOL_WINDOWS)
    for g, w in enumerate(POOL_WINDOWS):
        cols = slice(g * gw, (g + 1) * gw)
        e = ext[:, cols]
        s, span = e, 1
        while span < w:
            s = s + pltpu.roll(s, span, axis=0)
            span *= 2
        cnt = jnp.minimum(float(w), pos + 1.0)
        out = _pool_group(s[POOL_HALO:, :], e[POOL_HALO:, :], cnt, pw_ref[g], ps_ref[:, cols])
        o_ref[:, cols] = out.astype(o_ref.dtype)


def _pool_sample_kernel(sp_ref, u_ref, pw_ref, ps_ref, o_ref, np_ref, *, n_new, n_buf):
    rows = [sp_ref[r] for r in range(n_buf)] + [u_ref[r] for r in range(n_new)]
    for r in range(n_buf):
        np_ref[r] = rows[n_new + r]
    gw = rows[0].shape[1] // len(POOL_WINDOWS)
    for t in range(n_new):
        i = n_buf + t
        for g, w in enumerate(POOL_WINDOWS):
            cols = slice(g * gw, (g + 1) * gw)
            s = rows[i][:, cols]
            for back in range(1, w):
                s = s + rows[i - back][:, cols]
            out = _pool_group(s, rows[i][:, cols], float(w), pw_ref[g], ps_ref[:, cols])
            o_ref[t, :, cols] = out


def _pool_sample(state_pool_t, u_t, pool_w_b, pool_scale):
    n_buf, bs, pw = state_pool_t.shape
    n_new = u_t.shape[0]
    full = lambda a: pl.BlockSpec(a.shape, lambda i: (0,) * a.ndim)
    return pl.pallas_call(
        functools.partial(_pool_sample_kernel, n_new=n_new, n_buf=n_buf),
        grid=(1,),
        in_specs=[full(state_pool_t), full(u_t), full(pool_w_b), full(pool_scale)],
        out_specs=[pl.BlockSpec((n_new, bs, pw), lambda i: (0, 0, 0)),
                   pl.BlockSpec((n_buf, bs, pw), lambda i: (0, 0, 0))],
        out_shape=[jax.ShapeDtypeStruct((n_new, bs, pw), F32),
                   jax.ShapeDtypeStruct((n_buf, bs, pw), F32)],
        compiler_params=_cparams("arbitrary"),
        name="pool_sample",
    )(state_pool_t, u_t, pool_w_b, pool_scale)


def _lambda_full(lq1, lk1, lq2, lk2, lam_init):
    e1 = jnp.exp(jnp.sum(lq1 * lk1, axis=-1, keepdims=True))
    e2 = jnp.exp(jnp.sum(lq2 * lk2, axis=-1, keepdims=True))
    return e1 - e2 + lam_init


def _split_halves(q):
    lane = lax.broadcasted_iota(I32, q.shape, 1)
    zero = jnp.zeros_like(q)
    return jnp.concatenate([jnp.where(lane < HEAD_DIM, q, zero), jnp.where(lane >= HEAD_DIM, q, zero)], axis=0)


def _subln(o, g, lam_init):
    return _rms(o, g) * (1.0 - lam_init)


def _attn_prompt_kernel(slopes_ref, q_ref, k_ref, vt_ref, lq1_ref, lk1_ref, lq2_ref, lk2_ref, g_ref, o_ref,
                        *, tq, lam_init, heads):
    hg = pl.program_id(1)
    i = pl.program_id(2)
    lam = _lambda_full(lq1_ref[...], lk1_ref[...], lq2_ref[...], lk2_ref[...], lam_init)
    head_cols = [slice(hh * HEAD_W, (hh + 1) * HEAD_W) for hh in range(heads)]
    row = lax.broadcasted_iota(I32, (KEY_TILES * tq, LANES), 0)
    lane = lax.broadcasted_iota(I32, (KEY_TILES * tq, LANES), 1)
    pos_feat = jnp.where(lane < 2 * N_SPLIT, jnp.where(lane % 2 == 0, row & ~(POS_LOW - 1), row & (POS_LOW - 1)), 0)
    pos_feat = pos_feat.astype(F32).astype(BF16)
    qlane = lax.broadcasted_iota(I32, (2 * tq, LANES), 1)
    slopes2, qqs = [], []
    for hh in range(heads):
        s2 = slopes_ref[hg * heads + hh] * LOG2E
        slopes2.append(s2)
        rest = jnp.full((2 * tq, LANES), s2, F32)
        feat = jnp.zeros((2 * tq, LANES), F32)
        for part in range(N_SPLIT):
            piece = rest.astype(BF16).astype(F32)
            rest = rest - piece
            feat = jnp.where(qlane // 2 == part, piece, feat)
        qq = _split_halves((q_ref[:, head_cols[hh]] * LOG2E).astype(BF16))
        qqs.append(jnp.concatenate([qq, feat.astype(BF16)], axis=1))
    def head_step(hh, j, tiles, carry, diag_tile):
        m, acc = carry
        n_keys = tiles * tq
        start = pl.multiple_of(j * tq, tq)
        keys = jnp.concatenate([k_ref[pl.ds(start, n_keys), head_cols[hh]], pos_feat[:n_keys]], axis=1)
        st = _dot_nt(keys, qqs[hh])
        if diag_tile is not None:
            kr = lax.broadcasted_iota(I32, st.shape, 0) - diag_tile * tq
            qc = lax.broadcasted_iota(I32, st.shape, 1)
            qc = jnp.where(qc >= tq, qc - tq, qc)
            st = jnp.where(qc >= kr, st, NEG_INF)
        offset = slopes2[hh] * ((j - i) * tq).astype(F32)
        m_new = jnp.maximum(m, jnp.max(st, axis=0, keepdims=True) + offset)
        alpha = jnp.exp2(m - m_new)
        p = jnp.exp2(st - (m_new - offset)).astype(BF16)
        values = jnp.concatenate([vt_ref[j + t, head_cols[hh], :] for t in range(tiles)], axis=1)
        values = jnp.concatenate([values, jnp.ones((BF16_ROWS, n_keys), BF16)], axis=0)
        acc = alpha * acc + _dot(values, p)
        return m_new, acc

    def step(j, tiles, carries, diag_tile):
        return tuple(head_step(hh, j, tiles, carries[hh], diag_tile) for hh in range(heads))

    init = (jnp.full((1, 2 * tq), NEG_INF, F32), jnp.zeros((HEAD_W + BF16_ROWS, 2 * tq), F32))
    carries = lax.fori_loop(0, i // KEY_TILES, lambda g, cr: step(g * KEY_TILES, KEY_TILES, cr, None),
                            (init,) * heads)
    rest = i % KEY_TILES
    carries = lax.switch(rest, [functools.partial(step, i - r, r + 1, diag_tile=r) for r in range(KEY_TILES)],
                         carries)
    for hh, (_, acc) in enumerate(carries):
        o = acc[:HEAD_W] / acc[HEAD_W:HEAD_W + 1]
        o = (o[:, :tq] - lam * o[:, tq:]).T
        o_ref[:, head_cols[hh]] = _subln(o, g_ref[...], lam_init).astype(o_ref.dtype)


def _attn_prompt(q, kb, vt, slopes, lams, subln_g, batch, seq, n_heads, lam_init, tq):
    heads = math.gcd(n_heads, ATTN_HEADS)
    width = heads * HEAD_W
    q3, k3 = (a.reshape(batch, seq, n_heads * HEAD_W) for a in (q, kb))
    small = lambda a: pl.BlockSpec(a.shape, lambda b, h, i, s: (0,) * a.ndim)
    grid_spec = pltpu.PrefetchScalarGridSpec(
        num_scalar_prefetch=1,
        grid=(batch, n_heads // heads, seq // tq),
        in_specs=[pl.BlockSpec((None, tq, width), lambda b, h, i, s: (b, i, h)),
                  pl.BlockSpec((None, seq, width), lambda b, h, i, s: (b, 0, h)),
                  pl.BlockSpec((seq // tq, width, tq), lambda b, h, i, s: (b, h, 0)),
                  *[small(a) for a in lams], small(subln_g)],
        out_specs=pl.BlockSpec((None, tq, width), lambda b, h, i, s: (b, i, h)),
    )
    out = pl.pallas_call(
        functools.partial(_attn_prompt_kernel, tq=tq, lam_init=lam_init, heads=heads),
        grid_spec=grid_spec,
        out_shape=jax.ShapeDtypeStruct((batch, seq, n_heads * HEAD_W), BF16),
        compiler_params=_cparams("parallel", "parallel", "arbitrary"),
        name="attn_prompt",
    )(slopes, q3, k3, vt, *lams, subln_g)
    return out.reshape(batch * seq, n_heads * HEAD_W)


def _attn_sample_kernel(pt_ref, slopes_ref, q_ref, kn_ref, vn_ref, kc_ref, vc_ref,
                        lq1_ref, lk1_ref, lq2_ref, lk2_ref, g_ref, o_ref, kbuf, vbuf, sem,
                        *, n_seqs, per_step, n_pages, page_rows, n_heads, n_new, lam_init):
    step = pl.program_id(0)
    lam = _lambda_full(lq1_ref[...], lk1_ref[...], lq2_ref[...], lk2_ref[...], lam_init)

    def page_copies(seq, sl, pg):
        p = pt_ref[seq * n_pages + pg]
        dst = pl.ds(pg * page_rows, page_rows)
        return (pltpu.make_async_copy(kc_ref.at[p], kbuf.at[sl, dst, :], sem.at[sl, 0]),
                pltpu.make_async_copy(vc_ref.at[p], vbuf.at[sl, dst, :], sem.at[sl, 1]))

    def start_seq(seq, sl):
        for pg in range(n_pages):
            for cp in page_copies(seq, sl, pg):
                cp.start()

    @pl.when(step == 0)
    def _():
        start_seq(0, 0)

    for u in range(per_step):
        seq = step * per_step + u
        slot = seq % 2

        @pl.when(seq + 1 < n_seqs)
        def _():
            start_seq(seq + 1, 1 - slot)

        for pg in range(n_pages):
            for cp in page_copies(seq, slot, pg):
                cp.wait()
        _sample_seq(slot, lam, slopes_ref, q_ref.at[u], kn_ref.at[u], vn_ref.at[u], kbuf, vbuf, g_ref, o_ref.at[u],
                    past=n_pages * page_rows // n_heads, n_heads=n_heads, n_new=n_new, lam_init=lam_init)


def _sample_seq(slot, lam, slopes_ref, q_ref, kn_ref, vn_ref, kbuf, vbuf, g_ref, o_ref, *, past, n_heads, n_new,
                lam_init):
    grp = 2 * SUBLANES
    rows = SAMPLE_HEADS * grp
    r = lax.broadcasted_iota(I32, (rows, past), 0) % grp
    t_past = jnp.where(r >= n_new, r - n_new, r)
    dist_past = (past + t_past - lax.broadcasted_iota(I32, (rows, past), 1)).astype(F32)
    r1 = lax.broadcasted_iota(I32, (rows, 1), 0)
    t_new = jnp.where(r1 % grp >= n_new, r1 % grp - n_new, r1 % grp)
    pad = jnp.zeros((grp - 2 * n_new, HEAD_W), F32)
    for h0 in range(0, n_heads, SAMPLE_HEADS):
        heads = range(h0, h0 + SAMPLE_HEADS)
        cols = slice(h0 * HEAD_W, (h0 + SAMPLE_HEADS) * HEAD_W)
        slope = jnp.concatenate([jnp.full((grp, 1), slopes_ref[h], F32) for h in heads], axis=0)
        blocks = []
        for i, h in enumerate(heads):
            qh = jnp.concatenate([_split_halves(q_ref[:, h * HEAD_W:(h + 1) * HEAD_W]), pad], axis=0)
            zero = jnp.zeros_like(qh)
            blocks.append(jnp.concatenate([qh if j == i else zero for j in range(SAMPLE_HEADS)], axis=1))
        qq = jnp.concatenate(blocks, axis=0)
        kh = jnp.concatenate([kbuf[slot, pl.ds(h, past, stride=n_heads), :] for h in heads], axis=1).astype(BF16)
        vh = jnp.concatenate([vbuf[slot, pl.ds(h, past, stride=n_heads), :] for h in heads], axis=1).astype(BF16)
        s = _dot_nt(qq.astype(BF16), kh) - slope * dist_past
        kn = kn_ref[:, cols]
        vn = vn_ref[:, cols]
        s_new = []
        for c in range(n_new):
            sc = jnp.sum(qq * kn[c:c + 1, :], axis=-1, keepdims=True) - slope * (t_new - c).astype(F32)
            s_new.append(jnp.where(t_new >= c, sc, NEG_INF))
        m = jnp.max(s, axis=-1, keepdims=True)
        for sc in s_new:
            m = jnp.maximum(m, sc)
        p = jnp.exp(s - m)
        l = jnp.sum(p, axis=-1, keepdims=True)
        acc = _dot(p.astype(BF16), vh)
        for c, sc in enumerate(s_new):
            pc = jnp.exp(sc - m)
            l = l + pc
            acc = acc + pc * vn[c:c + 1, :]
        o = acc / l
        for i, h in enumerate(heads):
            oh = o[i * grp:(i + 1) * grp, i * HEAD_W:(i + 1) * HEAD_W]
            oh = oh[0:n_new] - lam * oh[n_new:2 * n_new]
            o_ref[:, h * HEAD_W:(h + 1) * HEAD_W] = _subln(oh, g_ref[...], lam_init)


def _attn_sample(q3, kn3, vn3, kc, vc, page_table, slopes, lams, subln_g, n_heads, lam_init):
    bs, n_new, qk_w = q3.shape
    n_pages = page_table.shape[1]
    page_rows = kc.shape[1]
    per_step = math.gcd(bs, SAMPLE_SEQS)
    small = lambda a: pl.BlockSpec(a.shape, lambda b, pt, s: (0,) * a.ndim)
    tok = pl.BlockSpec((per_step, n_new, qk_w), lambda b, pt, s: (b, 0, 0))
    grid_spec = pltpu.PrefetchScalarGridSpec(
        num_scalar_prefetch=2,
        grid=(bs // per_step,),
        in_specs=[tok, tok, tok, pl.BlockSpec(memory_space=pl.ANY), pl.BlockSpec(memory_space=pl.ANY),
                  *[small(a) for a in lams], small(subln_g)],
        out_specs=tok,
        scratch_shapes=[pltpu.VMEM((2, n_pages * page_rows, LANES), F32),
                        pltpu.VMEM((2, n_pages * page_rows, LANES), F32),
                        pltpu.SemaphoreType.DMA((2, 2))],
    )
    return pl.pallas_call(
        functools.partial(_attn_sample_kernel, n_seqs=bs, per_step=per_step, n_pages=n_pages, page_rows=page_rows,
                          n_heads=n_heads, n_new=n_new, lam_init=lam_init),
        grid_spec=grid_spec,
        out_shape=jax.ShapeDtypeStruct((bs, n_new, qk_w), F32),
        compiler_params=_cparams("arbitrary"),
        name="attn_sample",
    )(page_table.reshape(-1), slopes, q3, kn3, vn3, kc, vc, *lams, subln_g)


def _outproj_kernel(pool_p_ref, o_p_ref, x_p_ref, pool_s_ref, o_s_ref, x_s_ref, wo_ref, g2_ref, rwt_ref, rb_ref,
                    earlier_ref, h_ref, hn_ref, code_ref, gate_ref, cnt_out_ref, run_ref, *, tm, n_experts,
                    prompt_tiles):
    i = pl.program_id(0)

    @pl.when(i == 0)
    def _():
        run_ref[...] = jnp.zeros_like(run_ref)

    tile = functools.partial(_outproj_tile, wo_ref=wo_ref, g2_ref=g2_ref, rwt_ref=rwt_ref, rb_ref=rb_ref,
                             earlier_ref=earlier_ref, h_ref=h_ref, hn_ref=hn_ref, code_ref=code_ref, gate_ref=gate_ref,
                             cnt_out_ref=cnt_out_ref, run_ref=run_ref, tm=tm, n_experts=n_experts)
    pl.when(i < prompt_tiles)(functools.partial(tile, pool_p_ref, o_p_ref, x_p_ref))
    pl.when(i >= prompt_tiles)(functools.partial(tile, pool_s_ref, o_s_ref, x_s_ref))


def _outproj_tile(pool_ref, o_ref, x_ref, *, wo_ref, g2_ref, rwt_ref, rb_ref, earlier_ref, h_ref, hn_ref, code_ref,
                  gate_ref, cnt_out_ref, run_ref, tm, n_experts):
    pw = pool_ref.shape[1]
    mix = _dot(pool_ref[...].astype(BF16), wo_ref[0:pw, :]) + _dot(o_ref[...].astype(BF16), wo_ref[pw:, :])
    h = x_ref[...] + mix
    h_ref[...] = h
    hn = _rms(h, g2_ref[...])
    for s in range(hn.shape[1] // LANES):
        hn_ref[pl.ds(s, tm, stride=SUBLANES), :] = hn[:, s * LANES:(s + 1) * LANES]
    logits = _dot_nt(rwt_ref[...], hn.astype(BF16)) + rb_ref[...]
    expert = lax.broadcasted_iota(I32, logits.shape, 0)
    vals, idxs = [], []
    for _ in range(TOP_K):
        m = jnp.max(logits, axis=0, keepdims=True)
        idx = jnp.min(jnp.where(logits == m, expert, n_experts), axis=0, keepdims=True)
        vals.append(m)
        idxs.append(idx)
        logits = jnp.where(expert == idx, -jnp.inf, logits)
    ex = [jnp.exp(v - vals[0]) for v in vals]
    den = ex[0]
    for e in ex[1:]:
        den = den + e
    chosen = jnp.zeros(logits.shape, F32)
    for idx in idxs:
        chosen = chosen + jnp.where(expert == idx, 1.0, 0.0)
    before = _dot(chosen.astype(BF16), earlier_ref[...]) + run_ref[...]
    for k in range(TOP_K):
        gate_ref[k:k + 1, :] = ex[k] / den
        rank = jnp.sum(jnp.where(expert == idxs[k], before, 0.0), axis=0, keepdims=True).astype(I32)
        code_ref[k:k + 1, :] = idxs[k] * (1 << RANK_BITS) + rank
    gate_ref[TOP_K:, :] = jnp.zeros((gate_ref.shape[0] - TOP_K, tm), F32)
    run_ref[...] = run_ref[...] + jnp.sum(chosen, axis=1, keepdims=True)
    cnt_out_ref[...] = run_ref[...]


def _outproj(prompt, sample, w_out_b, g2, rwt_b, rb):
    np_, d = prompt[2].shape
    ns_ = sample[2].shape[0]
    n_total = np_ + ns_
    n_experts = rwt_b.shape[0]
    tm = min(ROW_TILE, math.gcd(np_, ns_))
    prompt_tiles = np_ // tm
    assert d == SUBLANES * LANES and n_total < (1 << RANK_BITS)
    first = lambda a: pl.BlockSpec((tm, a.shape[1]), lambda i: (jnp.minimum(i, prompt_tiles - 1), 0))
    second = lambda a: pl.BlockSpec((tm, a.shape[1]), lambda i: (jnp.maximum(i - prompt_tiles, 0), 0))
    row = lambda w: pl.BlockSpec((tm, w), lambda i: (i, 0))
    col = lambda r: pl.BlockSpec((r, tm), lambda i: (0, i))
    full = lambda a: pl.BlockSpec(a.shape, lambda i: (0,) * a.ndim)
    counts = jax.ShapeDtypeStruct((n_experts, 1), F32)
    earlier = jnp.triu(jnp.ones((tm, tm), BF16), 1)
    return pl.pallas_call(
        functools.partial(_outproj_kernel, tm=tm, n_experts=n_experts, prompt_tiles=prompt_tiles),
        grid=(n_total // tm,),
        in_specs=[*[first(a) for a in prompt], *[second(a) for a in sample],
                  full(w_out_b), full(g2), full(rwt_b), full(rb), full(earlier)],
        out_specs=[row(d), pl.BlockSpec((tm * SUBLANES, LANES), lambda i: (i, 0)),
                   col(TOP_K), col(SUBLANES), pl.BlockSpec(counts.shape, lambda i: (0, 0))],
        out_shape=[jax.ShapeDtypeStruct((n_total, d), F32), jax.ShapeDtypeStruct((n_total * SUBLANES, LANES), F32),
                   jax.ShapeDtypeStruct((TOP_K, n_total), I32), jax.ShapeDtypeStruct((SUBLANES, n_total), F32),
                   counts],
        scratch_shapes=[pltpu.VMEM(counts.shape, F32)],
        compiler_params=_cparams("arbitrary"),
        name="outproj_router",
    )(*prompt, *sample, w_out_b, g2, rwt_b, rb, earlier)


def _sorted_rows_kernel(start_ref, code_ref, rows_ref, *, n_experts):
    code = code_ref[...]
    expert = code >> RANK_BITS
    rows = code & ((1 << RANK_BITS) - 1)
    for e in range(n_experts):
        rows = rows + jnp.where(expert == e, start_ref[e], 0)
    rows_ref[...] = rows


def _sorted_rows(starts, code):
    return pl.pallas_call(
        functools.partial(_sorted_rows_kernel, n_experts=starts.shape[0]),
        grid=(1,),
        in_specs=[pl.BlockSpec(memory_space=pltpu.SMEM), pl.BlockSpec(code.shape, lambda i: (0, 0))],
        out_specs=pl.BlockSpec(code.shape, lambda i: (0, 0)),
        out_shape=jax.ShapeDtypeStruct(code.shape, I32),
        compiler_params=_cparams("arbitrary"),
        name="sorted_rows",
    )(starts, code)


MXU_DIM = 256


def _split_w1_tile(w_ref, g_ref, l_ref):
    r = lax.broadcasted_iota(I32, (MXU_DIM, MXU_DIM), 0)
    c = lax.broadcasted_iota(I32, (MXU_DIM, MXU_DIM), 1)
    half = MXU_DIM // 2
    source = jnp.where(c < half, 2 * c, 2 * (c - half) + 1)
    perm = jnp.where(r == source, 1.0, 0.0).astype(BF16)
    for blk in range(w_ref.shape[1] // MXU_DIM):
        res = _dot(w_ref[:, blk * MXU_DIM:(blk + 1) * MXU_DIM].astype(BF16), perm)
        g_ref[:, blk * half:(blk + 1) * half] = res[:, :half].astype(BF16)
        l_ref[:, blk * half:(blk + 1) * half] = res[:, half:].astype(BF16)


def _dispatch_kernel(row_ref, zlo_ref, zhi_ref, nu_ref, hn_ref, w1_ref, xs_ref, w1g_ref, w1l_ref, zero_ref, sem,
                     *, tm, n_total, rows, n_experts, n_blocks):
    i = pl.program_id(0)
    tile = tm * SUBLANES
    block = rows * SUBLANES

    def pad_pieces(e):
        lo, n = zlo_ref[e], zhi_ref[e] - zlo_ref[e]
        for bit in range((rows - 1).bit_length()):
            size = 1 << bit
            below = n & (size - 1)
            start = pl.multiple_of((lo + below) * SUBLANES, SUBLANES)
            cp = pltpu.make_async_copy(zero_ref.at[pl.ds(0, size * SUBLANES), :],
                                       xs_ref.at[pl.ds(start, size * SUBLANES), :], sem.at[1])
            yield (n & size) != 0, cp

    def tail_copy(blk):
        start = pl.multiple_of(blk * block, block)
        return pltpu.make_async_copy(zero_ref, xs_ref.at[pl.ds(start, block), :], sem.at[1])

    def for_zero_copies(fn):
        def per_expert(e, carry):
            for used, cp in pad_pieces(e):
                pl.when(used)(functools.partial(fn, cp))
            return carry
        lax.fori_loop(0, n_experts, per_expert, 0)

        def per_block(blk, carry):
            fn(tail_copy(blk))
            return carry
        lax.fori_loop(nu_ref[0], n_blocks, per_block, 0)

    @pl.when(i == 0)
    def _():
        zero_ref[...] = jnp.zeros_like(zero_ref)
        for_zero_copies(lambda cp: cp.start())

    def body(t, carry):
        src = pl.multiple_of(t * SUBLANES, SUBLANES)
        for k in range(TOP_K):
            dst = pl.multiple_of(row_ref[k * n_total + i * tm + t] * SUBLANES, SUBLANES)
            pltpu.make_async_copy(hn_ref.at[pl.ds(src, SUBLANES), :], xs_ref.at[pl.ds(dst, SUBLANES), :],
                                  sem.at[0]).start()
        return carry
    lax.fori_loop(0, tm, body, 0)

    _split_w1_tile(w1_ref, w1g_ref, w1l_ref)

    for _ in range(TOP_K):
        pltpu.make_async_copy(hn_ref, xs_ref.at[pl.ds(0, tile), :], sem.at[0]).wait()

    @pl.when(i == pl.num_programs(0) - 1)
    def _():
        for_zero_copies(lambda cp: cp.wait())


def _dispatch(sorted_rows, zlo, zhi, n_used, hn, w1, n_blocks):
    n_total = hn.shape[0] // SUBLANES
    n_exp, d, ff2 = w1.shape
    tr = next(t for t in (ROW_TILE, 2 * ROW_TILE, d) if d % t == 0 and n_total % (n_exp * (d // t)) == 0)
    row_tiles = d // tr
    steps = n_exp * row_tiles
    tm = n_total // steps
    rows = EXPERT_ROWS
    assert rows % SUBLANES == 0
    half = pl.BlockSpec((None, tr, ff2 // 2), lambda i, *_: (i // row_tiles, i % row_tiles, 0))
    grid_spec = pltpu.PrefetchScalarGridSpec(
        num_scalar_prefetch=4,
        grid=(steps,),
        in_specs=[pl.BlockSpec((tm * SUBLANES, LANES), lambda i, *_: (i, 0)),
                  pl.BlockSpec((None, tr, ff2), lambda i, *_: (i // row_tiles, i % row_tiles, 0))],
        out_specs=[pl.BlockSpec(memory_space=pl.ANY), half, half],
        scratch_shapes=[pltpu.VMEM((rows * SUBLANES, LANES), F32), pltpu.SemaphoreType.DMA((2,))],
    )
    return pl.pallas_call(
        functools.partial(_dispatch_kernel, tm=tm, n_total=n_total, rows=rows, n_experts=n_exp, n_blocks=n_blocks),
        grid_spec=grid_spec,
        out_shape=[jax.ShapeDtypeStruct((n_blocks * rows * SUBLANES, LANES), F32),
                   jax.ShapeDtypeStruct((n_exp, d, ff2 // 2), BF16), jax.ShapeDtypeStruct((n_exp, d, ff2 // 2), BF16)],
        compiler_params=_cparams("arbitrary"),
        name="dispatch_split_w1",
    )(sorted_rows, zlo, zhi, n_used, hn, w1)


def _expert_kernel(be_ref, nu_ref, xs_ref, w1g_ref, w1l_ref, b1g_ref, b1l_ref, w2_ref, b2_ref, y_ref, *, rows):
    j = pl.program_id(0)
    n_used = nu_ref[0]

    @pl.when(j < n_used)
    def _():
        x = jnp.concatenate([xs_ref[pl.ds(s, rows, stride=SUBLANES), :] for s in range(SUBLANES)],
                            axis=-1).astype(BF16)
        glu = jnp.minimum(_dot(x, w1g_ref[...]) + b1g_ref[...], SWIGLU_LIMIT)
        lin = jnp.clip(_dot(x, w1l_ref[...]) + b1l_ref[...], -SWIGLU_LIMIT, SWIGLU_LIMIT)
        act = glu * jax.nn.sigmoid(SWIGLU_ALPHA * glu) * (lin + 1.0)
        y = _dot(act.astype(BF16), w2_ref[...].astype(BF16)) + b2_ref[...]
        for s in range(SUBLANES):
            y_ref[pl.ds(s, rows, stride=SUBLANES), :] = y[:, s * LANES:(s + 1) * LANES]

    @pl.when(j >= n_used)
    def _():
        y_ref[...] = jnp.zeros_like(y_ref)


def _experts(block_e, n_used, xs, w1g, w1l, b1g, b1l, w2, b2, n_blocks):
    rows = EXPERT_ROWS
    tile = rows * SUBLANES
    wspec = lambda a: pl.BlockSpec((None,) + a.shape[1:], lambda j, be, nu: (be[j], 0, 0))
    grid_spec = pltpu.PrefetchScalarGridSpec(
        num_scalar_prefetch=2,
        grid=(n_blocks,),
        in_specs=[pl.BlockSpec((tile, LANES), lambda j, be, nu: (jnp.minimum(j, nu[0] - 1), 0)),
                  wspec(w1g), wspec(w1l), wspec(b1g), wspec(b1l), wspec(w2), wspec(b2)],
        out_specs=pl.BlockSpec((tile, LANES), lambda j, be, nu: (j, 0)),
    )
    return pl.pallas_call(
        functools.partial(_expert_kernel, rows=rows),
        grid_spec=grid_spec,
        out_shape=jax.ShapeDtypeStruct((n_blocks * tile, LANES), F32),
        compiler_params=_cparams("arbitrary"),
        name="experts",
    )(block_e, n_used, xs, w1g, w1l, b1g, b1l, w2, b2)


def _combine_kernel(row_ref, yb_ref, gate_ref, h_ref, g_ref, yp_ref, ys_ref, cbuf, sem, *, tm, n_total,
                    prompt_tiles):
    i = pl.program_id(0)
    n = pl.num_programs(0)
    slot = i % 2
    tile = tm * SUBLANES

    def gather(blk, sl):
        def body(t2, carry):
            for t in (2 * t2, 2 * t2 + 1):
                for k in range(TOP_K):
                    src = pl.multiple_of(row_ref[k * n_total + blk * tm + t] * SUBLANES, SUBLANES)
                    dst = pl.multiple_of((k * tm + t) * SUBLANES, SUBLANES)
                    pltpu.make_async_copy(yb_ref.at[pl.ds(src, SUBLANES), :], cbuf.at[sl, pl.ds(dst, SUBLANES), :],
                                          sem.at[sl]).start()
            return carry
        lax.fori_loop(0, tm // 2, body, 0)

    @pl.when(i == 0)
    def _():
        gather(0, 0)

    @pl.when(i + 1 < n)
    def _():
        gather(i + 1, 1 - slot)

    pltpu.make_async_copy(yb_ref.at[pl.ds(0, TOP_K * tile), :], cbuf.at[slot], sem.at[slot]).wait()
    gates = gate_ref[...].T
    slabs = []
    for s in range(SUBLANES):
        acc = None
        for k in range(TOP_K):
            part = gates[:, k:k + 1] * cbuf[slot, pl.ds(k * tile + s, tm, stride=SUBLANES), :]
            acc = part if acc is None else acc + part
        slabs.append(acc)
    hf = h_ref[...] + jnp.concatenate(slabs, axis=-1)
    y = _rms(hf, g_ref[...])

    @pl.when(i < prompt_tiles)
    def _():
        yp_ref[...] = y

    @pl.when(i >= prompt_tiles)
    def _():
        ys_ref[...] = y


def _combine(rows, yb, gates, h, gf, np_):
    n, d = h.shape
    ns_ = n - np_
    tm = min(COMBINE_TILE, math.gcd(np_, ns_))
    prompt_tiles = np_ // tm
    row = lambda w: pl.BlockSpec((tm, w), lambda i, *_: (i, 0))
    grid_spec = pltpu.PrefetchScalarGridSpec(
        num_scalar_prefetch=1,
        grid=(n // tm,),
        in_specs=[pl.BlockSpec(memory_space=pl.ANY), pl.BlockSpec((gates.shape[0], tm), lambda i, *_: (0, i)), row(d),
                  pl.BlockSpec(gf.shape, lambda i, *_: (0, 0))],
        out_specs=[pl.BlockSpec((tm, d), lambda i, *_: (jnp.minimum(i, prompt_tiles - 1), 0)),
                   pl.BlockSpec((tm, d), lambda i, *_: (jnp.maximum(i - prompt_tiles, 0), 0))],
        scratch_shapes=[pltpu.VMEM((2, TOP_K * tm * SUBLANES, LANES), F32), pltpu.SemaphoreType.DMA((2,))],
    )
    return pl.pallas_call(
        functools.partial(_combine_kernel, tm=tm, n_total=n, prompt_tiles=prompt_tiles),
        grid_spec=grid_spec,
        out_shape=[jax.ShapeDtypeStruct((np_, d), F32), jax.ShapeDtypeStruct((ns_, d), F32)],
        compiler_params=_cparams("arbitrary"),
        name="combine_norm",
    )(rows, yb, gates, h, gf)


def kernel(x_prompt, x_sample, state_pool, cache_k, cache_v, page_table, norm1_g, w_in, pool_w, pool_scale,
           lambda_q1, lambda_k1, lambda_q2, lambda_k2, subln_g, w_out, norm2_g, router_w, router_b,
           w1, b1, w2, b2, normf_g):
    batch, seq, d = x_prompt.shape
    bs, n_new, _ = x_sample.shape
    depth = w_in.shape[0]
    assert depth == 1, "single-layer trunk"
    n_heads = cache_k.shape[3]
    page_size = cache_k.shape[2]
    pw = pool_w.shape[1] * pool_w.shape[2]
    qk_w = n_heads * HEAD_W
    n_experts = router_w.shape[2]
    np_, ns_ = batch * seq, bs * n_new
    n_total = np_ + ns_
    lam_init = 0.8 - 0.6 * math.exp(-0.3 * 0)

    g1 = norm1_g[0][None]
    w_in_b = w_in[0].astype(BF16)
    wvt_b = w_in_b[:, pw + 2 * qk_w:].T
    pool_w_b = pool_w[0].astype(BF16)
    ps = pool_scale[0][None]
    lams = [a[0][None] for a in (lambda_q1, lambda_k1, lambda_q2, lambda_k2)]
    sg = subln_g[0][None]
    slopes = jnp.exp2(-8.0 * (jnp.arange(n_heads, dtype=F32) + 1.0) / n_heads)
    w_out_b = w_out[0].astype(BF16)
    g2 = norm2_g[0][None]
    rwt_b = router_w[0].T.astype(BF16)
    rb = router_b[0][:, None]
    b1_pairs = b1[0].T.reshape(-1, 2, n_experts)
    b1g = b1_pairs[:, 0, :].T[:, None, :]
    b1l = b1_pairs[:, 1, :].T[:, None, :]
    w2b = w2.reshape(w2.shape[1:])
    b2e = b2[0][:, None, :]
    gf = normf_g[None]

    xp = x_prompt.reshape(np_, d)
    tq = min(ATTN_TILE, seq)
    pool_p, tail_p, q_p, k_p, v_p, kb_p, vt_p = _inproj(xp, g1, w_in_b, wvt_b, pw, qk_w,
                                                        prompt=(pool_w_b, ps, seq, tq))
    o_p = _attn_prompt(q_p, kb_p, vt_p, slopes, lams, sg, batch, seq, n_heads, lam_init, tq)

    xs = x_sample.reshape(ns_, d)
    u_s, q_s, k_s, v_s = _inproj(xs, g1, w_in_b, wvt_b, pw, qk_w)
    pool_s, new_pool_s = _pool_sample(jnp.transpose(state_pool[0], (1, 0, 2)),
                                      jnp.transpose(u_s.reshape(bs, n_new, pw), (1, 0, 2)), pool_w_b, ps)
    pool_s = jnp.transpose(pool_s, (1, 0, 2))
    new_pool_s = jnp.transpose(new_pool_s, (1, 0, 2))
    kc = cache_k.reshape(cache_k.shape[1], page_size * n_heads, HEAD_W)
    vc = cache_v.reshape(cache_v.shape[1], page_size * n_heads, HEAD_W)
    o_s = _attn_sample(q_s.reshape(bs, n_new, qk_w), k_s.reshape(bs, n_new, qk_w), v_s.reshape(bs, n_new, qk_w),
                       kc, vc, page_table, slopes, lams, sg, n_heads, lam_init)

    h, hn, code, gates, cnt = _outproj((pool_p, o_p, xp), (pool_s.reshape(ns_, pw), o_s.reshape(ns_, qk_w), xs),
                                       w_out_b, g2, rwt_b, rb)

    counts = cnt[:, 0].astype(I32)
    n_blocks = -(-n_total * TOP_K // EXPERT_ROWS) + n_experts
    padded = (counts + EXPERT_ROWS - 1) // EXPERT_ROWS * EXPERT_ROWS
    pad_ends = jnp.cumsum(padded)
    pad_starts = pad_ends - padded
    n_used = (pad_ends[-1:] // EXPERT_ROWS).astype(I32)
    block_start = jnp.arange(n_blocks, dtype=I32) * EXPERT_ROWS
    block_e = jnp.minimum(jnp.sum((pad_ends[None, :] <= block_start[:, None]).astype(I32), axis=1), n_experts - 1)

    sorted_rows = _sorted_rows(pad_starts, code).reshape(-1)
    xs_rows, w1g, w1l = _dispatch(sorted_rows, pad_starts + counts, pad_ends, n_used, hn, w1.reshape(w1.shape[1:]),
                                  n_blocks)
    yb = _experts(block_e, n_used, xs_rows, w1g, w1l, b1g, b1l, w2b, b2e, n_blocks)
    y_p, y_s = _combine(sorted_rows, yb, gates, h, gf, np_)

    return (y_p.reshape(batch, seq, d), y_s.reshape(bs, n_new, d),
            tail_p[None, :, POOL_HALO - state_pool.shape[2]:],
            new_pool_s[None],
            k_p.reshape(1, batch, seq, n_heads, HEAD_W), v_p.reshape(1, batch, seq, n_heads, HEAD_W),
            k_s.reshape(1, bs, n_new, n_heads, HEAD_W), v_s.reshape(1, bs, n_new, n_heads, HEAD_W))
```

```python
import functools
import math

import jax
import jax.numpy as jnp
from jax import lax
from jax.experimental import pallas as pl
from jax.experimental.pallas import tpu as pltpu

F32 = jnp.float32
BF16 = jnp.bfloat16
I32 = jnp.int32

RMS_EPS = 1e-6
NEG_INF = -1e30
HEAD_DIM = 64
HEAD_W = 2 * HEAD_DIM
POOL_WINDOWS = (2, 4, 8, 16)
POOL_HALO = 16
TOP_K = 4
RANK_BITS = 20
SWIGLU_LIMIT = 7.0
SWIGLU_ALPHA = 1.702
LANES = 128
SUBLANES = 8
ROW_TILE = 512
INPROJ_TILE = 1024
ATTN_TILE = 512
SAMPLE_HEADS = 2
SAMPLE_SEQS = 4
KEY_TILES = 1
ATTN_HEADS = 4
LOG2E = 1.4426950408889634
N_SPLIT = 3
POS_LOW = 16
BF16_ROWS = 16
EXPERT_ROWS = 768
COMBINE_TILE = 512
VMEM_LIMIT = 48 * 1024 * 1024


def _cparams(*sem):
    return pltpu.CompilerParams(dimension_semantics=sem, vmem_limit_bytes=VMEM_LIMIT)


def _dot(a, b):
    return jnp.dot(a, b, preferred_element_type=F32)


def _dot_nt(a, b):
    return lax.dot_general(a, b, (((1,), (1,)), ((), ())), preferred_element_type=F32)


def _rms(x, g):
    return x * lax.rsqrt(jnp.mean(x * x, axis=-1, keepdims=True) + RMS_EPS) * g


def _inproj_sample_kernel(x_ref, g_ref, w_ref, wvt_ref, u_ref, q_ref, k_ref, v_ref, *, pool_w, qk_w):
    del wvt_ref
    u = _inproj_tile(x_ref, g_ref, w_ref, q_ref, k_ref, v_ref, pool_w=pool_w, qk_w=qk_w)[0]
    u_ref[...] = u


def _inproj_prompt_kernel(x_ref, g_ref, w_ref, wvt_ref, pw_ref, ps_ref, pool_ref, tail_ref, q_ref, k_ref, v_ref, kb_ref,
                          vt_ref, carry_ref, *, pool_w, qk_w, key_tile, seq_tiles):
    u, k, xb = _inproj_tile(x_ref, g_ref, w_ref, q_ref, k_ref, v_ref, pool_w=pool_w, qk_w=qk_w)
    _pool_tile(u, pl.program_id(0) % seq_tiles, pw_ref, ps_ref, pool_ref, carry_ref)
    tail_ref[...] = u[u.shape[0] - POOL_HALO:, :]
    kb_ref[...] = k.astype(BF16)
    vt = _dot_nt(wvt_ref[...], xb)
    for c in range(vt_ref.shape[0]):
        vt_ref[c] = vt[:, c * key_tile:(c + 1) * key_tile].astype(BF16)


def _inproj_tile(x_ref, g_ref, w_ref, q_ref, k_ref, v_ref, *, pool_w, qk_w):
    xb = _rms(x_ref[...], g_ref[...]).astype(BF16)
    c0, c1, c2 = pool_w, pool_w + qk_w, pool_w + 2 * qk_w
    u = _dot(xb, w_ref[:, 0:c0])
    q_ref[...] = _dot(xb, w_ref[:, c0:c1]) * (HEAD_DIM ** -0.5)
    k = _dot(xb, w_ref[:, c1:c2])
    v = _dot(xb, w_ref[:, c2:])
    n_heads = qk_w // HEAD_W
    tm = k.shape[0]
    for h in range(n_heads):
        k_ref[pl.ds(h, tm, stride=n_heads), :] = k[:, h * HEAD_W:(h + 1) * HEAD_W]
        v_ref[pl.ds(h, tm, stride=n_heads), :] = v[:, h * HEAD_W:(h + 1) * HEAD_W]
    return u, k, xb


def _inproj(x, g1, w_in_b, wvt_b, pool_w, qk_w, prompt=None):
    n, d = x.shape
    val_w = wvt_b.shape[0]
    tm = min(INPROJ_TILE, n if prompt is None else prompt[2])
    row = lambda w: pl.BlockSpec((tm, w), lambda i: (i, 0))
    full = lambda a: pl.BlockSpec(a.shape, lambda i: (0,) * a.ndim)
    n_heads = qk_w // HEAD_W
    assert val_w == qk_w and n % tm == 0
    heads = pl.BlockSpec((tm * n_heads, HEAD_W), lambda i: (i, 0))
    qkv_specs = [row(qk_w), heads, heads]
    qkv_shape = [jax.ShapeDtypeStruct((n, qk_w), F32), jax.ShapeDtypeStruct((n * n_heads, HEAD_W), F32),
                 jax.ShapeDtypeStruct((n * n_heads, HEAD_W), F32)]
    if prompt is None:
        return pl.pallas_call(
            functools.partial(_inproj_sample_kernel, pool_w=pool_w, qk_w=qk_w),
            grid=(n // tm,),
            in_specs=[row(d), full(g1), full(w_in_b), full(wvt_b)],
            out_specs=[row(pool_w)] + qkv_specs,
            out_shape=[jax.ShapeDtypeStruct((n, pool_w), F32)] + qkv_shape,
            compiler_params=_cparams("parallel"),
            name="inproj_sample",
        )(x, g1, w_in_b, wvt_b)
    pool_w_b, pool_scale, seq, key_tile = prompt
    assert tm % key_tile == 0 and seq % tm == 0 and tm >= POOL_HALO
    seq_tiles = seq // tm
    return pl.pallas_call(
        functools.partial(_inproj_prompt_kernel, pool_w=pool_w, qk_w=qk_w, key_tile=key_tile, seq_tiles=seq_tiles),
        grid=(n // tm,),
        in_specs=[row(d), full(g1), full(w_in_b), full(wvt_b), full(pool_w_b), full(pool_scale)],
        out_specs=[row(pool_w), pl.BlockSpec((None, POOL_HALO, pool_w), lambda i: (i // seq_tiles, 0, 0))] + qkv_specs
        + [row(qk_w), pl.BlockSpec((tm // key_tile, val_w, key_tile), lambda i: (i, 0, 0))],
        out_shape=[jax.ShapeDtypeStruct((n, pool_w), BF16), jax.ShapeDtypeStruct((n // seq, POOL_HALO, pool_w), F32)]
        + qkv_shape + [jax.ShapeDtypeStruct((n, qk_w), BF16),
                       jax.ShapeDtypeStruct((n // key_tile, val_w, key_tile), BF16)],
        scratch_shapes=[pltpu.VMEM((POOL_HALO, pool_w), F32)],
        compiler_params=_cparams("arbitrary"),
        name="inproj_pool_prompt",
    )(x, g1, w_in_b, wvt_b, pool_w_b, pool_scale)


def _pool_group(s_win, tok, cnt, pw, ps):
    d = s_win / cnt - tok
    return _dot(d.astype(BF16), pw) * ps


def _pool_tile(cur, j, pw_ref, ps_ref, o_ref, carry_ref):
    tt = cur.shape[0]

    @pl.when(j == 0)
    def _():
        carry_ref[...] = jnp.zeros_like(carry_ref)

    ext = jnp.concatenate([carry_ref[...], cur], axis=0)
    carry_ref[...] = cur[tt - POOL_HALO:, :]
    pos = (j * tt + lax.broadcasted_iota(I32, (tt, 1), 0)).astype(F32)
    gw = cur.shape[1] // len(POOL_WINDOWS)
    for g, w in enumerate(POOL_WINDOWS):
        cols = slice(g * gw, (g + 1) * gw)
        e = ext[:, cols]
        s, span = e, 1
        while span < w:
            s = s + pltpu.roll(s, span, axis=0)
            span *= 2
        cnt = jnp.minimum(float(w), pos + 1.0)
        out = _pool_group(s[POOL_HALO:, :], e[POOL_HALO:, :], cnt, pw_ref[g], ps_ref[:, cols])
        o_ref[:, cols] = out.astype(o_ref.dtype)


def _pool_sample_kernel(sp_ref, u_ref, pw_ref, ps_ref, o_ref, np_ref, *, n_new, n_buf):
    rows = [sp_ref[r] for r in range(n_buf)] + [u_ref[r] for r in range(n_new)]
    for r in range(n_buf):
        np_ref[r] = rows[n_new + r]
    gw = rows[0].shape[1] // len(POOL_WINDOWS)
    for t in range(n_new):
        i = n_buf + t
        for g, w in enumerate(POOL_WINDOWS):
            cols = slice(g * gw, (g + 1) * gw)
            s = rows[i][:, cols]
            for back in range(1, w):
                s = s + rows[i - back][:, cols]
            out = _pool_group(s, rows[i][:, cols], float(w), pw_ref[g], ps_ref[:, cols])
            o_ref[t, :, cols] = out


def _pool_sample(state_pool_t, u_t, pool_w_b, pool_scale):
    n_buf, bs, pw = state_pool_t.shape
    n_new = u_t.shape[0]
    full = lambda a: pl.BlockSpec(a.shape, lambda i: (0,) * a.ndim)
    return pl.pallas_call(
        functools.partial(_pool_sample_kernel, n_new=n_new, n_buf=n_buf),
        grid=(1,),
        in_specs=[full(state_pool_t), full(u_t), full(pool_w_b), full(pool_scale)],
        out_specs=[pl.BlockSpec((n_new, bs, pw), lambda i: (0, 0, 0)),
                   pl.BlockSpec((n_buf, bs, pw), lambda i: (0, 0, 0))],
        out_shape=[jax.ShapeDtypeStruct((n_new, bs, pw), F32),
                   jax.ShapeDtypeStruct((n_buf, bs, pw), F32)],
        compiler_params=_cparams("arbitrary"),
        name="pool_sample",
    )(state_pool_t, u_t, pool_w_b, pool_scale)


def _lambda_full(lq1, lk1, lq2, lk2, lam_init):
    e1 = jnp.exp(jnp.sum(lq1 * lk1, axis=-1, keepdims=True))
    e2 = jnp.exp(jnp.sum(lq2 * lk2, axis=-1, keepdims=True))
    return e1 - e2 + lam_init


def _split_halves(q):
    lane = lax.broadcasted_iota(I32, q.shape, 1)
    zero = jnp.zeros_like(q)
    return jnp.concatenate([jnp.where(lane < HEAD_DIM, q, zero), jnp.where(lane >= HEAD_DIM, q, zero)], axis=0)


def _subln(o, g, lam_init):
    return _rms(o, g) * (1.0 - lam_init)


def _attn_prompt_kernel(slopes_ref, q_ref, k_ref, vt_ref, lq1_ref, lk1_ref, lq2_ref, lk2_ref, g_ref, o_ref,
                        *, tq, lam_init, heads):
    hg = pl.program_id(1)
    i = pl.program_id(2)
    lam = _lambda_full(lq1_ref[...], lk1_ref[...], lq2_ref[...], lk2_ref[...], lam_init)
    head_cols = [slice(hh * HEAD_W, (hh + 1) * HEAD_W) for hh in range(heads)]
    row = lax.broadcasted_iota(I32, (KEY_TILES * tq, LANES), 0)
    lane = lax.broadcasted_iota(I32, (KEY_TILES * tq, LANES), 1)
    pos_feat = jnp.where(lane < 2 * N_SPLIT, jnp.where(lane % 2 == 0, row & ~(POS_LOW - 1), row & (POS_LOW - 1)), 0)
    pos_feat = pos_feat.astype(F32).astype(BF16)
    qlane = lax.broadcasted_iota(I32, (2 * tq, LANES), 1)
    slopes2, qqs = [], []
    for hh in range(heads):
        s2 = slopes_ref[hg * heads + hh] * LOG2E
        slopes2.append(s2)
        rest = jnp.full((2 * tq, LANES), s2, F32)
        feat = jnp.zeros((2 * tq, LANES), F32)
        for part in range(N_SPLIT):
            piece = rest.astype(BF16).astype(F32)
            rest = rest - piece
            feat = jnp.where(qlane // 2 == part, piece, feat)
        qq = _split_halves((q_ref[:, head_cols[hh]] * LOG2E).astype(BF16))
        qqs.append(jnp.concatenate([qq, feat.astype(BF16)], axis=1))
    def head_step(hh, j, tiles, carry, diag_tile):
        m, acc = carry
        n_keys = tiles * tq
        start = pl.multiple_of(j * tq, tq)
        keys = jnp.concatenate([k_ref[pl.ds(start, n_keys), head_cols[hh]], pos_feat[:n_keys]], axis=1)
        st = _dot_nt(keys, qqs[hh])
        if diag_tile is not None:
            kr = lax.broadcasted_iota(I32, st.shape, 0) - diag_tile * tq
            qc = lax.broadcasted_iota(I32, st.shape, 1)
            qc = jnp.where(qc >= tq, qc - tq, qc)
            st = jnp.where(qc >= kr, st, NEG_INF)
        offset = slopes2[hh] * ((j - i) * tq).astype(F32)
        m_new = jnp.maximum(m, jnp.max(st, axis=0, keepdims=True) + offset)
        alpha = jnp.exp2(m - m_new)
        p = jnp.exp2(st - (m_new - offset)).astype(BF16)
        values = jnp.concatenate([vt_ref[j + t, head_cols[hh], :] for t in range(tiles)], axis=1)
        values = jnp.concatenate([values, jnp.ones((BF16_ROWS, n_keys), BF16)], axis=0)
        acc = alpha * acc + _dot(values, p)
        return m_new, acc

    def step(j, tiles, carries, diag_tile):
        return tuple(head_step(hh, j, tiles, carries[hh], diag_tile) for hh in range(heads))

    init = (jnp.full((1, 2 * tq), NEG_INF, F32), jnp.zeros((HEAD_W + BF16_ROWS, 2 * tq), F32))
    carries = lax.fori_loop(0, i // KEY_TILES, lambda g, cr: step(g * KEY_TILES, KEY_TILES, cr, None),
                            (init,) * heads)
    rest = i % KEY_TILES
    carries = lax.switch(rest, [functools.partial(step, i - r, r + 1, diag_tile=r) for r in range(KEY_TILES)],
                         carries)
    for hh, (_, acc) in enumerate(carries):
        o = acc[:HEAD_W] / acc[HEAD_W:HEAD_W + 1]
        o = (o[:, :tq] - lam * o[:, tq:]).T
        o_ref[:, head_cols[hh]] = _subln(o, g_ref[...], lam_init).astype(o_ref.dtype)


def _attn_prompt(q, kb, vt, slopes, lams, subln_g, batch, seq, n_heads, lam_init, tq):
    heads = math.gcd(n_heads, ATTN_HEADS)
    width = heads * HEAD_W
    q3, k3 = (a.reshape(batch, seq, n_heads * HEAD_W) for a in (q, kb))
    small = lambda a: pl.BlockSpec(a.shape, lambda b, h, i, s: (0,) * a.ndim)
    grid_spec = pltpu.PrefetchScalarGridSpec(
        num_scalar_prefetch=1,
        grid=(batch, n_heads // heads, seq // tq),
        in_specs=[pl.BlockSpec((None, tq, width), lambda b, h, i, s: (b, i, h)),
                  pl.BlockSpec((None, seq, width), lambda b, h, i, s: (b, 0, h)),
                  pl.BlockSpec((seq // tq, width, tq), lambda b, h, i, s: (b, h, 0)),
                  *[small(a) for a in lams], small(subln_g)],
        out_specs=pl.BlockSpec((None, tq, width), lambda b, h, i, s: (b, i, h)),
    )
    out = pl.pallas_call(
        functools.partial(_attn_prompt_kernel, tq=tq, lam_init=lam_init, heads=heads),
        grid_spec=grid_spec,
        out_shape=jax.ShapeDtypeStruct((batch, seq, n_heads * HEAD_W), BF16),
        compiler_params=_cparams("parallel", "parallel", "arbitrary"),
        name="attn_prompt",
    )(slopes, q3, k3, vt, *lams, subln_g)
    return out.reshape(batch * seq, n_heads * HEAD_W)


def _attn_sample_kernel(pt_ref, slopes_ref, q_ref, kn_ref, vn_ref, kc_ref, vc_ref,
                        lq1_ref, lk1_ref, lq2_ref, lk2_ref, g_ref, o_ref, kbuf, vbuf, sem,
                        *, n_seqs, per_step, n_pages, page_rows, n_heads, n_new, lam_init):
    step = pl.program_id(0)
    lam = _lambda_full(lq1_ref[...], lk1_ref[...], lq2_ref[...], lk2_ref[...], lam_init)

    def page_copies(seq, sl, pg):
        p = pt_ref[seq * n_pages + pg]
        dst = pl.ds(pg * page_rows, page_rows)
        return (pltpu.make_async_copy(kc_ref.at[p], kbuf.at[sl, dst, :], sem.at[sl, 0]),
                pltpu.make_async_copy(vc_ref.at[p], vbuf.at[sl, dst, :], sem.at[sl, 1]))

    def start_seq(seq, sl):
        for pg in range(n_pages):
            for cp in page_copies(seq, sl, pg):
                cp.start()

    @pl.when(step == 0)
    def _():
        start_seq(0, 0)

    for u in range(per_step):
        seq = step * per_step + u
        slot = seq % 2

        @pl.when(seq + 1 < n_seqs)
        def _():
            start_seq(seq + 1, 1 - slot)

        for pg in range(n_pages):
            for cp in page_copies(seq, slot, pg):
                cp.wait()
        _sample_seq(slot, lam, slopes_ref, q_ref.at[u], kn_ref.at[u], vn_ref.at[u], kbuf, vbuf, g_ref, o_ref.at[u],
                    past=n_pages * page_rows // n_heads, n_heads=n_heads, n_new=n_new, lam_init=lam_init)


def _sample_seq(slot, lam, slopes_ref, q_ref, kn_ref, vn_ref, kbuf, vbuf, g_ref, o_ref, *, past, n_heads, n_new,
                lam_init):
    grp = 2 * SUBLANES
    rows = SAMPLE_HEADS * grp
    r = lax.broadcasted_iota(I32, (rows, past), 0) % grp
    t_past = jnp.where(r >= n_new, r - n_new, r)
    dist_past = (past + t_past - lax.broadcasted_iota(I32, (rows, past), 1)).astype(F32)
    r1 = lax.broadcasted_iota(I32, (rows, 1), 0)
    t_new = jnp.where(r1 % grp >= n_new, r1 % grp - n_new, r1 % grp)
    pad = jnp.zeros((grp - 2 * n_new, HEAD_W), F32)
    for h0 in range(0, n_heads, SAMPLE_HEADS):
        heads = range(h0, h0 + SAMPLE_HEADS)
        cols = slice(h0 * HEAD_W, (h0 + SAMPLE_HEADS) * HEAD_W)
        slope = jnp.concatenate([jnp.full((grp, 1), slopes_ref[h], F32) for h in heads], axis=0)
        blocks = []
        for i, h in enumerate(heads):
            qh = jnp.concatenate([_split_halves(q_ref[:, h * HEAD_W:(h + 1) * HEAD_W]), pad], axis=0)
            zero = jnp.zeros_like(qh)
            blocks.append(jnp.concatenate([qh if j == i else zero for j in range(SAMPLE_HEADS)], axis=1))
        qq = jnp.concatenate(blocks, axis=0)
        kh = jnp.concatenate([kbuf[slot, pl.ds(h, past, stride=n_heads), :] for h in heads], axis=1).astype(BF16)
        vh = jnp.concatenate([vbuf[slot, pl.ds(h, past, stride=n_heads), :] for h in heads], axis=1).astype(BF16)
        s = _dot_nt(qq.astype(BF16), kh) - slope * dist_past
        kn = kn_ref[:, cols]
        vn = vn_ref[:, cols]
        s_new = []
        for c in range(n_new):
            sc = jnp.sum(qq * kn[c:c + 1, :], axis=-1, keepdims=True) - slope * (t_new - c).astype(F32)
            s_new.append(jnp.where(t_new >= c, sc, NEG_INF))
        m = jnp.max(s, axis=-1, keepdims=True)
        for sc in s_new:
            m = jnp.maximum(m, sc)
        p = jnp.exp(s - m)
        l = jnp.sum(p, axis=-1, keepdims=True)
        acc = _dot(p.astype(BF16), vh)
        for c, sc in enumerate(s_new):
            pc = jnp.exp(sc - m)
            l = l + pc
            acc = acc + pc * vn[c:c + 1, :]
        o = acc / l
        for i, h in enumerate(heads):
            oh = o[i * grp:(i + 1) * grp, i * HEAD_W:(i + 1) * HEAD_W]
            oh = oh[0:n_new] - lam * oh[n_new:2 * n_new]
            o_ref[:, h * HEAD_W:(h + 1) * HEAD_W] = _subln(oh, g_ref[...], lam_init)


def _attn_sample(q3, kn3, vn3, kc, vc, page_table, slopes, lams, subln_g, n_heads, lam_init):
    bs, n_new, qk_w = q3.shape
    n_pages = page_table.shape[1]
    page_rows = kc.shape[1]
    per_step = math.gcd(bs, SAMPLE_SEQS)
    small = lambda a: pl.BlockSpec(a.shape, lambda b, pt, s: (0,) * a.ndim)
    tok = pl.BlockSpec((per_step, n_new, qk_w), lambda b, pt, s: (b, 0, 0))
    grid_spec = pltpu.PrefetchScalarGridSpec(
        num_scalar_prefetch=2,
        grid=(bs // per_step,),
        in_specs=[tok, tok, tok, pl.BlockSpec(memory_space=pl.ANY), pl.BlockSpec(memory_space=pl.ANY),
                  *[small(a) for a in lams], small(subln_g)],
        out_specs=tok,
        scratch_shapes=[pltpu.VMEM((2, n_pages * page_rows, LANES), F32),
                        pltpu.VMEM((2, n_pages * page_rows, LANES), F32),
                        pltpu.SemaphoreType.DMA((2, 2))],
    )
    return pl.pallas_call(
        functools.partial(_attn_sample_kernel, n_seqs=bs, per_step=per_step, n_pages=n_pages, page_rows=page_rows,
                          n_heads=n_heads, n_new=n_new, lam_init=lam_init),
        grid_spec=grid_spec,
        out_shape=jax.ShapeDtypeStruct((bs, n_new, qk_w), F32),
        compiler_params=_cparams("arbitrary"),
        name="attn_sample",
    )(page_table.reshape(-1), slopes, q3, kn3, vn3, kc, vc, *lams, subln_g)


def _outproj_kernel(pool_p_ref, o_p_ref, x_p_ref, pool_s_ref, o_s_ref, x_s_ref, wo_ref, g2_ref, rwt_ref, rb_ref,
                    earlier_ref, h_ref, hn_ref, code_ref, gate_ref, cnt_out_ref, run_ref, *, tm, n_experts,
                    prompt_tiles):
    i = pl.program_id(0)

    @pl.when(i == 0)
    def _():
        run_ref[...] = jnp.zeros_like(run_ref)

    tile = functools.partial(_outproj_tile, wo_ref=wo_ref, g2_ref=g2_ref, rwt_ref=rwt_ref, rb_ref=rb_ref,
                             earlier_ref=earlier_ref, h_ref=h_ref, hn_ref=hn_ref, code_ref=code_ref, gate_ref=gate_ref,
                             cnt_out_ref=cnt_out_ref, run_ref=run_ref, tm=tm, n_experts=n_experts)
    pl.when(i < prompt_tiles)(functools.partial(tile, pool_p_ref, o_p_ref, x_p_ref))
    pl.when(i >= prompt_tiles)(functools.partial(tile, pool_s_ref, o_s_ref, x_s_ref))


def _outproj_tile(pool_ref, o_ref, x_ref, *, wo_ref, g2_ref, rwt_ref, rb_ref, earlier_ref, h_ref, hn_ref, code_ref,
                  gate_ref, cnt_out_ref, run_ref, tm, n_experts):
    pw = pool_ref.shape[1]
    mix = _dot(pool_ref[...].astype(BF16), wo_ref[0:pw, :]) + _dot(o_ref[...].astype(BF16), wo_ref[pw:, :])
    h = x_ref[...] + mix
    h_ref[...] = h
    hn = _rms(h, g2_ref[...])
    for s in range(hn.shape[1] // LANES):
        hn_ref[pl.ds(s, tm, stride=SUBLANES), :] = hn[:, s * LANES:(s + 1) * LANES]
    logits = _dot_nt(rwt_ref[...], hn.astype(BF16)) + rb_ref[...]
    expert = lax.broadcasted_iota(I32, logits.shape, 0)
    vals, idxs = [], []
    for _ in range(TOP_K):
        m = jnp.max(logits, axis=0, keepdims=True)
        idx = jnp.min(jnp.where(logits == m, expert, n_experts), axis=0, keepdims=True)
        vals.append(m)
        idxs.append(idx)
        logits = jnp.where(expert == idx, -jnp.inf, logits)
    ex = [jnp.exp(v - vals[0]) for v in vals]
    den = ex[0]
    for e in ex[1:]:
        den = den + e
    chosen = jnp.zeros(logits.shape, F32)
    for idx in idxs:
        chosen = chosen + jnp.where(expert == idx, 1.0, 0.0)
    before = _dot(chosen.astype(BF16), earlier_ref[...]) + run_ref[...]
    for k in range(TOP_K):
        gate_ref[k:k + 1, :] = ex[k] / den
        rank = jnp.sum(jnp.where(expert == idxs[k], before, 0.0), axis=0, keepdims=True).astype(I32)
        code_ref[k:k + 1, :] = idxs[k] * (1 << RANK_BITS) + rank
    gate_ref[TOP_K:, :] = jnp.zeros((gate_ref.shape[0] - TOP_K, tm), F32)
    run_ref[...] = run_ref[...] + jnp.sum(chosen, axis=1, keepdims=True)
    cnt_out_ref[...] = run_ref[...]


def _outproj(prompt, sample, w_out_b, g2, rwt_b, rb):
    np_, d = prompt[2].shape
    ns_ = sample[2].shape[0]
    n_total = np_ + ns_
    n_experts = rwt_b.shape[0]
    tm = min(ROW_TILE, math.gcd(np_, ns_))
    prompt_tiles = np_ // tm
    assert d == SUBLANES * LANES and n_total < (1 << RANK_BITS)
    first = lambda a: pl.BlockSpec((tm, a.shape[1]), lambda i: (jnp.minimum(i, prompt_tiles - 1), 0))
    second = lambda a: pl.BlockSpec((tm, a.shape[1]), lambda i: (jnp.maximum(i - prompt_tiles, 0), 0))
    row = lambda w: pl.BlockSpec((tm, w), lambda i: (i, 0))
    col = lambda r: pl.BlockSpec((r, tm), lambda i: (0, i))
    full = lambda a: pl.BlockSpec(a.shape, lambda i: (0,) * a.ndim)
    counts = jax.ShapeDtypeStruct((n_experts, 1), F32)
    earlier = jnp.triu(jnp.ones((tm, tm), BF16), 1)
    return pl.pallas_call(
        functools.partial(_outproj_kernel, tm=tm, n_experts=n_experts, prompt_tiles=prompt_tiles),
        grid=(n_total // tm,),
        in_specs=[*[first(a) for a in prompt], *[second(a) for a in sample],
                  full(w_out_b), full(g2), full(rwt_b), full(rb), full(earlier)],
        out_specs=[row(d), pl.BlockSpec((tm * SUBLANES, LANES), lambda i: (i, 0)),
                   col(TOP_K), col(SUBLANES), pl.BlockSpec(counts.shape, lambda i: (0, 0))],
        out_shape=[jax.ShapeDtypeStruct((n_total, d), F32), jax.ShapeDtypeStruct((n_total * SUBLANES, LANES), F32),
                   jax.ShapeDtypeStruct((TOP_K, n_total), I32), jax.ShapeDtypeStruct((SUBLANES, n_total), F32),
                   counts],
        scratch_shapes=[pltpu.VMEM(counts.shape, F32)],
        compiler_params=_cparams("arbitrary"),
        name="outproj_router",
    )(*prompt, *sample, w_out_b, g2, rwt_b, rb, earlier)


def _sorted_rows_kernel(start_ref, code_ref, rows_ref, *, n_experts):
    code = code_ref[...]
    expert = code >> RANK_BITS
    rows = code & ((1 << RANK_BITS) - 1)
    for e in range(n_experts):
        rows = rows + jnp.where(expert == e, start_ref[e], 0)
    rows_ref[...] = rows


def _sorted_rows(starts, code):
    return pl.pallas_call(
        functools.partial(_sorted_rows_kernel, n_experts=starts.shape[0]),
        grid=(1,),
        in_specs=[pl.BlockSpec(memory_space=pltpu.SMEM), pl.BlockSpec(code.shape, lambda i: (0, 0))],
        out_specs=pl.BlockSpec(code.shape, lambda i: (0, 0)),
        out_shape=jax.ShapeDtypeStruct(code.shape, I32),
        compiler_params=_cparams("arbitrary"),
        name="sorted_rows",
    )(starts, code)


MXU_DIM = 256


def _split_w1_tile(w_ref, g_ref, l_ref):
    r = lax.broadcasted_iota(I32, (MXU_DIM, MXU_DIM), 0)
    c = lax.broadcasted_iota(I32, (MXU_DIM, MXU_DIM), 1)
    half = MXU_DIM // 2
    source = jnp.where(c < half, 2 * c, 2 * (c - half) + 1)
    perm = jnp.where(r == source, 1.0, 0.0).astype(BF16)
    for blk in range(w_ref.shape[1] // MXU_DIM):
        res = _dot(w_ref[:, blk * MXU_DIM:(blk + 1) * MXU_DIM].astype(BF16), perm)
        g_ref[:, blk * half:(blk + 1) * half] = res[:, :half].astype(BF16)
        l_ref[:, blk * half:(blk + 1) * half] = res[:, half:].astype(BF16)


def _dispatch_kernel(row_ref, zlo_ref, zhi_ref, nu_ref, hn_ref, w1_ref, xs_ref, w1g_ref, w1l_ref, zero_ref, sem,
                     *, tm, n_total, rows, n_experts, n_blocks):
    i = pl.program_id(0)
    tile = tm * SUBLANES
    block = rows * SUBLANES

    def pad_pieces(e):
        lo, n = zlo_ref[e], zhi_ref[e] - zlo_ref[e]
        for bit in range((rows - 1).bit_length()):
            size = 1 << bit
            below = n & (size - 1)
            start = pl.multiple_of((lo + below) * SUBLANES, SUBLANES)
            cp = pltpu.make_async_copy(zero_ref.at[pl.ds(0, size * SUBLANES), :],
                                       xs_ref.at[pl.ds(start, size * SUBLANES), :], sem.at[1])
            yield (n & size) != 0, cp

    def tail_copy(blk):
        start = pl.multiple_of(blk * block, block)
        return pltpu.make_async_copy(zero_ref, xs_ref.at[pl.ds(start, block), :], sem.at[1])

    def for_zero_copies(fn):
        def per_expert(e, carry):
            for used, cp in pad_pieces(e):
                pl.when(used)(functools.partial(fn, cp))
            return carry
        lax.fori_loop(0, n_experts, per_expert, 0)

        def per_block(blk, carry):
            fn(tail_copy(blk))
            return carry
        lax.fori_loop(nu_ref[0], n_blocks, per_block, 0)

    @pl.when(i == 0)
    def _():
        zero_ref[...] = jnp.zeros_like(zero_ref)
        for_zero_copies(lambda cp: cp.start())

    def body(t, carry):
        src = pl.multiple_of(t * SUBLANES, SUBLANES)
        for k in range(TOP_K):
            dst = pl.multiple_of(row_ref[k * n_total + i * tm + t] * SUBLANES, SUBLANES)
            pltpu.make_async_copy(hn_ref.at[pl.ds(src, SUBLANES), :], xs_ref.at[pl.ds(dst, SUBLANES), :],
                                  sem.at[0]).start()
        return carry
    lax.fori_loop(0, tm, body, 0)

    _split_w1_tile(w1_ref, w1g_ref, w1l_ref)

    for _ in range(TOP_K):
        pltpu.make_async_copy(hn_ref, xs_ref.at[pl.ds(0, tile), :], sem.at[0]).wait()

    @pl.when(i == pl.num_programs(0) - 1)
    def _():
        for_zero_copies(lambda cp: cp.wait())


def _dispatch(sorted_rows, zlo, zhi, n_used, hn, w1, n_blocks):
    n_total = hn.shape[0] // SUBLANES
    n_exp, d, ff2 = w1.shape
    tr = next(t for t in (2 * ROW_TILE, ROW_TILE, d) if d % t == 0 and n_total % (n_exp * (d // t)) == 0)
    row_tiles = d // tr
    steps = n_exp * row_tiles
    tm = n_total // steps
    rows = EXPERT_ROWS
    assert rows % SUBLANES == 0
    half = pl.BlockSpec((None, tr, ff2 // 2), lambda i, *_: (i // row_tiles, i % row_tiles, 0))
    grid_spec = pltpu.PrefetchScalarGridSpec(
        num_scalar_prefetch=4,
        grid=(steps,),
        in_specs=[pl.BlockSpec((tm * SUBLANES, LANES), lambda i, *_: (i, 0)),
                  pl.BlockSpec((None, tr, ff2), lambda i, *_: (i // row_tiles, i % row_tiles, 0))],
        out_specs=[pl.BlockSpec(memory_space=pl.ANY), half, half],
        scratch_shapes=[pltpu.VMEM((rows * SUBLANES, LANES), F32), pltpu.SemaphoreType.DMA((2,))],
    )
    return pl.pallas_call(
        functools.partial(_dispatch_kernel, tm=tm, n_total=n_total, rows=rows, n_experts=n_exp, n_blocks=n_blocks),
        grid_spec=grid_spec,
        out_shape=[jax.ShapeDtypeStruct((n_blocks * rows * SUBLANES, LANES), F32),
                   jax.ShapeDtypeStruct((n_exp, d, ff2 // 2), BF16), jax.ShapeDtypeStruct((n_exp, d, ff2 // 2), BF16)],
        compiler_params=_cparams("arbitrary"),
        name="dispatch_split_w1",
    )(sorted_rows, zlo, zhi, n_used, hn, w1)


def _expert_kernel(be_ref, nu_ref, xs_ref, w1g_ref, w1l_ref, b1g_ref, b1l_ref, w2_ref, b2_ref, y_ref, *, rows):
    j = pl.program_id(0)
    n_used = nu_ref[0]

    @pl.when(j < n_used)
    def _():
        x = jnp.concatenate([xs_ref[pl.ds(s, rows, stride=SUBLANES), :] for s in range(SUBLANES)],
                            axis=-1).astype(BF16)
        glu = jnp.minimum(_dot(x, w1g_ref[...]) + b1g_ref[...], SWIGLU_LIMIT)
        lin = jnp.clip(_dot(x, w1l_ref[...]) + b1l_ref[...], -SWIGLU_LIMIT, SWIGLU_LIMIT)
        act = glu * jax.nn.sigmoid(SWIGLU_ALPHA * glu) * (lin + 1.0)
        y = _dot(act.astype(BF16), w2_ref[...].astype(BF16)) + b2_ref[...]
        for s in range(SUBLANES):
            y_ref[pl.ds(s, rows, stride=SUBLANES), :] = y[:, s * LANES:(s + 1) * LANES]

    @pl.when(j >= n_used)
    def _():
        y_ref[...] = jnp.zeros_like(y_ref)


def _experts(block_e, n_used, xs, w1g, w1l, b1g, b1l, w2, b2, n_blocks):
    rows = EXPERT_ROWS
    tile = rows * SUBLANES
    wspec = lambda a: pl.BlockSpec((None,) + a.shape[1:], lambda j, be, nu: (be[j], 0, 0))
    grid_spec = pltpu.PrefetchScalarGridSpec(
        num_scalar_prefetch=2,
        grid=(n_blocks,),
        in_specs=[pl.BlockSpec((tile, LANES), lambda j, be, nu: (jnp.minimum(j, nu[0] - 1), 0)),
                  wspec(w1g), wspec(w1l), wspec(b1g), wspec(b1l), wspec(w2), wspec(b2)],
        out_specs=pl.BlockSpec((tile, LANES), lambda j, be, nu: (j, 0)),
    )
    return pl.pallas_call(
        functools.partial(_expert_kernel, rows=rows),
        grid_spec=grid_spec,
        out_shape=jax.ShapeDtypeStruct((n_blocks * tile, LANES), F32),
        compiler_params=_cparams("arbitrary"),
        name="experts",
    )(block_e, n_used, xs, w1g, w1l, b1g, b1l, w2, b2)


def _combine_kernel(row_ref, yb_ref, gate_ref, h_ref, g_ref, yp_ref, ys_ref, cbuf, sem, *, tm, n_total,
                    prompt_tiles):
    i = pl.program_id(0)
    n = pl.num_programs(0)
    slot = i % 2
    tile = tm * SUBLANES

    def gather(blk, sl):
        def body(t2, carry):
            for t in (2 * t2, 2 * t2 + 1):
                for k in range(TOP_K):
                    src = pl.multiple_of(row_ref[k * n_total + blk * tm + t] * SUBLANES, SUBLANES)
                    dst = pl.multiple_of((k * tm + t) * SUBLANES, SUBLANES)
                    pltpu.make_async_copy(yb_ref.at[pl.ds(src, SUBLANES), :], cbuf.at[sl, pl.ds(dst, SUBLANES), :],
                                          sem.at[sl]).start()
            return carry
        lax.fori_loop(0, tm // 2, body, 0)

    @pl.when(i == 0)
    def _():
        gather(0, 0)

    @pl.when(i + 1 < n)
    def _():
        gather(i + 1, 1 - slot)

    pltpu.make_async_copy(yb_ref.at[pl.ds(0, TOP_K * tile), :], cbuf.at[slot], sem.at[slot]).wait()
    gates = gate_ref[...].T
    slabs = []
    for s in range(SUBLANES):
        acc = None
        for k in range(TOP_K):
            part = gates[:, k:k + 1] * cbuf[slot, pl.ds(k * tile + s, tm, stride=SUBLANES), :]
            acc = part if acc is None else acc + part
        slabs.append(acc)
    hf = h_ref[...] + jnp.concatenate(slabs, axis=-1)
    y = _rms(hf, g_ref[...])

    @pl.when(i < prompt_tiles)
    def _():
        yp_ref[...] = y

    @pl.when(i >= prompt_tiles)
    def _():
        ys_ref[...] = y


def _combine(rows, yb, gates, h, gf, np_):
    n, d = h.shape
    ns_ = n - np_
    tm = min(COMBINE_TILE, math.gcd(np_, ns_))
    prompt_tiles = np_ // tm
    row = lambda w: pl.BlockSpec((tm, w), lambda i, *_: (i, 0))
    grid_spec = pltpu.PrefetchScalarGridSpec(
        num_scalar_prefetch=1,
        grid=(n // tm,),
        in_specs=[pl.BlockSpec(memory_space=pl.ANY), pl.BlockSpec((gates.shape[0], tm), lambda i, *_: (0, i)), row(d),
                  pl.BlockSpec(gf.shape, lambda i, *_: (0, 0))],
        out_specs=[pl.BlockSpec((tm, d), lambda i, *_: (jnp.minimum(i, prompt_tiles - 1), 0)),
                   pl.BlockSpec((tm, d), lambda i, *_: (jnp.maximum(i - prompt_tiles, 0), 0))],
        scratch_shapes=[pltpu.VMEM((2, TOP_K * tm * SUBLANES, LANES), F32), pltpu.SemaphoreType.DMA((2,))],
    )
    return pl.pallas_call(
        functools.partial(_combine_kernel, tm=tm, n_total=n, prompt_tiles=prompt_tiles),
        grid_spec=grid_spec,
        out_shape=[jax.ShapeDtypeStruct((np_, d), F32), jax.ShapeDtypeStruct((ns_, d), F32)],
        compiler_params=_cparams("arbitrary"),
        name="combine_norm",
    )(rows, yb, gates, h, gf)


def kernel(x_prompt, x_sample, state_pool, cache_k, cache_v, page_table, norm1_g, w_in, pool_w, pool_scale,
           lambda_q1, lambda_k1, lambda_q2, lambda_k2, subln_g, w_out, norm2_g, router_w, router_b,
           w1, b1, w2, b2, normf_g):
    batch, seq, d = x_prompt.shape
    bs, n_new, _ = x_sample.shape
    depth = w_in.shape[0]
    assert depth == 1, "single-layer trunk"
    n_heads = cache_k.shape[3]
    page_size = cache_k.shape[2]
    pw = pool_w.shape[1] * pool_w.shape[2]
    qk_w = n_heads * HEAD_W
    n_experts = router_w.shape[2]
    np_, ns_ = batch * seq, bs * n_new
    n_total = np_ + ns_
    lam_init = 0.8 - 0.6 * math.exp(-0.3 * 0)

    g1 = norm1_g[0][None]
    w_in_b = w_in[0].astype(BF16)
    wvt_b = w_in_b[:, pw + 2 * qk_w:].T
    pool_w_b = pool_w[0].astype(BF16)
    ps = pool_scale[0][None]
    lams = [a[0][None] for a in (lambda_q1, lambda_k1, lambda_q2, lambda_k2)]
    sg = subln_g[0][None]
    slopes = jnp.exp2(-8.0 * (jnp.arange(n_heads, dtype=F32) + 1.0) / n_heads)
    w_out_b = w_out[0].astype(BF16)
    g2 = norm2_g[0][None]
    rwt_b = router_w[0].T.astype(BF16)
    rb = router_b[0][:, None]
    b1_pairs = b1[0].T.reshape(-1, 2, n_experts)
    b1g = b1_pairs[:, 0, :].T[:, None, :]
    b1l = b1_pairs[:, 1, :].T[:, None, :]
    w2b = w2.reshape(w2.shape[1:])
    b2e = b2[0][:, None, :]
    gf = normf_g[None]

    xp = x_prompt.reshape(np_, d)
    tq = min(ATTN_TILE, seq)
    pool_p, tail_p, q_p, k_p, v_p, kb_p, vt_p = _inproj(xp, g1, w_in_b, wvt_b, pw, qk_w,
                                                        prompt=(pool_w_b, ps, seq, tq))
    o_p = _attn_prompt(q_p, kb_p, vt_p, slopes, lams, sg, batch, seq, n_heads, lam_init, tq)

    xs = x_sample.reshape(ns_, d)
    u_s, q_s, k_s, v_s = _inproj(xs, g1, w_in_b, wvt_b, pw, qk_w)
    pool_s, new_pool_s = _pool_sample(jnp.transpose(state_pool[0], (1, 0, 2)),
                                      jnp.transpose(u_s.reshape(bs, n_new, pw), (1, 0, 2)), pool_w_b, ps)
    pool_s = jnp.transpose(pool_s, (1, 0, 2))
    new_pool_s = jnp.transpose(new_pool_s, (1, 0, 2))
    kc = cache_k.reshape(cache_k.shape[1], page_size * n_heads, HEAD_W)
    vc = cache_v.reshape(cache_v.shape[1], page_size * n_heads, HEAD_W)
    o_s = _attn_sample(q_s.reshape(bs, n_new, qk_w), k_s.reshape(bs, n_new, qk_w), v_s.reshape(bs, n_new, qk_w),
                       kc, vc, page_table, slopes, lams, sg, n_heads, lam_init)

    h, hn, code, gates, cnt = _outproj((pool_p, o_p, xp), (pool_s.reshape(ns_, pw), o_s.reshape(ns_, qk_w), xs),
                                       w_out_b, g2, rwt_b, rb)

    counts = cnt[:, 0].astype(I32)
    n_blocks = -(-n_total * TOP_K // EXPERT_ROWS) + n_experts
    padded = (counts + EXPERT_ROWS - 1) // EXPERT_ROWS * EXPERT_ROWS
    pad_ends = jnp.cumsum(padded)
    pad_starts = pad_ends - padded
    n_used = (pad_ends[-1:] // EXPERT_ROWS).astype(I32)
    block_start = jnp.arange(n_blocks, dtype=I32) * EXPERT_ROWS
    block_e = jnp.minimum(jnp.sum((pad_ends[None, :] <= block_start[:, None]).astype(I32), axis=1), n_experts - 1)

    sorted_rows = _sorted_rows(pad_starts, code).reshape(-1)
    xs_rows, w1g, w1l = _dispatch(sorted_rows, pad_starts + counts, pad_ends, n_used, hn, w1.reshape(w1.shape[1:]),
                                  n_blocks)
    yb = _experts(block_e, n_used, xs_rows, w1g, w1l, b1g, b1l, w2b, b2e, n_blocks)
    y_p, y_s = _combine(sorted_rows, yb, gates, h, gf, np_)

    return (y_p.reshape(batch, seq, d), y_s.reshape(bs, n_new, d),
            tail_p[None, :, POOL_HALO - state_pool.shape[2]:],
            new_pool_s[None],
            k_p.reshape(1, batch, seq, n_heads, HEAD_W), v_p.reshape(1, batch, seq, n_heads, HEAD_W),
            k_s.reshape(1, bs, n_new, n_heads, HEAD_W), v_s.reshape(1, bs, n_new, n_heads, HEAD_W))
```

```python
import functools
import math

import jax
import jax.numpy as jnp
from jax import lax
from jax.experimental import pallas as pl
from jax.experimental.pallas import tpu as pltpu

F32 = jnp.float32
BF16 = jnp.bfloat16
I32 = jnp.int32

RMS_EPS = 1e-6
NEG_INF = -1e30
HEAD_DIM = 64
HEAD_W = 2 * HEAD_DIM
POOL_WINDOWS = (2, 4, 8, 16)
POOL_HALO = 16
TOP_K = 4
RANK_BITS = 20
SWIGLU_LIMIT = 7.0
SWIGLU_ALPHA = 1.702
LANES = 128
SUBLANES = 8
ROW_TILE = 512
INPROJ_TILE = 1024
ATTN_TILE = 512
SAMPLE_HEADS = 2
SAMPLE_SEQS = 4
KEY_TILES = 1
ATTN_HEADS = 4
LOG2E = 1.4426950408889634
N_SPLIT = 3
POS_LOW = 16
BF16_ROWS = 16
EXPERT_ROWS = 768
COMBINE_TILE = 512
VMEM_LIMIT = 48 * 1024 * 1024


def _cparams(*sem):
    return pltpu.CompilerParams(dimension_semantics=sem, vmem_limit_bytes=VMEM_LIMIT)


def _dot(a, b):
    return jnp.dot(a, b, preferred_element_type=F32)


def _dot_nt(a, b):
    return lax.dot_general(a, b, (((1,), (1,)), ((), ())), preferred_element_type=F32)


def _rms(x, g):
    return x * lax.rsqrt(jnp.mean(x * x, axis=-1, keepdims=True) + RMS_EPS) * g


def _inproj_sample_kernel(x_ref, g_ref, w_ref, wvt_ref, u_ref, q_ref, k_ref, v_ref, *, pool_w, qk_w):
    del wvt_ref
    u = _inproj_tile(x_ref, g_ref, w_ref, q_ref, k_ref, v_ref, pool_w=pool_w, qk_w=qk_w)[0]
    u_ref[...] = u


def _inproj_prompt_kernel(x_ref, g_ref, w_ref, wvt_ref, pw_ref, ps_ref, pool_ref, tail_ref, q_ref, k_ref, v_ref, kb_ref,
                          vt_ref, carry_ref, *, pool_w, qk_w, key_tile, seq_tiles):
    u, k, xb = _inproj_tile(x_ref, g_ref, w_ref, q_ref, k_ref, v_ref, pool_w=pool_w, qk_w=qk_w)
    _pool_tile(u, pl.program_id(0) % seq_tiles, pw_ref, ps_ref, pool_ref, carry_ref)
    tail_ref[...] = u[u.shape[0] - POOL_HALO:, :]
    kb_ref[...] = k.astype(BF16)
    vt = _dot_nt(wvt_ref[...], xb)
    for c in range(vt_ref.shape[0]):
        vt_ref[c] = vt[:, c * key_tile:(c + 1) * key_tile].astype(BF16)


def _inproj_tile(x_ref, g_ref, w_ref, q_ref, k_ref, v_ref, *, pool_w, qk_w):
    xb = _rms(x_ref[...], g_ref[...]).astype(BF16)
    c0, c1, c2 = pool_w, pool_w + qk_w, pool_w + 2 * qk_w
    u = _dot(xb, w_ref[:, 0:c0])
    q_ref[...] = _dot(xb, w_ref[:, c0:c1]) * (HEAD_DIM ** -0.5)
    k = _dot(xb, w_ref[:, c1:c2])
    v = _dot(xb, w_ref[:, c2:])
    n_heads = qk_w // HEAD_W
    tm = k.shape[0]
    for h in range(n_heads):
        k_ref[pl.ds(h, tm, stride=n_heads), :] = k[:, h * HEAD_W:(h + 1) * HEAD_W]
        v_ref[pl.ds(h, tm, stride=n_heads), :] = v[:, h * HEAD_W:(h + 1) * HEAD_W]
    return u, k, xb


def _inproj(x, g1, w_in_b, wvt_b, pool_w, qk_w, prompt=None):
    n, d = x.shape
    val_w = wvt_b.shape[0]
    tm = min(INPROJ_TILE, n if prompt is None else prompt[2])
    row = lambda w: pl.BlockSpec((tm, w), lambda i: (i, 0))
    full = lambda a: pl.BlockSpec(a.shape, lambda i: (0,) * a.ndim)
    n_heads = qk_w // HEAD_W
    assert val_w == qk_w and n % tm == 0
    heads = pl.BlockSpec((tm * n_heads, HEAD_W), lambda i: (i, 0))
    qkv_specs = [row(qk_w), heads, heads]
    qkv_shape = [jax.ShapeDtypeStruct((n, qk_w), F32), jax.ShapeDtypeStruct((n * n_heads, HEAD_W), F32),
                 jax.ShapeDtypeStruct((n * n_heads, HEAD_W), F32)]
    if prompt is None:
        return pl.pallas_call(
            functools.partial(_inproj_sample_kernel, pool_w=pool_w, qk_w=qk_w),
            grid=(n // tm,),
            in_specs=[row(d), full(g1), full(w_in_b), full(wvt_b)],
            out_specs=[row(pool_w)] + qkv_specs,
            out_shape=[jax.ShapeDtypeStruct((n, pool_w), F32)] + qkv_shape,
            compiler_params=_cparams("parallel"),
            name="inproj_sample",
        )(x, g1, w_in_b, wvt_b)
    pool_w_b, pool_scale, seq, key_tile = prompt
    assert tm % key_tile == 0 and seq % tm == 0 and tm >= POOL_HALO
    seq_tiles = seq // tm
    return pl.pallas_call(
        functools.partial(_inproj_prompt_kernel, pool_w=pool_w, qk_w=qk_w, key_tile=key_tile, seq_tiles=seq_tiles),
        grid=(n // tm,),
        in_specs=[row(d), full(g1), full(w_in_b), full(wvt_b), full(pool_w_b), full(pool_scale)],
        out_specs=[row(pool_w), pl.BlockSpec((None, POOL_HALO, pool_w), lambda i: (i // seq_tiles, 0, 0))] + qkv_specs
        + [row(qk_w), pl.BlockSpec((tm // key_tile, val_w, key_tile), lambda i: (i, 0, 0))],
        out_shape=[jax.ShapeDtypeStruct((n, pool_w), BF16), jax.ShapeDtypeStruct((n // seq, POOL_HALO, pool_w), F32)]
        + qkv_shape + [jax.ShapeDtypeStruct((n, qk_w), BF16),
                       jax.ShapeDtypeStruct((n // key_tile, val_w, key_tile), BF16)],
        scratch_shapes=[pltpu.VMEM((POOL_HALO, pool_w), F32)],
        compiler_params=_cparams("arbitrary"),
        name="inproj_pool_prompt",
    )(x, g1, w_in_b, wvt_b, pool_w_b, pool_scale)


def _pool_group(s_win, tok, cnt, pw, ps):
    d = s_win / cnt - tok
    return _dot(d.astype(BF16), pw) * ps


def _pool_tile(cur, j, pw_ref, ps_ref, o_ref, carry_ref):
    tt = cur.shape[0]

    @pl.when(j == 0)
    def _():
        carry_ref[...] = jnp.zeros_like(carry_ref)

    ext = jnp.concatenate([carry_ref[...], cur], axis=0)
    carry_ref[...] = cur[tt - POOL_HALO:, :]
    pos = (j * tt + lax.broadcasted_iota(I32, (tt, 1), 0)).astype(F32)
    gw = cur.shape[1] // len(POOL_WINDOWS)
    for g, w in enumerate(POOL_WINDOWS):
        cols = slice(g * gw, (g + 1) * gw)
        e = ext[:, cols]
        s, span = e, 1
        while span < w:
            s = s + pltpu.roll(s, span, axis=0)
            span *= 2
        cnt = jnp.minimum(float(w), pos + 1.0)
        out = _pool_group(s[POOL_HALO:, :], e[POOL_HALO:, :], cnt, pw_ref[g], ps_ref[:, cols])
        o_ref[:, cols] = out.astype(o_ref.dtype)


def _pool_sample_kernel(sp_ref, u_ref, pw_ref, ps_ref, o_ref, np_ref, *, n_new, n_buf):
    rows = [sp_ref[r] for r in range(n_buf)] + [u_ref[r] for r in range(n_new)]
    for r in range(n_buf):
        np_ref[r] = rows[n_new + r]
    gw = rows[0].shape[1] // len(POOL_WINDOWS)
    for t in range(n_new):
        i = n_buf + t
        for g, w in enumerate(POOL_WINDOWS):
            cols = slice(g * gw, (g + 1) * gw)
            s = rows[i][:, cols]
            for back in range(1, w):
                s = s + rows[i - back][:, cols]
            out = _pool_group(s, rows[i][:, cols], float(w), pw_ref[g], ps_ref[:, cols])
            o_ref[t, :, cols] = out


def _pool_sample(state_pool_t, u_t, pool_w_b, pool_scale):
    n_buf, bs, pw = state_pool_t.shape
    n_new = u_t.shape[0]
    full = lambda a: pl.BlockSpec(a.shape, lambda i: (0,) * a.ndim)
    return pl.pallas_call(
        functools.partial(_pool_sample_kernel, n_new=n_new, n_buf=n_buf),
        grid=(1,),
        in_specs=[full(state_pool_t), full(u_t), full(pool_w_b), full(pool_scale)],
        out_specs=[pl.BlockSpec((n_new, bs, pw), lambda i: (0, 0, 0)),
                   pl.BlockSpec((n_buf, bs, pw), lambda i: (0, 0, 0))],
        out_shape=[jax.ShapeDtypeStruct((n_new, bs, pw), F32),
                   jax.ShapeDtypeStruct((n_buf, bs, pw), F32)],
        compiler_params=_cparams("arbitrary"),
        name="pool_sample",
    )(state_pool_t, u_t, pool_w_b, pool_scale)


def _lambda_full(lq1, lk1, lq2, lk2, lam_init):
    e1 = jnp.exp(jnp.sum(lq1 * lk1, axis=-1, keepdims=True))
    e2 = jnp.exp(jnp.sum(lq2 * lk2, axis=-1, keepdims=True))
    return e1 - e2 + lam_init


def _split_halves(q):
    lane = lax.broadcasted_iota(I32, q.shape, 1)
    zero = jnp.zeros_like(q)
    return jnp.concatenate([jnp.where(lane < HEAD_DIM, q, zero), jnp.where(lane >= HEAD_DIM, q, zero)], axis=0)


def _subln(o, g, lam_init):
    return _rms(o, g) * (1.0 - lam_init)


def _attn_prompt_kernel(slopes_ref, q_ref, k_ref, vt_ref, lq1_ref, lk1_ref, lq2_ref, lk2_ref, g_ref, o_ref,
                        *, tq, lam_init, heads):
    hg = pl.program_id(1)
    i = pl.program_id(2)
    lam = _lambda_full(lq1_ref[...], lk1_ref[...], lq2_ref[...], lk2_ref[...], lam_init)
    head_cols = [slice(hh * HEAD_W, (hh + 1) * HEAD_W) for hh in range(heads)]
    row = lax.broadcasted_iota(I32, (KEY_TILES * tq, LANES), 0)
    lane = lax.broadcasted_iota(I32, (KEY_TILES * tq, LANES), 1)
    pos_feat = jnp.where(lane < 2 * N_SPLIT, jnp.where(lane % 2 == 0, row & ~(POS_LOW - 1), row & (POS_LOW - 1)), 0)
    pos_feat = pos_feat.astype(F32).astype(BF16)
    qlane = lax.broadcasted_iota(I32, (2 * tq, LANES), 1)
    slopes2, qqs = [], []
    for hh in range(heads):
        s2 = slopes_ref[hg * heads + hh] * LOG2E
        slopes2.append(s2)
        rest = jnp.full((2 * tq, LANES), s2, F32)
        feat = jnp.zeros((2 * tq, LANES), F32)
        for part in range(N_SPLIT):
            piece = rest.astype(BF16).astype(F32)
            rest = rest - piece
            feat = jnp.where(qlane // 2 == part, piece, feat)
        qq = _split_halves((q_ref[:, head_cols[hh]] * LOG2E).astype(BF16))
        qqs.append(jnp.concatenate([qq, feat.astype(BF16)], axis=1))
    def head_step(hh, j, tiles, carry, diag_tile):
        m, acc = carry
        n_keys = tiles * tq
        start = pl.multiple_of(j * tq, tq)
        keys = jnp.concatenate([k_ref[pl.ds(start, n_keys), head_cols[hh]], pos_feat[:n_keys]], axis=1)
        st = _dot_nt(keys, qqs[hh])
        if diag_tile is not None:
            kr = lax.broadcasted_iota(I32, st.shape, 0) - diag_tile * tq
            qc = lax.broadcasted_iota(I32, st.shape, 1)
            qc = jnp.where(qc >= tq, qc - tq, qc)
            st = jnp.where(qc >= kr, st, NEG_INF)
        offset = slopes2[hh] * ((j - i) * tq).astype(F32)
        m_new = jnp.maximum(m, jnp.max(st, axis=0, keepdims=True) + offset)
        alpha = jnp.exp2(m - m_new)
        p = jnp.exp2(st - (m_new - offset)).astype(BF16)
        values = jnp.concatenate([vt_ref[j + t, head_cols[hh], :] for t in range(tiles)], axis=1)
        values = jnp.concatenate([values, jnp.ones((BF16_ROWS, n_keys), BF16)], axis=0)
        acc = alpha * acc + _dot(values, p)
        return m_new, acc

    def step(j, tiles, carries, diag_tile):
        return tuple(head_step(hh, j, tiles, carries[hh], diag_tile) for hh in range(heads))

    init = (jnp.full((1, 2 * tq), NEG_INF, F32), jnp.zeros((HEAD_W + BF16_ROWS, 2 * tq), F32))
    carries = lax.fori_loop(0, i // KEY_TILES, lambda g, cr: step(g * KEY_TILES, KEY_TILES, cr, None),
                            (init,) * heads)
    rest = i % KEY_TILES
    carries = lax.switch(rest, [functools.partial(step, i - r, r + 1, diag_tile=r) for r in range(KEY_TILES)],
                         carries)
    for hh, (_, acc) in enumerate(carries):
        o = acc[:HEAD_W] / acc[HEAD_W:HEAD_W + 1]
        o = (o[:, :tq] - lam * o[:, tq:]).T
        o_ref[:, head_cols[hh]] = _subln(o, g_ref[...], lam_init).astype(o_ref.dtype)


def _attn_prompt(q, kb, vt, slopes, lams, subln_g, batch, seq, n_heads, lam_init, tq):
    heads = math.gcd(n_heads, ATTN_HEADS)
    width = heads * HEAD_W
    q3, k3 = (a.reshape(batch, seq, n_heads * HEAD_W) for a in (q, kb))
    small = lambda a: pl.BlockSpec(a.shape, lambda b, h, i, s: (0,) * a.ndim)
    grid_spec = pltpu.PrefetchScalarGridSpec(
        num_scalar_prefetch=1,
        grid=(batch, n_heads // heads, seq // tq),
        in_specs=[pl.BlockSpec((None, tq, width), lambda b, h, i, s: (b, i, h)),
                  pl.BlockSpec((None, seq, width), lambda b, h, i, s: (b, 0, h)),
                  pl.BlockSpec((seq // tq, width, tq), lambda b, h, i, s: (b, h, 0)),
                  *[small(a) for a in lams], small(subln_g)],
        out_specs=pl.BlockSpec((None, tq, width), lambda b, h, i, s: (b, i, h)),
    )
    out = pl.pallas_call(
        functools.partial(_attn_prompt_kernel, tq=tq, lam_init=lam_init, heads=heads),
        grid_spec=grid_spec,
        out_shape=jax.ShapeDtypeStruct((batch, seq, n_heads * HEAD_W), BF16),
        compiler_params=_cparams("parallel", "parallel", "arbitrary"),
        name="attn_prompt",
    )(slopes, q3, k3, vt, *lams, subln_g)
    return out.reshape(batch * seq, n_heads * HEAD_W)


def _attn_sample_kernel(pt_ref, slopes_ref, q_ref, kn_ref, vn_ref, kc_ref, vc_ref,
                        lq1_ref, lk1_ref, lq2_ref, lk2_ref, g_ref, o_ref, kbuf, vbuf, sem,
                        *, n_seqs, per_step, n_pages, page_rows, n_heads, n_new, lam_init):
    step = pl.program_id(0)
    lam = _lambda_full(lq1_ref[...], lk1_ref[...], lq2_ref[...], lk2_ref[...], lam_init)

    def page_copies(seq, sl, pg):
        p = pt_ref[seq * n_pages + pg]
        dst = pl.ds(pg * page_rows, page_rows)
        return (pltpu.make_async_copy(kc_ref.at[p], kbuf.at[sl, dst, :], sem.at[sl, 0]),
                pltpu.make_async_copy(vc_ref.at[p], vbuf.at[sl, dst, :], sem.at[sl, 1]))

    def start_seq(seq, sl):
        for pg in range(n_pages):
            for cp in page_copies(seq, sl, pg):
                cp.start()

    @pl.when(step == 0)
    def _():
        start_seq(0, 0)

    for u in range(per_step):
        seq = step * per_step + u
        slot = seq % 2

        @pl.when(seq + 1 < n_seqs)
        def _():
            start_seq(seq + 1, 1 - slot)

        for pg in range(n_pages):
            for cp in page_copies(seq, slot, pg):
                cp.wait()
        _sample_seq(slot, lam, slopes_ref, q_ref.at[u], kn_ref.at[u], vn_ref.at[u], kbuf, vbuf, g_ref, o_ref.at[u],
                    past=n_pages * page_rows // n_heads, n_heads=n_heads, n_new=n_new, lam_init=lam_init)


def _sample_seq(slot, lam, slopes_ref, q_ref, kn_ref, vn_ref, kbuf, vbuf, g_ref, o_ref, *, past, n_heads, n_new,
                lam_init):
    grp = 2 * SUBLANES
    rows = SAMPLE_HEADS * grp
    r = lax.broadcasted_iota(I32, (rows, past), 0) % grp
    t_past = jnp.where(r >= n_new, r - n_new, r)
    dist_past = (past + t_past - lax.broadcasted_iota(I32, (rows, past), 1)).astype(F32)
    r1 = lax.broadcasted_iota(I32, (rows, 1), 0)
    t_new = jnp.where(r1 % grp >= n_new, r1 % grp - n_new, r1 % grp)
    pad = jnp.zeros((grp - 2 * n_new, HEAD_W), F32)
    for h0 in range(0, n_heads, SAMPLE_HEADS):
        heads = range(h0, h0 + SAMPLE_HEADS)
        cols = slice(h0 * HEAD_W, (h0 + SAMPLE_HEADS) * HEAD_W)
        slope = jnp.concatenate([jnp.full((grp, 1), slopes_ref[h], F32) for h in heads], axis=0)
        blocks = []
        for i, h in enumerate(heads):
            qh = jnp.concatenate([_split_halves(q_ref[:, h * HEAD_W:(h + 1) * HEAD_W]), pad], axis=0)
            zero = jnp.zeros_like(qh)
            blocks.append(jnp.concatenate([qh if j == i else zero for j in range(SAMPLE_HEADS)], axis=1))
        qq = jnp.concatenate(blocks, axis=0)
        kh = jnp.concatenate([kbuf[slot, pl.ds(h, past, stride=n_heads), :] for h in heads], axis=1).astype(BF16)
        vh = jnp.concatenate([vbuf[slot, pl.ds(h, past, stride=n_heads), :] for h in heads], axis=1).astype(BF16)
        s = _dot_nt(qq.astype(BF16), kh) - slope * dist_past
        kn = kn_ref[:, cols]
        vn = vn_ref[:, cols]
        s_new = []
        for c in range(n_new):
            sc = jnp.sum(qq * kn[c:c + 1, :], axis=-1, keepdims=True) - slope * (t_new - c).astype(F32)
            s_new.append(jnp.where(t_new >= c, sc, NEG_INF))
        m = jnp.max(s, axis=-1, keepdims=True)
        for sc in s_new:
            m = jnp.maximum(m, sc)
        p = jnp.exp(s - m)
        l = jnp.sum(p, axis=-1, keepdims=True)
        acc = _dot(p.astype(BF16), vh)
        for c, sc in enumerate(s_new):
            pc = jnp.exp(sc - m)
            l = l + pc
            acc = acc + pc * vn[c:c + 1, :]
        o = acc / l
        for i, h in enumerate(heads):
            oh = o[i * grp:(i + 1) * grp, i * HEAD_W:(i + 1) * HEAD_W]
            oh = oh[0:n_new] - lam * oh[n_new:2 * n_new]
            o_ref[:, h * HEAD_W:(h + 1) * HEAD_W] = _subln(oh, g_ref[...], lam_init)


def _attn_sample(q3, kn3, vn3, kc, vc, page_table, slopes, lams, subln_g, n_heads, lam_init):
    bs, n_new, qk_w = q3.shape
    n_pages = page_table.shape[1]
    page_rows = kc.shape[1]
    per_step = math.gcd(bs, SAMPLE_SEQS)
    small = lambda a: pl.BlockSpec(a.shape, lambda b, pt, s: (0,) * a.ndim)
    tok = pl.BlockSpec((per_step, n_new, qk_w), lambda b, pt, s: (b, 0, 0))
    grid_spec = pltpu.PrefetchScalarGridSpec(
        num_scalar_prefetch=2,
        grid=(bs // per_step,),
        in_specs=[tok, tok, tok, pl.BlockSpec(memory_space=pl.ANY), pl.BlockSpec(memory_space=pl.ANY),
                  *[small(a) for a in lams], small(subln_g)],
        out_specs=tok,
        scratch_shapes=[pltpu.VMEM((2, n_pages * page_rows, LANES), F32),
                        pltpu.VMEM((2, n_pages * page_rows, LANES), F32),
                        pltpu.SemaphoreType.DMA((2, 2))],
    )
    return pl.pallas_call(
        functools.partial(_attn_sample_kernel, n_seqs=bs, per_step=per_step, n_pages=n_pages, page_rows=page_rows,
                          n_heads=n_heads, n_new=n_new, lam_init=lam_init),
        grid_spec=grid_spec,
        out_shape=jax.ShapeDtypeStruct((bs, n_new, qk_w), F32),
        compiler_params=_cparams("arbitrary"),
        name="attn_sample",
    )(page_table.reshape(-1), slopes, q3, kn3, vn3, kc, vc, *lams, subln_g)


def _outproj_kernel(pool_p_ref, o_p_ref, x_p_ref, pool_s_ref, o_s_ref, x_s_ref, wo_ref, g2_ref, rwt_ref, rb_ref,
                    earlier_ref, h_ref, hn_ref, code_ref, gate_ref, cnt_out_ref, run_ref, *, tm, n_experts,
                    prompt_tiles):
    i = pl.program_id(0)

    @pl.when(i == 0)
    def _():
        run_ref[...] = jnp.zeros_like(run_ref)

    tile = functools.partial(_outproj_tile, wo_ref=wo_ref, g2_ref=g2_ref, rwt_ref=rwt_ref, rb_ref=rb_ref,
                             earlier_ref=earlier_ref, h_ref=h_ref, hn_ref=hn_ref, code_ref=code_ref, gate_ref=gate_ref,
                             cnt_out_ref=cnt_out_ref, run_ref=run_ref, tm=tm, n_experts=n_experts)
    pl.when(i < prompt_tiles)(functools.partial(tile, pool_p_ref, o_p_ref, x_p_ref))
    pl.when(i >= prompt_tiles)(functools.partial(tile, pool_s_ref, o_s_ref, x_s_ref))


def _outproj_tile(pool_ref, o_ref, x_ref, *, wo_ref, g2_ref, rwt_ref, rb_ref, earlier_ref, h_ref, hn_ref, code_ref,
                  gate_ref, cnt_out_ref, run_ref, tm, n_experts):
    pw = pool_ref.shape[1]
    mix = _dot(pool_ref[...].astype(BF16), wo_ref[0:pw, :]) + _dot(o_ref[...].astype(BF16), wo_ref[pw:, :])
    h = x_ref[...] + mix
    h_ref[...] = h
    hn = _rms(h, g2_ref[...])
    for s in range(hn.shape[1] // LANES):
        hn_ref[pl.ds(s, tm, stride=SUBLANES), :] = hn[:, s * LANES:(s + 1) * LANES]
    logits = _dot_nt(rwt_ref[...], hn.astype(BF16)) + rb_ref[...]
    expert = lax.broadcasted_iota(I32, logits.shape, 0)
    vals, idxs = [], []
    for _ in range(TOP_K):
        m = jnp.max(logits, axis=0, keepdims=True)
        idx = jnp.min(jnp.where(logits == m, expert, n_experts), axis=0, keepdims=True)
        vals.append(m)
        idxs.append(idx)
        logits = jnp.where(expert == idx, -jnp.inf, logits)
    ex = [jnp.exp(v - vals[0]) for v in vals]
    den = ex[0]
    for e in ex[1:]:
        den = den + e
    chosen = jnp.zeros(logits.shape, F32)
    for idx in idxs:
        chosen = chosen + jnp.where(expert == idx, 1.0, 0.0)
    before = _dot(chosen.astype(BF16), earlier_ref[...]) + run_ref[...]
    for k in range(TOP_K):
        gate_ref[k:k + 1, :] = ex[k] / den
        rank = jnp.sum(jnp.where(expert == idxs[k], before, 0.0), axis=0, keepdims=True).astype(I32)
        code_ref[k:k + 1, :] = idxs[k] * (1 << RANK_BITS) + rank
    gate_ref[TOP_K:, :] = jnp.zeros((gate_ref.shape[0] - TOP_K, tm), F32)
    run_ref[...] = run_ref[...] + jnp.sum(chosen, axis=1, keepdims=True)
    cnt_out_ref[...] = run_ref[...]


def _outproj(prompt, sample, w_out_b, g2, rwt_b, rb):
    np_, d = prompt[2].shape
    ns_ = sample[2].shape[0]
    n_total = np_ + ns_
    n_experts = rwt_b.shape[0]
    tm = min(ROW_TILE, math.gcd(np_, ns_))
    prompt_tiles = np_ // tm
    assert d == SUBLANES * LANES and n_total < (1 << RANK_BITS)
    first = lambda a: pl.BlockSpec((tm, a.shape[1]), lambda i: (jnp.minimum(i, prompt_tiles - 1), 0))
    second = lambda a: pl.BlockSpec((tm, a.shape[1]), lambda i: (jnp.maximum(i - prompt_tiles, 0), 0))
    row = lambda w: pl.BlockSpec((tm, w), lambda i: (i, 0))
    col = lambda r: pl.BlockSpec((r, tm), lambda i: (0, i))
    full = lambda a: pl.BlockSpec(a.shape, lambda i: (0,) * a.ndim)
    counts = jax.ShapeDtypeStruct((n_experts, 1), F32)
    earlier = jnp.triu(jnp.ones((tm, tm), BF16), 1)
    return pl.pallas_call(
        functools.partial(_outproj_kernel, tm=tm, n_experts=n_experts, prompt_tiles=prompt_tiles),
        grid=(n_total // tm,),
        in_specs=[*[first(a) for a in prompt], *[second(a) for a in sample],
                  full(w_out_b), full(g2), full(rwt_b), full(rb), full(earlier)],
        out_specs=[row(d), pl.BlockSpec((tm * SUBLANES, LANES), lambda i: (i, 0)),
                   col(TOP_K), col(SUBLANES), pl.BlockSpec(counts.shape, lambda i: (0, 0))],
        out_shape=[jax.ShapeDtypeStruct((n_total, d), F32), jax.ShapeDtypeStruct((n_total * SUBLANES, LANES), F32),
                   jax.ShapeDtypeStruct((TOP_K, n_total), I32), jax.ShapeDtypeStruct((SUBLANES, n_total), F32),
                   counts],
        scratch_shapes=[pltpu.VMEM(counts.shape, F32)],
        compiler_params=_cparams("arbitrary"),
        name="outproj_router",
    )(*prompt, *sample, w_out_b, g2, rwt_b, rb, earlier)


def _sorted_rows_kernel(start_ref, code_ref, rows_ref, *, n_experts):
    code = code_ref[...]
    expert = code >> RANK_BITS
    rows = code & ((1 << RANK_BITS) - 1)
    for e in range(n_experts):
        rows = rows + jnp.where(expert == e, start_ref[e], 0)
    rows_ref[...] = rows


def _sorted_rows(starts, code):
    return pl.pallas_call(
        functools.partial(_sorted_rows_kernel, n_experts=starts.shape[0]),
        grid=(1,),
        in_specs=[pl.BlockSpec(memory_space=pltpu.SMEM), pl.BlockSpec(code.shape, lambda i: (0, 0))],
        out_specs=pl.BlockSpec(code.shape, lambda i: (0, 0)),
        out_shape=jax.ShapeDtypeStruct(code.shape, I32),
        compiler_params=_cparams("arbitrary"),
        name="sorted_rows",
    )(starts, code)


MXU_DIM = 256


def _split_w1_tile(w_ref, g_ref, l_ref):
    r = lax.broadcasted_iota(I32, (MXU_DIM, MXU_DIM), 0)
    c = lax.broadcasted_iota(I32, (MXU_DIM, MXU_DIM), 1)
    half = MXU_DIM // 2
    source = jnp.where(c < half, 2 * c, 2 * (c - half) + 1)
    perm = jnp.where(r == source, 1.0, 0.0).astype(BF16)
    for blk in range(w_ref.shape[1] // MXU_DIM):
        res = _dot(w_ref[:, blk * MXU_DIM:(blk + 1) * MXU_DIM].astype(BF16), perm)
        g_ref[:, blk * half:(blk + 1) * half] = res[:, :half].astype(BF16)
        l_ref[:, blk * half:(blk + 1) * half] = res[:, half:].astype(BF16)


def _dispatch_kernel(row_ref, zlo_ref, zhi_ref, nu_ref, hn_ref, w1_ref, xs_ref, w1g_ref, w1l_ref, zero_ref, sem,
                     *, tm, n_total, rows, n_experts, n_blocks):
    i = pl.program_id(0)
    tile = tm * SUBLANES
    block = rows * SUBLANES

    def pad_pieces(e):
        lo, n = zlo_ref[e], zhi_ref[e] - zlo_ref[e]
        for bit in range((rows - 1).bit_length()):
            size = 1 << bit
            below = n & (size - 1)
            start = pl.multiple_of((lo + below) * SUBLANES, SUBLANES)
            cp = pltpu.make_async_copy(zero_ref.at[pl.ds(0, size * SUBLANES), :],
                                       xs_ref.at[pl.ds(start, size * SUBLANES), :], sem.at[1])
            yield (n & size) != 0, cp

    def tail_copy(blk):
        start = pl.multiple_of(blk * block, block)
        return pltpu.make_async_copy(zero_ref, xs_ref.at[pl.ds(start, block), :], sem.at[1])

    def for_zero_copies(fn):
        def per_expert(e, carry):
            for used, cp in pad_pieces(e):
                pl.when(used)(functools.partial(fn, cp))
            return carry
        lax.fori_loop(0, n_experts, per_expert, 0)

        def per_block(blk, carry):
            fn(tail_copy(blk))
            return carry
        lax.fori_loop(nu_ref[0], n_blocks, per_block, 0)

    @pl.when(i == 0)
    def _():
        zero_ref[...] = jnp.zeros_like(zero_ref)
        for_zero_copies(lambda cp: cp.start())

    def body(t, carry):
        src = pl.multiple_of(t * SUBLANES, SUBLANES)
        for k in range(TOP_K):
            dst = pl.multiple_of(row_ref[k * n_total + i * tm + t] * SUBLANES, SUBLANES)
            pltpu.make_async_copy(hn_ref.at[pl.ds(src, SUBLANES), :], xs_ref.at[pl.ds(dst, SUBLANES), :],
                                  sem.at[0]).start()
        return carry
    lax.fori_loop(0, tm, body, 0)

    _split_w1_tile(w1_ref, w1g_ref, w1l_ref)

    for _ in range(TOP_K):
        pltpu.make_async_copy(hn_ref, xs_ref.at[pl.ds(0, tile), :], sem.at[0]).wait()

    @pl.when(i == pl.num_programs(0) - 1)
    def _():
        for_zero_copies(lambda cp: cp.wait())


def _dispatch(sorted_rows, zlo, zhi, n_used, hn, w1, n_blocks):
    n_total = hn.shape[0] // SUBLANES
    n_exp, d, ff2 = w1.shape
    assert n_total % n_exp == 0, "dispatch splits the tokens evenly over its (expert, w1 row tile) grid steps"
    tr = next(t for t in (2 * ROW_TILE, ROW_TILE, d) if d % t == 0 and n_total % (n_exp * (d // t)) == 0)
    row_tiles = d // tr
    steps = n_exp * row_tiles
    tm = n_total // steps
    rows = EXPERT_ROWS
    assert rows % SUBLANES == 0
    half = pl.BlockSpec((None, tr, ff2 // 2), lambda i, *_: (i // row_tiles, i % row_tiles, 0))
    grid_spec = pltpu.PrefetchScalarGridSpec(
        num_scalar_prefetch=4,
        grid=(steps,),
        in_specs=[pl.BlockSpec((tm * SUBLANES, LANES), lambda i, *_: (i, 0)),
                  pl.BlockSpec((None, tr, ff2), lambda i, *_: (i // row_tiles, i % row_tiles, 0))],
        out_specs=[pl.BlockSpec(memory_space=pl.ANY), half, half],
        scratch_shapes=[pltpu.VMEM((rows * SUBLANES, LANES), F32), pltpu.SemaphoreType.DMA((2,))],
    )
    return pl.pallas_call(
        functools.partial(_dispatch_kernel, tm=tm, n_total=n_total, rows=rows, n_experts=n_exp, n_blocks=n_blocks),
        grid_spec=grid_spec,
        out_shape=[jax.ShapeDtypeStruct((n_blocks * rows * SUBLANES, LANES), F32),
                   jax.ShapeDtypeStruct((n_exp, d, ff2 // 2), BF16), jax.ShapeDtypeStruct((n_exp, d, ff2 // 2), BF16)],
        compiler_params=_cparams("arbitrary"),
        name="dispatch_split_w1",
    )(sorted_rows, zlo, zhi, n_used, hn, w1)


def _expert_kernel(be_ref, nu_ref, xs_ref, w1g_ref, w1l_ref, b1g_ref, b1l_ref, w2_ref, b2_ref, y_ref, *, rows):
    j = pl.program_id(0)
    n_used = nu_ref[0]

    @pl.when(j < n_used)
    def _():
        x = jnp.concatenate([xs_ref[pl.ds(s, rows, stride=SUBLANES), :] for s in range(SUBLANES)],
                            axis=-1).astype(BF16)
        glu = jnp.minimum(_dot(x, w1g_ref[...]) + b1g_ref[...], SWIGLU_LIMIT)
        lin = jnp.clip(_dot(x, w1l_ref[...]) + b1l_ref[...], -SWIGLU_LIMIT, SWIGLU_LIMIT)
        act = glu * jax.nn.sigmoid(SWIGLU_ALPHA * glu) * (lin + 1.0)
        y = _dot(act.astype(BF16), w2_ref[...].astype(BF16)) + b2_ref[...]
        for s in range(SUBLANES):
            y_ref[pl.ds(s, rows, stride=SUBLANES), :] = y[:, s * LANES:(s + 1) * LANES]

    @pl.when(j >= n_used)
    def _():
        y_ref[...] = jnp.zeros_like(y_ref)


def _experts(block_e, n_used, xs, w1g, w1l, b1g, b1l, w2, b2, n_blocks):
    rows = EXPERT_ROWS
    tile = rows * SUBLANES
    wspec = lambda a: pl.BlockSpec((None,) + a.shape[1:], lambda j, be, nu: (be[j], 0, 0))
    grid_spec = pltpu.PrefetchScalarGridSpec(
        num_scalar_prefetch=2,
        grid=(n_blocks,),
        in_specs=[pl.BlockSpec((tile, LANES), lambda j, be, nu: (jnp.minimum(j, nu[0] - 1), 0)),
                  wspec(w1g), wspec(w1l), wspec(b1g), wspec(b1l), wspec(w2), wspec(b2)],
        out_specs=pl.BlockSpec((tile, LANES), lambda j, be, nu: (j, 0)),
    )
    return pl.pallas_call(
        functools.partial(_expert_kernel, rows=rows),
        grid_spec=grid_spec,
        out_shape=jax.ShapeDtypeStruct((n_blocks * tile, LANES), F32),
        compiler_params=_cparams("arbitrary"),
        name="experts",
    )(block_e, n_used, xs, w1g, w1l, b1g, b1l, w2, b2)


def _combine_kernel(row_ref, yb_ref, gate_ref, h_ref, g_ref, yp_ref, ys_ref, cbuf, sem, *, tm, n_total,
                    prompt_tiles):
    i = pl.program_id(0)
    n = pl.num_programs(0)
    slot = i % 2
    tile = tm * SUBLANES

    def gather(blk, sl):
        def body(t2, carry):
            for t in (2 * t2, 2 * t2 + 1):
                for k in range(TOP_K):
                    src = pl.multiple_of(row_ref[k * n_total + blk * tm + t] * SUBLANES, SUBLANES)
                    dst = pl.multiple_of((k * tm + t) * SUBLANES, SUBLANES)
                    pltpu.make_async_copy(yb_ref.at[pl.ds(src, SUBLANES), :], cbuf.at[sl, pl.ds(dst, SUBLANES), :],
                                          sem.at[sl]).start()
            return carry
        lax.fori_loop(0, tm // 2, body, 0)

    @pl.when(i == 0)
    def _():
        gather(0, 0)

    @pl.when(i + 1 < n)
    def _():
        gather(i + 1, 1 - slot)

    pltpu.make_async_copy(yb_ref.at[pl.ds(0, TOP_K * tile), :], cbuf.at[slot], sem.at[slot]).wait()
    gates = gate_ref[...].T
    slabs = []
    for s in range(SUBLANES):
        acc = None
        for k in range(TOP_K):
            part = gates[:, k:k + 1] * cbuf[slot, pl.ds(k * tile + s, tm, stride=SUBLANES), :]
            acc = part if acc is None else acc + part
        slabs.append(acc)
    hf = h_ref[...] + jnp.concatenate(slabs, axis=-1)
    y = _rms(hf, g_ref[...])

    @pl.when(i < prompt_tiles)
    def _():
        yp_ref[...] = y

    @pl.when(i >= prompt_tiles)
    def _():
        ys_ref[...] = y


def _combine(rows, yb, gates, h, gf, np_):
    n, d = h.shape
    ns_ = n - np_
    tm = min(COMBINE_TILE, math.gcd(np_, ns_))
    prompt_tiles = np_ // tm
    row = lambda w: pl.BlockSpec((tm, w), lambda i, *_: (i, 0))
    grid_spec = pltpu.PrefetchScalarGridSpec(
        num_scalar_prefetch=1,
        grid=(n // tm,),
        in_specs=[pl.BlockSpec(memory_space=pl.ANY), pl.BlockSpec((gates.shape[0], tm), lambda i, *_: (0, i)), row(d),
                  pl.BlockSpec(gf.shape, lambda i, *_: (0, 0))],
        out_specs=[pl.BlockSpec((tm, d), lambda i, *_: (jnp.minimum(i, prompt_tiles - 1), 0)),
                   pl.BlockSpec((tm, d), lambda i, *_: (jnp.maximum(i - prompt_tiles, 0), 0))],
        scratch_shapes=[pltpu.VMEM((2, TOP_K * tm * SUBLANES, LANES), F32), pltpu.SemaphoreType.DMA((2,))],
    )
    return pl.pallas_call(
        functools.partial(_combine_kernel, tm=tm, n_total=n, prompt_tiles=prompt_tiles),
        grid_spec=grid_spec,
        out_shape=[jax.ShapeDtypeStruct((np_, d), F32), jax.ShapeDtypeStruct((ns_, d), F32)],
        compiler_params=_cparams("arbitrary"),
        name="combine_norm",
    )(rows, yb, gates, h, gf)


def kernel(x_prompt, x_sample, state_pool, cache_k, cache_v, page_table, norm1_g, w_in, pool_w, pool_scale,
           lambda_q1, lambda_k1, lambda_q2, lambda_k2, subln_g, w_out, norm2_g, router_w, router_b,
           w1, b1, w2, b2, normf_g):
    batch, seq, d = x_prompt.shape
    bs, n_new, _ = x_sample.shape
    depth = w_in.shape[0]
    assert depth == 1, "single-layer trunk"
    n_heads = cache_k.shape[3]
    page_size = cache_k.shape[2]
    pw = pool_w.shape[1] * pool_w.shape[2]
    qk_w = n_heads * HEAD_W
    n_experts = router_w.shape[2]
    np_, ns_ = batch * seq, bs * n_new
    n_total = np_ + ns_
    lam_init = 0.8 - 0.6 * math.exp(-0.3 * 0)

    g1 = norm1_g[0][None]
    w_in_b = w_in[0].astype(BF16)
    wvt_b = w_in_b[:, pw + 2 * qk_w:].T
    pool_w_b = pool_w[0].astype(BF16)
    ps = pool_scale[0][None]
    lams = [a[0][None] for a in (lambda_q1, lambda_k1, lambda_q2, lambda_k2)]
    sg = subln_g[0][None]
    slopes = jnp.exp2(-8.0 * (jnp.arange(n_heads, dtype=F32) + 1.0) / n_heads)
    w_out_b = w_out[0].astype(BF16)
    g2 = norm2_g[0][None]
    rwt_b = router_w[0].T.astype(BF16)
    rb = router_b[0][:, None]
    b1_pairs = b1[0].T.reshape(-1, 2, n_experts)
    b1g = b1_pairs[:, 0, :].T[:, None, :]
    b1l = b1_pairs[:, 1, :].T[:, None, :]
    w2b = w2.reshape(w2.shape[1:])
    b2e = b2[0][:, None, :]
    gf = normf_g[None]

    xp = x_prompt.reshape(np_, d)
    tq = min(ATTN_TILE, seq)
    pool_p, tail_p, q_p, k_p, v_p, kb_p, vt_p = _inproj(xp, g1, w_in_b, wvt_b, pw, qk_w,
                                                        prompt=(pool_w_b, ps, seq, tq))
    o_p = _attn_prompt(q_p, kb_p, vt_p, slopes, lams, sg, batch, seq, n_heads, lam_init, tq)

    xs = x_sample.reshape(ns_, d)
    u_s, q_s, k_s, v_s = _inproj(xs, g1, w_in_b, wvt_b, pw, qk_w)
    pool_s, new_pool_s = _pool_sample(jnp.transpose(state_pool[0], (1, 0, 2)),
                                      jnp.transpose(u_s.reshape(bs, n_new, pw), (1, 0, 2)), pool_w_b, ps)
    pool_s = jnp.transpose(pool_s, (1, 0, 2))
    new_pool_s = jnp.transpose(new_pool_s, (1, 0, 2))
    kc = cache_k.reshape(cache_k.shape[1], page_size * n_heads, HEAD_W)
    vc = cache_v.reshape(cache_v.shape[1], page_size * n_heads, HEAD_W)
    o_s = _attn_sample(q_s.reshape(bs, n_new, qk_w), k_s.reshape(bs, n_new, qk_w), v_s.reshape(bs, n_new, qk_w),
                       kc, vc, page_table, slopes, lams, sg, n_heads, lam_init)

    h, hn, code, gates, cnt = _outproj((pool_p, o_p, xp), (pool_s.reshape(ns_, pw), o_s.reshape(ns_, qk_w), xs),
                                       w_out_b, g2, rwt_b, rb)

    counts = cnt[:, 0].astype(I32)
    n_blocks = -(-n_total * TOP_K // EXPERT_ROWS) + n_experts
    padded = (counts + EXPERT_ROWS - 1) // EXPERT_ROWS * EXPERT_ROWS
    pad_ends = jnp.cumsum(padded)
    pad_starts = pad_ends - padded
    n_used = (pad_ends[-1:] // EXPERT_ROWS).astype(I32)
    block_start = jnp.arange(n_blocks, dtype=I32) * EXPERT_ROWS
    block_e = jnp.minimum(jnp.sum((pad_ends[None, :] <= block_start[:, None]).astype(I32), axis=1), n_experts - 1)

    sorted_rows = _sorted_rows(pad_starts, code).reshape(-1)
    xs_rows, w1g, w1l = _dispatch(sorted_rows, pad_starts + counts, pad_ends, n_used, hn, w1.reshape(w1.shape[1:]),
                                  n_blocks)
    yb = _experts(block_e, n_used, xs_rows, w1g, w1l, b1g, b1l, w2b, b2e, n_blocks)
    y_p, y_s = _combine(sorted_rows, yb, gates, h, gf, np_)

    return (y_p.reshape(batch, seq, d), y_s.reshape(bs, n_new, d),
            tail_p[None, :, POOL_HALO - state_pool.shape[2]:],
            new_pool_s[None],
            k_p.reshape(1, batch, seq, n_heads, HEAD_W), v_p.reshape(1, batch, seq, n_heads, HEAD_W),
            k_s.reshape(1, bs, n_new, n_heads, HEAD_W), v_s.reshape(1, bs, n_new, n_heads, HEAD_W))
```
